```python
import jax, jax.numpy as jnp
from jax import lax
import numpy as np

D_MODEL = 2048
BATCH = 4
SEQ = 2048
DEPTH = 1
DEC_BATCH = 128
DEC_SEQ = 8
PAST_LEN = 16384
PAGE_SIZE = 128

MIX_WIDTH = D_MODEL
RW_WIDTH = MIX_WIDTH // 2
RW_HEAD_DIM = 64
RW_HEADS = RW_WIDTH // RW_HEAD_DIM
RW_DECAY_LORA = max(32, int(round(1.8 * RW_WIDTH ** 0.5 / 32)) * 32)
RW_A_LORA = max(32, int(round(1.8 * RW_WIDTH ** 0.5 / 32)) * 32)
RW_GATE_LORA = max(32, int(round(0.6 * RW_WIDTH ** 0.8 / 32)) * 32)
RW_PROJ = 3 * RW_WIDTH + RW_DECAY_LORA + RW_A_LORA + RW_GATE_LORA
RW_GN_EPS = 64e-5
GLA_WIDTH = MIX_WIDTH - RW_WIDTH
GLA_HEADS = 4
GLA_KEY_WIDTH = GLA_WIDTH // 2
GLA_DK = GLA_KEY_WIDTH // GLA_HEADS
GLA_DV = GLA_WIDTH // GLA_HEADS
GLA_GATE_RANK = 16
GLA_GATE_NORM = 16.0
GLA_CHUNK = 64
GLA_PROJ = 2 * GLA_KEY_WIDTH + GLA_WIDTH + GLA_GATE_RANK + GLA_WIDTH
IN_WIDTH = RW_PROJ + GLA_PROJ
D_FF = 4 * D_MODEL
NORM_EPS = 1e-6
HEAD_NORM_EPS = 1e-5

kernel_name = 'hybrid_rwkv7_gla_decoder_step'


def rms_norm(x, g, eps=NORM_EPS):
    xf = x.astype(jnp.float32)
    y = xf * lax.rsqrt(jnp.mean(xf * xf, axis=-1, keepdims=True) + eps)
    return (y * g.astype(jnp.float32)).astype(x.dtype)


def _split(t, sizes):
    offsets = np.cumsum(sizes)[:-1].tolist()
    return jnp.split(t, offsets, axis=-1)


def rwkv7_mixer(p, shift_prev, s0, mu, w0, w2, a0, a2, g2, k_k, k_a, r_k, ln_w, ln_b):
    b, l, _ = p.shape
    prev = jnp.concatenate([shift_prev[:, None, :].astype(p.dtype), p[:, :-1]], axis=1)
    m = p + (prev - p) * mu
    r, k, v, xw, xa, xg = _split(m, [RW_WIDTH, RW_WIDTH, RW_WIDTH, RW_DECAY_LORA, RW_A_LORA, RW_GATE_LORA])
    w_log = -jax.nn.softplus(-(w0 + jnp.tanh(xw) @ w2)) - 0.5
    decay = jnp.exp(-jnp.exp(w_log))
    a = jax.nn.sigmoid(a0 + xa @ a2)
    g = jax.nn.sigmoid(xg) @ g2
    hd = lambda t: t.reshape(b, l, RW_HEADS, RW_HEAD_DIM)
    kkf = hd(k * k_k).astype(jnp.float32)
    kk = (kkf / jnp.maximum(jnp.sqrt(jnp.sum(kkf * kkf, axis=-1, keepdims=True)), 1e-12)).astype(p.dtype)
    k = k * (1 + (a - 1) * k_a)
    r, k, v, decay, a = hd(r), hd(k), hd(v), hd(decay), hd(a)

    def step(S, inp):
        r_t, w_t, k_t, v_t, kk_t, a_t = inp
        sa = jnp.einsum('bhij,bhj->bhi', S, -kk_t)
        S = (S * w_t[:, :, None, :] + sa[..., None] * (kk_t * a_t)[:, :, None, :]
             + v_t[..., None] * k_t[:, :, None, :]).astype(s0.dtype)
        y = jnp.einsum('bhij,bhj->bhi', S, r_t)
        return S, y

    xs = tuple(jnp.swapaxes(t, 0, 1) for t in (r, decay, k, v, kk, a))
    s_fin, ys = lax.scan(step, s0, xs)
    y = jnp.swapaxes(ys, 0, 1).astype(jnp.float32)
    mean = jnp.mean(y, axis=-1, keepdims=True)
    var = jnp.mean(jnp.square(y - mean), axis=-1, keepdims=True)
    yn = ((y - mean) * lax.rsqrt(var + RW_GN_EPS)).reshape(b, l, RW_WIDTH) * ln_w + ln_b
    bonus = jnp.sum(r * k * r_k, axis=-1, keepdims=True) * v
    out = (yn.astype(p.dtype) + bonus.reshape(b, l, RW_WIDTH)) * g
    return out, p[:, -1], s_fin


def gla_mixer(p, s0, gw2, gb, norm_w):
    b, l, _ = p.shape
    f32 = jnp.float32
    q, k, v, xgate, gout = _split(p, [GLA_KEY_WIDTH, GLA_KEY_WIDTH, GLA_WIDTH, GLA_GATE_RANK, GLA_WIDTH])
    gk = jax.nn.log_sigmoid((xgate @ gw2 + gb).astype(f32)) / GLA_GATE_NORM
    q = q.astype(f32) * GLA_DK ** -0.5
    c = min(GLA_CHUNK, l)
    n = -(-l // c)
    pad = n * c - l

    def chunks(t, d):
        t = jnp.pad(t.astype(f32), ((0, 0), (0, pad), (0, 0)))
        return t.reshape(b, n, c, GLA_HEADS, d)

    q, k, gk, v = chunks(q, GLA_DK), chunks(k, GLA_DK), chunks(gk, GLA_DK), chunks(v, GLA_DV)
    G = jnp.cumsum(gk, axis=2)
    q_dec = q * jnp.exp(G)
    k_inv = k * jnp.exp(-G)
    causal = jnp.tril(jnp.ones((c, c), dtype=bool))
    A = jnp.where(causal, jnp.einsum('bnchd,bnshd->bnhcs', q_dec, k_inv), 0.0)
    o_intra = jnp.einsum('bnhcs,bnshe->bnche', A, v)
    g_last = G[:, :, -1]
    k_end = k * jnp.exp(g_last[:, :, None] - G)

    def step(S, inp):
        qd, ke, vc, gl = inp
        o = jnp.einsum('bchd,bhde->bche', qd, S)
        S = S * jnp.exp(gl)[..., None] + jnp.einsum('bchd,bche->bhde', ke, vc)
        return S, o

    sw = lambda t: jnp.swapaxes(t, 0, 1)
    s_fin, o_inter = lax.scan(step, s0.astype(f32), (sw(q_dec), sw(k_end), sw(v), sw(g_last)))
    o = (o_intra + sw(o_inter)).reshape(b, n * c, GLA_HEADS, GLA_DV)[:, :l]
    on = o * lax.rsqrt(jnp.mean(o * o, axis=-1, keepdims=True) + HEAD_NORM_EPS) * norm_w.astype(f32)
    on = on * jax.nn.silu(gout.astype(f32)).reshape(b, l, GLA_HEADS, GLA_DV)
    return on.reshape(b, l, GLA_WIDTH).astype(p.dtype), s_fin.astype(s0.dtype)


def hybrid_layer(x, shift_prev, s_rw, s_gla, norm1_g, w_in, rw_mu, rw_w0, rw_w2, rw_a0, rw_a2,
                 rw_g2, rw_k_k, rw_k_a, rw_r_k, rw_ln_w, rw_ln_b, gla_gw2, gla_gb, gla_norm_w,
                 w_out, norm2_g, w_up, w_down):
    h = rms_norm(x, norm1_g)
    proj = h @ w_in
    p_rw, p_gla = proj[..., :RW_PROJ], proj[..., RW_PROJ:]
    o_rw, shift_new, s_rw_new = rwkv7_mixer(p_rw, shift_prev, s_rw, rw_mu, rw_w0, rw_w2, rw_a0, rw_a2,
                                            rw_g2, rw_k_k, rw_k_a, rw_r_k, rw_ln_w, rw_ln_b)
    o_gla, s_gla_new = gla_mixer(p_gla, s_gla, gla_gw2, gla_gb, gla_norm_w)
    x = x + jnp.concatenate([o_rw, o_gla], axis=-1) @ w_out
    u = jax.nn.relu(rms_norm(x, norm2_g) @ w_up)
    x = x + (u * u) @ w_down
    return x, shift_new, s_rw_new, s_gla_new


def trunk(x, shift, s_rw, s_gla, layer_weights, norm_f_g):
    sh_out, rw_out, gla_out = [], [], []
    for layer in range(DEPTH):
        lw = [w[layer] for w in layer_weights]
        x, sh, srw, sg = hybrid_layer(x, shift[layer], s_rw[layer], s_gla[layer], *lw)
        sh_out.append(sh)
        rw_out.append(srw)
        gla_out.append(sg)
    y = rms_norm(x, norm_f_g)
    return y, jnp.stack(sh_out), jnp.stack(rw_out), jnp.stack(gla_out)


def setup_inputs(seed: int = 0) -> dict:
    key = jax.random.key(seed)
    ks = jax.random.split(key, 32)
    L = DEPTH
    nrm = lambda k, shape, s: jax.random.normal(k, shape, jnp.float32) * s
    return {
        'x_prompt': nrm(ks[0], (BATCH, SEQ, D_MODEL), 1.0),
        'x_sample': nrm(ks[1], (DEC_BATCH, DEC_SEQ, D_MODEL), 1.0),
        'state_rwkv_shift': nrm(ks[2], (L, DEC_BATCH, RW_PROJ), 1.0),
        'state_rwkv_wkv': nrm(ks[3], (L, DEC_BATCH, RW_HEADS, RW_HEAD_DIM, RW_HEAD_DIM), 0.3),
        'state_gla': nrm(ks[4], (L, DEC_BATCH, GLA_HEADS, GLA_DK, GLA_DV), 0.3),
        'norm1_g': 1.0 + nrm(ks[5], (L, D_MODEL), 0.02),
        'w_in': nrm(ks[6], (L, D_MODEL, IN_WIDTH), D_MODEL ** -0.5),
        'rw_mu': jax.random.uniform(ks[7], (L, RW_PROJ), jnp.float32),
        'rw_w0': jax.random.uniform(ks[8], (L, RW_WIDTH), jnp.float32, minval=-6.0, maxval=-1.0),
        'rw_w2': nrm(ks[9], (L, RW_DECAY_LORA, RW_WIDTH), 0.5 * RW_DECAY_LORA ** -0.5),
        'rw_a0': nrm(ks[10], (L, RW_WIDTH), 0.1),
        'rw_a2': nrm(ks[11], (L, RW_A_LORA, RW_WIDTH), 0.5 * RW_A_LORA ** -0.5),
        'rw_g2': nrm(ks[12], (L, RW_GATE_LORA, RW_WIDTH), RW_GATE_LORA ** -0.5),
        'rw_k_k': 0.85 + nrm(ks[13], (L, RW_WIDTH), 0.02),
        'rw_k_a': 1.0 + nrm(ks[14], (L, RW_WIDTH), 0.02),
        'rw_r_k': nrm(ks[15], (L, RW_HEADS, RW_HEAD_DIM), 0.1),
        'rw_ln_w': 1.0 + nrm(ks[16], (L, RW_WIDTH), 0.02),
        'rw_ln_b': nrm(ks[17], (L, RW_WIDTH), 0.02),
        'gla_gw2': nrm(ks[18], (L, GLA_GATE_RANK, GLA_KEY_WIDTH), GLA_GATE_RANK ** -0.5),
        'gla_gb': nrm(ks[19], (L, GLA_KEY_WIDTH), 0.1),
        'gla_norm_w': 1.0 + nrm(ks[20], (L, GLA_DV), 0.02),
        'w_out': nrm(ks[21], (L, MIX_WIDTH, D_MODEL), MIX_WIDTH ** -0.5),
        'norm2_g': 1.0 + nrm(ks[22], (L, D_MODEL), 0.02),
        'w_up': nrm(ks[23], (L, D_MODEL, D_FF), D_MODEL ** -0.5),
        'w_down': nrm(ks[24], (L, D_FF, D_MODEL), D_FF ** -0.5),
        'norm_f_g': 1.0 + nrm(ks[25], (D_MODEL,), 0.02),
    }


def reference(x_prompt, x_sample, state_rwkv_shift, state_rwkv_wkv, state_gla, norm1_g, w_in, rw_mu,
              rw_w0, rw_w2, rw_a0, rw_a2, rw_g2, rw_k_k, rw_k_a, rw_r_k, rw_ln_w, rw_ln_b, gla_gw2,
              gla_gb, gla_norm_w, w_out, norm2_g, w_up, w_down, norm_f_g):
    layer_weights = (norm1_g, w_in, rw_mu, rw_w0, rw_w2, rw_a0, rw_a2, rw_g2, rw_k_k, rw_k_a, rw_r_k,
                     rw_ln_w, rw_ln_b, gla_gw2, gla_gb, gla_norm_w, w_out, norm2_g, w_up, w_down)
    dt = x_prompt.dtype
    shift0 = jnp.zeros((DEPTH, BATCH, RW_PROJ), dt)
    wkv0 = jnp.zeros((DEPTH, BATCH, RW_HEADS, RW_HEAD_DIM, RW_HEAD_DIM), dt)
    gla0 = jnp.zeros((DEPTH, BATCH, GLA_HEADS, GLA_DK, GLA_DV), dt)
    y_prompt, sh_p, wkv_p, gla_p = trunk(x_prompt, shift0, wkv0, gla0, layer_weights, norm_f_g)
    y_sample, sh_s, wkv_s, gla_s = trunk(x_sample, state_rwkv_shift, state_rwkv_wkv, state_gla,
                                         layer_weights, norm_f_g)
    return (y_prompt, y_sample, sh_p, wkv_p, gla_p, sh_s, wkv_s, gla_s)
```

```python
import functools

import jax
import jax.numpy as jnp
from jax import lax
from jax.experimental import pallas as pl
from jax.experimental.pallas import tpu as pltpu

F32 = jnp.float32
BF16 = jnp.bfloat16

D_MODEL = 2048
RW_WIDTH = 1024
RW_HEADS = 16
RW_HEAD_DIM = 64
RW_LORA_W = 64
RW_LORA_A = 64
RW_LORA_G = 160
RW_PROJ = 3 * RW_WIDTH + RW_LORA_W + RW_LORA_A + RW_LORA_G
RW_GN_EPS = 64e-5
GLA_WIDTH = 1024
GLA_HEADS = 4
GLA_KEY_WIDTH = 512
GLA_DK = 128
GLA_DV = 256
GLA_GATE_RANK = 16
GLA_GATE_NORM = 16.0
GLA_CHUNK = 64
GLA_PROJ = 2 * GLA_KEY_WIDTH + GLA_WIDTH + GLA_GATE_RANK + GLA_WIDTH
D_FF = 4 * D_MODEL
NORM_EPS = 1e-6
HEAD_NORM_EPS = 1e-5

LANES = 128
SUBLANES = 8

GROUP_PAD = 3456
RW_OFF_XWA = 3 * RW_WIDTH
RW_OFF_XG = RW_OFF_XWA + LANES
GLA_OFF_K = GLA_KEY_WIDTH
GLA_OFF_V = 2 * GLA_KEY_WIDTH
GLA_OFF_GATE = GLA_OFF_V + GLA_WIDTH
GLA_OFF_GOUT = GLA_OFF_GATE + LANES
NP = 2 * GROUP_PAD

ROWS = 128
VMEM_LIMIT = 56 * 1024 * 1024

NN = (((1,), (0,)), ((), ()))
NT = (((1,), (1,)), ((), ()))


def _dot(a, b, dims=NN):
    return lax.dot_general(a, b, dims, preferred_element_type=F32)


def _parts(x, n):
    if x.dtype == BF16:
        return [x]
    out = []
    rem = x
    for i in range(n):
        h = rem.astype(BF16)
        out.append(h)
        if i + 1 < n:
            rem = rem - h.astype(F32)
    return out


def _mm(a, b, pa=1, pb=1, dims=NN):
    pa = 1 if a.dtype == BF16 else pa
    pb = 1 if b.dtype == BF16 else pb
    aa = _parts(a, pa)
    bb = _parts(b, pb)
    n = max(pa, pb)
    acc = None
    for i in range(pa):
        for j in range(pb):
            if i + j < n:
                t = _dot(aa[i], bb[j], dims)
                acc = t if acc is None else acc + t
    return acc


def _softplus(x):
    return jnp.maximum(x, 0.0) + jnp.log1p(jnp.exp(-jnp.abs(x)))


def _sigmoid(x):
    return 1.0 / (1.0 + jnp.exp(-x))


def _chunk_masks(m, c):
    sh = c.bit_length() - 1
    ri = lax.broadcasted_iota(jnp.int32, (m, m), 0)
    ci = lax.broadcasted_iota(jnp.int32, (m, m), 1)
    same = (ri >> sh) == (ci >> sh)
    return same, ri, ci


def _inproj_kernel(x_ref, g_ref, w_ref, o_ref, h_ref):
    @pl.when(pl.program_id(1) == 0)
    def _():
        x = x_ref[...]
        ms = jnp.mean(x * x, axis=-1, keepdims=True)
        h_ref[...] = (x * lax.rsqrt(ms + NORM_EPS) * g_ref[...]).astype(BF16)

    o_ref[...] = _dot(h_ref[...], w_ref[...])


def _inproj(x2d, g, w_p, tm, tn):
    t = x2d.shape[0]
    return pl.pallas_call(
        _inproj_kernel,
        grid=(t // tm, NP // tn),
        in_specs=[
            pl.BlockSpec((tm, D_MODEL), lambda i, j: (i, 0)),
            pl.BlockSpec((1, D_MODEL), lambda i, j: (0, 0)),
            pl.BlockSpec((D_MODEL, tn), lambda i, j: (0, j)),
        ],
        out_specs=pl.BlockSpec((tm, tn), lambda i, j: (i, j)),
        out_shape=jax.ShapeDtypeStruct((t, NP), F32),
        scratch_shapes=[pltpu.VMEM((tm, D_MODEL), BF16)],
        compiler_params=pltpu.CompilerParams(
            dimension_semantics=("parallel", "arbitrary"), vmem_limit_bytes=VMEM_LIMIT),
        name="inproj",
    )(x2d, g, w_p)


def _rwkv_kernel(p_ref, sh_ref, s0_ref, mu_ref, wwa_ref, w0_ref, a0_ref, g2_ref, kk_ref, ka_ref,
                 rk_ref, lnw_ref, lnb_ref, e1_ref, e2_ref,
                 o_ref, shout_ref, st_ref,
                 carry_ref, bk_ref, ke_ref, be_ref, pc_ref, art_ref, vt_ref, arb_ref, uy_ref, yt_ref,
                 *, nb, tb, c, pp):
    j = pl.program_id(1)
    m_rows = nb * tb
    n_lvl = c.bit_length() - 1
    hd = RW_HEAD_DIM

    @pl.when(j == 0)
    def _():
        st_ref[...] = s0_ref[...]
        carry_ref[...] = sh_ref[...]

    yt_ref[...] = jnp.zeros_like(yt_ref)

    p3 = p_ref[...]
    first = carry_ref[...]
    t3 = lax.broadcasted_iota(jnp.int32, p3.shape, 1)
    prev3 = jnp.where(t3 == 0, first, pltpu.roll(p3, 1, axis=1))
    last = p3[:, tb - 1:tb, :]
    carry_ref[...] = last
    shout_ref[...] = last
    m = (p3 + (prev3 - p3) * mu_ref[...]).reshape(m_rows, GROUP_PAD)

    r = m[:, 0:RW_WIDTH]
    k = m[:, RW_WIDTH:2 * RW_WIDTH]
    v = m[:, 2 * RW_WIDTH:3 * RW_WIDTH]

    slab = m[:, RW_OFF_XWA:RW_OFF_XWA + LANES]
    lane = lax.broadcasted_iota(jnp.int32, slab.shape, 1)
    lhs = jnp.where(lane < RW_LORA_W, jnp.tanh(slab), slab).astype(BF16)
    wa = _dot(lhs, wwa_ref[...])
    xg = m[:, RW_OFF_XG:RW_OFF_XG + 2 * LANES]
    g = _dot(_sigmoid(xg).astype(BF16), g2_ref[...])

    logw = -jnp.exp(-_softplus(-(w0_ref[...] + wa[:, :RW_WIDTH])) - 0.5)
    a = _sigmoid(a0_ref[...] + wa[:, RW_WIDTH:])

    def gsum(x):
        s = _mm(x, e1_ref[...], pa=2)
        return _mm(s, e2_ref[...], pa=2)

    kkf = k * kk_ref[...]
    kk = kkf / jnp.maximum(jnp.sqrt(gsum(kkf * kkf)), 1e-12)
    k2 = k * (1.0 + (a - 1.0) * ka_ref[...])
    beta = kk * a

    same, ri, ci = _chunk_masks(m_rows, c)
    lmask = jnp.where(same & (ci <= ri), 1.0, 0.0).astype(BF16)
    umask = jnp.where(same & (ci > ri), 1.0, 0.0).astype(BF16)
    gcum = _mm(lmask, logw, pb=3)
    grev = _mm(umask, logw, pb=3)
    e_inv = jnp.exp(-gcum)
    e_rev = jnp.exp(grev)
    at = -kk * jnp.exp(gcum - logw)
    rt = r * jnp.exp(gcum)
    kh = k2 * e_inv
    bh = beta * e_inv
    ke = k2 * e_rev
    be = beta * e_rev
    pc = jnp.exp(gcum + grev)

    art_ref[:, 0:m_rows] = at.T
    art_ref[:, m_rows:2 * m_rows] = rt.T
    vt_ref[...] = v.T

    def to_heads(x, dst, row0):
        xr = pltpu.roll(x, RW_WIDTH - hd, axis=1)
        for h in range(RW_HEADS):
            src = xr if h % 2 else x
            lo = (h // 2) * LANES
            dst[h, row0:row0 + m_rows, :] = src[:, lo:lo + hd]

    to_heads(bh, bk_ref, 0)
    to_heads(kh, bk_ref, m_rows)
    to_heads(ke, ke_ref, 0)
    to_heads(be, be_ref, 0)
    to_heads(pc, pc_ref, 0)

    strict_t = same & (ri < ci)
    incl_t = same & (ri <= ci)

    def head_body(h, carry):
        hs = pl.multiple_of(h * hd, hd)
        art_h = art_ref[pl.ds(hs, hd), :]
        vt_h = vt_ref[pl.ds(hs, hd), :]
        xt = _mm(bk_ref[h], art_h, pp, pp)
        n_t = jnp.where(strict_t, xt[0:m_rows, 0:m_rows], 0.0)
        aak_t = jnp.where(strict_t, xt[m_rows:, 0:m_rows], 0.0)
        arb_t = jnp.where(incl_t, xt[0:m_rows, m_rows:], 0.0)
        ark_t = jnp.where(incl_t, xt[m_rows:, m_rows:], 0.0)
        va = _mm(vt_h, jnp.concatenate([aak_t, ark_t], axis=1), pp, pp)
        z = jnp.concatenate([art_h[:, 0:m_rows], va[:, 0:m_rows]], axis=0)
        npow = n_t
        for lvl in range(n_lvl):
            z = z + _mm(z, npow, pp, pp)
            if lvl + 1 < n_lvl:
                npow = _mm(npow, npow, pp, pp)
        ry = jnp.concatenate([art_h[:, m_rows:], va[:, m_rows:]], axis=0) + _mm(z, arb_t, pp, pp)
        arb_ref[pl.ds(hs, hd), 0:m_rows] = z[0:hd]
        arb_ref[pl.ds(hs, hd), m_rows:] = ry[0:hd]
        uy_ref[pl.ds(hs, hd), 0:m_rows] = z[hd:]
        uy_ref[pl.ds(hs, hd), m_rows:] = ry[hd:]
        return carry

    lax.fori_loop(0, RW_HEADS, head_body, 0)

    sh_c = c.bit_length() - 1
    cpb = tb // c
    row_id = lax.broadcasted_iota(jnp.int32, (m_rows, hd), 0) >> sh_c
    col_id = lax.broadcasted_iota(jnp.int32, (hd, m_rows), 1) >> sh_c

    def chunk_body(i, carry):
        b = i // cpb
        r0 = pl.multiple_of(i * c, c)
        rmask = row_id == i
        cmask = col_id == i

        def head_step(h, carry2):
            hs = pl.multiple_of(h * hd, hd)
            s = st_ref[b, h]
            zs = _mm(s, arb_ref[pl.ds(hs, hd), :], pp, pp) + uy_ref[pl.ds(hs, hd), :]
            ut = zs[:, 0:m_rows]
            yt_ref[pl.ds(hs, hd), :] = jnp.where(cmask, zs[:, m_rows:], yt_ref[pl.ds(hs, hd), :])
            kez = jnp.where(rmask, ke_ref[h], 0.0)
            bez = jnp.where(rmask, be_ref[h], 0.0)
            s_new = (s * pc_ref[h, pl.ds(r0, 1), :]
                     + _mm(vt_ref[pl.ds(hs, hd), :], kez, pp, pp) + _mm(ut, bez, pp, pp))
            st_ref[b, h] = s_new
            return carry2

        lax.fori_loop(0, RW_HEADS, head_step, 0, unroll=4)
        return carry

    lax.fori_loop(0, m_rows // c, chunk_body, 0)

    y = yt_ref[...].T
    inv_n = 1.0 / hd
    mean = gsum(y) * inv_n
    yc = y - mean
    var = gsum(yc * yc) * inv_n
    yn = yc * lax.rsqrt(var + RW_GN_EPS) * lnw_ref[...] + lnb_ref[...]
    bonus = gsum(r * k2 * rk_ref[...]) * v
    o_ref[...] = ((yn + bonus) * g).reshape(nb, tb, RW_WIDTH)


def _rwkv(p, shift_prev, s0, wts, nb, tb, c, pp):
    bsz, seq, _ = p.shape
    m_rows = nb * tb
    hd = RW_HEAD_DIM
    const = lambda shape: pl.BlockSpec(shape, lambda i, j: (0,) * len(shape))
    kern = functools.partial(_rwkv_kernel, nb=nb, tb=tb, c=c, pp=pp)
    return pl.pallas_call(
        kern,
        grid=(bsz // nb, seq // tb),
        in_specs=[
            pl.BlockSpec((nb, tb, GROUP_PAD), lambda i, j: (i, j, 0)),
            pl.BlockSpec((nb, 1, GROUP_PAD), lambda i, j: (i, 0, 0)),
            pl.BlockSpec((nb, RW_HEADS, hd, hd), lambda i, j: (i, 0, 0, 0)),
            const((1, GROUP_PAD)),
            const((LANES, 2 * RW_WIDTH)),
            const((1, RW_WIDTH)),
            const((1, RW_WIDTH)),
            const((2 * LANES, RW_WIDTH)),
            const((1, RW_WIDTH)),
            const((1, RW_WIDTH)),
            const((1, RW_WIDTH)),
            const((1, RW_WIDTH)),
            const((1, RW_WIDTH)),
            const((RW_WIDTH, LANES)),
            const((LANES, RW_WIDTH)),
        ],
        out_specs=[
            pl.BlockSpec((nb, tb, RW_WIDTH), lambda i, j: (i, j, 0)),
            pl.BlockSpec((nb, 1, GROUP_PAD), lambda i, j: (i, 0, 0)),
            pl.BlockSpec((nb, RW_HEADS, hd, hd), lambda i, j: (i, 0, 0, 0)),
        ],
        out_shape=[
            jax.ShapeDtypeStruct((bsz, seq, RW_WIDTH), F32),
            jax.ShapeDtypeStruct((bsz, 1, GROUP_PAD), F32),
            jax.ShapeDtypeStruct((bsz, RW_HEADS, hd, hd), F32),
        ],
        scratch_shapes=[
            pltpu.VMEM((nb, 1, GROUP_PAD), F32),
            pltpu.VMEM((RW_HEADS, 2 * m_rows, hd), F32),
            pltpu.VMEM((RW_HEADS, m_rows, hd), F32),
            pltpu.VMEM((RW_HEADS, m_rows, hd), F32),
            pltpu.VMEM((RW_HEADS, m_rows, hd), F32),
            pltpu.VMEM((RW_WIDTH, 2 * m_rows), F32),
            pltpu.VMEM((RW_WIDTH, m_rows), F32),
            pltpu.VMEM((RW_WIDTH, 2 * m_rows), F32),
            pltpu.VMEM((RW_WIDTH, 2 * m_rows), F32),
            pltpu.VMEM((RW_WIDTH, m_rows), F32),
        ],
        compiler_params=pltpu.CompilerParams(
            dimension_semantics=("parallel", "arbitrary"), vmem_limit_bytes=VMEM_LIMIT),
        name="rwkv7",
    )(p, shift_prev, s0, *wts)


def _gla_kernel(p_ref, s0_ref, gw2_ref, gb_ref, nw_ref, o_ref, st_ref,
                acc_ref, qd_ref, v_ref, kt_ref, et_ref, *, nb, tb, c, pp):
    j = pl.program_id(1)
    m_rows = nb * tb
    dk, dv = GLA_DK, GLA_DV

    @pl.when(j == 0)
    def _():
        st_ref[...] = s0_ref[...]

    p = p_ref[...].reshape(m_rows, GROUP_PAD)
    q = p[:, 0:GLA_KEY_WIDTH] * (dk ** -0.5)
    k = p[:, GLA_OFF_K:GLA_OFF_K + GLA_KEY_WIDTH]
    v = p[:, GLA_OFF_V:GLA_OFF_V + GLA_WIDTH]
    xs = p[:, GLA_OFF_GATE:GLA_OFF_GATE + LANES].astype(BF16)
    gout = p[:, GLA_OFF_GOUT:GLA_OFF_GOUT + GLA_WIDTH]
    gk = -_softplus(-(_dot(xs, gw2_ref[...]) + gb_ref[...])) / GLA_GATE_NORM

    same, ri, ci = _chunk_masks(m_rows, c)
    lmask = jnp.where(same & (ci <= ri), 1.0, 0.0).astype(BF16)
    umask = jnp.where(same & (ci > ri), 1.0, 0.0).astype(BF16)
    causal = same & (ci <= ri)
    gcum = _mm(lmask, gk, pb=3)
    grev = _mm(umask, gk, pb=3)
    qd = q * jnp.exp(gcum)
    kinv = k * jnp.exp(-gcum)
    qd_ref[...] = qd
    v_ref[...] = v
    kt_ref[...] = (k * jnp.exp(grev)).T
    et_ref[...] = jnp.exp(gcum + grev).T

    for h in range(GLA_HEADS):
        a = _mm(qd[:, h * dk:(h + 1) * dk], kinv[:, h * dk:(h + 1) * dk], pp, pp, NT)
        a = jnp.where(causal, a, 0.0)
        acc_ref[:, h * dv:(h + 1) * dv] = _mm(a, v[:, h * dv:(h + 1) * dv], pp, pp)

    sh_c = c.bit_length() - 1
    cpb = tb // c
    row_id = lax.broadcasted_iota(jnp.int32, (m_rows, dv), 0) >> sh_c
    for i in range(m_rows // c):
        b = i // cpb
        r0 = i * c
        rmask = row_id == i
        for h in range(GLA_HEADS):
            s = st_ref[b, h]
            acc_ref[r0:r0 + c, h * dv:(h + 1) * dv] += _mm(
                qd_ref[r0:r0 + c, h * dk:(h + 1) * dk], s, pp, pp)
            vz = jnp.where(rmask, v_ref[:, h * dv:(h + 1) * dv], 0.0)
            st_ref[b, h] = (s * et_ref[h * dk:(h + 1) * dk, r0:r0 + 1]
                            + _mm(kt_ref[h * dk:(h + 1) * dk, :], vz, pp, pp))

    for h in range(GLA_HEADS):
        o = acc_ref[:, h * dv:(h + 1) * dv]
        on = o * lax.rsqrt(jnp.mean(o * o, axis=-1, keepdims=True) + HEAD_NORM_EPS) * nw_ref[...]
        gh = gout[:, h * dv:(h + 1) * dv]
        o_ref[:, :, h * dv:(h + 1) * dv] = (on * (gh * _sigmoid(gh))).reshape(nb, tb, dv)


def _gla(p, s0, wts, nb, tb, c, pp):
    bsz, seq, _ = p.shape
    m_rows = nb * tb
    const = lambda shape: pl.BlockSpec(shape, lambda i, j: (0,) * len(shape))
    kern = functools.partial(_gla_kernel, nb=nb, tb=tb, c=c, pp=pp)
    return pl.pallas_call(
        kern,
        grid=(bsz // nb, seq // tb),
        in_specs=[
            pl.BlockSpec((nb, tb, GROUP_PAD), lambda i, j: (i, j, 1)),
            pl.BlockSpec((nb, GLA_HEADS, GLA_DK, GLA_DV), lambda i, j: (i, 0, 0, 0)),
            const((LANES, GLA_KEY_WIDTH)),
            const((1, GLA_KEY_WIDTH)),
            const((1, GLA_DV)),
        ],
        out_specs=[
            pl.BlockSpec((nb, tb, GLA_WIDTH), lambda i, j: (i, j, 0)),
            pl.BlockSpec((nb, GLA_HEADS, GLA_DK, GLA_DV), lambda i, j: (i, 0, 0, 0)),
        ],
        out_shape=[
            jax.ShapeDtypeStruct((bsz, seq, GLA_WIDTH), F32),
            jax.ShapeDtypeStruct((bsz, GLA_HEADS, GLA_DK, GLA_DV), F32),
        ],
        scratch_shapes=[
            pltpu.VMEM((m_rows, GLA_WIDTH), F32),
            pltpu.VMEM((m_rows, GLA_KEY_WIDTH), F32),
            pltpu.VMEM((m_rows, GLA_WIDTH), F32),
            pltpu.VMEM((GLA_KEY_WIDTH, m_rows), F32),
            pltpu.VMEM((GLA_KEY_WIDTH, m_rows), F32),
        ],
        compiler_params=pltpu.CompilerParams(
            dimension_semantics=("parallel", "arbitrary"), vmem_limit_bytes=VMEM_LIMIT),
        name="gla",
    )(p, s0, *wts)


def _outproj_kernel(x_ref, orw_ref, ogla_ref, w_ref, g_ref, x1_ref, h_ref):
    x1 = (x_ref[...]
          + _dot(orw_ref[...].astype(BF16), w_ref[0:RW_WIDTH, :])
          + _dot(ogla_ref[...].astype(BF16), w_ref[RW_WIDTH:, :]))
    x1_ref[...] = x1
    ms = jnp.mean(x1 * x1, axis=-1, keepdims=True)
    h_ref[...] = (x1 * lax.rsqrt(ms + NORM_EPS) * g_ref[...]).astype(BF16)


def _outproj(x2d, o_rw, o_gla, w_out, g2, tm):
    t = x2d.shape[0]
    return pl.pallas_call(
        _outproj_kernel,
        grid=(t // tm,),
        in_specs=[
            pl.BlockSpec((tm, D_MODEL), lambda i: (i, 0)),
            pl.BlockSpec((tm, RW_WIDTH), lambda i: (i, 0)),
            pl.BlockSpec((tm, GLA_WIDTH), lambda i: (i, 0)),
            pl.BlockSpec((D_MODEL, D_MODEL), lambda i: (0, 0)),
            pl.BlockSpec((1, D_MODEL), lambda i: (0, 0)),
        ],
        out_specs=[
            pl.BlockSpec((tm, D_MODEL), lambda i: (i, 0)),
            pl.BlockSpec((tm, D_MODEL), lambda i: (i, 0)),
        ],
        out_shape=[
            jax.ShapeDtypeStruct((t, D_MODEL), F32),
            jax.ShapeDtypeStruct((t, D_MODEL), BF16),
        ],
        compiler_params=pltpu.CompilerParams(
            dimension_semantics=("parallel",), vmem_limit_bytes=VMEM_LIMIT),
        name="outproj",
    )(x2d, o_rw, o_gla, w_out, g2)


def _mlp_kernel(x1_ref, h_ref, wu_ref, wd_ref, g_ref, o_ref):
    jf = pl.program_id(1)

    @pl.when(jf == 0)
    def _():
        o_ref[...] = x1_ref[...]

    u = jnp.maximum(_dot(h_ref[...], wu_ref[...]), 0.0)
    o_ref[...] += _dot((u * u).astype(BF16), wd_ref[...])

    @pl.when(jf == pl.num_programs(1) - 1)
    def _():
        x2 = o_ref[...]
        ms = jnp.mean(x2 * x2, axis=-1, keepdims=True)
        o_ref[...] = x2 * lax.rsqrt(ms + NORM_EPS) * g_ref[...]


def _mlp(x1, h2, w_up, w_down, gf, tm, tf):
    t = x1.shape[0]
    return pl.pallas_call(
        _mlp_kernel,
        grid=(t // tm, D_FF // tf),
        in_specs=[
            pl.BlockSpec((tm, D_MODEL), lambda i, j: (i, 0)),
            pl.BlockSpec((tm, D_MODEL), lambda i, j: (i, 0)),
            pl.BlockSpec((D_MODEL, tf), lambda i, j: (0, j)),
            pl.BlockSpec((tf, D_MODEL), lambda i, j: (j, 0)),
            pl.BlockSpec((1, D_MODEL), lambda i, j: (0, 0)),
        ],
        out_specs=pl.BlockSpec((tm, D_MODEL), lambda i, j: (i, 0)),
        out_shape=jax.ShapeDtypeStruct((t, D_MODEL), F32),
        compiler_params=pltpu.CompilerParams(
            dimension_semantics=("parallel", "arbitrary"), vmem_limit_bytes=VMEM_LIMIT),
        name="mlp",
    )(x1, h2, w_up, w_down, gf)


def _pad_cols(w, n):
    return jnp.pad(w, ((0, 0), (0, n - w.shape[1])))


def _prep_weights(w_in, rw_mu, rw_w0, rw_w2, rw_a0, rw_a2, rw_g2, rw_k_k, rw_k_a, rw_r_k, rw_ln_w, rw_ln_b,
                  gla_gw2, gla_gb, gla_norm_w):
    go = RW_PROJ
    gla_head = w_in[:, go:go + GLA_OFF_GATE + GLA_GATE_RANK]
    gla_gout = w_in[:, go + GLA_OFF_GATE + GLA_GATE_RANK:]
    w_p = jnp.concatenate([
        _pad_cols(w_in[:, :RW_PROJ], GROUP_PAD),
        _pad_cols(gla_head, GLA_OFF_GOUT),
        _pad_cols(gla_gout, GROUP_PAD - GLA_OFF_GOUT),
    ], axis=1).astype(BF16)

    row = lambda x: x.reshape(1, -1).astype(F32)
    mu = _pad_cols(row(rw_mu), GROUP_PAD)
    wwa = jnp.zeros((LANES, 2 * RW_WIDTH), F32)
    wwa = wwa.at[0:RW_LORA_W, 0:RW_WIDTH].set(rw_w2)
    wwa = wwa.at[RW_LORA_W:RW_LORA_W + RW_LORA_A, RW_WIDTH:].set(rw_a2)
    g2p = jnp.pad(rw_g2, ((0, 2 * LANES - RW_LORA_G), (0, 0)))
    head_of_col = jnp.arange(RW_WIDTH) // RW_HEAD_DIM
    e1 = (head_of_col[:, None] == jnp.arange(LANES)[None, :]).astype(BF16)
    rw_wts = (mu, wwa.astype(BF16), row(rw_w0), row(rw_a0), g2p.astype(BF16), row(rw_k_k), row(rw_k_a),
              row(rw_r_k), row(rw_ln_w), row(rw_ln_b), e1, e1.T)
    gw2p = jnp.pad(gla_gw2, ((0, LANES - GLA_GATE_RANK), (0, 0))).astype(BF16)
    gla_wts = (gw2p, row(gla_gb), row(gla_norm_w))
    return w_p, rw_wts, gla_wts


RW_PIECES = 2
GLA_PIECES = 2


def _trunk(x, shift, s_rw, s_gla, norm1_g, w_p, rw_wts, gla_wts, w_out, norm2_g, w_up, w_down, norm_f_g,
           nb, tb, c_rw, c_gla, tm):
    bsz, seq, _ = x.shape
    t = bsz * seq
    x2d = x.reshape(t, D_MODEL)
    proj = _inproj(x2d, norm1_g.reshape(1, -1), w_p, tm, GROUP_PAD // 3).reshape(bsz, seq, NP)
    shift_p = _pad_cols(shift, GROUP_PAD).reshape(bsz, 1, GROUP_PAD)
    o_rw, sh_new, s_rw_new = _rwkv(proj, shift_p, s_rw, rw_wts, nb, tb, c_rw, RW_PIECES)
    o_gla, s_gla_new = _gla(proj, s_gla, gla_wts, nb, tb, c_gla, GLA_PIECES)
    x1, h2 = _outproj(x2d, o_rw.reshape(t, RW_WIDTH), o_gla.reshape(t, GLA_WIDTH), w_out,
                      norm2_g.reshape(1, -1), 512)
    y = _mlp(x1, h2, w_up, w_down, norm_f_g.reshape(1, -1), 512, 512)
    return (y.reshape(bsz, seq, D_MODEL), sh_new[:, 0, :RW_PROJ][None], s_rw_new[None], s_gla_new[None])


def kernel(x_prompt, x_sample, state_rwkv_shift, state_rwkv_wkv, state_gla, norm1_g, w_in, rw_mu, rw_w0,
           rw_w2, rw_a0, rw_a2, rw_g2, rw_k_k, rw_k_a, rw_r_k, rw_ln_w, rw_ln_b, gla_gw2, gla_gb, gla_norm_w,
           w_out, norm2_g, w_up, w_down, norm_f_g):
    w_p, rw_wts, gla_wts = _prep_weights(
        w_in[0], rw_mu[0], rw_w0[0], rw_w2[0], rw_a0[0], rw_a2[0], rw_g2[0], rw_k_k[0], rw_k_a[0],
        rw_r_k[0].reshape(-1), rw_ln_w[0], rw_ln_b[0], gla_gw2[0], gla_gb[0], gla_norm_w[0])
    shared = (norm1_g[0], w_p, rw_wts, gla_wts, w_out[0].astype(BF16), norm2_g[0], w_up[0].astype(BF16),
              w_down[0].astype(BF16), norm_f_g)

    bp, lp, _ = x_prompt.shape
    bs, ls, _ = x_sample.shape
    dt = x_prompt.dtype
    out_p = _trunk(x_prompt, jnp.zeros((bp, RW_PROJ), dt),
                   jnp.zeros((bp, RW_HEADS, RW_HEAD_DIM, RW_HEAD_DIM), dt),
                   jnp.zeros((bp, GLA_HEADS, GLA_DK, GLA_DV), dt), *shared,
                   nb=1, tb=ROWS, c_rw=32, c_gla=GLA_CHUNK, tm=1024)
    out_s = _trunk(x_sample, state_rwkv_shift[0], state_rwkv_wkv[0], state_gla[0], *shared,
                   nb=ROWS // ls, tb=ls, c_rw=ls, c_gla=ls, tm=1024)
    return (out_p[0], out_s[0], out_p[1], out_p[2], out_p[3], out_s[1], out_s[2], out_s[3])
```

```python
import functools

import jax
import jax.numpy as jnp
from jax import lax
from jax.experimental import pallas as pl
from jax.experimental.pallas import tpu as pltpu

F32 = jnp.float32
BF16 = jnp.bfloat16

D_MODEL = 2048
RW_WIDTH = 1024
RW_HEADS = 16
RW_HEAD_DIM = 64
RW_LORA_W = 64
RW_LORA_A = 64
RW_LORA_G = 160
RW_PROJ = 3 * RW_WIDTH + RW_LORA_W + RW_LORA_A + RW_LORA_G
RW_GN_EPS = 64e-5
GLA_WIDTH = 1024
GLA_HEADS = 4
GLA_KEY_WIDTH = 512
GLA_DK = 128
GLA_DV = 256
GLA_GATE_RANK = 16
GLA_GATE_NORM = 16.0
GLA_CHUNK = 64
GLA_PROJ = 2 * GLA_KEY_WIDTH + GLA_WIDTH + GLA_GATE_RANK + GLA_WIDTH
D_FF = 4 * D_MODEL
NORM_EPS = 1e-6
HEAD_NORM_EPS = 1e-5

LANES = 128
SUBLANES = 8

GROUP_PAD = 3456
RW_OFF_XWA = 3 * RW_WIDTH
RW_OFF_XG = RW_OFF_XWA + LANES
GLA_OFF_K = GLA_KEY_WIDTH
GLA_OFF_V = 2 * GLA_KEY_WIDTH
GLA_OFF_GATE = GLA_OFF_V + GLA_WIDTH
GLA_OFF_GOUT = GLA_OFF_GATE + LANES
NP = 2 * GROUP_PAD

ROWS = 128
VMEM_LIMIT = 56 * 1024 * 1024

NN = (((1,), (0,)), ((), ()))
NT = (((1,), (1,)), ((), ()))


def _dot(a, b, dims=NN):
    return lax.dot_general(a, b, dims, preferred_element_type=F32)


def _parts(x, n):
    if x.dtype == BF16:
        return [x]
    out = []
    rem = x
    for i in range(n):
        h = rem.astype(BF16)
        out.append(h)
        if i + 1 < n:
            rem = rem - h.astype(F32)
    return out


def _mm(a, b, pa=1, pb=1, dims=NN):
    pa = 1 if a.dtype == BF16 else pa
    pb = 1 if b.dtype == BF16 else pb
    aa = _parts(a, pa)
    bb = _parts(b, pb)
    n = max(pa, pb)
    acc = None
    for i in range(pa):
        for j in range(pb):
            if i + j < n:
                t = _dot(aa[i], bb[j], dims)
                acc = t if acc is None else acc + t
    return acc


def _softplus(x):
    return jnp.maximum(x, 0.0) + jnp.log1p(jnp.exp(-jnp.abs(x)))


def _sigmoid(x):
    return 1.0 / (1.0 + jnp.exp(-x))


def _chunk_masks(m, c):
    sh = c.bit_length() - 1
    ri = lax.broadcasted_iota(jnp.int32, (m, m), 0)
    ci = lax.broadcasted_iota(jnp.int32, (m, m), 1)
    same = (ri >> sh) == (ci >> sh)
    return same, ri, ci


def _inproj_kernel(x_ref, g_ref, w_ref, o_ref, h_ref):
    @pl.when(pl.program_id(1) == 0)
    def _():
        x = x_ref[...]
        ms = jnp.mean(x * x, axis=-1, keepdims=True)
        h_ref[...] = (x * lax.rsqrt(ms + NORM_EPS) * g_ref[...]).astype(BF16)

    o_ref[...] = _dot(h_ref[...], w_ref[...])


def _inproj(x2d, g, w_p, tm, tn):
    t = x2d.shape[0]
    return pl.pallas_call(
        _inproj_kernel,
        grid=(t // tm, NP // tn),
        in_specs=[
            pl.BlockSpec((tm, D_MODEL), lambda i, j: (i, 0)),
            pl.BlockSpec((1, D_MODEL), lambda i, j: (0, 0)),
            pl.BlockSpec((D_MODEL, tn), lambda i, j: (0, j)),
        ],
        out_specs=pl.BlockSpec((tm, tn), lambda i, j: (i, j)),
        out_shape=jax.ShapeDtypeStruct((t, NP), F32),
        scratch_shapes=[pltpu.VMEM((tm, D_MODEL), BF16)],
        compiler_params=pltpu.CompilerParams(
            dimension_semantics=("parallel", "arbitrary"), vmem_limit_bytes=VMEM_LIMIT),
        name="inproj",
    )(x2d, g, w_p)


def _rwkv_kernel(p_ref, sh_ref, s0_ref, mu_ref, wwa_ref, w0_ref, a0_ref, g2_ref, kk_ref, ka_ref,
                 rk_ref, lnw_ref, lnb_ref, e1_ref, e2_ref,
                 o_ref, shout_ref, st_ref,
                 carry_ref, bk_ref, kb_ref, pc_ref, art_ref, vt_ref, arb_ref, uy_ref, yt_ref,
                 *, nb, tb, c, pp):
    j = pl.program_id(1)
    m_rows = nb * tb
    n_lvl = c.bit_length() - 1
    hd = RW_HEAD_DIM

    @pl.when(j == 0)
    def _():
        st_ref[...] = s0_ref[...]
        carry_ref[...] = sh_ref[...]

    yt_ref[...] = jnp.zeros_like(yt_ref)

    p3 = p_ref[...]
    first = carry_ref[...]
    t3 = lax.broadcasted_iota(jnp.int32, p3.shape, 1)
    prev3 = jnp.where(t3 == 0, first, pltpu.roll(p3, 1, axis=1))
    last = p3[:, tb - 1:tb, :]
    carry_ref[...] = last
    shout_ref[...] = last
    m = (p3 + (prev3 - p3) * mu_ref[...]).reshape(m_rows, GROUP_PAD)

    r = m[:, 0:RW_WIDTH]
    k = m[:, RW_WIDTH:2 * RW_WIDTH]
    v = m[:, 2 * RW_WIDTH:3 * RW_WIDTH]

    slab = m[:, RW_OFF_XWA:RW_OFF_XWA + LANES]
    lane = lax.broadcasted_iota(jnp.int32, slab.shape, 1)
    lhs = jnp.where(lane < RW_LORA_W, jnp.tanh(slab), slab).astype(BF16)
    wa = _dot(lhs, wwa_ref[...])
    xg = m[:, RW_OFF_XG:RW_OFF_XG + 2 * LANES]
    g = _dot(_sigmoid(xg).astype(BF16), g2_ref[...])

    logw = -jnp.exp(-_softplus(-(w0_ref[...] + wa[:, :RW_WIDTH])) - 0.5)
    a = _sigmoid(a0_ref[...] + wa[:, RW_WIDTH:])

    def gsum(x):
        s = _mm(x, e1_ref[...], pa=2)
        return _mm(s, e2_ref[...], pa=2)

    kkf = k * kk_ref[...]
    kk = kkf / jnp.maximum(jnp.sqrt(gsum(kkf * kkf)), 1e-12)
    k2 = k * (1.0 + (a - 1.0) * ka_ref[...])
    beta = kk * a

    same, ri, ci = _chunk_masks(m_rows, c)
    lmask = jnp.where(same & (ci <= ri), 1.0, 0.0).astype(BF16)
    umask = jnp.where(same & (ci > ri), 1.0, 0.0).astype(BF16)
    gcum = _mm(lmask, logw, pb=3)
    grev = _mm(umask, logw, pb=3)
    e_inv = jnp.exp(-gcum)
    e_rev = jnp.exp(grev)
    at = -kk * jnp.exp(gcum - logw)
    rt = r * jnp.exp(gcum)
    kh = k2 * e_inv
    bh = beta * e_inv
    ke = k2 * e_rev
    be = beta * e_rev
    pc = jnp.exp(gcum + grev)

    art_ref[:, 0:m_rows] = at.T
    art_ref[:, m_rows:2 * m_rows] = rt.T
    vt_ref[...] = v.T

    def to_heads(x, dst, row0):
        xr = pltpu.roll(x, RW_WIDTH - hd, axis=1)
        for h in range(RW_HEADS):
            src = xr if h % 2 else x
            lo = (h // 2) * LANES
            dst[h, row0:row0 + m_rows, :] = src[:, lo:lo + hd]

    to_heads(bh, bk_ref, 0)
    to_heads(kh, bk_ref, m_rows)
    to_heads(ke, kb_ref, 0)
    to_heads(be, kb_ref, m_rows)
    to_heads(pc, pc_ref, 0)

    strict_t = same & (ri < ci)
    incl_t = same & (ri <= ci)

    def head_group(gi, carry):
        heads = [gi * HEAD_UNROLL + u for u in range(HEAD_UNROLL)]
        hss = [pl.multiple_of(h * hd, hd) for h in heads]
        xts = [_mm(bk_ref[h], art_ref[pl.ds(hs, hd), :], pp, pp) for h, hs in zip(heads, hss)]
        nps = [jnp.where(strict_t, xt[0:m_rows, 0:m_rows], 0.0) for xt in xts]
        vas = [_mm(vt_ref[pl.ds(hs, hd), :],
                   jnp.concatenate([jnp.where(strict_t, xt[m_rows:, 0:m_rows], 0.0),
                                    jnp.where(incl_t, xt[m_rows:, m_rows:], 0.0)], axis=1), pp, pp)
               for hs, xt in zip(hss, xts)]
        zs = [jnp.concatenate([art_ref[pl.ds(hs, hd), 0:m_rows], va[:, 0:m_rows]], axis=0)
              for hs, va in zip(hss, vas)]
        for lvl in range(n_lvl):
            if lvl + 1 < n_lvl:
                ts = [_mm(jnp.concatenate([z, npow], axis=0), npow, pp, pp) for z, npow in zip(zs, nps)]
                zs = [z + t[0:2 * hd] for z, t in zip(zs, ts)]
                nps = [t[2 * hd:] for t in ts]
            else:
                zs = [z + _mm(z, npow, pp, pp) for z, npow in zip(zs, nps)]
        rys = [jnp.concatenate([art_ref[pl.ds(hs, hd), m_rows:], va[:, m_rows:]], axis=0)
               + _mm(z, jnp.where(incl_t, xt[0:m_rows, m_rows:], 0.0), pp, pp)
               for hs, va, z, xt in zip(hss, vas, zs, xts)]
        for hs, z, ry in zip(hss, zs, rys):
            arb_ref[pl.ds(hs, hd), 0:m_rows] = z[0:hd]
            arb_ref[pl.ds(hs, hd), m_rows:] = ry[0:hd]
            uy_ref[pl.ds(hs, hd), 0:m_rows] = z[hd:]
            uy_ref[pl.ds(hs, hd), m_rows:] = ry[hd:]
        return carry

    lax.fori_loop(0, RW_HEADS // HEAD_UNROLL, head_group, 0)

    sh_c = c.bit_length() - 1
    cpb = tb // c
    row_id2 = (lax.broadcasted_iota(jnp.int32, (2 * m_rows, hd), 0) & (m_rows - 1)) >> sh_c
    col_id = lax.broadcasted_iota(jnp.int32, (hd, m_rows), 1) >> sh_c

    def chunk_body(i, carry):
        b = i // cpb
        r0 = pl.multiple_of(i * c, c)
        rmask2 = row_id2 == i
        cmask = col_id == i

        def head_step(h, carry2):
            hs = pl.multiple_of(h * hd, hd)
            s = st_ref[b, h]
            zs = _mm(s, arb_ref[pl.ds(hs, hd), :], pp, pp) + uy_ref[pl.ds(hs, hd), :]
            ut = zs[:, 0:m_rows]
            yt_ref[pl.ds(hs, hd), :] = jnp.where(cmask, zs[:, m_rows:], yt_ref[pl.ds(hs, hd), :])
            kbz = jnp.where(rmask2, kb_ref[h], 0.0)
            vu = jnp.concatenate([vt_ref[pl.ds(hs, hd), :], ut], axis=1)
            st_ref[b, h] = s * pc_ref[h, pl.ds(r0, 1), :] + _mm(vu, kbz, pp, pp)
            return carry2

        lax.fori_loop(0, RW_HEADS, head_step, 0, unroll=STEP_UNROLL)
        return carry

    lax.fori_loop(0, m_rows // c, chunk_body, 0)

    y = yt_ref[...].T
    inv_n = 1.0 / hd
    mean = gsum(y) * inv_n
    yc = y - mean
    var = gsum(yc * yc) * inv_n
    yn = yc * lax.rsqrt(var + RW_GN_EPS) * lnw_ref[...] + lnb_ref[...]
    bonus = gsum(r * k2 * rk_ref[...]) * v
    o_ref[...] = ((yn + bonus) * g).reshape(nb, tb, RW_WIDTH)


def _rwkv(p, shift_prev, s0, wts, nb, tb, c, pp):
    bsz, seq, _ = p.shape
    m_rows = nb * tb
    hd = RW_HEAD_DIM
    const = lambda shape: pl.BlockSpec(shape, lambda i, j: (0,) * len(shape))
    kern = functools.partial(_rwkv_kernel, nb=nb, tb=tb, c=c, pp=pp)
    return pl.pallas_call(
        kern,
        grid=(bsz // nb, seq // tb),
        in_specs=[
            pl.BlockSpec((nb, tb, GROUP_PAD), lambda i, j: (i, j, 0)),
            pl.BlockSpec((nb, 1, GROUP_PAD), lambda i, j: (i, 0, 0)),
            pl.BlockSpec((nb, RW_HEADS, hd, hd), lambda i, j: (i, 0, 0, 0)),
            const((1, GROUP_PAD)),
            const((LANES, 2 * RW_WIDTH)),
            const((1, RW_WIDTH)),
            const((1, RW_WIDTH)),
            const((2 * LANES, RW_WIDTH)),
            const((1, RW_WIDTH)),
            const((1, RW_WIDTH)),
            const((1, RW_WIDTH)),
            const((1, RW_WIDTH)),
            const((1, RW_WIDTH)),
            const((RW_WIDTH, LANES)),
            const((LANES, RW_WIDTH)),
        ],
        out_specs=[
            pl.BlockSpec((nb, tb, RW_WIDTH), lambda i, j: (i, j, 0)),
            pl.BlockSpec((nb, 1, GROUP_PAD), lambda i, j: (i, 0, 0)),
            pl.BlockSpec((nb, RW_HEADS, hd, hd), lambda i, j: (i, 0, 0, 0)),
        ],
        out_shape=[
            jax.ShapeDtypeStruct((bsz, seq, RW_WIDTH), F32),
            jax.ShapeDtypeStruct((bsz, 1, GROUP_PAD), F32),
            jax.ShapeDtypeStruct((bsz, RW_HEADS, hd, hd), F32),
        ],
        scratch_shapes=[
            pltpu.VMEM((nb, 1, GROUP_PAD), F32),
            pltpu.VMEM((RW_HEADS, 2 * m_rows, hd), F32),
            pltpu.VMEM((RW_HEADS, 2 * m_rows, hd), F32),
            pltpu.VMEM((RW_HEADS, m_rows, hd), F32),
            pltpu.VMEM((RW_WIDTH, 2 * m_rows), F32),
            pltpu.VMEM((RW_WIDTH, m_rows), F32),
            pltpu.VMEM((RW_WIDTH, 2 * m_rows), F32),
            pltpu.VMEM((RW_WIDTH, 2 * m_rows), F32),
            pltpu.VMEM((RW_WIDTH, m_rows), F32),
        ],
        compiler_params=pltpu.CompilerParams(
            dimension_semantics=("parallel", "arbitrary"), vmem_limit_bytes=VMEM_LIMIT),
        name="rwkv7",
    )(p, shift_prev, s0, *wts)


def _gla_kernel(p_ref, s0_ref, gw2_ref, gb_ref, nw_ref, o_ref, st_ref,
                acc_ref, qd_ref, v_ref, kt_ref, et_ref, *, nb, tb, c, pp):
    j = pl.program_id(1)
    m_rows = nb * tb
    dk, dv = GLA_DK, GLA_DV

    @pl.when(j == 0)
    def _():
        st_ref[...] = s0_ref[...]

    p = p_ref[...].reshape(m_rows, GROUP_PAD)
    q = p[:, 0:GLA_KEY_WIDTH] * (dk ** -0.5)
    k = p[:, GLA_OFF_K:GLA_OFF_K + GLA_KEY_WIDTH]
    v = p[:, GLA_OFF_V:GLA_OFF_V + GLA_WIDTH]
    xs = p[:, GLA_OFF_GATE:GLA_OFF_GATE + LANES].astype(BF16)
    gout = p[:, GLA_OFF_GOUT:GLA_OFF_GOUT + GLA_WIDTH]
    gk = -_softplus(-(_dot(xs, gw2_ref[...]) + gb_ref[...])) / GLA_GATE_NORM

    same, ri, ci = _chunk_masks(m_rows, c)
    lmask = jnp.where(same & (ci <= ri), 1.0, 0.0).astype(BF16)
    umask = jnp.where(same & (ci > ri), 1.0, 0.0).astype(BF16)
    causal = same & (ci <= ri)
    gcum = _mm(lmask, gk, pb=3)
    grev = _mm(umask, gk, pb=3)
    qd = q * jnp.exp(gcum)
    kinv = k * jnp.exp(-gcum)
    qd_ref[...] = qd
    v_ref[...] = v
    kt_ref[...] = (k * jnp.exp(grev)).T
    et_ref[...] = jnp.exp(gcum + grev).T

    for h in range(GLA_HEADS):
        a = _mm(qd[:, h * dk:(h + 1) * dk], kinv[:, h * dk:(h + 1) * dk], pp, pp, NT)
        a = jnp.where(causal, a, 0.0)
        acc_ref[:, h * dv:(h + 1) * dv] = _mm(a, v[:, h * dv:(h + 1) * dv], pp, pp)

    sh_c = c.bit_length() - 1
    cpb = tb // c
    row_id = lax.broadcasted_iota(jnp.int32, (m_rows, dv), 0) >> sh_c
    for i in range(m_rows // c):
        b = i // cpb
        r0 = i * c
        rmask = row_id == i
        for h in range(GLA_HEADS):
            s = st_ref[b, h]
            acc_ref[r0:r0 + c, h * dv:(h + 1) * dv] += _mm(
                qd_ref[r0:r0 + c, h * dk:(h + 1) * dk], s, pp, pp)
            vz = jnp.where(rmask, v_ref[:, h * dv:(h + 1) * dv], 0.0)
            st_ref[b, h] = (s * et_ref[h * dk:(h + 1) * dk, r0:r0 + 1]
                            + _mm(kt_ref[h * dk:(h + 1) * dk, :], vz, pp, pp))

    for h in range(GLA_HEADS):
        o = acc_ref[:, h * dv:(h + 1) * dv]
        on = o * lax.rsqrt(jnp.mean(o * o, axis=-1, keepdims=True) + HEAD_NORM_EPS) * nw_ref[...]
        gh = gout[:, h * dv:(h + 1) * dv]
        o_ref[:, :, h * dv:(h + 1) * dv] = (on * (gh * _sigmoid(gh))).reshape(nb, tb, dv)


def _gla(p, s0, wts, nb, tb, c, pp):
    bsz, seq, _ = p.shape
    m_rows = nb * tb
    const = lambda shape: pl.BlockSpec(shape, lambda i, j: (0,) * len(shape))
    kern = functools.partial(_gla_kernel, nb=nb, tb=tb, c=c, pp=pp)
    return pl.pallas_call(
        kern,
        grid=(bsz // nb, seq // tb),
        in_specs=[
            pl.BlockSpec((nb, tb, GROUP_PAD), lambda i, j: (i, j, 1)),
            pl.BlockSpec((nb, GLA_HEADS, GLA_DK, GLA_DV), lambda i, j: (i, 0, 0, 0)),
            const((LANES, GLA_KEY_WIDTH)),
            const((1, GLA_KEY_WIDTH)),
            const((1, GLA_DV)),
        ],
        out_specs=[
            pl.BlockSpec((nb, tb, GLA_WIDTH), lambda i, j: (i, j, 0)),
            pl.BlockSpec((nb, GLA_HEADS, GLA_DK, GLA_DV), lambda i, j: (i, 0, 0, 0)),
        ],
        out_shape=[
            jax.ShapeDtypeStruct((bsz, seq, GLA_WIDTH), F32),
            jax.ShapeDtypeStruct((bsz, GLA_HEADS, GLA_DK, GLA_DV), F32),
        ],
        scratch_shapes=[
            pltpu.VMEM((m_rows, GLA_WIDTH), F32),
            pltpu.VMEM((m_rows, GLA_KEY_WIDTH), F32),
            pltpu.VMEM((m_rows, GLA_WIDTH), F32),
            pltpu.VMEM((GLA_KEY_WIDTH, m_rows), F32),
            pltpu.VMEM((GLA_KEY_WIDTH, m_rows), F32),
        ],
        compiler_params=pltpu.CompilerParams(
            dimension_semantics=("parallel", "arbitrary"), vmem_limit_bytes=VMEM_LIMIT),
        name="gla",
    )(p, s0, *wts)


def _outproj_kernel(x_ref, orw_ref, ogla_ref, w_ref, g_ref, x1_ref, h_ref):
    x1 = (x_ref[...]
          + _dot(orw_ref[...].astype(BF16), w_ref[0:RW_WIDTH, :])
          + _dot(ogla_ref[...].astype(BF16), w_ref[RW_WIDTH:, :]))
    x1_ref[...] = x1
    ms = jnp.mean(x1 * x1, axis=-1, keepdims=True)
    h_ref[...] = (x1 * lax.rsqrt(ms + NORM_EPS) * g_ref[...]).astype(BF16)


def _outproj(x2d, o_rw, o_gla, w_out, g2, tm):
    t = x2d.shape[0]
    return pl.pallas_call(
        _outproj_kernel,
        grid=(t // tm,),
        in_specs=[
            pl.BlockSpec((tm, D_MODEL), lambda i: (i, 0)),
            pl.BlockSpec((tm, RW_WIDTH), lambda i: (i, 0)),
            pl.BlockSpec((tm, GLA_WIDTH), lambda i: (i, 0)),
            pl.BlockSpec((D_MODEL, D_MODEL), lambda i: (0, 0)),
            pl.BlockSpec((1, D_MODEL), lambda i: (0, 0)),
        ],
        out_specs=[
            pl.BlockSpec((tm, D_MODEL), lambda i: (i, 0)),
            pl.BlockSpec((tm, D_MODEL), lambda i: (i, 0)),
        ],
        out_shape=[
            jax.ShapeDtypeStruct((t, D_MODEL), F32),
            jax.ShapeDtypeStruct((t, D_MODEL), BF16),
        ],
        compiler_params=pltpu.CompilerParams(
            dimension_semantics=("parallel",), vmem_limit_bytes=VMEM_LIMIT),
        name="outproj",
    )(x2d, o_rw, o_gla, w_out, g2)


def _mlp_kernel(x1_ref, h_ref, wu_ref, wd_ref, g_ref, o_ref):
    jf = pl.program_id(1)

    @pl.when(jf == 0)
    def _():
        o_ref[...] = x1_ref[...]

    u = jnp.maximum(_dot(h_ref[...], wu_ref[...]), 0.0)
    o_ref[...] += _dot((u * u).astype(BF16), wd_ref[...])

    @pl.when(jf == pl.num_programs(1) - 1)
    def _():
        x2 = o_ref[...]
        ms = jnp.mean(x2 * x2, axis=-1, keepdims=True)
        o_ref[...] = x2 * lax.rsqrt(ms + NORM_EPS) * g_ref[...]


def _mlp(x1, h2, w_up, w_down, gf, tm, tf):
    t = x1.shape[0]
    return pl.pallas_call(
        _mlp_kernel,
        grid=(t // tm, D_FF // tf),
        in_specs=[
            pl.BlockSpec((tm, D_MODEL), lambda i, j: (i, 0)),
            pl.BlockSpec((tm, D_MODEL), lambda i, j: (i, 0)),
            pl.BlockSpec((D_MODEL, tf), lambda i, j: (0, j)),
            pl.BlockSpec((tf, D_MODEL), lambda i, j: (j, 0)),
            pl.BlockSpec((1, D_MODEL), lambda i, j: (0, 0)),
        ],
        out_specs=pl.BlockSpec((tm, D_MODEL), lambda i, j: (i, 0)),
        out_shape=jax.ShapeDtypeStruct((t, D_MODEL), F32),
        compiler_params=pltpu.CompilerParams(
            dimension_semantics=("parallel", "arbitrary"), vmem_limit_bytes=VMEM_LIMIT),
        name="mlp",
    )(x1, h2, w_up, w_down, gf)


def _pad_cols(w, n):
    return jnp.pad(w, ((0, 0), (0, n - w.shape[1])))


def _prep_weights(w_in, rw_mu, rw_w0, rw_w2, rw_a0, rw_a2, rw_g2, rw_k_k, rw_k_a, rw_r_k, rw_ln_w, rw_ln_b,
                  gla_gw2, gla_gb, gla_norm_w):
    go = RW_PROJ
    gla_head = w_in[:, go:go + GLA_OFF_GATE + GLA_GATE_RANK]
    gla_gout = w_in[:, go + GLA_OFF_GATE + GLA_GATE_RANK:]
    w_p = jnp.concatenate([
        _pad_cols(w_in[:, :RW_PROJ], GROUP_PAD),
        _pad_cols(gla_head, GLA_OFF_GOUT),
        _pad_cols(gla_gout, GROUP_PAD - GLA_OFF_GOUT),
    ], axis=1).astype(BF16)

    row = lambda x: x.reshape(1, -1).astype(F32)
    mu = _pad_cols(row(rw_mu), GROUP_PAD)
    wwa = jnp.zeros((LANES, 2 * RW_WIDTH), F32)
    wwa = wwa.at[0:RW_LORA_W, 0:RW_WIDTH].set(rw_w2)
    wwa = wwa.at[RW_LORA_W:RW_LORA_W + RW_LORA_A, RW_WIDTH:].set(rw_a2)
    g2p = jnp.pad(rw_g2, ((0, 2 * LANES - RW_LORA_G), (0, 0)))
    head_of_col = jnp.arange(RW_WIDTH) // RW_HEAD_DIM
    e1 = (head_of_col[:, None] == jnp.arange(LANES)[None, :]).astype(BF16)
    rw_wts = (mu, wwa.astype(BF16), row(rw_w0), row(rw_a0), g2p.astype(BF16), row(rw_k_k), row(rw_k_a),
              row(rw_r_k), row(rw_ln_w), row(rw_ln_b), e1, e1.T)
    gw2p = jnp.pad(gla_gw2, ((0, LANES - GLA_GATE_RANK), (0, 0))).astype(BF16)
    gla_wts = (gw2p, row(gla_gb), row(gla_norm_w))
    return w_p, rw_wts, gla_wts


RW_PIECES = 1
GLA_PIECES = 1
HEAD_UNROLL = 8
STEP_UNROLL = 16


def _trunk(x, shift, s_rw, s_gla, norm1_g, w_p, rw_wts, gla_wts, w_out, norm2_g, w_up, w_down, norm_f_g,
           nb, tb, c_rw, c_gla, tm):
    bsz, seq, _ = x.shape
    t = bsz * seq
    x2d = x.reshape(t, D_MODEL)
    proj = _inproj(x2d, norm1_g.reshape(1, -1), w_p, tm, GROUP_PAD // 3).reshape(bsz, seq, NP)
    shift_p = _pad_cols(shift, GROUP_PAD).reshape(bsz, 1, GROUP_PAD)
    o_rw, sh_new, s_rw_new = _rwkv(proj, shift_p, s_rw, rw_wts, nb, tb, c_rw, RW_PIECES)
    o_gla, s_gla_new = _gla(proj, s_gla, gla_wts, nb, tb, c_gla, GLA_PIECES)
    x1, h2 = _outproj(x2d, o_rw.reshape(t, RW_WIDTH), o_gla.reshape(t, GLA_WIDTH), w_out,
                      norm2_g.reshape(1, -1), 512)
    y = _mlp(x1, h2, w_up, w_down, norm_f_g.reshape(1, -1), 512, 512)
    return (y.reshape(bsz, seq, D_MODEL), sh_new[:, 0, :RW_PROJ][None], s_rw_new[None], s_gla_new[None])


def kernel(x_prompt, x_sample, state_rwkv_shift, state_rwkv_wkv, state_gla, norm1_g, w_in, rw_mu, rw_w0,
           rw_w2, rw_a0, rw_a2, rw_g2, rw_k_k, rw_k_a, rw_r_k, rw_ln_w, rw_ln_b, gla_gw2, gla_gb, gla_norm_w,
           w_out, norm2_g, w_up, w_down, norm_f_g):
    w_p, rw_wts, gla_wts = _prep_weights(
        w_in[0], rw_mu[0], rw_w0[0], rw_w2[0], rw_a0[0], rw_a2[0], rw_g2[0], rw_k_k[0], rw_k_a[0],
        rw_r_k[0].reshape(-1), rw_ln_w[0], rw_ln_b[0], gla_gw2[0], gla_gb[0], gla_norm_w[0])
    shared = (norm1_g[0], w_p, rw_wts, gla_wts, w_out[0].astype(BF16), norm2_g[0], w_up[0].astype(BF16),
              w_down[0].astype(BF16), norm_f_g)

    bp, lp, _ = x_prompt.shape
    bs, ls, _ = x_sample.shape
    dt = x_prompt.dtype
    out_p = _trunk(x_prompt, jnp.zeros((bp, RW_PROJ), dt),
                   jnp.zeros((bp, RW_HEADS, RW_HEAD_DIM, RW_HEAD_DIM), dt),
                   jnp.zeros((bp, GLA_HEADS, GLA_DK, GLA_DV), dt), *shared,
                   nb=1, tb=ROWS, c_rw=32, c_gla=GLA_CHUNK, tm=1024)
    out_s = _trunk(x_sample, state_rwkv_shift[0], state_rwkv_wkv[0], state_gla[0], *shared,
                   nb=ROWS // ls, tb=ls, c_rw=ls, c_gla=ls, tm=1024)
    return (out_p[0], out_s[0], out_p[1], out_p[2], out_p[3], out_s[1], out_s[2], out_s[3])
```

```python
import functools

import jax
import jax.numpy as jnp
from jax import lax
from jax.experimental import pallas as pl
from jax.experimental.pallas import tpu as pltpu

F32 = jnp.float32
BF16 = jnp.bfloat16

D_MODEL = 2048
RW_WIDTH = 1024
RW_HEADS = 16
RW_HEAD_DIM = 64
RW_LORA_W = 64
RW_LORA_A = 64
RW_LORA_G = 160
RW_PROJ = 3 * RW_WIDTH + RW_LORA_W + RW_LORA_A + RW_LORA_G
RW_GN_EPS = 64e-5
GLA_WIDTH = 1024
GLA_HEADS = 4
GLA_KEY_WIDTH = 512
GLA_DK = 128
GLA_DV = 256
GLA_GATE_RANK = 16
GLA_GATE_NORM = 16.0
GLA_CHUNK = 64
GLA_PROJ = 2 * GLA_KEY_WIDTH + GLA_WIDTH + GLA_GATE_RANK + GLA_WIDTH
D_FF = 4 * D_MODEL
NORM_EPS = 1e-6
HEAD_NORM_EPS = 1e-5
DECAY_SCALE = 0.6065306597126334

LANES = 128
SUBLANES = 8

GROUP_PAD = 3456
RW_OFF_XWA = 3 * RW_WIDTH
RW_OFF_XG = RW_OFF_XWA + LANES
GLA_OFF_K = GLA_KEY_WIDTH
GLA_OFF_V = 2 * GLA_KEY_WIDTH
GLA_OFF_GATE = GLA_OFF_V + GLA_WIDTH
GLA_OFF_GOUT = GLA_OFF_GATE + LANES
NP = 2 * GROUP_PAD

ROWS = 128
VMEM_LIMIT = 56 * 1024 * 1024

NN = (((1,), (0,)), ((), ()))
NT = (((1,), (1,)), ((), ()))


def _dot(a, b, dims=NN):
    return lax.dot_general(a, b, dims, preferred_element_type=F32)


def _parts(x, n):
    if x.dtype == BF16:
        return [x]
    out = []
    rem = x
    for i in range(n):
        h = rem.astype(BF16)
        out.append(h)
        if i + 1 < n:
            rem = rem - h.astype(F32)
    return out


def _mm(a, b, pa=1, pb=1, dims=NN):
    pa = 1 if a.dtype == BF16 else pa
    pb = 1 if b.dtype == BF16 else pb
    aa = _parts(a, pa)
    bb = _parts(b, pb)
    n = max(pa, pb)
    acc = None
    for i in range(pa):
        for j in range(pb):
            if i + j < n:
                t = _dot(aa[i], bb[j], dims)
                acc = t if acc is None else acc + t
    return acc


def _softplus(x):
    return jnp.maximum(x, 0.0) + jnp.log(1.0 + jnp.exp(-jnp.abs(x)))


def _sigmoid(x):
    return 0.5 * jnp.tanh(0.5 * x) + 0.5


def _chunk_masks(m, c):
    sh = c.bit_length() - 1
    ri = lax.broadcasted_iota(jnp.int32, (m, m), 0)
    ci = lax.broadcasted_iota(jnp.int32, (m, m), 1)
    same = (ri >> sh) == (ci >> sh)
    return same, ri, ci


def _inproj_kernel(x_ref, g_ref, w_ref, o_ref, h_ref):
    @pl.when(pl.program_id(1) == 0)
    def _():
        x = x_ref[...]
        ms = jnp.mean(x * x, axis=-1, keepdims=True)
        h_ref[...] = (x * lax.rsqrt(ms + NORM_EPS) * g_ref[...]).astype(BF16)

    o_ref[...] = _dot(h_ref[...], w_ref[...])


def _inproj(x2d, g, w_p, tm, tn):
    t = x2d.shape[0]
    return pl.pallas_call(
        _inproj_kernel,
        grid=(t // tm, NP // tn),
        in_specs=[
            pl.BlockSpec((tm, D_MODEL), lambda i, j: (i, 0)),
            pl.BlockSpec((1, D_MODEL), lambda i, j: (0, 0)),
            pl.BlockSpec((D_MODEL, tn), lambda i, j: (0, j)),
        ],
        out_specs=pl.BlockSpec((tm, tn), lambda i, j: (i, j)),
        out_shape=jax.ShapeDtypeStruct((t, NP), F32),
        scratch_shapes=[pltpu.VMEM((tm, D_MODEL), BF16)],
        compiler_params=pltpu.CompilerParams(
            dimension_semantics=("parallel", "arbitrary"), vmem_limit_bytes=VMEM_LIMIT),
        name="inproj",
    )(x2d, g, w_p)


def _rwkv_kernel(p_ref, sh_ref, s0_ref, mu_ref, wwa_ref, w0_ref, a0_ref, g2_ref, kk_ref, ka_ref,
                 rk_ref, lnw_ref, lnb_ref, e1_ref, e2_ref,
                 o_ref, shout_ref, sout_ref,
                 carry_ref, st_ref, bk_ref, kb_ref, pc_ref, art_ref, vt_ref, arbd_ref, uy_ref, yt_ref,
                 g_ref, bonus_ref, *, nb, tb, c, pp):
    j = pl.program_id(1)
    m_rows = nb * tb
    n_lvl = c.bit_length() - 1
    hd = RW_HEAD_DIM

    @pl.when(j == 0)
    def _():
        for p in range(RW_HEADS // 2):
            st_ref[:, p] = jnp.concatenate([s0_ref[:, 2 * p], s0_ref[:, 2 * p + 1]], axis=-1)
        carry_ref[...] = sh_ref[...]

    yt_ref[...] = jnp.zeros_like(yt_ref)

    def shifted(c0, w):
        pc3 = p_ref[:, :, c0:c0 + w]
        t3 = lax.broadcasted_iota(jnp.int32, pc3.shape, 1)
        prev3 = jnp.where(t3 == 0, carry_ref[:, :, c0:c0 + w], pltpu.roll(pc3, 1, axis=1))
        return (pc3 + (prev3 - pc3) * mu_ref[:, c0:c0 + w]).reshape(m_rows, w)

    slab = shifted(RW_OFF_XWA, LANES)
    lane = lax.broadcasted_iota(jnp.int32, slab.shape, 1)
    lhs = jnp.where(lane < RW_LORA_W, jnp.tanh(slab), slab).astype(BF16)
    sg = _sigmoid(shifted(RW_OFF_XG, 2 * LANES)).astype(BF16)

    same, ri, ci = _chunk_masks(m_rows, c)
    lmask = jnp.where(same & (ci <= ri), 1.0, 0.0).astype(BF16)
    umask = jnp.where(same & (ci > ri), 1.0, 0.0).astype(BF16)
    strict_t = same & (ri < ci)
    incl_t = same & (ri <= ci)

    sw = STRIP
    strips = [slice(c0, c0 + sw) for c0 in range(0, RW_WIDTH, sw)]

    def head_stat(x, cs):
        return _dot(x.astype(BF16), e1_ref[cs, :])

    def head_bcast(s, cs):
        return _mm(s, e2_ref[:, cs], pa=2)

    lws = [_dot(lhs, wwa_ref[:, cs]) for cs in strips]
    las = [_dot(lhs, wwa_ref[:, RW_WIDTH + cs.start:RW_WIDTH + cs.stop]) for cs in strips]
    for cs in strips:
        g_ref[:, cs] = _dot(sg, g2_ref[:, cs])
    logws = [-DECAY_SCALE * _sigmoid(w0_ref[:, cs] + lw) for cs, lw in zip(strips, lws)]
    gcums = [_mm(lmask, logw, pb=2) for logw in logws]
    grevs = [_mm(umask, logw, pb=2) for logw in logws]
    avs = [_sigmoid(a0_ref[:, cs] + la) for cs, la in zip(strips, las)]
    ks = [shifted(RW_WIDTH + cs.start, sw) for cs in strips]
    kkfs = [k * kk_ref[:, cs] for cs, k in zip(strips, ks)]
    rinvs = [lax.rsqrt(jnp.maximum(head_stat(kkf * kkf, cs), 1e-24)) for cs, kkf in zip(strips, kkfs)]
    k2s = [k * (1.0 + (a - 1.0) * ka_ref[:, cs]) for cs, k, a in zip(strips, ks, avs)]
    rs = [shifted(cs.start, sw) for cs in strips]
    bstats = [head_stat(r * k2 * rk_ref[:, cs], cs) for cs, r, k2 in zip(strips, rs, k2s)]
    kks = [kkf * head_bcast(rinv, cs) for cs, kkf, rinv in zip(strips, kkfs, rinvs)]
    for cs, bstat in zip(strips, bstats):
        v = shifted(2 * RW_WIDTH + cs.start, sw)
        bonus_ref[:, cs] = head_bcast(bstat, cs) * v
        vt_ref[cs, :] = v.T
    for cs, logw, gcum, grev, a, kk, k2, r in zip(strips, logws, gcums, grevs, avs, kks, k2s, rs):
        beta = kk * a
        e_inv = jnp.exp(-gcum)
        e_rev = jnp.exp(grev)
        art_ref[cs, 0:m_rows] = (-kk * jnp.exp(gcum - logw)).T
        art_ref[cs, m_rows:2 * m_rows] = (r * jnp.exp(gcum)).T
        bk_ref[0:m_rows, cs] = beta * e_inv
        bk_ref[m_rows:, cs] = k2 * e_inv
        kb_ref[0:m_rows, cs] = k2 * e_rev
        kb_ref[m_rows:, cs] = beta * e_rev
        pc_ref[:, cs] = jnp.exp(gcum + grev)

    last = p_ref[:, tb - 1:tb, :]
    carry_ref[...] = last
    shout_ref[...] = last

    pw = 2 * hd
    cat = jnp.concatenate
    top = lax.broadcasted_iota(jnp.int32, (pw, 1), 0) < hd
    left = lax.broadcasted_iota(jnp.int32, (1, 2 * m_rows), 1) < m_rows

    def split_rows(x):
        return cat([jnp.where(top, x, 0.0), jnp.where(top, 0.0, x)], axis=1)

    def block_diag(xc):
        return cat([jnp.where(left, xc, 0.0), jnp.where(left, 0.0, xc)], axis=0)

    def blk(xt, u, rh, ch):
        col = (2 * u + ch) * m_rows
        return xt[rh * m_rows:(rh + 1) * m_rows, col:col + m_rows]

    def pair_group(pairs):
        los = [p * pw for p in pairs]
        arts = [art_ref[lo:lo + pw, :] for lo in los]
        xts = [_mm(bk_ref[:, lo:lo + pw], split_rows(art), pp, pp) for lo, art in zip(los, arts)]
        npcs = [cat([jnp.where(strict_t, blk(xt, 0, 0, 0), 0.0), jnp.where(strict_t, blk(xt, 1, 0, 0), 0.0)], axis=1)
                for xt in xts]
        vas = [_mm(split_rows(vt_ref[lo:lo + pw, :]),
                   cat([cat([jnp.where(strict_t, blk(xt, u, 1, 0), 0.0), jnp.where(incl_t, blk(xt, u, 1, 1), 0.0)],
                            axis=1) for u in range(2)], axis=0), pp, pp)
               for lo, xt in zip(los, xts)]
        zcs = [cat([cat([art[u * hd:(u + 1) * hd, 0:m_rows], va[u * hd:(u + 1) * hd, 0:m_rows]], axis=0)
                    for u in range(2)], axis=1) for art, va in zip(arts, vas)]
        for lvl in range(n_lvl):
            bds = [block_diag(npc) for npc in npcs]
            if lvl + 1 < n_lvl:
                ts = [_mm(cat([zc, npc], axis=0), bd, pp, pp) for zc, npc, bd in zip(zcs, npcs, bds)]
                zcs = [zc + t[0:2 * hd] for zc, t in zip(zcs, ts)]
                npcs = [t[2 * hd:] for t in ts]
            else:
                zcs = [zc + _mm(zc, bd, pp, pp) for zc, bd in zip(zcs, bds)]
        rycs = [cat([cat([art[u * hd:(u + 1) * hd, m_rows:], va[u * hd:(u + 1) * hd, m_rows:]], axis=0)
                     for u in range(2)], axis=1)
                + _mm(zc, block_diag(cat([jnp.where(incl_t, blk(xt, 0, 0, 1), 0.0),
                                          jnp.where(incl_t, blk(xt, 1, 0, 1), 0.0)], axis=1)), pp, pp)
                for art, va, zc, xt in zip(arts, vas, zcs, xts)]
        zero = jnp.zeros((hd, 2 * m_rows), BF16)
        for p, zc, ry in zip(pairs, zcs, rycs):
            arbd_ref[p, 0:hd, 0:2 * m_rows] = cat([zc[0:hd, 0:m_rows], ry[0:hd, 0:m_rows]], axis=1).astype(BF16)
            arbd_ref[p, 0:hd, 2 * m_rows:] = zero
            arbd_ref[p, hd:, 0:2 * m_rows] = zero
            arbd_ref[p, hd:, 2 * m_rows:] = cat([zc[0:hd, m_rows:], ry[0:hd, m_rows:]], axis=1).astype(BF16)
            uy_ref[p] = cat([zc[hd:, 0:m_rows], ry[hd:, 0:m_rows], zc[hd:, m_rows:], ry[hd:, m_rows:]], axis=1)

    n_pairs = RW_HEADS // 2
    for g0 in range(0, n_pairs, PAIR_GROUP):
        pair_group(list(range(g0, g0 + PAIR_GROUP)))

    sh_c = c.bit_length() - 1
    cpb = tb // c
    row_id2 = (lax.broadcasted_iota(jnp.int32, (2 * m_rows, pw), 0) & (m_rows - 1)) >> sh_c
    col_id = lax.broadcasted_iota(jnp.int32, (hd, m_rows), 1) >> sh_c
    lane_lo = lax.broadcasted_iota(jnp.int32, (1, pw), 1) < hd

    def chunk_body(i, carry):
        b = i // cpb
        r0 = pl.multiple_of(i * c, c)
        rmask2 = row_id2 == i
        cmask = col_id == i
        prs = range(n_pairs)
        ss = [st_ref[b, p] for p in prs]
        zss = [_mm(s, arbd_ref[p], pp, pp) + uy_ref[p] for p, s in zip(prs, ss)]
        upd = []
        for p, zs in zip(prs, zss):
            lo = p * pw
            kbz = jnp.where(rmask2, kb_ref[:, lo:lo + pw], 0.0)
            rhs = cat([jnp.where(lane_lo, kbz, 0.0), jnp.where(lane_lo, 0.0, kbz)], axis=0)
            vu = cat([vt_ref[lo:lo + hd, :], zs[:, 0:m_rows],
                      vt_ref[lo + hd:lo + pw, :], zs[:, 2 * m_rows:3 * m_rows]], axis=1)
            upd.append(_mm(vu, rhs, pp, pp))
        for p, s, zs, d in zip(prs, ss, zss, upd):
            lo = p * pw
            st_ref[b, p] = s * pc_ref[pl.ds(r0, 1), lo:lo + pw] + d
            for u in range(2):
                rows = slice((2 * p + u) * hd, (2 * p + u + 1) * hd)
                yt_ref[rows, :] = jnp.where(cmask, zs[:, (2 * u + 1) * m_rows:(2 * u + 2) * m_rows], yt_ref[rows, :])
        return carry

    lax.fori_loop(0, m_rows // c, chunk_body, 0)

    @pl.when(j == pl.num_programs(1) - 1)
    def _():
        for p in range(n_pairs):
            sp = st_ref[:, p]
            sout_ref[:, 2 * p] = sp[:, :, 0:hd]
            sout_ref[:, 2 * p + 1] = sp[:, :, hd:]

    inv_n = 1.0 / hd
    ys = [yt_ref[cs, :].T for cs in strips]
    means = [head_stat(y, cs) * inv_n for cs, y in zip(strips, ys)]
    ycs = [y - head_bcast(mean, cs) for cs, y, mean in zip(strips, ys, means)]
    rstds = [lax.rsqrt(head_stat(yc * yc, cs) * inv_n + RW_GN_EPS) for cs, yc in zip(strips, ycs)]
    for cs, yc, rstd in zip(strips, ycs, rstds):
        yn = yc * head_bcast(rstd, cs) * lnw_ref[:, cs] + lnb_ref[:, cs]
        o_ref[:, :, cs] = ((yn + bonus_ref[:, cs]) * g_ref[:, cs]).reshape(nb, tb, sw)


def _rwkv(p, shift_prev, s0, wts, nb, tb, c, pp):
    bsz, seq, _ = p.shape
    m_rows = nb * tb
    hd = RW_HEAD_DIM
    const = lambda shape: pl.BlockSpec(shape, lambda i, j: (0,) * len(shape))
    kern = functools.partial(_rwkv_kernel, nb=nb, tb=tb, c=c, pp=pp)
    return pl.pallas_call(
        kern,
        grid=(bsz // nb, seq // tb),
        in_specs=[
            pl.BlockSpec((nb, tb, GROUP_PAD), lambda i, j: (i, j, 0)),
            pl.BlockSpec((nb, 1, GROUP_PAD), lambda i, j: (i, 0, 0)),
            pl.BlockSpec((nb, RW_HEADS, hd, hd), lambda i, j: (i, 0, 0, 0)),
            const((1, GROUP_PAD)),
            const((LANES, 2 * RW_WIDTH)),
            const((1, RW_WIDTH)),
            const((1, RW_WIDTH)),
            const((2 * LANES, RW_WIDTH)),
            const((1, RW_WIDTH)),
            const((1, RW_WIDTH)),
            const((1, RW_WIDTH)),
            const((1, RW_WIDTH)),
            const((1, RW_WIDTH)),
            const((RW_WIDTH, LANES)),
            const((LANES, RW_WIDTH)),
        ],
        out_specs=[
            pl.BlockSpec((nb, tb, RW_WIDTH), lambda i, j: (i, j, 0)),
            pl.BlockSpec((nb, 1, GROUP_PAD), lambda i, j: (i, 0, 0)),
            pl.BlockSpec((nb, RW_HEADS, hd, hd), lambda i, j: (i, 0, 0, 0)),
        ],
        out_shape=[
            jax.ShapeDtypeStruct((bsz, seq, RW_WIDTH), F32),
            jax.ShapeDtypeStruct((bsz, 1, GROUP_PAD), F32),
            jax.ShapeDtypeStruct((bsz, RW_HEADS, hd, hd), F32),
        ],
        scratch_shapes=[
            pltpu.VMEM((nb, 1, GROUP_PAD), F32),
            pltpu.VMEM((nb, RW_HEADS // 2, hd, 2 * hd), F32),
            pltpu.VMEM((2 * m_rows, RW_WIDTH), F32),
            pltpu.VMEM((2 * m_rows, RW_WIDTH), F32),
            pltpu.VMEM((m_rows, RW_WIDTH), F32),
            pltpu.VMEM((RW_WIDTH, 2 * m_rows), F32),
            pltpu.VMEM((RW_WIDTH, m_rows), F32),
            pltpu.VMEM((RW_HEADS // 2, 2 * hd, 4 * m_rows), BF16),
            pltpu.VMEM((RW_HEADS // 2, hd, 4 * m_rows), F32),
            pltpu.VMEM((RW_WIDTH, m_rows), F32),
            pltpu.VMEM((m_rows, RW_WIDTH), F32),
            pltpu.VMEM((m_rows, RW_WIDTH), F32),
        ],
        compiler_params=pltpu.CompilerParams(
            dimension_semantics=("parallel", "arbitrary"), vmem_limit_bytes=VMEM_LIMIT),
        name="rwkv7",
    )(p, shift_prev, s0, *wts)


def _gla_kernel(p_ref, s0_ref, gw2_ref, gb_ref, nw_ref, o_ref, st_ref,
                acc_ref, qd_ref, v_ref, kt_ref, et_ref, *, nb, tb, c, pp):
    j = pl.program_id(1)
    m_rows = nb * tb
    dk, dv = GLA_DK, GLA_DV

    @pl.when(j == 0)
    def _():
        st_ref[...] = s0_ref[...]

    p = p_ref[...].reshape(m_rows, GROUP_PAD)
    q = p[:, 0:GLA_KEY_WIDTH] * (dk ** -0.5)
    k = p[:, GLA_OFF_K:GLA_OFF_K + GLA_KEY_WIDTH]
    v = p[:, GLA_OFF_V:GLA_OFF_V + GLA_WIDTH]
    xs = p[:, GLA_OFF_GATE:GLA_OFF_GATE + LANES].astype(BF16)
    gout = p[:, GLA_OFF_GOUT:GLA_OFF_GOUT + GLA_WIDTH]
    gk = -_softplus(-(_dot(xs, gw2_ref[...]) + gb_ref[...])) / GLA_GATE_NORM

    same, ri, ci = _chunk_masks(m_rows, c)
    lmask = jnp.where(same & (ci <= ri), 1.0, 0.0).astype(BF16)
    umask = jnp.where(same & (ci > ri), 1.0, 0.0).astype(BF16)
    causal = same & (ci <= ri)
    gcum = _mm(lmask, gk, pb=3)
    grev = _mm(umask, gk, pb=3)
    qd = q * jnp.exp(gcum)
    kinv = k * jnp.exp(-gcum)
    qd_ref[...] = qd
    v_ref[...] = v
    kt_ref[...] = (k * jnp.exp(grev)).T
    et_ref[...] = jnp.exp(gcum + grev).T

    for h in range(GLA_HEADS):
        a = _mm(qd[:, h * dk:(h + 1) * dk], kinv[:, h * dk:(h + 1) * dk], pp, pp, NT)
        a = jnp.where(causal, a, 0.0)
        acc_ref[:, h * dv:(h + 1) * dv] = _mm(a, v[:, h * dv:(h + 1) * dv], pp, pp)

    sh_c = c.bit_length() - 1
    cpb = tb // c
    row_id = lax.broadcasted_iota(jnp.int32, (m_rows, dv), 0) >> sh_c
    for i in range(m_rows // c):
        b = i // cpb
        r0 = i * c
        rmask = row_id == i
        for h in range(GLA_HEADS):
            s = st_ref[b, h]
            acc_ref[r0:r0 + c, h * dv:(h + 1) * dv] += _mm(
                qd_ref[r0:r0 + c, h * dk:(h + 1) * dk], s, pp, pp)
            vz = jnp.where(rmask, v_ref[:, h * dv:(h + 1) * dv], 0.0)
            st_ref[b, h] = (s * et_ref[h * dk:(h + 1) * dk, r0:r0 + 1]
                            + _mm(kt_ref[h * dk:(h + 1) * dk, :], vz, pp, pp))

    for h in range(GLA_HEADS):
        o = acc_ref[:, h * dv:(h + 1) * dv]
        on = o * lax.rsqrt(jnp.mean(o * o, axis=-1, keepdims=True) + HEAD_NORM_EPS) * nw_ref[...]
        gh = gout[:, h * dv:(h + 1) * dv]
        o_ref[:, :, h * dv:(h + 1) * dv] = (on * (gh * _sigmoid(gh))).reshape(nb, tb, dv)


def _gla(p, s0, wts, nb, tb, c, pp):
    bsz, seq, _ = p.shape
    m_rows = nb * tb
    const = lambda shape: pl.BlockSpec(shape, lambda i, j: (0,) * len(shape))
    kern = functools.partial(_gla_kernel, nb=nb, tb=tb, c=c, pp=pp)
    return pl.pallas_call(
        kern,
        grid=(bsz // nb, seq // tb),
        in_specs=[
            pl.BlockSpec((nb, tb, GROUP_PAD), lambda i, j: (i, j, 1)),
            pl.BlockSpec((nb, GLA_HEADS, GLA_DK, GLA_DV), lambda i, j: (i, 0, 0, 0)),
            const((LANES, GLA_KEY_WIDTH)),
            const((1, GLA_KEY_WIDTH)),
            const((1, GLA_DV)),
        ],
        out_specs=[
            pl.BlockSpec((nb, tb, GLA_WIDTH), lambda i, j: (i, j, 0)),
            pl.BlockSpec((nb, GLA_HEADS, GLA_DK, GLA_DV), lambda i, j: (i, 0, 0, 0)),
        ],
        out_shape=[
            jax.ShapeDtypeStruct((bsz, seq, GLA_WIDTH), F32),
            jax.ShapeDtypeStruct((bsz, GLA_HEADS, GLA_DK, GLA_DV), F32),
        ],
        scratch_shapes=[
            pltpu.VMEM((m_rows, GLA_WIDTH), F32),
            pltpu.VMEM((m_rows, GLA_KEY_WIDTH), F32),
            pltpu.VMEM((m_rows, GLA_WIDTH), F32),
            pltpu.VMEM((GLA_KEY_WIDTH, m_rows), F32),
            pltpu.VMEM((GLA_KEY_WIDTH, m_rows), F32),
        ],
        compiler_params=pltpu.CompilerParams(
            dimension_semantics=("parallel", "arbitrary"), vmem_limit_bytes=VMEM_LIMIT),
        name="gla",
    )(p, s0, *wts)


def _outproj_kernel(x_ref, orw_ref, ogla_ref, w_ref, g_ref, x1_ref, h_ref):
    x1 = (x_ref[...]
          + _dot(orw_ref[...].astype(BF16), w_ref[0:RW_WIDTH, :])
          + _dot(ogla_ref[...].astype(BF16), w_ref[RW_WIDTH:, :]))
    x1_ref[...] = x1
    ms = jnp.mean(x1 * x1, axis=-1, keepdims=True)
    h_ref[...] = (x1 * lax.rsqrt(ms + NORM_EPS) * g_ref[...]).astype(BF16)


def _outproj(x2d, o_rw, o_gla, w_out, g2, tm):
    t = x2d.shape[0]
    return pl.pallas_call(
        _outproj_kernel,
        grid=(t // tm,),
        in_specs=[
            pl.BlockSpec((tm, D_MODEL), lambda i: (i, 0)),
            pl.BlockSpec((tm, RW_WIDTH), lambda i: (i, 0)),
            pl.BlockSpec((tm, GLA_WIDTH), lambda i: (i, 0)),
            pl.BlockSpec((D_MODEL, D_MODEL), lambda i: (0, 0)),
            pl.BlockSpec((1, D_MODEL), lambda i: (0, 0)),
        ],
        out_specs=[
            pl.BlockSpec((tm, D_MODEL), lambda i: (i, 0)),
            pl.BlockSpec((tm, D_MODEL), lambda i: (i, 0)),
        ],
        out_shape=[
            jax.ShapeDtypeStruct((t, D_MODEL), F32),
            jax.ShapeDtypeStruct((t, D_MODEL), BF16),
        ],
        compiler_params=pltpu.CompilerParams(
            dimension_semantics=("parallel",), vmem_limit_bytes=VMEM_LIMIT),
        name="outproj",
    )(x2d, o_rw, o_gla, w_out, g2)


def _mlp_kernel(x1_ref, h_ref, wu_ref, wd_ref, g_ref, o_ref):
    jf = pl.program_id(1)

    @pl.when(jf == 0)
    def _():
        o_ref[...] = x1_ref[...]

    u = jnp.maximum(_dot(h_ref[...], wu_ref[...]), 0.0)
    o_ref[...] += _dot((u * u).astype(BF16), wd_ref[...])

    @pl.when(jf == pl.num_programs(1) - 1)
    def _():
        x2 = o_ref[...]
        ms = jnp.mean(x2 * x2, axis=-1, keepdims=True)
        o_ref[...] = x2 * lax.rsqrt(ms + NORM_EPS) * g_ref[...]


def _mlp(x1, h2, w_up, w_down, gf, tm, tf):
    t = x1.shape[0]
    return pl.pallas_call(
        _mlp_kernel,
        grid=(t // tm, D_FF // tf),
        in_specs=[
            pl.BlockSpec((tm, D_MODEL), lambda i, j: (i, 0)),
            pl.BlockSpec((tm, D_MODEL), lambda i, j: (i, 0)),
            pl.BlockSpec((D_MODEL, tf), lambda i, j: (0, j)),
            pl.BlockSpec((tf, D_MODEL), lambda i, j: (j, 0)),
            pl.BlockSpec((1, D_MODEL), lambda i, j: (0, 0)),
        ],
        out_specs=pl.BlockSpec((tm, D_MODEL), lambda i, j: (i, 0)),
        out_shape=jax.ShapeDtypeStruct((t, D_MODEL), F32),
        compiler_params=pltpu.CompilerParams(
            dimension_semantics=("parallel", "arbitrary"), vmem_limit_bytes=VMEM_LIMIT),
        name="mlp",
    )(x1, h2, w_up, w_down, gf)


def _pad_cols(w, n):
    return jnp.pad(w, ((0, 0), (0, n - w.shape[1])))


def _prep_weights(w_in, rw_mu, rw_w0, rw_w2, rw_a0, rw_a2, rw_g2, rw_k_k, rw_k_a, rw_r_k, rw_ln_w, rw_ln_b,
                  gla_gw2, gla_gb, gla_norm_w):
    go = RW_PROJ
    gla_head = w_in[:, go:go + GLA_OFF_GATE + GLA_GATE_RANK]
    gla_gout = w_in[:, go + GLA_OFF_GATE + GLA_GATE_RANK:]
    w_p = jnp.concatenate([
        _pad_cols(w_in[:, :RW_PROJ], GROUP_PAD),
        _pad_cols(gla_head, GLA_OFF_GOUT),
        _pad_cols(gla_gout, GROUP_PAD - GLA_OFF_GOUT),
    ], axis=1).astype(BF16)

    row = lambda x: x.reshape(1, -1).astype(F32)
    mu = _pad_cols(row(rw_mu), GROUP_PAD)
    wwa = jnp.zeros((LANES, 2 * RW_WIDTH), F32)
    wwa = wwa.at[0:RW_LORA_W, 0:RW_WIDTH].set(rw_w2)
    wwa = wwa.at[RW_LORA_W:RW_LORA_W + RW_LORA_A, RW_WIDTH:].set(rw_a2)
    g2p = jnp.pad(rw_g2, ((0, 2 * LANES - RW_LORA_G), (0, 0)))
    head_of_col = jnp.arange(RW_WIDTH) // RW_HEAD_DIM
    e1 = (head_of_col[:, None] == jnp.arange(LANES)[None, :]).astype(BF16)
    rw_wts = (mu, wwa.astype(BF16), row(rw_w0), row(rw_a0), g2p.astype(BF16), row(rw_k_k), row(rw_k_a),
              row(rw_r_k), row(rw_ln_w), row(rw_ln_b), e1, e1.T)
    gw2p = jnp.pad(gla_gw2, ((0, LANES - GLA_GATE_RANK), (0, 0))).astype(BF16)
    gla_wts = (gw2p, row(gla_gb), row(gla_norm_w))
    return w_p, rw_wts, gla_wts


RW_PIECES = 1
GLA_PIECES = 1
PAIR_GROUP = 8
STRIP = 256


def _trunk(x, shift, s_rw, s_gla, norm1_g, w_p, rw_wts, gla_wts, w_out, norm2_g, w_up, w_down, norm_f_g,
           nb, tb, c_rw, c_gla, tm):
    bsz, seq, _ = x.shape
    t = bsz * seq
    x2d = x.reshape(t, D_MODEL)
    proj = _inproj(x2d, norm1_g.reshape(1, -1), w_p, tm, GROUP_PAD // 3).reshape(bsz, seq, NP)
    shift_p = _pad_cols(shift, GROUP_PAD).reshape(bsz, 1, GROUP_PAD)
    o_rw, sh_new, s_rw_new = _rwkv(proj, shift_p, s_rw, rw_wts, nb, tb, c_rw, RW_PIECES)
    o_gla, s_gla_new = _gla(proj, s_gla, gla_wts, nb, tb, c_gla, GLA_PIECES)
    x1, h2 = _outproj(x2d, o_rw.reshape(t, RW_WIDTH), o_gla.reshape(t, GLA_WIDTH), w_out,
                      norm2_g.reshape(1, -1), 512)
    y = _mlp(x1, h2, w_up, w_down, norm_f_g.reshape(1, -1), 512, 512)
    return (y.reshape(bsz, seq, D_MODEL), sh_new[:, 0, :RW_PROJ][None], s_rw_new[None], s_gla_new[None])


def kernel(x_prompt, x_sample, state_rwkv_shift, state_rwkv_wkv, state_gla, norm1_g, w_in, rw_mu, rw_w0,
           rw_w2, rw_a0, rw_a2, rw_g2, rw_k_k, rw_k_a, rw_r_k, rw_ln_w, rw_ln_b, gla_gw2, gla_gb, gla_norm_w,
           w_out, norm2_g, w_up, w_down, norm_f_g):
    w_p, rw_wts, gla_wts = _prep_weights(
        w_in[0], rw_mu[0], rw_w0[0], rw_w2[0], rw_a0[0], rw_a2[0], rw_g2[0], rw_k_k[0], rw_k_a[0],
        rw_r_k[0].reshape(-1), rw_ln_w[0], rw_ln_b[0], gla_gw2[0], gla_gb[0], gla_norm_w[0])
    shared = (norm1_g[0], w_p, rw_wts, gla_wts, w_out[0].astype(BF16), norm2_g[0], w_up[0].astype(BF16),
              w_down[0].astype(BF16), norm_f_g)

    bp, lp, _ = x_prompt.shape
    bs, ls, _ = x_sample.shape
    dt = x_prompt.dtype
    out_p = _trunk(x_prompt, jnp.zeros((bp, RW_PROJ), dt),
                   jnp.zeros((bp, RW_HEADS, RW_HEAD_DIM, RW_HEAD_DIM), dt),
                   jnp.zeros((bp, GLA_HEADS, GLA_DK, GLA_DV), dt), *shared,
                   nb=1, tb=ROWS, c_rw=32, c_gla=GLA_CHUNK, tm=1024)
    out_s = _trunk(x_sample, state_rwkv_shift[0], state_rwkv_wkv[0], state_gla[0], *shared,
                   nb=ROWS // ls, tb=ls, c_rw=ls, c_gla=ls, tm=1024)
    return (out_p[0], out_s[0], out_p[1], out_p[2], out_p[3], out_s[1], out_s[2], out_s[3])
```

```python
import functools

import jax
import jax.numpy as jnp
from jax import lax
from jax.experimental import pallas as pl
from jax.experimental.pallas import tpu as pltpu

F32 = jnp.float32
BF16 = jnp.bfloat16

D_MODEL = 2048
RW_WIDTH = 1024
RW_HEADS = 16
RW_HEAD_DIM = 64
RW_LORA_W = 64
RW_LORA_A = 64
RW_LORA_G = 160
RW_PROJ = 3 * RW_WIDTH + RW_LORA_W + RW_LORA_A + RW_LORA_G
RW_GN_EPS = 64e-5
GLA_WIDTH = 1024
GLA_HEADS = 4
GLA_KEY_WIDTH = 512
GLA_DK = 128
GLA_DV = 256
GLA_GATE_RANK = 16
GLA_GATE_NORM = 16.0
GLA_CHUNK = 64
GLA_PROJ = 2 * GLA_KEY_WIDTH + GLA_WIDTH + GLA_GATE_RANK + GLA_WIDTH
D_FF = 4 * D_MODEL
NORM_EPS = 1e-6
HEAD_NORM_EPS = 1e-5
DECAY_SCALE = 0.6065306597126334

LANES = 128
SUBLANES = 8

GROUP_PAD = 3456
RW_OFF_XWA = 3 * RW_WIDTH
RW_OFF_XG = RW_OFF_XWA + LANES
GLA_OFF_K = GLA_KEY_WIDTH
GLA_OFF_V = 2 * GLA_KEY_WIDTH
GLA_OFF_GATE = GLA_OFF_V + GLA_WIDTH
GLA_OFF_GOUT = GLA_OFF_GATE + LANES
NP = 2 * GROUP_PAD

ROWS = 128
MXU_WIDTH = 256
INPROJ_TN = 3 * MXU_WIDTH
VMEM_LIMIT = 56 * 1024 * 1024

NN = (((1,), (0,)), ((), ()))
NT = (((1,), (1,)), ((), ()))


def _dot(a, b, dims=NN):
    return lax.dot_general(a, b, dims, preferred_element_type=F32)


def _parts(x, n):
    if x.dtype == BF16:
        return [x]
    out = []
    rem = x
    for i in range(n):
        h = rem.astype(BF16)
        out.append(h)
        if i + 1 < n:
            rem = rem - h.astype(F32)
    return out


def _mm(a, b, pa=1, pb=1, dims=NN):
    pa = 1 if a.dtype == BF16 else pa
    pb = 1 if b.dtype == BF16 else pb
    aa = _parts(a, pa)
    bb = _parts(b, pb)
    n = max(pa, pb)
    acc = None
    for i in range(pa):
        for j in range(pb):
            if i + j < n:
                t = _dot(aa[i], bb[j], dims)
                acc = t if acc is None else acc + t
    return acc


def _softplus(x):
    return jnp.maximum(x, 0.0) + jnp.log(1.0 + jnp.exp(-jnp.abs(x)))


def _sigmoid(x):
    return 0.5 * jnp.tanh(0.5 * x) + 0.5


def _chunk_masks(m, c):
    sh = c.bit_length() - 1
    ri = lax.broadcasted_iota(jnp.int32, (m, m), 0)
    ci = lax.broadcasted_iota(jnp.int32, (m, m), 1)
    same = (ri >> sh) == (ci >> sh)
    return same, ri, ci


def _inproj_kernel(x_ref, g_ref, w_ref, o_ref, h_ref):
    @pl.when(pl.program_id(1) == 0)
    def _():
        x = x_ref[...]
        ms = jnp.mean(x * x, axis=-1, keepdims=True)
        h_ref[...] = (x * lax.rsqrt(ms + NORM_EPS) * g_ref[...]).astype(BF16)

    o_ref[...] = _dot(h_ref[...], w_ref[...])


def _inproj(x2d, g, w_p, tm, tn):
    t = x2d.shape[0]
    return pl.pallas_call(
        _inproj_kernel,
        grid=(t // tm, NP // tn),
        in_specs=[
            pl.BlockSpec((tm, D_MODEL), lambda i, j: (i, 0)),
            pl.BlockSpec((1, D_MODEL), lambda i, j: (0, 0)),
            pl.BlockSpec((D_MODEL, tn), lambda i, j: (0, j)),
        ],
        out_specs=pl.BlockSpec((tm, tn), lambda i, j: (i, j)),
        out_shape=jax.ShapeDtypeStruct((t, NP), F32),
        scratch_shapes=[pltpu.VMEM((tm, D_MODEL), BF16)],
        compiler_params=pltpu.CompilerParams(
            dimension_semantics=("parallel", "arbitrary"), vmem_limit_bytes=VMEM_LIMIT),
        name="inproj",
    )(x2d, g, w_p)


def _rwkv_kernel(p_ref, sh_ref, s0_ref, mu_ref, wwa_ref, w0_ref, a0_ref, g2_ref, kk_ref, ka_ref,
                 rk_ref, lnw_ref, lnb_ref, e1_ref, e2_ref,
                 o_ref, shout_ref, sout_ref,
                 carry_ref, st_ref, bk_ref, kb_ref, pc_ref, art_ref, vt_ref, arbd_ref, uy_ref, yt_ref,
                 g_ref, bonus_ref, *, nb, tb, c, pp):
    j = pl.program_id(1)
    m_rows = nb * tb
    n_lvl = c.bit_length() - 1
    hd = RW_HEAD_DIM

    @pl.when(j == 0)
    def _():
        for p in range(RW_HEADS // 2):
            st_ref[:, p] = jnp.concatenate([s0_ref[:, 2 * p], s0_ref[:, 2 * p + 1]], axis=-1)
        carry_ref[...] = sh_ref[...]

    yt_ref[...] = jnp.zeros_like(yt_ref)

    def shifted(c0, w):
        pc3 = p_ref[:, :, c0:c0 + w]
        t3 = lax.broadcasted_iota(jnp.int32, pc3.shape, 1)
        prev3 = jnp.where(t3 == 0, carry_ref[:, :, c0:c0 + w], pltpu.roll(pc3, 1, axis=1))
        return (pc3 + (prev3 - pc3) * mu_ref[:, c0:c0 + w]).reshape(m_rows, w)

    slab = shifted(RW_OFF_XWA, LANES)
    lane = lax.broadcasted_iota(jnp.int32, slab.shape, 1)
    lhs = jnp.where(lane < RW_LORA_W, jnp.tanh(slab), slab).astype(BF16)
    sg = _sigmoid(shifted(RW_OFF_XG, 2 * LANES)).astype(BF16)

    same, ri, ci = _chunk_masks(m_rows, c)
    lmask = jnp.where(same & (ci <= ri), 1.0, 0.0).astype(BF16)
    umask = jnp.where(same & (ci > ri), 1.0, 0.0).astype(BF16)
    strict_t = same & (ri < ci)
    incl_t = same & (ri <= ci)

    sw = STRIP
    strips = [slice(c0, c0 + sw) for c0 in range(0, RW_WIDTH, sw)]

    def head_stat(x, cs):
        return _dot(x.astype(BF16), e1_ref[cs, :])

    def head_bcast(s, cs):
        return _mm(s, e2_ref[:, cs], pa=2)

    lws = [_dot(lhs, wwa_ref[:, cs]) for cs in strips]
    las = [_dot(lhs, wwa_ref[:, RW_WIDTH + cs.start:RW_WIDTH + cs.stop]) for cs in strips]
    for cs in strips:
        g_ref[:, cs] = _dot(sg, g2_ref[:, cs])
    logws = [-DECAY_SCALE * _sigmoid(w0_ref[:, cs] + lw) for cs, lw in zip(strips, lws)]
    gcums = [_mm(lmask, logw, pb=2) for logw in logws]
    grevs = [_mm(umask, logw, pb=2) for logw in logws]
    avs = [_sigmoid(a0_ref[:, cs] + la) for cs, la in zip(strips, las)]
    ks = [shifted(RW_WIDTH + cs.start, sw) for cs in strips]
    kkfs = [k * kk_ref[:, cs] for cs, k in zip(strips, ks)]
    rinvs = [lax.rsqrt(jnp.maximum(head_stat(kkf * kkf, cs), 1e-24)) for cs, kkf in zip(strips, kkfs)]
    k2s = [k * (1.0 + (a - 1.0) * ka_ref[:, cs]) for cs, k, a in zip(strips, ks, avs)]
    rs = [shifted(cs.start, sw) for cs in strips]
    bstats = [head_stat(r * k2 * rk_ref[:, cs], cs) for cs, r, k2 in zip(strips, rs, k2s)]
    kks = [kkf * head_bcast(rinv, cs) for cs, kkf, rinv in zip(strips, kkfs, rinvs)]
    for cs, bstat in zip(strips, bstats):
        v = shifted(2 * RW_WIDTH + cs.start, sw)
        bonus_ref[:, cs] = head_bcast(bstat, cs) * v
        vt_ref[cs, :] = v.T
    for cs, logw, gcum, grev, a, kk, k2, r in zip(strips, logws, gcums, grevs, avs, kks, k2s, rs):
        beta = kk * a
        e_inv = jnp.exp(-gcum)
        e_rev = jnp.exp(grev)
        art_ref[cs, 0:m_rows] = (-kk * jnp.exp(gcum - logw)).T
        art_ref[cs, m_rows:2 * m_rows] = (r * jnp.exp(gcum)).T
        bk_ref[0:m_rows, cs] = beta * e_inv
        bk_ref[m_rows:, cs] = k2 * e_inv
        kb_ref[0:m_rows, cs] = k2 * e_rev
        kb_ref[m_rows:, cs] = beta * e_rev
        pc_ref[:, cs] = jnp.exp(gcum + grev)

    last = p_ref[:, tb - 1:tb, :]
    carry_ref[...] = last
    shout_ref[...] = last

    pw = 2 * hd
    cat = jnp.concatenate
    top = lax.broadcasted_iota(jnp.int32, (pw, 1), 0) < hd
    left = lax.broadcasted_iota(jnp.int32, (1, 2 * m_rows), 1) < m_rows

    def split_rows(x):
        return cat([jnp.where(top, x, 0.0), jnp.where(top, 0.0, x)], axis=1)

    def block_diag(xc):
        return cat([jnp.where(left, xc, 0.0), jnp.where(left, 0.0, xc)], axis=0)

    def blk(xt, u, rh, ch):
        col = (2 * u + ch) * m_rows
        return xt[rh * m_rows:(rh + 1) * m_rows, col:col + m_rows]

    def pair_group(pairs):
        los = [p * pw for p in pairs]
        arts = [art_ref[lo:lo + pw, :] for lo in los]
        xts = [_mm(bk_ref[:, lo:lo + pw], split_rows(art), pp, pp) for lo, art in zip(los, arts)]
        npcs = [cat([jnp.where(strict_t, blk(xt, 0, 0, 0), 0.0), jnp.where(strict_t, blk(xt, 1, 0, 0), 0.0)], axis=1)
                for xt in xts]
        vas = [_mm(split_rows(vt_ref[lo:lo + pw, :]),
                   cat([cat([jnp.where(strict_t, blk(xt, u, 1, 0), 0.0), jnp.where(incl_t, blk(xt, u, 1, 1), 0.0)],
                            axis=1) for u in range(2)], axis=0), pp, pp)
               for lo, xt in zip(los, xts)]
        zcs = [cat([cat([art[u * hd:(u + 1) * hd, 0:m_rows], va[u * hd:(u + 1) * hd, 0:m_rows]], axis=0)
                    for u in range(2)], axis=1) for art, va in zip(arts, vas)]
        for lvl in range(n_lvl):
            bds = [block_diag(npc) for npc in npcs]
            if lvl + 1 < n_lvl:
                ts = [_mm(cat([zc, npc], axis=0), bd, pp, pp) for zc, npc, bd in zip(zcs, npcs, bds)]
                zcs = [zc + t[0:2 * hd] for zc, t in zip(zcs, ts)]
                npcs = [t[2 * hd:] for t in ts]
            else:
                zcs = [zc + _mm(zc, bd, pp, pp) for zc, bd in zip(zcs, bds)]
        rycs = [cat([cat([art[u * hd:(u + 1) * hd, m_rows:], va[u * hd:(u + 1) * hd, m_rows:]], axis=0)
                     for u in range(2)], axis=1)
                + _mm(zc, block_diag(cat([jnp.where(incl_t, blk(xt, 0, 0, 1), 0.0),
                                          jnp.where(incl_t, blk(xt, 1, 0, 1), 0.0)], axis=1)), pp, pp)
                for art, va, zc, xt in zip(arts, vas, zcs, xts)]
        zero = jnp.zeros((hd, 2 * m_rows), BF16)
        for p, zc, ry in zip(pairs, zcs, rycs):
            arbd_ref[p, 0:hd, 0:2 * m_rows] = cat([zc[0:hd, 0:m_rows], ry[0:hd, 0:m_rows]], axis=1).astype(BF16)
            arbd_ref[p, 0:hd, 2 * m_rows:] = zero
            arbd_ref[p, hd:, 0:2 * m_rows] = zero
            arbd_ref[p, hd:, 2 * m_rows:] = cat([zc[0:hd, m_rows:], ry[0:hd, m_rows:]], axis=1).astype(BF16)
            uy_ref[p] = cat([zc[hd:, 0:m_rows], ry[hd:, 0:m_rows], zc[hd:, m_rows:], ry[hd:, m_rows:]], axis=1)

    n_pairs = RW_HEADS // 2
    for g0 in range(0, n_pairs, PAIR_GROUP):
        pair_group(list(range(g0, g0 + PAIR_GROUP)))

    sh_c = c.bit_length() - 1
    cpb = tb // c
    row_id2 = (lax.broadcasted_iota(jnp.int32, (2 * m_rows, pw), 0) & (m_rows - 1)) >> sh_c
    col_id = lax.broadcasted_iota(jnp.int32, (hd, m_rows), 1) >> sh_c
    lane_lo = lax.broadcasted_iota(jnp.int32, (1, pw), 1) < hd

    def chunk_body(i, carry):
        b = i // cpb
        r0 = pl.multiple_of(i * c, c)
        rmask2 = row_id2 == i
        cmask = col_id == i
        prs = range(n_pairs)
        ss = [st_ref[b, p] for p in prs]
        zss = [_mm(s, arbd_ref[p], pp, pp) + uy_ref[p] for p, s in zip(prs, ss)]
        upd = []
        for p, zs in zip(prs, zss):
            lo = p * pw
            kbz = jnp.where(rmask2, kb_ref[:, lo:lo + pw], 0.0)
            rhs = cat([jnp.where(lane_lo, kbz, 0.0), jnp.where(lane_lo, 0.0, kbz)], axis=0)
            vu = cat([vt_ref[lo:lo + hd, :], zs[:, 0:m_rows],
                      vt_ref[lo + hd:lo + pw, :], zs[:, 2 * m_rows:3 * m_rows]], axis=1)
            upd.append(_mm(vu, rhs, pp, pp))
        for p, s, zs, d in zip(prs, ss, zss, upd):
            lo = p * pw
            st_ref[b, p] = s * pc_ref[pl.ds(r0, 1), lo:lo + pw] + d
            for u in range(2):
                rows = slice((2 * p + u) * hd, (2 * p + u + 1) * hd)
                yt_ref[rows, :] = jnp.where(cmask, zs[:, (2 * u + 1) * m_rows:(2 * u + 2) * m_rows], yt_ref[rows, :])
        return carry

    lax.fori_loop(0, m_rows // c, chunk_body, 0)

    @pl.when(j == pl.num_programs(1) - 1)
    def _():
        for p in range(n_pairs):
            sp = st_ref[:, p]
            sout_ref[:, 2 * p] = sp[:, :, 0:hd]
            sout_ref[:, 2 * p + 1] = sp[:, :, hd:]

    inv_n = 1.0 / hd
    ys = [yt_ref[cs, :].T for cs in strips]
    means = [head_stat(y, cs) * inv_n for cs, y in zip(strips, ys)]
    ycs = [y - head_bcast(mean, cs) for cs, y, mean in zip(strips, ys, means)]
    rstds = [lax.rsqrt(head_stat(yc * yc, cs) * inv_n + RW_GN_EPS) for cs, yc in zip(strips, ycs)]
    for cs, yc, rstd in zip(strips, ycs, rstds):
        yn = yc * head_bcast(rstd, cs) * lnw_ref[:, cs] + lnb_ref[:, cs]
        o_ref[:, :, cs] = ((yn + bonus_ref[:, cs]) * g_ref[:, cs]).reshape(nb, tb, sw)


def _rwkv(p, shift_prev, s0, wts, nb, tb, c, pp):
    bsz, seq, _ = p.shape
    m_rows = nb * tb
    hd = RW_HEAD_DIM
    const = lambda shape: pl.BlockSpec(shape, lambda i, j: (0,) * len(shape))
    kern = functools.partial(_rwkv_kernel, nb=nb, tb=tb, c=c, pp=pp)
    return pl.pallas_call(
        kern,
        grid=(bsz // nb, seq // tb),
        in_specs=[
            pl.BlockSpec((nb, tb, GROUP_PAD), lambda i, j: (i, j, 0)),
            pl.BlockSpec((nb, 1, GROUP_PAD), lambda i, j: (i, 0, 0)),
            pl.BlockSpec((nb, RW_HEADS, hd, hd), lambda i, j: (i, 0, 0, 0)),
            const((1, GROUP_PAD)),
            const((LANES, 2 * RW_WIDTH)),
            const((1, RW_WIDTH)),
            const((1, RW_WIDTH)),
            const((2 * LANES, RW_WIDTH)),
            const((1, RW_WIDTH)),
            const((1, RW_WIDTH)),
            const((1, RW_WIDTH)),
            const((1, RW_WIDTH)),
            const((1, RW_WIDTH)),
            const((RW_WIDTH, LANES)),
            const((LANES, RW_WIDTH)),
        ],
        out_specs=[
            pl.BlockSpec((nb, tb, RW_WIDTH), lambda i, j: (i, j, 0)),
            pl.BlockSpec((nb, 1, GROUP_PAD), lambda i, j: (i, 0, 0)),
            pl.BlockSpec((nb, RW_HEADS, hd, hd), lambda i, j: (i, 0, 0, 0)),
        ],
        out_shape=[
            jax.ShapeDtypeStruct((bsz, seq, RW_WIDTH), F32),
            jax.ShapeDtypeStruct((bsz, 1, GROUP_PAD), F32),
            jax.ShapeDtypeStruct((bsz, RW_HEADS, hd, hd), F32),
        ],
        scratch_shapes=[
            pltpu.VMEM((nb, 1, GROUP_PAD), F32),
            pltpu.VMEM((nb, RW_HEADS // 2, hd, 2 * hd), F32),
            pltpu.VMEM((2 * m_rows, RW_WIDTH), F32),
            pltpu.VMEM((2 * m_rows, RW_WIDTH), F32),
            pltpu.VMEM((m_rows, RW_WIDTH), F32),
            pltpu.VMEM((RW_WIDTH, 2 * m_rows), F32),
            pltpu.VMEM((RW_WIDTH, m_rows), F32),
            pltpu.VMEM((RW_HEADS // 2, 2 * hd, 4 * m_rows), BF16),
            pltpu.VMEM((RW_HEADS // 2, hd, 4 * m_rows), F32),
            pltpu.VMEM((RW_WIDTH, m_rows), F32),
            pltpu.VMEM((m_rows, RW_WIDTH), F32),
            pltpu.VMEM((m_rows, RW_WIDTH), F32),
        ],
        compiler_params=pltpu.CompilerParams(
            dimension_semantics=("parallel", "arbitrary"), vmem_limit_bytes=VMEM_LIMIT),
        name="rwkv7",
    )(p, shift_prev, s0, *wts)


def _gla_kernel(p_ref, s0_ref, gw2_ref, gb_ref, nw_ref, o_ref, st_ref,
                acc_ref, qd_ref, v_ref, kt_ref, et_ref, *, nb, tb, c, pp):
    j = pl.program_id(1)
    m_rows = nb * tb
    dk, dv = GLA_DK, GLA_DV

    @pl.when(j == 0)
    def _():
        st_ref[...] = s0_ref[...]

    p = p_ref[...].reshape(m_rows, GROUP_PAD)
    q = p[:, 0:GLA_KEY_WIDTH] * (dk ** -0.5)
    k = p[:, GLA_OFF_K:GLA_OFF_K + GLA_KEY_WIDTH]
    v = p[:, GLA_OFF_V:GLA_OFF_V + GLA_WIDTH]
    xs = p[:, GLA_OFF_GATE:GLA_OFF_GATE + LANES].astype(BF16)
    gout = p[:, GLA_OFF_GOUT:GLA_OFF_GOUT + GLA_WIDTH]
    gk = -_softplus(-(_dot(xs, gw2_ref[...]) + gb_ref[...])) / GLA_GATE_NORM

    same, ri, ci = _chunk_masks(m_rows, c)
    lmask = jnp.where(same & (ci <= ri), 1.0, 0.0).astype(BF16)
    umask = jnp.where(same & (ci > ri), 1.0, 0.0).astype(BF16)
    causal = same & (ci <= ri)
    gcum = _mm(lmask, gk, pb=3)
    grev = _mm(umask, gk, pb=3)
    qd = q * jnp.exp(gcum)
    kinv = k * jnp.exp(-gcum)
    qd_ref[...] = qd
    v_ref[...] = v
    kt_ref[...] = (k * jnp.exp(grev)).T
    et_ref[...] = jnp.exp(gcum + grev).T

    for h in range(GLA_HEADS):
        a = _mm(qd[:, h * dk:(h + 1) * dk], kinv[:, h * dk:(h + 1) * dk], pp, pp, NT)
        a = jnp.where(causal, a, 0.0)
        acc_ref[:, h * dv:(h + 1) * dv] = _mm(a, v[:, h * dv:(h + 1) * dv], pp, pp)

    sh_c = c.bit_length() - 1
    cpb = tb // c
    row_id = lax.broadcasted_iota(jnp.int32, (m_rows, dv), 0) >> sh_c
    for i in range(m_rows // c):
        b = i // cpb
        r0 = i * c
        rmask = row_id == i
        for h in range(GLA_HEADS):
            s = st_ref[b, h]
            acc_ref[r0:r0 + c, h * dv:(h + 1) * dv] += _mm(
                qd_ref[r0:r0 + c, h * dk:(h + 1) * dk], s, pp, pp)
            vz = jnp.where(rmask, v_ref[:, h * dv:(h + 1) * dv], 0.0)
            st_ref[b, h] = (s * et_ref[h * dk:(h + 1) * dk, r0:r0 + 1]
                            + _mm(kt_ref[h * dk:(h + 1) * dk, :], vz, pp, pp))

    for h in range(GLA_HEADS):
        o = acc_ref[:, h * dv:(h + 1) * dv]
        on = o * lax.rsqrt(jnp.mean(o * o, axis=-1, keepdims=True) + HEAD_NORM_EPS) * nw_ref[...]
        gh = gout[:, h * dv:(h + 1) * dv]
        o_ref[:, :, h * dv:(h + 1) * dv] = (on * (gh * _sigmoid(gh))).reshape(nb, tb, dv)


def _gla(p, s0, wts, nb, tb, c, pp):
    bsz, seq, _ = p.shape
    m_rows = nb * tb
    const = lambda shape: pl.BlockSpec(shape, lambda i, j: (0,) * len(shape))
    kern = functools.partial(_gla_kernel, nb=nb, tb=tb, c=c, pp=pp)
    return pl.pallas_call(
        kern,
        grid=(bsz // nb, seq // tb),
        in_specs=[
            pl.BlockSpec((nb, tb, GROUP_PAD), lambda i, j: (i, j, 1)),
            pl.BlockSpec((nb, GLA_HEADS, GLA_DK, GLA_DV), lambda i, j: (i, 0, 0, 0)),
            const((LANES, GLA_KEY_WIDTH)),
            const((1, GLA_KEY_WIDTH)),
            const((1, GLA_DV)),
        ],
        out_specs=[
            pl.BlockSpec((nb, tb, GLA_WIDTH), lambda i, j: (i, j, 0)),
            pl.BlockSpec((nb, GLA_HEADS, GLA_DK, GLA_DV), lambda i, j: (i, 0, 0, 0)),
        ],
        out_shape=[
            jax.ShapeDtypeStruct((bsz, seq, GLA_WIDTH), F32),
            jax.ShapeDtypeStruct((bsz, GLA_HEADS, GLA_DK, GLA_DV), F32),
        ],
        scratch_shapes=[
            pltpu.VMEM((m_rows, GLA_WIDTH), F32),
            pltpu.VMEM((m_rows, GLA_KEY_WIDTH), F32),
            pltpu.VMEM((m_rows, GLA_WIDTH), F32),
            pltpu.VMEM((GLA_KEY_WIDTH, m_rows), F32),
            pltpu.VMEM((GLA_KEY_WIDTH, m_rows), F32),
        ],
        compiler_params=pltpu.CompilerParams(
            dimension_semantics=("parallel", "arbitrary"), vmem_limit_bytes=VMEM_LIMIT),
        name="gla",
    )(p, s0, *wts)


def _outproj_kernel(x_ref, orw_ref, ogla_ref, w_ref, g_ref, x1_ref, h_ref):
    x1 = (x_ref[...]
          + _dot(orw_ref[...].astype(BF16), w_ref[0:RW_WIDTH, :])
          + _dot(ogla_ref[...].astype(BF16), w_ref[RW_WIDTH:, :]))
    x1_ref[...] = x1
    ms = jnp.mean(x1 * x1, axis=-1, keepdims=True)
    h_ref[...] = (x1 * lax.rsqrt(ms + NORM_EPS) * g_ref[...]).astype(BF16)


def _outproj(x2d, o_rw, o_gla, w_out, g2, tm):
    t = x2d.shape[0]
    return pl.pallas_call(
        _outproj_kernel,
        grid=(t // tm,),
        in_specs=[
            pl.BlockSpec((tm, D_MODEL), lambda i: (i, 0)),
            pl.BlockSpec((tm, RW_WIDTH), lambda i: (i, 0)),
            pl.BlockSpec((tm, GLA_WIDTH), lambda i: (i, 0)),
            pl.BlockSpec((D_MODEL, D_MODEL), lambda i: (0, 0)),
            pl.BlockSpec((1, D_MODEL), lambda i: (0, 0)),
        ],
        out_specs=[
            pl.BlockSpec((tm, D_MODEL), lambda i: (i, 0)),
            pl.BlockSpec((tm, D_MODEL), lambda i: (i, 0)),
        ],
        out_shape=[
            jax.ShapeDtypeStruct((t, D_MODEL), F32),
            jax.ShapeDtypeStruct((t, D_MODEL), BF16),
        ],
        compiler_params=pltpu.CompilerParams(
            dimension_semantics=("parallel",), vmem_limit_bytes=VMEM_LIMIT),
        name="outproj",
    )(x2d, o_rw, o_gla, w_out, g2)


def _mlp_kernel(x1_ref, h_ref, wu_ref, wd_ref, g_ref, o_ref):
    jf = pl.program_id(1)

    @pl.when(jf == 0)
    def _():
        o_ref[...] = x1_ref[...]

    u = jnp.maximum(_dot(h_ref[...], wu_ref[...]), 0.0)
    o_ref[...] += _dot((u * u).astype(BF16), wd_ref[...])

    @pl.when(jf == pl.num_programs(1) - 1)
    def _():
        x2 = o_ref[...]
        ms = jnp.mean(x2 * x2, axis=-1, keepdims=True)
        o_ref[...] = x2 * lax.rsqrt(ms + NORM_EPS) * g_ref[...]


def _mlp(x1, h2, w_up, w_down, gf, tm, tf):
    t = x1.shape[0]
    return pl.pallas_call(
        _mlp_kernel,
        grid=(t // tm, D_FF // tf),
        in_specs=[
            pl.BlockSpec((tm, D_MODEL), lambda i, j: (i, 0)),
            pl.BlockSpec((tm, D_MODEL), lambda i, j: (i, 0)),
            pl.BlockSpec((D_MODEL, tf), lambda i, j: (0, j)),
            pl.BlockSpec((tf, D_MODEL), lambda i, j: (j, 0)),
            pl.BlockSpec((1, D_MODEL), lambda i, j: (0, 0)),
        ],
        out_specs=pl.BlockSpec((tm, D_MODEL), lambda i, j: (i, 0)),
        out_shape=jax.ShapeDtypeStruct((t, D_MODEL), F32),
        compiler_params=pltpu.CompilerParams(
            dimension_semantics=("parallel", "arbitrary"), vmem_limit_bytes=VMEM_LIMIT),
        name="mlp",
    )(x1, h2, w_up, w_down, gf)


def _pad_cols(w, n):
    return jnp.pad(w, ((0, 0), (0, n - w.shape[1])))


def _prep_weights(w_in, rw_mu, rw_w0, rw_w2, rw_a0, rw_a2, rw_g2, rw_k_k, rw_k_a, rw_r_k, rw_ln_w, rw_ln_b,
                  gla_gw2, gla_gb, gla_norm_w):
    go = RW_PROJ
    gla_head = w_in[:, go:go + GLA_OFF_GATE + GLA_GATE_RANK]
    gla_gout = w_in[:, go + GLA_OFF_GATE + GLA_GATE_RANK:]
    w_p = jnp.concatenate([
        _pad_cols(w_in[:, :RW_PROJ], GROUP_PAD),
        _pad_cols(gla_head, GLA_OFF_GOUT),
        _pad_cols(gla_gout, GROUP_PAD - GLA_OFF_GOUT),
    ], axis=1).astype(BF16)

    row = lambda x: x.reshape(1, -1).astype(F32)
    mu = _pad_cols(row(rw_mu), GROUP_PAD)
    wwa = jnp.zeros((LANES, 2 * RW_WIDTH), F32)
    wwa = wwa.at[0:RW_LORA_W, 0:RW_WIDTH].set(rw_w2)
    wwa = wwa.at[RW_LORA_W:RW_LORA_W + RW_LORA_A, RW_WIDTH:].set(rw_a2)
    g2p = jnp.pad(rw_g2, ((0, 2 * LANES - RW_LORA_G), (0, 0)))
    head_of_col = jnp.arange(RW_WIDTH) // RW_HEAD_DIM
    e1 = (head_of_col[:, None] == jnp.arange(LANES)[None, :]).astype(BF16)
    rw_wts = (mu, wwa.astype(BF16), row(rw_w0), row(rw_a0), g2p.astype(BF16), row(rw_k_k), row(rw_k_a),
              row(rw_r_k), row(rw_ln_w), row(rw_ln_b), e1, e1.T)
    gw2p = jnp.pad(gla_gw2, ((0, LANES - GLA_GATE_RANK), (0, 0))).astype(BF16)
    gla_wts = (gw2p, row(gla_gb), row(gla_norm_w))
    return w_p, rw_wts, gla_wts


RW_PIECES = 1
GLA_PIECES = 1
PAIR_GROUP = 8
STRIP = 256


def _trunk(x, shift, s_rw, s_gla, norm1_g, w_p, rw_wts, gla_wts, w_out, norm2_g, w_up, w_down, norm_f_g,
           nb, tb, c_rw, c_gla, tm):
    bsz, seq, _ = x.shape
    t = bsz * seq
    x2d = x.reshape(t, D_MODEL)
    proj = _inproj(x2d, norm1_g.reshape(1, -1), w_p, tm, INPROJ_TN).reshape(bsz, seq, NP)
    shift_p = _pad_cols(shift, GROUP_PAD).reshape(bsz, 1, GROUP_PAD)
    o_rw, sh_new, s_rw_new = _rwkv(proj, shift_p, s_rw, rw_wts, nb, tb, c_rw, RW_PIECES)
    o_gla, s_gla_new = _gla(proj, s_gla, gla_wts, nb, tb, c_gla, GLA_PIECES)
    x1, h2 = _outproj(x2d, o_rw.reshape(t, RW_WIDTH), o_gla.reshape(t, GLA_WIDTH), w_out,
                      norm2_g.reshape(1, -1), 512)
    y = _mlp(x1, h2, w_up, w_down, norm_f_g.reshape(1, -1), 512, 1024)
    return (y.reshape(bsz, seq, D_MODEL), sh_new[:, 0, :RW_PROJ][None], s_rw_new[None], s_gla_new[None])


def kernel(x_prompt, x_sample, state_rwkv_shift, state_rwkv_wkv, state_gla, norm1_g, w_in, rw_mu, rw_w0,
           rw_w2, rw_a0, rw_a2, rw_g2, rw_k_k, rw_k_a, rw_r_k, rw_ln_w, rw_ln_b, gla_gw2, gla_gb, gla_norm_w,
           w_out, norm2_g, w_up, w_down, norm_f_g):
    w_p, rw_wts, gla_wts = _prep_weights(
        w_in[0], rw_mu[0], rw_w0[0], rw_w2[0], rw_a0[0], rw_a2[0], rw_g2[0], rw_k_k[0], rw_k_a[0],
        rw_r_k[0].reshape(-1), rw_ln_w[0], rw_ln_b[0], gla_gw2[0], gla_gb[0], gla_norm_w[0])
    shared = (norm1_g[0], w_p, rw_wts, gla_wts, w_out[0].astype(BF16), norm2_g[0], w_up[0].astype(BF16),
              w_down[0].astype(BF16), norm_f_g)

    bp, lp, _ = x_prompt.shape
    bs, ls, _ = x_sample.shape
    dt = x_prompt.dtype
    out_p = _trunk(x_prompt, jnp.zeros((bp, RW_PROJ), dt),
                   jnp.zeros((bp, RW_HEADS, RW_HEAD_DIM, RW_HEAD_DIM), dt),
                   jnp.zeros((bp, GLA_HEADS, GLA_DK, GLA_DV), dt), *shared,
                   nb=1, tb=ROWS, c_rw=64, c_gla=GLA_CHUNK, tm=1024)
    out_s = _trunk(x_sample, state_rwkv_shift[0], state_rwkv_wkv[0], state_gla[0], *shared,
                   nb=ROWS // ls, tb=ls, c_rw=ls, c_gla=ls, tm=1024)
    return (out_p[0], out_s[0], out_p[1], out_p[2], out_p[3], out_s[1], out_s[2], out_s[3])
```

```python
import functools

import jax
import jax.numpy as jnp
from jax import lax
from jax.experimental import pallas as pl
from jax.experimental.pallas import tpu as pltpu

F32 = jnp.float32
BF16 = jnp.bfloat16

D_MODEL = 2048
RW_WIDTH = 1024
RW_HEADS = 16
RW_HEAD_DIM = 64
RW_LORA_W = 64
RW_LORA_A = 64
RW_LORA_G = 160
RW_PROJ = 3 * RW_WIDTH + RW_LORA_W + RW_LORA_A + RW_LORA_G
RW_GN_EPS = 64e-5
GLA_WIDTH = 1024
GLA_HEADS = 4
GLA_KEY_WIDTH = 512
GLA_DK = 128
GLA_DV = 256
GLA_GATE_RANK = 16
GLA_GATE_NORM = 16.0
GLA_CHUNK = 64
GLA_PROJ = 2 * GLA_KEY_WIDTH + GLA_WIDTH + GLA_GATE_RANK + GLA_WIDTH
D_FF = 4 * D_MODEL
NORM_EPS = 1e-6
HEAD_NORM_EPS = 1e-5
DECAY_SCALE = 0.6065306597126334

LANES = 128
SUBLANES = 8

GROUP_PAD = 3456
RW_OFF_XWA = 3 * RW_WIDTH
RW_OFF_XG = RW_OFF_XWA + LANES
GLA_OFF_K = GLA_KEY_WIDTH
GLA_OFF_V = 2 * GLA_KEY_WIDTH
GLA_OFF_GATE = GLA_OFF_V + GLA_WIDTH
GLA_OFF_GOUT = GLA_OFF_GATE + LANES
NP = 2 * GROUP_PAD

ROWS = 128
MXU_WIDTH = 256
INPROJ_TN = 3 * MXU_WIDTH
POST_TM = 512
POST_TF = 1024
VMEM_LIMIT = 56 * 1024 * 1024

NN = (((1,), (0,)), ((), ()))
NT = (((1,), (1,)), ((), ()))


def _dot(a, b, dims=NN):
    return lax.dot_general(a, b, dims, preferred_element_type=F32)


def _parts(x, n):
    if x.dtype == BF16:
        return [x]
    out = []
    rem = x
    for i in range(n):
        h = rem.astype(BF16)
        out.append(h)
        if i + 1 < n:
            rem = rem - h.astype(F32)
    return out


def _mm(a, b, pa=1, pb=1, dims=NN):
    pa = 1 if a.dtype == BF16 else pa
    pb = 1 if b.dtype == BF16 else pb
    aa = _parts(a, pa)
    bb = _parts(b, pb)
    n = max(pa, pb)
    acc = None
    for i in range(pa):
        for j in range(pb):
            if i + j < n:
                t = _dot(aa[i], bb[j], dims)
                acc = t if acc is None else acc + t
    return acc


def _softplus(x):
    return jnp.maximum(x, 0.0) + jnp.log(1.0 + jnp.exp(-jnp.abs(x)))


def _sigmoid(x):
    return 0.5 * jnp.tanh(0.5 * x) + 0.5


def _chunk_masks(m, c):
    sh = c.bit_length() - 1
    ri = lax.broadcasted_iota(jnp.int32, (m, m), 0)
    ci = lax.broadcasted_iota(jnp.int32, (m, m), 1)
    same = (ri >> sh) == (ci >> sh)
    return same, ri, ci


def _inproj_kernel(x_ref, g_ref, w_ref, o_ref, h_ref):
    @pl.when(pl.program_id(1) == 0)
    def _():
        x = x_ref[...]
        ms = jnp.mean(x * x, axis=-1, keepdims=True)
        h_ref[...] = (x * lax.rsqrt(ms + NORM_EPS) * g_ref[...]).astype(BF16)

    o_ref[...] = _dot(h_ref[...], w_ref[...], NT)


def _inproj(x2d, g, w_p, tm, tn):
    t = x2d.shape[0]
    return pl.pallas_call(
        _inproj_kernel,
        grid=(t // tm, NP // tn),
        in_specs=[
            pl.BlockSpec((tm, D_MODEL), lambda i, j: (i, 0)),
            pl.BlockSpec((1, D_MODEL), lambda i, j: (0, 0)),
            pl.BlockSpec((tn, D_MODEL), lambda i, j: (j, 0)),
        ],
        out_specs=pl.BlockSpec((tm, tn), lambda i, j: (i, j)),
        out_shape=jax.ShapeDtypeStruct((t, NP), F32),
        scratch_shapes=[pltpu.VMEM((tm, D_MODEL), BF16)],
        compiler_params=pltpu.CompilerParams(
            dimension_semantics=("parallel", "arbitrary"), vmem_limit_bytes=VMEM_LIMIT),
        name="inproj",
    )(x2d, g, w_p)


def _rwkv_kernel(p_ref, sh_ref, s0_ref, mu_ref, wwa_ref, w0_ref, a0_ref, g2_ref, kk_ref, ka_ref,
                 rk_ref, lnw_ref, lnb_ref, e1_ref, e2_ref,
                 o_ref, shout_ref, sout_ref,
                 carry_ref, st_ref, bk_ref, kb_ref, pc_ref, art_ref, vt_ref, arbd_ref, uy_ref, yt_ref,
                 g_ref, bonus_ref, *, nb, tb, c, pp):
    j = pl.program_id(1)
    m_rows = nb * tb
    n_lvl = c.bit_length() - 1
    hd = RW_HEAD_DIM

    @pl.when(j == 0)
    def _():
        for p in range(RW_HEADS // 2):
            st_ref[:, p] = jnp.concatenate([s0_ref[:, 2 * p], s0_ref[:, 2 * p + 1]], axis=-1)
        carry_ref[...] = sh_ref[...]

    yt_ref[...] = jnp.zeros_like(yt_ref)

    def shifted(c0, w):
        pc3 = p_ref[:, :, c0:c0 + w]
        t3 = lax.broadcasted_iota(jnp.int32, pc3.shape, 1)
        prev3 = jnp.where(t3 == 0, carry_ref[:, :, c0:c0 + w], pltpu.roll(pc3, 1, axis=1))
        return (pc3 + (prev3 - pc3) * mu_ref[:, c0:c0 + w]).reshape(m_rows, w)

    slab = shifted(RW_OFF_XWA, LANES)
    lane = lax.broadcasted_iota(jnp.int32, slab.shape, 1)
    lhs = jnp.where(lane < RW_LORA_W, jnp.tanh(slab), slab).astype(BF16)
    sg = _sigmoid(shifted(RW_OFF_XG, 2 * LANES)).astype(BF16)

    same, ri, ci = _chunk_masks(m_rows, c)
    lmask = jnp.where(same & (ci <= ri), 1.0, 0.0).astype(BF16)
    umask = jnp.where(same & (ci > ri), 1.0, 0.0).astype(BF16)
    strict_t = same & (ri < ci)
    incl_t = same & (ri <= ci)

    sw = STRIP
    strips = [slice(c0, c0 + sw) for c0 in range(0, RW_WIDTH, sw)]

    def head_stat(x, cs):
        return _dot(x.astype(BF16), e1_ref[cs, :])

    def head_bcast(s, cs):
        return _mm(s, e2_ref[:, cs], pa=2)

    lws = [_dot(lhs, wwa_ref[:, cs]) for cs in strips]
    las = [_dot(lhs, wwa_ref[:, RW_WIDTH + cs.start:RW_WIDTH + cs.stop]) for cs in strips]
    for cs in strips:
        g_ref[:, cs] = _dot(sg, g2_ref[:, cs])
    logws = [-DECAY_SCALE * _sigmoid(w0_ref[:, cs] + lw) for cs, lw in zip(strips, lws)]
    gcums = [_mm(lmask, logw, pb=2) for logw in logws]
    grevs = [_mm(umask, logw, pb=2) for logw in logws]
    avs = [_sigmoid(a0_ref[:, cs] + la) for cs, la in zip(strips, las)]
    ks = [shifted(RW_WIDTH + cs.start, sw) for cs in strips]
    kkfs = [k * kk_ref[:, cs] for cs, k in zip(strips, ks)]
    rinvs = [lax.rsqrt(jnp.maximum(head_stat(kkf * kkf, cs), 1e-24)) for cs, kkf in zip(strips, kkfs)]
    k2s = [k * (1.0 + (a - 1.0) * ka_ref[:, cs]) for cs, k, a in zip(strips, ks, avs)]
    rs = [shifted(cs.start, sw) for cs in strips]
    bstats = [head_stat(r * k2 * rk_ref[:, cs], cs) for cs, r, k2 in zip(strips, rs, k2s)]
    kks = [kkf * head_bcast(rinv, cs) for cs, kkf, rinv in zip(strips, kkfs, rinvs)]
    for cs, bstat in zip(strips, bstats):
        v = shifted(2 * RW_WIDTH + cs.start, sw)
        bonus_ref[:, cs] = head_bcast(bstat, cs) * v
        vt_ref[cs, :] = v.T
    for cs, logw, gcum, grev, a, kk, k2, r in zip(strips, logws, gcums, grevs, avs, kks, k2s, rs):
        beta = kk * a
        e_inv = jnp.exp(-gcum)
        e_rev = jnp.exp(grev)
        art_ref[cs, 0:m_rows] = (-kk * jnp.exp(gcum - logw)).T
        art_ref[cs, m_rows:2 * m_rows] = (r * jnp.exp(gcum)).T
        bk_ref[0:m_rows, cs] = beta * e_inv
        bk_ref[m_rows:, cs] = k2 * e_inv
        kb_ref[0:m_rows, cs] = k2 * e_rev
        kb_ref[m_rows:, cs] = beta * e_rev
        pc_ref[:, cs] = jnp.exp(gcum + grev)

    last = p_ref[:, tb - 1:tb, :]
    carry_ref[...] = last
    shout_ref[...] = last

    pw = 2 * hd
    cat = jnp.concatenate
    top = lax.broadcasted_iota(jnp.int32, (pw, 1), 0) < hd
    left = lax.broadcasted_iota(jnp.int32, (1, 2 * m_rows), 1) < m_rows

    def split_rows(x):
        return cat([jnp.where(top, x, 0.0), jnp.where(top, 0.0, x)], axis=1)

    def block_diag(xc):
        return cat([jnp.where(left, xc, 0.0), jnp.where(left, 0.0, xc)], axis=0)

    def blk(xt, u, rh, ch):
        col = (2 * u + ch) * m_rows
        return xt[rh * m_rows:(rh + 1) * m_rows, col:col + m_rows]

    def pair_group(pairs):
        los = [p * pw for p in pairs]
        arts = [art_ref[lo:lo + pw, :] for lo in los]
        xts = [_mm(bk_ref[:, lo:lo + pw], split_rows(art), pp, pp) for lo, art in zip(los, arts)]
        npcs = [cat([jnp.where(strict_t, blk(xt, 0, 0, 0), 0.0), jnp.where(strict_t, blk(xt, 1, 0, 0), 0.0)], axis=1)
                for xt in xts]
        vas = [_mm(split_rows(vt_ref[lo:lo + pw, :]),
                   cat([cat([jnp.where(strict_t, blk(xt, u, 1, 0), 0.0), jnp.where(incl_t, blk(xt, u, 1, 1), 0.0)],
                            axis=1) for u in range(2)], axis=0), pp, pp)
               for lo, xt in zip(los, xts)]
        zcs = [cat([cat([art[u * hd:(u + 1) * hd, 0:m_rows], va[u * hd:(u + 1) * hd, 0:m_rows]], axis=0)
                    for u in range(2)], axis=1) for art, va in zip(arts, vas)]
        for lvl in range(n_lvl):
            bds = [block_diag(npc) for npc in npcs]
            if lvl + 1 < n_lvl:
                ts = [_mm(cat([zc, npc], axis=0), bd, pp, pp) for zc, npc, bd in zip(zcs, npcs, bds)]
                zcs = [zc + t[0:2 * hd] for zc, t in zip(zcs, ts)]
                npcs = [t[2 * hd:] for t in ts]
            else:
                zcs = [zc + _mm(zc, bd, pp, pp) for zc, bd in zip(zcs, bds)]
        rycs = [cat([cat([art[u * hd:(u + 1) * hd, m_rows:], va[u * hd:(u + 1) * hd, m_rows:]], axis=0)
                     for u in range(2)], axis=1)
                + _mm(zc, block_diag(cat([jnp.where(incl_t, blk(xt, 0, 0, 1), 0.0),
                                          jnp.where(incl_t, blk(xt, 1, 0, 1), 0.0)], axis=1)), pp, pp)
                for art, va, zc, xt in zip(arts, vas, zcs, xts)]
        zero = jnp.zeros((hd, 2 * m_rows), BF16)
        for p, zc, ry in zip(pairs, zcs, rycs):
            arbd_ref[p, 0:hd, 0:2 * m_rows] = cat([zc[0:hd, 0:m_rows], ry[0:hd, 0:m_rows]], axis=1).astype(BF16)
            arbd_ref[p, 0:hd, 2 * m_rows:] = zero
            arbd_ref[p, hd:, 0:2 * m_rows] = zero
            arbd_ref[p, hd:, 2 * m_rows:] = cat([zc[0:hd, m_rows:], ry[0:hd, m_rows:]], axis=1).astype(BF16)
            uy_ref[p] = cat([zc[hd:, 0:m_rows], ry[hd:, 0:m_rows], zc[hd:, m_rows:], ry[hd:, m_rows:]], axis=1)

    n_pairs = RW_HEADS // 2
    for g0 in range(0, n_pairs, PAIR_GROUP):
        pair_group(list(range(g0, g0 + PAIR_GROUP)))

    sh_c = c.bit_length() - 1
    cpb = tb // c
    row_id2 = (lax.broadcasted_iota(jnp.int32, (2 * m_rows, pw), 0) & (m_rows - 1)) >> sh_c
    col_id = lax.broadcasted_iota(jnp.int32, (hd, m_rows), 1) >> sh_c
    lane_lo = lax.broadcasted_iota(jnp.int32, (1, pw), 1) < hd

    def chunk_body(i, carry):
        b = i // cpb
        r0 = pl.multiple_of(i * c, c)
        rmask2 = row_id2 == i
        cmask = col_id == i
        prs = range(n_pairs)
        ss = [st_ref[b, p] for p in prs]
        zss = [_mm(s, arbd_ref[p], pp, pp) + uy_ref[p] for p, s in zip(prs, ss)]
        upd = []
        for p, zs in zip(prs, zss):
            lo = p * pw
            kbz = jnp.where(rmask2, kb_ref[:, lo:lo + pw], 0.0)
            rhs = cat([jnp.where(lane_lo, kbz, 0.0), jnp.where(lane_lo, 0.0, kbz)], axis=0)
            vu = cat([vt_ref[lo:lo + hd, :], zs[:, 0:m_rows],
                      vt_ref[lo + hd:lo + pw, :], zs[:, 2 * m_rows:3 * m_rows]], axis=1)
            upd.append(_mm(vu, rhs, pp, pp))
        for p, s, zs, d in zip(prs, ss, zss, upd):
            lo = p * pw
            st_ref[b, p] = s * pc_ref[pl.ds(r0, 1), lo:lo + pw] + d
            for u in range(2):
                rows = slice((2 * p + u) * hd, (2 * p + u + 1) * hd)
                yt_ref[rows, :] = jnp.where(cmask, zs[:, (2 * u + 1) * m_rows:(2 * u + 2) * m_rows], yt_ref[rows, :])
        return carry

    lax.fori_loop(0, m_rows // c, chunk_body, 0)

    @pl.when(j == pl.num_programs(1) - 1)
    def _():
        for p in range(n_pairs):
            sp = st_ref[:, p]
            sout_ref[:, 2 * p] = sp[:, :, 0:hd]
            sout_ref[:, 2 * p + 1] = sp[:, :, hd:]

    inv_n = 1.0 / hd
    ys = [yt_ref[cs, :].T for cs in strips]
    means = [head_stat(y, cs) * inv_n for cs, y in zip(strips, ys)]
    ycs = [y - head_bcast(mean, cs) for cs, y, mean in zip(strips, ys, means)]
    rstds = [lax.rsqrt(head_stat(yc * yc, cs) * inv_n + RW_GN_EPS) for cs, yc in zip(strips, ycs)]
    for cs, yc, rstd in zip(strips, ycs, rstds):
        yn = yc * head_bcast(rstd, cs) * lnw_ref[:, cs] + lnb_ref[:, cs]
        o_ref[:, :, cs] = ((yn + bonus_ref[:, cs]) * g_ref[:, cs]).reshape(nb, tb, sw)


def _rwkv(p, shift_prev, s0, wts, nb, tb, c, pp):
    bsz, seq, _ = p.shape
    m_rows = nb * tb
    hd = RW_HEAD_DIM
    const = lambda shape: pl.BlockSpec(shape, lambda i, j: (0,) * len(shape))
    kern = functools.partial(_rwkv_kernel, nb=nb, tb=tb, c=c, pp=pp)
    return pl.pallas_call(
        kern,
        grid=(bsz // nb, seq // tb),
        in_specs=[
            pl.BlockSpec((nb, tb, GROUP_PAD), lambda i, j: (i, j, 0)),
            pl.BlockSpec((nb, 1, GROUP_PAD), lambda i, j: (i, 0, 0)),
            pl.BlockSpec((nb, RW_HEADS, hd, hd), lambda i, j: (i, 0, 0, 0)),
            const((1, GROUP_PAD)),
            const((LANES, 2 * RW_WIDTH)),
            const((1, RW_WIDTH)),
            const((1, RW_WIDTH)),
            const((2 * LANES, RW_WIDTH)),
            const((1, RW_WIDTH)),
            const((1, RW_WIDTH)),
            const((1, RW_WIDTH)),
            const((1, RW_WIDTH)),
            const((1, RW_WIDTH)),
            const((RW_WIDTH, LANES)),
            const((LANES, RW_WIDTH)),
        ],
        out_specs=[
            pl.BlockSpec((nb, tb, RW_WIDTH), lambda i, j: (i, j, 0)),
            pl.BlockSpec((nb, 1, GROUP_PAD), lambda i, j: (i, 0, 0)),
            pl.BlockSpec((nb, RW_HEADS, hd, hd), lambda i, j: (i, 0, 0, 0)),
        ],
        out_shape=[
            jax.ShapeDtypeStruct((bsz, seq, RW_WIDTH), F32),
            jax.ShapeDtypeStruct((bsz, 1, GROUP_PAD), F32),
            jax.ShapeDtypeStruct((bsz, RW_HEADS, hd, hd), F32),
        ],
        scratch_shapes=[
            pltpu.VMEM((nb, 1, GROUP_PAD), F32),
            pltpu.VMEM((nb, RW_HEADS // 2, hd, 2 * hd), F32),
            pltpu.VMEM((2 * m_rows, RW_WIDTH), F32),
            pltpu.VMEM((2 * m_rows, RW_WIDTH), F32),
            pltpu.VMEM((m_rows, RW_WIDTH), F32),
            pltpu.VMEM((RW_WIDTH, 2 * m_rows), F32),
            pltpu.VMEM((RW_WIDTH, m_rows), F32),
            pltpu.VMEM((RW_HEADS // 2, 2 * hd, 4 * m_rows), BF16),
            pltpu.VMEM((RW_HEADS // 2, hd, 4 * m_rows), F32),
            pltpu.VMEM((RW_WIDTH, m_rows), F32),
            pltpu.VMEM((m_rows, RW_WIDTH), F32),
            pltpu.VMEM((m_rows, RW_WIDTH), F32),
        ],
        compiler_params=pltpu.CompilerParams(
            dimension_semantics=("parallel", "arbitrary"), vmem_limit_bytes=VMEM_LIMIT),
        name="rwkv7",
    )(p, shift_prev, s0, *wts)


def _gla_kernel(p_ref, s0_ref, gw2_ref, gb_ref, nw_ref, o_ref, st_ref,
                acc_ref, qd_ref, v_ref, kt_ref, et_ref, *, nb, tb, c, pp):
    j = pl.program_id(1)
    m_rows = nb * tb
    dk, dv = GLA_DK, GLA_DV

    @pl.when(j == 0)
    def _():
        st_ref[...] = s0_ref[...]

    p = p_ref[...].reshape(m_rows, GROUP_PAD)
    q = p[:, 0:GLA_KEY_WIDTH] * (dk ** -0.5)
    k = p[:, GLA_OFF_K:GLA_OFF_K + GLA_KEY_WIDTH]
    v = p[:, GLA_OFF_V:GLA_OFF_V + GLA_WIDTH]
    xs = p[:, GLA_OFF_GATE:GLA_OFF_GATE + LANES].astype(BF16)
    gout = p[:, GLA_OFF_GOUT:GLA_OFF_GOUT + GLA_WIDTH]
    gk = -_softplus(-(_dot(xs, gw2_ref[...]) + gb_ref[...])) / GLA_GATE_NORM

    same, ri, ci = _chunk_masks(m_rows, c)
    lmask = jnp.where(same & (ci <= ri), 1.0, 0.0).astype(BF16)
    umask = jnp.where(same & (ci > ri), 1.0, 0.0).astype(BF16)
    causal = same & (ci <= ri)
    gcum = _mm(lmask, gk, pb=3)
    grev = _mm(umask, gk, pb=3)
    qd = q * jnp.exp(gcum)
    kinv = k * jnp.exp(-gcum)
    qd_ref[...] = qd
    v_ref[...] = v
    kt_ref[...] = (k * jnp.exp(grev)).T
    et_ref[...] = jnp.exp(gcum + grev).T

    for h in range(GLA_HEADS):
        a = _mm(qd[:, h * dk:(h + 1) * dk], kinv[:, h * dk:(h + 1) * dk], pp, pp, NT)
        a = jnp.where(causal, a, 0.0)
        acc_ref[:, h * dv:(h + 1) * dv] = _mm(a, v[:, h * dv:(h + 1) * dv], pp, pp)

    sh_c = c.bit_length() - 1
    cpb = tb // c
    row_id = lax.broadcasted_iota(jnp.int32, (m_rows, dv), 0) >> sh_c
    for i in range(m_rows // c):
        b = i // cpb
        r0 = i * c
        rmask = row_id == i
        for h in range(GLA_HEADS):
            s = st_ref[b, h]
            acc_ref[r0:r0 + c, h * dv:(h + 1) * dv] += _mm(
                qd_ref[r0:r0 + c, h * dk:(h + 1) * dk], s, pp, pp)
            vz = jnp.where(rmask, v_ref[:, h * dv:(h + 1) * dv], 0.0)
            st_ref[b, h] = (s * et_ref[h * dk:(h + 1) * dk, r0:r0 + 1]
                            + _mm(kt_ref[h * dk:(h + 1) * dk, :], vz, pp, pp))

    for h in range(GLA_HEADS):
        o = acc_ref[:, h * dv:(h + 1) * dv]
        on = o * lax.rsqrt(jnp.mean(o * o, axis=-1, keepdims=True) + HEAD_NORM_EPS) * nw_ref[...]
        gh = gout[:, h * dv:(h + 1) * dv]
        o_ref[:, :, h * dv:(h + 1) * dv] = (on * (gh * _sigmoid(gh))).reshape(nb, tb, dv)


def _gla(p, s0, wts, nb, tb, c, pp):
    bsz, seq, _ = p.shape
    m_rows = nb * tb
    const = lambda shape: pl.BlockSpec(shape, lambda i, j: (0,) * len(shape))
    kern = functools.partial(_gla_kernel, nb=nb, tb=tb, c=c, pp=pp)
    return pl.pallas_call(
        kern,
        grid=(bsz // nb, seq // tb),
        in_specs=[
            pl.BlockSpec((nb, tb, GROUP_PAD), lambda i, j: (i, j, 1)),
            pl.BlockSpec((nb, GLA_HEADS, GLA_DK, GLA_DV), lambda i, j: (i, 0, 0, 0)),
            const((LANES, GLA_KEY_WIDTH)),
            const((1, GLA_KEY_WIDTH)),
            const((1, GLA_DV)),
        ],
        out_specs=[
            pl.BlockSpec((nb, tb, GLA_WIDTH), lambda i, j: (i, j, 0)),
            pl.BlockSpec((nb, GLA_HEADS, GLA_DK, GLA_DV), lambda i, j: (i, 0, 0, 0)),
        ],
        out_shape=[
            jax.ShapeDtypeStruct((bsz, seq, GLA_WIDTH), F32),
            jax.ShapeDtypeStruct((bsz, GLA_HEADS, GLA_DK, GLA_DV), F32),
        ],
        scratch_shapes=[
            pltpu.VMEM((m_rows, GLA_WIDTH), F32),
            pltpu.VMEM((m_rows, GLA_KEY_WIDTH), F32),
            pltpu.VMEM((m_rows, GLA_WIDTH), F32),
            pltpu.VMEM((GLA_KEY_WIDTH, m_rows), F32),
            pltpu.VMEM((GLA_KEY_WIDTH, m_rows), F32),
        ],
        compiler_params=pltpu.CompilerParams(
            dimension_semantics=("parallel", "arbitrary"), vmem_limit_bytes=VMEM_LIMIT),
        name="gla",
    )(p, s0, *wts)


def _post_kernel(x_ref, orw_ref, ogla_ref, wo_ref, g2_ref, wu_ref, wd_ref, gf_ref, o_ref, h_ref):
    jf = pl.program_id(1)

    @pl.when(jf == 0)
    def _():
        x1 = (x_ref[...]
              + _dot(orw_ref[...].astype(BF16), wo_ref[0:RW_WIDTH, :])
              + _dot(ogla_ref[...].astype(BF16), wo_ref[RW_WIDTH:, :]))
        o_ref[...] = x1
        ms = jnp.mean(x1 * x1, axis=-1, keepdims=True)
        h_ref[...] = (x1 * lax.rsqrt(ms + NORM_EPS) * g2_ref[...]).astype(BF16)

    u = jnp.maximum(_dot(h_ref[...], wu_ref[...]), 0.0)
    o_ref[...] += _dot((u * u).astype(BF16), wd_ref[...])

    @pl.when(jf == pl.num_programs(1) - 1)
    def _():
        x2 = o_ref[...]
        ms = jnp.mean(x2 * x2, axis=-1, keepdims=True)
        o_ref[...] = x2 * lax.rsqrt(ms + NORM_EPS) * gf_ref[...]


def _post(x2d, o_rw, o_gla, w_out, g2, w_up, w_down, gf, tm, tf):
    t = x2d.shape[0]
    return pl.pallas_call(
        _post_kernel,
        grid=(t // tm, D_FF // tf),
        in_specs=[
            pl.BlockSpec((tm, D_MODEL), lambda i, j: (i, 0)),
            pl.BlockSpec((tm, RW_WIDTH), lambda i, j: (i, 0)),
            pl.BlockSpec((tm, GLA_WIDTH), lambda i, j: (i, 0)),
            pl.BlockSpec((D_MODEL, D_MODEL), lambda i, j: (0, 0), pipeline_mode=pl.Buffered(1)),
            pl.BlockSpec((1, D_MODEL), lambda i, j: (0, 0)),
            pl.BlockSpec((D_MODEL, tf), lambda i, j: (0, j)),
            pl.BlockSpec((tf, D_MODEL), lambda i, j: (j, 0)),
            pl.BlockSpec((1, D_MODEL), lambda i, j: (0, 0)),
        ],
        out_specs=pl.BlockSpec((tm, D_MODEL), lambda i, j: (i, 0)),
        out_shape=jax.ShapeDtypeStruct((t, D_MODEL), F32),
        scratch_shapes=[pltpu.VMEM((tm, D_MODEL), BF16)],
        compiler_params=pltpu.CompilerParams(
            dimension_semantics=("parallel", "arbitrary"), vmem_limit_bytes=VMEM_LIMIT),
        name="post",
    )(x2d, o_rw, o_gla, w_out, g2, w_up, w_down, gf)


def _pad_cols(w, n):
    return jnp.pad(w, ((0, 0), (0, n - w.shape[1])))


def _prep_weights(w_in, rw_mu, rw_w0, rw_w2, rw_a0, rw_a2, rw_g2, rw_k_k, rw_k_a, rw_r_k, rw_ln_w, rw_ln_b,
                  gla_gw2, gla_gb, gla_norm_w):
    go = RW_PROJ
    w_t = jnp.swapaxes(w_in, 0, 1).astype(BF16)
    pad_rows = lambda w, n: jnp.pad(w, ((0, n - w.shape[0]), (0, 0)))
    w_p = jnp.concatenate([
        pad_rows(w_t[:RW_PROJ], GROUP_PAD),
        pad_rows(w_t[go:go + GLA_OFF_GATE + GLA_GATE_RANK], GLA_OFF_GOUT),
        pad_rows(w_t[go + GLA_OFF_GATE + GLA_GATE_RANK:], GROUP_PAD - GLA_OFF_GOUT),
    ], axis=0)

    row = lambda x: x.reshape(1, -1).astype(F32)
    mu = _pad_cols(row(rw_mu), GROUP_PAD)
    wwa = jnp.zeros((LANES, 2 * RW_WIDTH), F32)
    wwa = wwa.at[0:RW_LORA_W, 0:RW_WIDTH].set(rw_w2)
    wwa = wwa.at[RW_LORA_W:RW_LORA_W + RW_LORA_A, RW_WIDTH:].set(rw_a2)
    g2p = jnp.pad(rw_g2, ((0, 2 * LANES - RW_LORA_G), (0, 0)))
    head_of_col = jnp.arange(RW_WIDTH) // RW_HEAD_DIM
    e1 = (head_of_col[:, None] == jnp.arange(LANES)[None, :]).astype(BF16)
    rw_wts = (mu, wwa.astype(BF16), row(rw_w0), row(rw_a0), g2p.astype(BF16), row(rw_k_k), row(rw_k_a),
              row(rw_r_k), row(rw_ln_w), row(rw_ln_b), e1, e1.T)
    gw2p = jnp.pad(gla_gw2, ((0, LANES - GLA_GATE_RANK), (0, 0))).astype(BF16)
    gla_wts = (gw2p, row(gla_gb), row(gla_norm_w))
    return w_p, rw_wts, gla_wts


RW_PIECES = 1
GLA_PIECES = 1
PAIR_GROUP = 8
STRIP = 256


def _trunk(x, shift, s_rw, s_gla, norm1_g, w_p, rw_wts, gla_wts, w_out, norm2_g, w_up, w_down, norm_f_g,
           nb, tb, c_rw, c_gla, tm):
    bsz, seq, _ = x.shape
    t = bsz * seq
    x2d = x.reshape(t, D_MODEL)
    proj = _inproj(x2d, norm1_g.reshape(1, -1), w_p, tm, INPROJ_TN).reshape(bsz, seq, NP)
    shift_p = _pad_cols(shift, GROUP_PAD).reshape(bsz, 1, GROUP_PAD)
    o_rw, sh_new, s_rw_new = _rwkv(proj, shift_p, s_rw, rw_wts, nb, tb, c_rw, RW_PIECES)
    o_gla, s_gla_new = _gla(proj, s_gla, gla_wts, nb, tb, c_gla, GLA_PIECES)
    y = _post(x2d, o_rw.reshape(t, RW_WIDTH), o_gla.reshape(t, GLA_WIDTH), w_out, norm2_g.reshape(1, -1),
              w_up, w_down, norm_f_g.reshape(1, -1), POST_TM, POST_TF)
    return (y.reshape(bsz, seq, D_MODEL), sh_new[:, 0, :RW_PROJ][None], s_rw_new[None], s_gla_new[None])


def kernel(x_prompt, x_sample, state_rwkv_shift, state_rwkv_wkv, state_gla, norm1_g, w_in, rw_mu, rw_w0,
           rw_w2, rw_a0, rw_a2, rw_g2, rw_k_k, rw_k_a, rw_r_k, rw_ln_w, rw_ln_b, gla_gw2, gla_gb, gla_norm_w,
           w_out, norm2_g, w_up, w_down, norm_f_g):
    w_p, rw_wts, gla_wts = _prep_weights(
        w_in[0], rw_mu[0], rw_w0[0], rw_w2[0], rw_a0[0], rw_a2[0], rw_g2[0], rw_k_k[0], rw_k_a[0],
        rw_r_k[0].reshape(-1), rw_ln_w[0], rw_ln_b[0], gla_gw2[0], gla_gb[0], gla_norm_w[0])
    shared = (norm1_g[0], w_p, rw_wts, gla_wts, w_out[0].astype(BF16), norm2_g[0], w_up[0].astype(BF16),
              w_down[0].astype(BF16), norm_f_g)

    bp, lp, _ = x_prompt.shape
    bs, ls, _ = x_sample.shape
    dt = x_prompt.dtype
    out_p = _trunk(x_prompt, jnp.zeros((bp, RW_PROJ), dt),
                   jnp.zeros((bp, RW_HEADS, RW_HEAD_DIM, RW_HEAD_DIM), dt),
                   jnp.zeros((bp, GLA_HEADS, GLA_DK, GLA_DV), dt), *shared,
                   nb=1, tb=ROWS, c_rw=64, c_gla=GLA_CHUNK, tm=1024)
    out_s = _trunk(x_sample, state_rwkv_shift[0], state_rwkv_wkv[0], state_gla[0], *shared,
                   nb=ROWS // ls, tb=ls, c_rw=ls, c_gla=ls, tm=1024)
    return (out_p[0], out_s[0], out_p[1], out_p[2], out_p[3], out_s[1], out_s[2], out_s[3])
```

```python
import functools

import jax
import jax.numpy as jnp
from jax import lax
from jax.experimental import pallas as pl
from jax.experimental.pallas import tpu as pltpu

F32 = jnp.float32
BF16 = jnp.bfloat16

D_MODEL = 2048
RW_WIDTH = 1024
RW_HEADS = 16
RW_HEAD_DIM = 64
RW_LORA_W = 64
RW_LORA_A = 64
RW_LORA_G = 160
RW_PROJ = 3 * RW_WIDTH + RW_LORA_W + RW_LORA_A + RW_LORA_G
RW_GN_EPS = 64e-5
GLA_WIDTH = 1024
GLA_HEADS = 4
GLA_KEY_WIDTH = 512
GLA_DK = 128
GLA_DV = 256
GLA_GATE_RANK = 16
GLA_GATE_NORM = 16.0
GLA_CHUNK = 64
GLA_PROJ = 2 * GLA_KEY_WIDTH + GLA_WIDTH + GLA_GATE_RANK + GLA_WIDTH
D_FF = 4 * D_MODEL
NORM_EPS = 1e-6
HEAD_NORM_EPS = 1e-5
DECAY_SCALE = 0.6065306597126334

LANES = 128
SUBLANES = 8

GROUP_PAD = 3456
RW_OFF_XWA = 3 * RW_WIDTH
RW_OFF_XG = RW_OFF_XWA + LANES
GLA_OFF_K = GLA_KEY_WIDTH
GLA_OFF_V = 2 * GLA_KEY_WIDTH
GLA_OFF_GATE = GLA_OFF_V + GLA_WIDTH
GLA_OFF_GOUT = GLA_OFF_GATE + LANES
NP = 2 * GROUP_PAD

ROWS = 128
MXU_WIDTH = 256
INPROJ_TN = 3 * MXU_WIDTH
POST_TM = 512
POST_TF = 1024
VMEM_LIMIT = 56 * 1024 * 1024

NN = (((1,), (0,)), ((), ()))
NT = (((1,), (1,)), ((), ()))


def _dot(a, b, dims=NN):
    return lax.dot_general(a, b, dims, preferred_element_type=F32)


def _parts(x, n):
    if x.dtype == BF16:
        return [x]
    out = []
    rem = x
    for i in range(n):
        h = rem.astype(BF16)
        out.append(h)
        if i + 1 < n:
            rem = rem - h.astype(F32)
    return out


def _mm(a, b, pa=1, pb=1, dims=NN):
    pa = 1 if a.dtype == BF16 else pa
    pb = 1 if b.dtype == BF16 else pb
    aa = _parts(a, pa)
    bb = _parts(b, pb)
    n = max(pa, pb)
    acc = None
    for i in range(pa):
        for j in range(pb):
            if i + j < n:
                t = _dot(aa[i], bb[j], dims)
                acc = t if acc is None else acc + t
    return acc


def _softplus(x):
    return jnp.maximum(x, 0.0) + jnp.log(1.0 + jnp.exp(-jnp.abs(x)))


def _sigmoid(x):
    return 0.5 * jnp.tanh(0.5 * x) + 0.5


def _chunk_masks(m, c):
    sh = c.bit_length() - 1
    ri = lax.broadcasted_iota(jnp.int32, (m, m), 0)
    ci = lax.broadcasted_iota(jnp.int32, (m, m), 1)
    same = (ri >> sh) == (ci >> sh)
    return same, ri, ci


def _inproj_kernel(x_ref, g_ref, w_ref, o_ref, h_ref):
    @pl.when(pl.program_id(1) == 0)
    def _():
        x = x_ref[...]
        ms = jnp.mean(x * x, axis=-1, keepdims=True)
        h_ref[...] = (x * lax.rsqrt(ms + NORM_EPS) * g_ref[...]).astype(BF16)

    o_ref[...] = _dot(h_ref[...], w_ref[...], NT)


def _inproj(x2d, g, w_p, tm, tn):
    t = x2d.shape[0]
    return pl.pallas_call(
        _inproj_kernel,
        grid=(t // tm, NP // tn),
        in_specs=[
            pl.BlockSpec((tm, D_MODEL), lambda i, j: (i, 0)),
            pl.BlockSpec((1, D_MODEL), lambda i, j: (0, 0)),
            pl.BlockSpec((tn, D_MODEL), lambda i, j: (j, 0)),
        ],
        out_specs=pl.BlockSpec((tm, tn), lambda i, j: (i, j)),
        out_shape=jax.ShapeDtypeStruct((t, NP), F32),
        scratch_shapes=[pltpu.VMEM((tm, D_MODEL), BF16)],
        compiler_params=pltpu.CompilerParams(
            dimension_semantics=("parallel", "arbitrary"), vmem_limit_bytes=VMEM_LIMIT),
        name="inproj",
    )(x2d, g, w_p)


def _rwkv_kernel(p_ref, sh_ref, s0_ref, mu_ref, wwa_ref, w0_ref, a0_ref, g2_ref, kk_ref, ka_ref,
                 rk_ref, lnw_ref, lnb_ref, e1_ref, e2_ref,
                 o_ref, shout_ref, sout_ref,
                 carry_ref, st_ref, bk_ref, kb_ref, pc_ref, art_ref, vt_ref, arbd_ref, uy_ref, yt_ref,
                 g_ref, bonus_ref, *, nb, tb, c, pp):
    j = pl.program_id(1)
    m_rows = ROWS
    nsub = nb * tb // m_rows
    tbs = tb // nsub
    n_lvl = c.bit_length() - 1
    hd = RW_HEAD_DIM

    @pl.when(j == 0)
    def _():
        for p in range(RW_HEADS // 2):
            st_ref[:, p] = jnp.concatenate([s0_ref[:, 2 * p], s0_ref[:, 2 * p + 1]], axis=-1)
        carry_ref[...] = sh_ref[...]

    yt_ref[...] = jnp.zeros_like(yt_ref)

    def shifted(sb, c0, w):
        t0 = sb * tbs
        pc3 = p_ref[:, t0:t0 + tbs, c0:c0 + w]
        before = carry_ref[:, :, c0:c0 + w] if sb == 0 else p_ref[:, t0 - 1:t0, c0:c0 + w]
        t3 = lax.broadcasted_iota(jnp.int32, pc3.shape, 1)
        prev3 = jnp.where(t3 == 0, before, pltpu.roll(pc3, 1, axis=1))
        return (pc3 + (prev3 - pc3) * mu_ref[:, c0:c0 + w]).reshape(m_rows, w)

    same, ri, ci = _chunk_masks(m_rows, c)
    lmask = jnp.where(same & (ci <= ri), 1.0, 0.0).astype(BF16)
    umask = jnp.where(same & (ci > ri), 1.0, 0.0).astype(BF16)
    strict_t = same & (ri < ci)
    incl_t = same & (ri <= ci)

    sw = STRIP
    strips = [slice(c0, c0 + sw) for c0 in range(0, RW_WIDTH, sw)]

    def head_stat(x, cs):
        return _dot(x.astype(BF16), e1_ref[cs, :])

    def head_bcast(s, cs):
        return _mm(s, e2_ref[:, cs], pa=2)


    def prologue(sb):
        slab = shifted(sb, RW_OFF_XWA, LANES)
        lane = lax.broadcasted_iota(jnp.int32, slab.shape, 1)
        lhs = jnp.where(lane < RW_LORA_W, jnp.tanh(slab), slab).astype(BF16)
        sg = _sigmoid(shifted(sb, RW_OFF_XG, 2 * LANES)).astype(BF16)
        yield
        lws = [_dot(lhs, wwa_ref[:, cs]) for cs in strips]
        las = [_dot(lhs, wwa_ref[:, RW_WIDTH + cs.start:RW_WIDTH + cs.stop]) for cs in strips]
        for cs in strips:
            g_ref[sb, :, cs] = _dot(sg, g2_ref[:, cs])
        yield
        logws = [-DECAY_SCALE * _sigmoid(w0_ref[:, cs] + lw) for cs, lw in zip(strips, lws)]
        gcums = [_mm(lmask, logw, pb=2) for logw in logws]
        grevs = [_mm(umask, logw, pb=2) for logw in logws]
        yield
        avs = [_sigmoid(a0_ref[:, cs] + la) for cs, la in zip(strips, las)]
        ks = [shifted(sb, RW_WIDTH + cs.start, sw) for cs in strips]
        kkfs = [k * kk_ref[:, cs] for cs, k in zip(strips, ks)]
        rinvs = [lax.rsqrt(jnp.maximum(head_stat(kkf * kkf, cs), 1e-24)) for cs, kkf in zip(strips, kkfs)]
        yield
        k2s = [k * (1.0 + (a - 1.0) * ka_ref[:, cs]) for cs, k, a in zip(strips, ks, avs)]
        rs = [shifted(sb, cs.start, sw) for cs in strips]
        bstats = [head_stat(r * k2 * rk_ref[:, cs], cs) for cs, r, k2 in zip(strips, rs, k2s)]
        kks = [kkf * head_bcast(rinv, cs) for cs, kkf, rinv in zip(strips, kkfs, rinvs)]
        yield
        for cs, bstat in zip(strips, bstats):
            v = shifted(sb, 2 * RW_WIDTH + cs.start, sw)
            bonus_ref[sb, :, cs] = head_bcast(bstat, cs) * v
            vt_ref[sb, cs, :] = v.T
        yield
        for cs, logw, gcum, grev, a, kk, k2, r in zip(strips, logws, gcums, grevs, avs, kks, k2s, rs):
            beta = kk * a
            e_inv = jnp.exp(-gcum)
            e_rev = jnp.exp(grev)
            art_ref[sb, cs, 0:m_rows] = (-kk * jnp.exp(gcum - logw)).T
            art_ref[sb, cs, m_rows:2 * m_rows] = (r * jnp.exp(gcum)).T
            bk_ref[sb, 0:m_rows, cs] = beta * e_inv
            bk_ref[sb, m_rows:, cs] = k2 * e_inv
            kb_ref[sb, 0:m_rows, cs] = k2 * e_rev
            kb_ref[sb, m_rows:, cs] = beta * e_rev
            pc_ref[sb, :, cs] = jnp.exp(gcum + grev)
            yield

    pw = 2 * hd
    cat = jnp.concatenate
    top = lax.broadcasted_iota(jnp.int32, (pw, 1), 0) < hd
    left = lax.broadcasted_iota(jnp.int32, (1, 2 * m_rows), 1) < m_rows

    def split_rows(x):
        return cat([jnp.where(top, x, 0.0), jnp.where(top, 0.0, x)], axis=1)

    def block_diag(xc):
        return cat([jnp.where(left, xc, 0.0), jnp.where(left, 0.0, xc)], axis=0)

    def blk(xt, u, rh, ch):
        col = (2 * u + ch) * m_rows
        return xt[rh * m_rows:(rh + 1) * m_rows, col:col + m_rows]

    n_pairs = RW_HEADS // 2
    pairs = range(n_pairs)
    los = [p * pw for p in pairs]

    def pair_phase(sb):
        arts = [art_ref[sb, lo:lo + pw, :] for lo in los]
        xts = [_mm(bk_ref[sb, :, lo:lo + pw], split_rows(art), pp, pp) for lo, art in zip(los, arts)]
        yield
        npcs = [cat([jnp.where(strict_t, blk(xt, 0, 0, 0), 0.0), jnp.where(strict_t, blk(xt, 1, 0, 0), 0.0)], axis=1)
                for xt in xts]
        vas = [_mm(split_rows(vt_ref[sb, lo:lo + pw, :]),
                   cat([cat([jnp.where(strict_t, blk(xt, u, 1, 0), 0.0), jnp.where(incl_t, blk(xt, u, 1, 1), 0.0)],
                            axis=1) for u in range(2)], axis=0), pp, pp)
               for lo, xt in zip(los, xts)]
        yield
        zcs = [cat([cat([art[u * hd:(u + 1) * hd, 0:m_rows], va[u * hd:(u + 1) * hd, 0:m_rows]], axis=0)
                    for u in range(2)], axis=1) for art, va in zip(arts, vas)]
        for lvl in range(n_lvl):
            bds = [block_diag(npc) for npc in npcs]
            if lvl + 1 < n_lvl:
                ts = [_mm(cat([zc, npc], axis=0), bd, pp, pp) for zc, npc, bd in zip(zcs, npcs, bds)]
                zcs = [zc + t[0:2 * hd] for zc, t in zip(zcs, ts)]
                npcs = [t[2 * hd:] for t in ts]
            else:
                zcs = [zc + _mm(zc, bd, pp, pp) for zc, bd in zip(zcs, bds)]
            yield
        rycs = [cat([cat([art[u * hd:(u + 1) * hd, m_rows:], va[u * hd:(u + 1) * hd, m_rows:]], axis=0)
                     for u in range(2)], axis=1)
                + _mm(zc, block_diag(cat([jnp.where(incl_t, blk(xt, 0, 0, 1), 0.0),
                                          jnp.where(incl_t, blk(xt, 1, 0, 1), 0.0)], axis=1)), pp, pp)
                for art, va, zc, xt in zip(arts, vas, zcs, xts)]
        yield
        zero = jnp.zeros((hd, 2 * m_rows), BF16)
        for p, zc, ry in zip(pairs, zcs, rycs):
            arbd_ref[sb, p, 0:hd, 0:2 * m_rows] = cat([zc[0:hd, 0:m_rows], ry[0:hd, 0:m_rows]], axis=1).astype(BF16)
            arbd_ref[sb, p, 0:hd, 2 * m_rows:] = zero
            arbd_ref[sb, p, hd:, 0:2 * m_rows] = zero
            arbd_ref[sb, p, hd:, 2 * m_rows:] = cat([zc[0:hd, m_rows:], ry[0:hd, m_rows:]], axis=1).astype(BF16)
            uy_ref[sb, p] = cat([zc[hd:, 0:m_rows], ry[hd:, 0:m_rows], zc[hd:, m_rows:], ry[hd:, m_rows:]], axis=1)
        yield

    sh_c = c.bit_length() - 1
    cps = m_rows // c
    cpb = tbs // c
    row_id2 = (lax.broadcasted_iota(jnp.int32, (2 * m_rows, pw), 0) & (m_rows - 1)) >> sh_c
    col_id = lax.broadcasted_iota(jnp.int32, (hd, m_rows), 1) >> sh_c
    lane_lo = lax.broadcasted_iota(jnp.int32, (1, pw), 1) < hd

    def chunk_step(sb, i):
        b = i // cpb
        r0 = i * c
        rmask2 = row_id2 == i
        cmask = col_id == i
        ss = [st_ref[b, p] for p in pairs]
        zss = [_mm(s, arbd_ref[sb, p], pp, pp) + uy_ref[sb, p] for p, s in zip(pairs, ss)]
        yield
        upd = []
        for lo, zs in zip(los, zss):
            kbz = jnp.where(rmask2, kb_ref[sb, :, lo:lo + pw], 0.0)
            rhs = cat([jnp.where(lane_lo, kbz, 0.0), jnp.where(lane_lo, 0.0, kbz)], axis=0)
            vu = cat([vt_ref[sb, lo:lo + hd, :], zs[:, 0:m_rows],
                      vt_ref[sb, lo + hd:lo + pw, :], zs[:, 2 * m_rows:3 * m_rows]], axis=1)
            upd.append(_mm(vu, rhs, pp, pp))
        yield
        for p, lo, s, zs, d in zip(pairs, los, ss, zss, upd):
            st_ref[b, p] = s * pc_ref[sb, pl.ds(r0, 1), lo:lo + pw] + d
            for u in range(2):
                rows = slice((2 * p + u) * hd, (2 * p + u + 1) * hd)
                yt_ref[sb, rows, :] = jnp.where(cmask, zs[:, (2 * u + 1) * m_rows:(2 * u + 2) * m_rows],
                                               yt_ref[sb, rows, :])
        yield

    inv_n = 1.0 / hd

    def epilogue(sb):
        ys = [yt_ref[sb, cs, :].T for cs in strips]
        means = [head_stat(y, cs) * inv_n for cs, y in zip(strips, ys)]
        yield
        ycs = [y - head_bcast(mean, cs) for cs, y, mean in zip(strips, ys, means)]
        yield
        rstds = [lax.rsqrt(head_stat(yc * yc, cs) * inv_n + RW_GN_EPS) for cs, yc in zip(strips, ycs)]
        yield
        for cs, yc, rstd in zip(strips, ycs, rstds):
            yn = yc * head_bcast(rstd, cs) * lnw_ref[:, cs] + lnb_ref[:, cs]
            o_ref[:, sb * tbs:(sb + 1) * tbs, cs] = (
                (yn + bonus_ref[sb, :, cs]) * g_ref[sb, :, cs]).reshape(nb, tbs, sw)
        yield

    def run(*gens):
        live = list(gens)
        while live:
            live = [g for g in live if next(g, live) is not live]

    def tail(sb):
        for i in range(cps):
            yield from chunk_step(sb, i)
        yield from epilogue(sb)

    if nsub == 1:
        run(prologue(0))
        run(pair_phase(0))

        def chunk_body(i, carry):
            run(chunk_step(0, i))
            return carry

        lax.fori_loop(0, cps, chunk_body, 0)
        run(epilogue(0))
    else:
        run(prologue(0))
        for sb in range(nsub):
            run(pair_phase(sb), *([prologue(sb + 1)] if sb + 1 < nsub else []), *([tail(sb - 1)] if sb else []))
        run(tail(nsub - 1))

    last = p_ref[:, tb - 1:tb, :]
    carry_ref[...] = last
    shout_ref[...] = last

    @pl.when(j == pl.num_programs(1) - 1)
    def _():
        for p in pairs:
            sp = st_ref[:, p]
            sout_ref[:, 2 * p] = sp[:, :, 0:hd]
            sout_ref[:, 2 * p + 1] = sp[:, :, hd:]


def _rwkv(p, shift_prev, s0, wts, nb, tb, c, pp):
    bsz, seq, _ = p.shape
    m_rows = ROWS
    nsub = nb * tb // m_rows
    assert nsub * m_rows == nb * tb and (nsub == 1 or nb == 1)
    hd = RW_HEAD_DIM
    const = lambda shape: pl.BlockSpec(shape, lambda i, j: (0,) * len(shape))
    kern = functools.partial(_rwkv_kernel, nb=nb, tb=tb, c=c, pp=pp)
    return pl.pallas_call(
        kern,
        grid=(bsz // nb, seq // tb),
        in_specs=[
            pl.BlockSpec((nb, tb, GROUP_PAD), lambda i, j: (i, j, 0)),
            pl.BlockSpec((nb, 1, GROUP_PAD), lambda i, j: (i, 0, 0)),
            pl.BlockSpec((nb, RW_HEADS, hd, hd), lambda i, j: (i, 0, 0, 0)),
            const((1, GROUP_PAD)),
            const((LANES, 2 * RW_WIDTH)),
            const((1, RW_WIDTH)),
            const((1, RW_WIDTH)),
            const((2 * LANES, RW_WIDTH)),
            const((1, RW_WIDTH)),
            const((1, RW_WIDTH)),
            const((1, RW_WIDTH)),
            const((1, RW_WIDTH)),
            const((1, RW_WIDTH)),
            const((RW_WIDTH, LANES)),
            const((LANES, RW_WIDTH)),
        ],
        out_specs=[
            pl.BlockSpec((nb, tb, RW_WIDTH), lambda i, j: (i, j, 0)),
            pl.BlockSpec((nb, 1, GROUP_PAD), lambda i, j: (i, 0, 0)),
            pl.BlockSpec((nb, RW_HEADS, hd, hd), lambda i, j: (i, 0, 0, 0)),
        ],
        out_shape=[
            jax.ShapeDtypeStruct((bsz, seq, RW_WIDTH), F32),
            jax.ShapeDtypeStruct((bsz, 1, GROUP_PAD), F32),
            jax.ShapeDtypeStruct((bsz, RW_HEADS, hd, hd), F32),
        ],
        scratch_shapes=[
            pltpu.VMEM((nb, 1, GROUP_PAD), F32),
            pltpu.VMEM((nb, RW_HEADS // 2, hd, 2 * hd), F32),
            pltpu.VMEM((nsub, 2 * m_rows, RW_WIDTH), F32),
            pltpu.VMEM((nsub, 2 * m_rows, RW_WIDTH), F32),
            pltpu.VMEM((nsub, m_rows, RW_WIDTH), F32),
            pltpu.VMEM((nsub, RW_WIDTH, 2 * m_rows), F32),
            pltpu.VMEM((nsub, RW_WIDTH, m_rows), F32),
            pltpu.VMEM((nsub, RW_HEADS // 2, 2 * hd, 4 * m_rows), BF16),
            pltpu.VMEM((nsub, RW_HEADS // 2, hd, 4 * m_rows), F32),
            pltpu.VMEM((nsub, RW_WIDTH, m_rows), F32),
            pltpu.VMEM((nsub, m_rows, RW_WIDTH), F32),
            pltpu.VMEM((nsub, m_rows, RW_WIDTH), F32),
        ],
        compiler_params=pltpu.CompilerParams(
            dimension_semantics=("parallel", "arbitrary"), vmem_limit_bytes=VMEM_LIMIT),
        name="rwkv7",
    )(p, shift_prev, s0, *wts)


def _gla_kernel(p_ref, s0_ref, gw2_ref, gb_ref, nw_ref, o_ref, st_ref,
                acc_ref, qd_ref, v_ref, kt_ref, et_ref, *, nb, tb, c, pp):
    j = pl.program_id(1)
    m_rows = nb * tb
    dk, dv = GLA_DK, GLA_DV

    @pl.when(j == 0)
    def _():
        st_ref[...] = s0_ref[...]

    p = p_ref[...].reshape(m_rows, GROUP_PAD)
    q = p[:, 0:GLA_KEY_WIDTH] * (dk ** -0.5)
    k = p[:, GLA_OFF_K:GLA_OFF_K + GLA_KEY_WIDTH]
    v = p[:, GLA_OFF_V:GLA_OFF_V + GLA_WIDTH]
    xs = p[:, GLA_OFF_GATE:GLA_OFF_GATE + LANES].astype(BF16)
    gout = p[:, GLA_OFF_GOUT:GLA_OFF_GOUT + GLA_WIDTH]
    gk = -_softplus(-(_dot(xs, gw2_ref[...]) + gb_ref[...])) / GLA_GATE_NORM

    same, ri, ci = _chunk_masks(m_rows, c)
    lmask = jnp.where(same & (ci <= ri), 1.0, 0.0).astype(BF16)
    umask = jnp.where(same & (ci > ri), 1.0, 0.0).astype(BF16)
    causal = same & (ci <= ri)
    gcum = _mm(lmask, gk, pb=3)
    grev = _mm(umask, gk, pb=3)
    qd = q * jnp.exp(gcum)
    kinv = k * jnp.exp(-gcum)
    qd_ref[...] = qd
    v_ref[...] = v
    kt_ref[...] = (k * jnp.exp(grev)).T
    et_ref[...] = jnp.exp(gcum + grev).T

    for h in range(GLA_HEADS):
        a = _mm(qd[:, h * dk:(h + 1) * dk], kinv[:, h * dk:(h + 1) * dk], pp, pp, NT)
        a = jnp.where(causal, a, 0.0)
        acc_ref[:, h * dv:(h + 1) * dv] = _mm(a, v[:, h * dv:(h + 1) * dv], pp, pp)

    sh_c = c.bit_length() - 1
    cpb = tb // c
    row_id = lax.broadcasted_iota(jnp.int32, (m_rows, dv), 0) >> sh_c
    for i in range(m_rows // c):
        b = i // cpb
        r0 = i * c
        rmask = row_id == i
        for h in range(GLA_HEADS):
            s = st_ref[b, h]
            acc_ref[r0:r0 + c, h * dv:(h + 1) * dv] += _mm(
                qd_ref[r0:r0 + c, h * dk:(h + 1) * dk], s, pp, pp)
            vz = jnp.where(rmask, v_ref[:, h * dv:(h + 1) * dv], 0.0)
            st_ref[b, h] = (s * et_ref[h * dk:(h + 1) * dk, r0:r0 + 1]
                            + _mm(kt_ref[h * dk:(h + 1) * dk, :], vz, pp, pp))

    for h in range(GLA_HEADS):
        o = acc_ref[:, h * dv:(h + 1) * dv]
        on = o * lax.rsqrt(jnp.mean(o * o, axis=-1, keepdims=True) + HEAD_NORM_EPS) * nw_ref[...]
        gh = gout[:, h * dv:(h + 1) * dv]
        o_ref[:, :, h * dv:(h + 1) * dv] = (on * (gh * _sigmoid(gh))).reshape(nb, tb, dv)


def _gla(p, s0, wts, nb, tb, c, pp):
    bsz, seq, _ = p.shape
    m_rows = nb * tb
    const = lambda shape: pl.BlockSpec(shape, lambda i, j: (0,) * len(shape))
    kern = functools.partial(_gla_kernel, nb=nb, tb=tb, c=c, pp=pp)
    return pl.pallas_call(
        kern,
        grid=(bsz // nb, seq // tb),
        in_specs=[
            pl.BlockSpec((nb, tb, GROUP_PAD), lambda i, j: (i, j, 1)),
            pl.BlockSpec((nb, GLA_HEADS, GLA_DK, GLA_DV), lambda i, j: (i, 0, 0, 0)),
            const((LANES, GLA_KEY_WIDTH)),
            const((1, GLA_KEY_WIDTH)),
            const((1, GLA_DV)),
        ],
        out_specs=[
            pl.BlockSpec((nb, tb, GLA_WIDTH), lambda i, j: (i, j, 0)),
            pl.BlockSpec((nb, GLA_HEADS, GLA_DK, GLA_DV), lambda i, j: (i, 0, 0, 0)),
        ],
        out_shape=[
            jax.ShapeDtypeStruct((bsz, seq, GLA_WIDTH), F32),
            jax.ShapeDtypeStruct((bsz, GLA_HEADS, GLA_DK, GLA_DV), F32),
        ],
        scratch_shapes=[
            pltpu.VMEM((m_rows, GLA_WIDTH), F32),
            pltpu.VMEM((m_rows, GLA_KEY_WIDTH), F32),
            pltpu.VMEM((m_rows, GLA_WIDTH), F32),
            pltpu.VMEM((GLA_KEY_WIDTH, m_rows), F32),
            pltpu.VMEM((GLA_KEY_WIDTH, m_rows), F32),
        ],
        compiler_params=pltpu.CompilerParams(
            dimension_semantics=("parallel", "arbitrary"), vmem_limit_bytes=VMEM_LIMIT),
        name="gla",
    )(p, s0, *wts)


def _post_kernel(x_ref, orw_ref, ogla_ref, wo_ref, g2_ref, wu_ref, wd_ref, gf_ref, o_ref, h_ref):
    jf = pl.program_id(1)

    @pl.when(jf == 0)
    def _():
        x1 = (x_ref[...]
              + _dot(orw_ref[...].astype(BF16), wo_ref[0:RW_WIDTH, :])
              + _dot(ogla_ref[...].astype(BF16), wo_ref[RW_WIDTH:, :]))
        o_ref[...] = x1
        ms = jnp.mean(x1 * x1, axis=-1, keepdims=True)
        h_ref[...] = (x1 * lax.rsqrt(ms + NORM_EPS) * g2_ref[...]).astype(BF16)

    u = jnp.maximum(_dot(h_ref[...], wu_ref[...]), 0.0)
    o_ref[...] += _dot((u * u).astype(BF16), wd_ref[...])

    @pl.when(jf == pl.num_programs(1) - 1)
    def _():
        x2 = o_ref[...]
        ms = jnp.mean(x2 * x2, axis=-1, keepdims=True)
        o_ref[...] = x2 * lax.rsqrt(ms + NORM_EPS) * gf_ref[...]


def _post(x2d, o_rw, o_gla, w_out, g2, w_up, w_down, gf, tm, tf):
    t = x2d.shape[0]
    return pl.pallas_call(
        _post_kernel,
        grid=(t // tm, D_FF // tf),
        in_specs=[
            pl.BlockSpec((tm, D_MODEL), lambda i, j: (i, 0)),
            pl.BlockSpec((tm, RW_WIDTH), lambda i, j: (i, 0)),
            pl.BlockSpec((tm, GLA_WIDTH), lambda i, j: (i, 0)),
            pl.BlockSpec((D_MODEL, D_MODEL), lambda i, j: (0, 0), pipeline_mode=pl.Buffered(1)),
            pl.BlockSpec((1, D_MODEL), lambda i, j: (0, 0)),
            pl.BlockSpec((D_MODEL, tf), lambda i, j: (0, j)),
            pl.BlockSpec((tf, D_MODEL), lambda i, j: (j, 0)),
            pl.BlockSpec((1, D_MODEL), lambda i, j: (0, 0)),
        ],
        out_specs=pl.BlockSpec((tm, D_MODEL), lambda i, j: (i, 0)),
        out_shape=jax.ShapeDtypeStruct((t, D_MODEL), F32),
        scratch_shapes=[pltpu.VMEM((tm, D_MODEL), BF16)],
        compiler_params=pltpu.CompilerParams(
            dimension_semantics=("parallel", "arbitrary"), vmem_limit_bytes=VMEM_LIMIT),
        name="post",
    )(x2d, o_rw, o_gla, w_out, g2, w_up, w_down, gf)


def _pad_cols(w, n):
    return jnp.pad(w, ((0, 0), (0, n - w.shape[1])))


def _prep_weights(w_in, rw_mu, rw_w0, rw_w2, rw_a0, rw_a2, rw_g2, rw_k_k, rw_k_a, rw_r_k, rw_ln_w, rw_ln_b,
                  gla_gw2, gla_gb, gla_norm_w):
    go = RW_PROJ
    w_t = jnp.swapaxes(w_in, 0, 1).astype(BF16)
    pad_rows = lambda w, n: jnp.pad(w, ((0, n - w.shape[0]), (0, 0)))
    w_p = jnp.concatenate([
        pad_rows(w_t[:RW_PROJ], GROUP_PAD),
        pad_rows(w_t[go:go + GLA_OFF_GATE + GLA_GATE_RANK], GLA_OFF_GOUT),
        pad_rows(w_t[go + GLA_OFF_GATE + GLA_GATE_RANK:], GROUP_PAD - GLA_OFF_GOUT),
    ], axis=0)

    row = lambda x: x.reshape(1, -1).astype(F32)
    mu = _pad_cols(row(rw_mu), GROUP_PAD)
    wwa = jnp.zeros((LANES, 2 * RW_WIDTH), F32)
    wwa = wwa.at[0:RW_LORA_W, 0:RW_WIDTH].set(rw_w2)
    wwa = wwa.at[RW_LORA_W:RW_LORA_W + RW_LORA_A, RW_WIDTH:].set(rw_a2)
    g2p = jnp.pad(rw_g2, ((0, 2 * LANES - RW_LORA_G), (0, 0)))
    head_of_col = jnp.arange(RW_WIDTH) // RW_HEAD_DIM
    e1 = (head_of_col[:, None] == jnp.arange(LANES)[None, :]).astype(BF16)
    rw_wts = (mu, wwa.astype(BF16), row(rw_w0), row(rw_a0), g2p.astype(BF16), row(rw_k_k), row(rw_k_a),
              row(rw_r_k), row(rw_ln_w), row(rw_ln_b), e1, e1.T)
    gw2p = jnp.pad(gla_gw2, ((0, LANES - GLA_GATE_RANK), (0, 0))).astype(BF16)
    gla_wts = (gw2p, row(gla_gb), row(gla_norm_w))
    return w_p, rw_wts, gla_wts


RW_PIECES = 1
GLA_PIECES = 1
RW_STEP_ROWS = 2 * ROWS
STRIP = 256


def _trunk(x, shift, s_rw, s_gla, norm1_g, w_p, rw_wts, gla_wts, w_out, norm2_g, w_up, w_down, norm_f_g,
           nb, tb, c_rw, c_gla, tm):
    bsz, seq, _ = x.shape
    t = bsz * seq
    x2d = x.reshape(t, D_MODEL)
    proj = _inproj(x2d, norm1_g.reshape(1, -1), w_p, tm, INPROJ_TN).reshape(bsz, seq, NP)
    shift_p = _pad_cols(shift, GROUP_PAD).reshape(bsz, 1, GROUP_PAD)
    rw_tb = RW_STEP_ROWS if nb == 1 and seq % RW_STEP_ROWS == 0 else tb
    o_rw, sh_new, s_rw_new = _rwkv(proj, shift_p, s_rw, rw_wts, nb, rw_tb, c_rw, RW_PIECES)
    o_gla, s_gla_new = _gla(proj, s_gla, gla_wts, nb, tb, c_gla, GLA_PIECES)
    y = _post(x2d, o_rw.reshape(t, RW_WIDTH), o_gla.reshape(t, GLA_WIDTH), w_out, norm2_g.reshape(1, -1),
              w_up, w_down, norm_f_g.reshape(1, -1), POST_TM, POST_TF)
    return (y.reshape(bsz, seq, D_MODEL), sh_new[:, 0, :RW_PROJ][None], s_rw_new[None], s_gla_new[None])


def kernel(x_prompt, x_sample, state_rwkv_shift, state_rwkv_wkv, state_gla, norm1_g, w_in, rw_mu, rw_w0,
           rw_w2, rw_a0, rw_a2, rw_g2, rw_k_k, rw_k_a, rw_r_k, rw_ln_w, rw_ln_b, gla_gw2, gla_gb, gla_norm_w,
           w_out, norm2_g, w_up, w_down, norm_f_g):
    w_p, rw_wts, gla_wts = _prep_weights(
        w_in[0], rw_mu[0], rw_w0[0], rw_w2[0], rw_a0[0], rw_a2[0], rw_g2[0], rw_k_k[0], rw_k_a[0],
        rw_r_k[0].reshape(-1), rw_ln_w[0], rw_ln_b[0], gla_gw2[0], gla_gb[0], gla_norm_w[0])
    shared = (norm1_g[0], w_p, rw_wts, gla_wts, w_out[0].astype(BF16), norm2_g[0], w_up[0].astype(BF16),
              w_down[0].astype(BF16), norm_f_g)

    bp, lp, _ = x_prompt.shape
    bs, ls, _ = x_sample.shape
    dt = x_prompt.dtype
    out_p = _trunk(x_prompt, jnp.zeros((bp, RW_PROJ), dt),
                   jnp.zeros((bp, RW_HEADS, RW_HEAD_DIM, RW_HEAD_DIM), dt),
                   jnp.zeros((bp, GLA_HEADS, GLA_DK, GLA_DV), dt), *shared,
                   nb=1, tb=ROWS, c_rw=64, c_gla=GLA_CHUNK, tm=1024)
    out_s = _trunk(x_sample, state_rwkv_shift[0], state_rwkv_wkv[0], state_gla[0], *shared,
                   nb=ROWS // ls, tb=ls, c_rw=ls, c_gla=ls, tm=1024)
    return (out_p[0], out_s[0], out_p[1], out_p[2], out_p[3], out_s[1], out_s[2], out_s[3])
```

```python
import functools

import jax
import jax.numpy as jnp
from jax import lax
from jax.experimental import pallas as pl
from jax.experimental.pallas import tpu as pltpu

F32 = jnp.float32
BF16 = jnp.bfloat16

D_MODEL = 2048
RW_WIDTH = 1024
RW_HEADS = 16
RW_HEAD_DIM = 64
RW_LORA_W = 64
RW_LORA_A = 64
RW_LORA_G = 160
RW_PROJ = 3 * RW_WIDTH + RW_LORA_W + RW_LORA_A + RW_LORA_G
RW_GN_EPS = 64e-5
GLA_WIDTH = 1024
GLA_HEADS = 4
GLA_KEY_WIDTH = 512
GLA_DK = 128
GLA_DV = 256
GLA_GATE_RANK = 16
GLA_GATE_NORM = 16.0
GLA_CHUNK = 64
GLA_PROJ = 2 * GLA_KEY_WIDTH + GLA_WIDTH + GLA_GATE_RANK + GLA_WIDTH
D_FF = 4 * D_MODEL
NORM_EPS = 1e-6
HEAD_NORM_EPS = 1e-5
DECAY_SCALE = 0.6065306597126334

LANES = 128
SUBLANES = 8

GROUP_PAD = 3456
RW_OFF_XWA = 3 * RW_WIDTH
RW_OFF_XG = RW_OFF_XWA + LANES
GLA_OFF_K = GLA_KEY_WIDTH
GLA_OFF_V = 2 * GLA_KEY_WIDTH
GLA_OFF_GATE = GLA_OFF_V + GLA_WIDTH
GLA_OFF_GOUT = GLA_OFF_GATE + LANES
NP = 2 * GROUP_PAD

ROWS = 128
MXU_WIDTH = 256
INPROJ_TN = 3 * MXU_WIDTH
POST_TM = 512
POST_TF = 1024
VMEM_LIMIT = 56 * 1024 * 1024

NN = (((1,), (0,)), ((), ()))
NT = (((1,), (1,)), ((), ()))


def _dot(a, b, dims=NN):
    return lax.dot_general(a, b, dims, preferred_element_type=F32)


def _parts(x, n):
    if x.dtype == BF16:
        return [x]
    out = []
    rem = x
    for i in range(n):
        h = rem.astype(BF16)
        out.append(h)
        if i + 1 < n:
            rem = rem - h.astype(F32)
    return out


def _mm(a, b, pa=1, pb=1, dims=NN):
    pa = 1 if a.dtype == BF16 else pa
    pb = 1 if b.dtype == BF16 else pb
    aa = _parts(a, pa)
    bb = _parts(b, pb)
    n = max(pa, pb)
    acc = None
    for i in range(pa):
        for j in range(pb):
            if i + j < n:
                t = _dot(aa[i], bb[j], dims)
                acc = t if acc is None else acc + t
    return acc


def _softplus(x):
    return jnp.maximum(x, 0.0) + jnp.log(1.0 + jnp.exp(-jnp.abs(x)))


def _sigmoid(x):
    return 0.5 * jnp.tanh(0.5 * x) + 0.5


def _chunk_masks(m, c):
    sh = c.bit_length() - 1
    ri = lax.broadcasted_iota(jnp.int32, (m, m), 0)
    ci = lax.broadcasted_iota(jnp.int32, (m, m), 1)
    same = (ri >> sh) == (ci >> sh)
    return same, ri, ci


def _inproj_kernel(x_ref, g_ref, w_ref, o_ref, h_ref):
    @pl.when(pl.program_id(1) == 0)
    def _():
        x = x_ref[...]
        ms = jnp.mean(x * x, axis=-1, keepdims=True)
        h_ref[...] = (x * lax.rsqrt(ms + NORM_EPS) * g_ref[...]).astype(BF16)

    o_ref[...] = _dot(h_ref[...], w_ref[...], NT)


def _inproj(x2d, g, w_p, tm, tn):
    t = x2d.shape[0]
    return pl.pallas_call(
        _inproj_kernel,
        grid=(t // tm, NP // tn),
        in_specs=[
            pl.BlockSpec((tm, D_MODEL), lambda i, j: (i, 0)),
            pl.BlockSpec((1, D_MODEL), lambda i, j: (0, 0)),
            pl.BlockSpec((tn, D_MODEL), lambda i, j: (j, 0)),
        ],
        out_specs=pl.BlockSpec((tm, tn), lambda i, j: (i, j)),
        out_shape=jax.ShapeDtypeStruct((t, NP), F32),
        scratch_shapes=[pltpu.VMEM((tm, D_MODEL), BF16)],
        compiler_params=pltpu.CompilerParams(
            dimension_semantics=("parallel", "arbitrary"), vmem_limit_bytes=VMEM_LIMIT),
        name="inproj",
    )(x2d, g, w_p)


def _rwkv_kernel(p_ref, sh_ref, s0_ref, mu_ref, wwa_ref, w0_ref, a0_ref, g2_ref, kk_ref, ka_ref,
                 rk_ref, lnw_ref, lnb_ref, e1_ref, e2_ref, e12_ref,
                 o_ref, shout_ref, sout_ref,
                 carry_ref, st_ref, bk_ref, kb_ref, pc_ref, art_ref, vt_ref, arbd_ref, uy_ref, yt_ref,
                 g_ref, bonus_ref, *, nb, tb, c, pp):
    j = pl.program_id(1)
    m_rows = ROWS
    nsub = nb * tb // m_rows
    tbs = tb // nsub
    n_lvl = c.bit_length() - 1
    hd = RW_HEAD_DIM

    @pl.when(j == 0)
    def _():
        for p in range(RW_HEADS // 2):
            st_ref[:, p] = jnp.concatenate([s0_ref[:, 2 * p], s0_ref[:, 2 * p + 1]], axis=-1)
        carry_ref[...] = sh_ref[...]

    yt_ref[...] = jnp.zeros_like(yt_ref)

    def shifted(sb, c0, w):
        t0 = sb * tbs
        pc3 = p_ref[:, t0:t0 + tbs, c0:c0 + w]
        before = carry_ref[:, :, c0:c0 + w] if sb == 0 else p_ref[:, t0 - 1:t0, c0:c0 + w]
        t3 = lax.broadcasted_iota(jnp.int32, pc3.shape, 1)
        prev3 = jnp.where(t3 == 0, before, pltpu.roll(pc3, 1, axis=1))
        return (pc3 + (prev3 - pc3) * mu_ref[:, c0:c0 + w]).reshape(m_rows, w)

    same, ri, ci = _chunk_masks(m_rows, c)
    lmask = jnp.where(same & (ci <= ri), 1.0, 0.0).astype(BF16)
    strict_t = same & (ri < ci)
    incl_t = same & (ri <= ci)

    sw = STRIP
    strips = [slice(c0, c0 + sw) for c0 in range(0, RW_WIDTH, sw)]

    def head_stat(x, cs):
        return _dot(x.astype(BF16), e1_ref[cs, :])

    def head_bcast(s, cs):
        return _mm(s, e2_ref[:, cs], pa=2)

    def head_sum(x):
        return _dot(x.astype(BF16), e12_ref[...])


    def prologue(sb):
        slab = shifted(sb, RW_OFF_XWA, LANES)
        lane = lax.broadcasted_iota(jnp.int32, slab.shape, 1)
        lhs = jnp.where(lane < RW_LORA_W, jnp.tanh(slab), slab).astype(BF16)
        sg = _sigmoid(shifted(sb, RW_OFF_XG, 2 * LANES)).astype(BF16)
        yield
        lws = [_dot(lhs, wwa_ref[:, cs]) for cs in strips]
        las = [_dot(lhs, wwa_ref[:, RW_WIDTH + cs.start:RW_WIDTH + cs.stop]) for cs in strips]
        for cs in strips:
            g_ref[sb, :, cs] = _dot(sg, g2_ref[:, cs])
        yield
        logws = [-DECAY_SCALE * _sigmoid(w0_ref[:, cs] + lw) for cs, lw in zip(strips, lws)]
        gcums = [_mm(lmask, logw, pb=2) for logw in logws]
        yield
        avs = [_sigmoid(a0_ref[:, cs] + la) for cs, la in zip(strips, las)]
        ks = [shifted(sb, RW_WIDTH + cs.start, sw) for cs in strips]
        kkfs = [k * kk_ref[:, cs] for cs, k in zip(strips, ks)]
        rinvs = [lax.rsqrt(jnp.maximum(head_stat(kkf * kkf, cs), 1e-24)) for cs, kkf in zip(strips, kkfs)]
        yield
        k2s = [k * (1.0 + (a - 1.0) * ka_ref[:, cs]) for cs, k, a in zip(strips, ks, avs)]
        rs = [shifted(sb, cs.start, sw) for cs in strips]
        bsums = [head_sum(r * k2 * rk_ref[:, cs]) for cs, r, k2 in zip(strips, rs, k2s)]
        kks = [kkf * head_bcast(rinv, cs) for cs, kkf, rinv in zip(strips, kkfs, rinvs)]
        yield
        for cs, bsum in zip(strips, bsums):
            v = shifted(sb, 2 * RW_WIDTH + cs.start, sw)
            bonus_ref[sb, :, cs] = bsum * v
            vt_ref[sb, cs, :] = v.T
        yield
        for cs, logw, gcum, a, kk, k2, r in zip(strips, logws, gcums, avs, kks, k2s, rs):
            beta = kk * a
            g3 = gcum.reshape(cps, c, sw)
            gtot = jnp.broadcast_to(g3[:, c - 1:c, :], g3.shape).reshape(m_rows, sw)
            e_inv = jnp.exp(-gcum)
            e_rev = jnp.exp(gtot - gcum)
            art_ref[sb, cs, 0:m_rows] = (-kk * jnp.exp(gcum - logw)).T
            art_ref[sb, cs, m_rows:2 * m_rows] = (r * jnp.exp(gcum)).T
            bk_ref[sb, 0:m_rows, cs] = beta * e_inv
            bk_ref[sb, m_rows:, cs] = k2 * e_inv
            kb_ref[sb, 0:m_rows, cs] = k2 * e_rev
            kb_ref[sb, m_rows:, cs] = beta * e_rev
            pc_ref[sb, :, cs] = jnp.exp(gtot)
            yield

    pw = 2 * hd
    cat = jnp.concatenate
    top = lax.broadcasted_iota(jnp.int32, (pw, 1), 0) < hd
    left = lax.broadcasted_iota(jnp.int32, (1, 2 * m_rows), 1) < m_rows

    def split_rows(x):
        return cat([jnp.where(top, x, 0.0), jnp.where(top, 0.0, x)], axis=1)

    def block_diag(xc):
        return cat([jnp.where(left, xc, 0.0), jnp.where(left, 0.0, xc)], axis=0)

    def blk(xt, u, rh, ch):
        col = (2 * u + ch) * m_rows
        return xt[rh * m_rows:(rh + 1) * m_rows, col:col + m_rows]

    n_pairs = RW_HEADS // 2
    pairs = range(n_pairs)
    los = [p * pw for p in pairs]

    def pair_phase(sb):
        arts = [art_ref[sb, lo:lo + pw, :] for lo in los]
        xts = [_mm(bk_ref[sb, :, lo:lo + pw], split_rows(art), pp, pp) for lo, art in zip(los, arts)]
        yield
        npcs = [cat([jnp.where(strict_t, blk(xt, 0, 0, 0), 0.0), jnp.where(strict_t, blk(xt, 1, 0, 0), 0.0)], axis=1)
                for xt in xts]
        vas = [_mm(split_rows(vt_ref[sb, lo:lo + pw, :]),
                   cat([cat([jnp.where(strict_t, blk(xt, u, 1, 0), 0.0), jnp.where(incl_t, blk(xt, u, 1, 1), 0.0)],
                            axis=1) for u in range(2)], axis=0), pp, pp)
               for lo, xt in zip(los, xts)]
        yield
        zcs = [cat([cat([art[u * hd:(u + 1) * hd, 0:m_rows], va[u * hd:(u + 1) * hd, 0:m_rows]], axis=0)
                    for u in range(2)], axis=1) for art, va in zip(arts, vas)]
        for lvl in range(n_lvl):
            bds = [block_diag(npc) for npc in npcs]
            if lvl + 1 < n_lvl:
                ts = [_mm(cat([zc, npc], axis=0), bd, pp, pp) for zc, npc, bd in zip(zcs, npcs, bds)]
                zcs = [zc + t[0:2 * hd] for zc, t in zip(zcs, ts)]
                npcs = [t[2 * hd:] for t in ts]
            else:
                zcs = [zc + _mm(zc, bd, pp, pp) for zc, bd in zip(zcs, bds)]
            yield
        rycs = [cat([cat([art[u * hd:(u + 1) * hd, m_rows:], va[u * hd:(u + 1) * hd, m_rows:]], axis=0)
                     for u in range(2)], axis=1)
                + _mm(zc, block_diag(cat([jnp.where(incl_t, blk(xt, 0, 0, 1), 0.0),
                                          jnp.where(incl_t, blk(xt, 1, 0, 1), 0.0)], axis=1)), pp, pp)
                for art, va, zc, xt in zip(arts, vas, zcs, xts)]
        yield
        zero = jnp.zeros((hd, 2 * m_rows), BF16)
        for p, zc, ry in zip(pairs, zcs, rycs):
            arbd_ref[sb, p, 0:hd, 0:2 * m_rows] = cat([zc[0:hd, 0:m_rows], ry[0:hd, 0:m_rows]], axis=1).astype(BF16)
            arbd_ref[sb, p, 0:hd, 2 * m_rows:] = zero
            arbd_ref[sb, p, hd:, 0:2 * m_rows] = zero
            arbd_ref[sb, p, hd:, 2 * m_rows:] = cat([zc[0:hd, m_rows:], ry[0:hd, m_rows:]], axis=1).astype(BF16)
            uy_ref[sb, p] = cat([zc[hd:, 0:m_rows], ry[hd:, 0:m_rows], zc[hd:, m_rows:], ry[hd:, m_rows:]], axis=1)
        yield

    sh_c = c.bit_length() - 1
    cps = m_rows // c
    cpb = tbs // c
    row_id2 = (lax.broadcasted_iota(jnp.int32, (2 * m_rows, pw), 0) & (m_rows - 1)) >> sh_c
    col_id = lax.broadcasted_iota(jnp.int32, (hd, m_rows), 1) >> sh_c
    lane_lo = lax.broadcasted_iota(jnp.int32, (1, pw), 1) < hd

    def chunk_step(sb, i):
        b = i // cpb
        r0 = i * c
        rmask2 = row_id2 == i
        cmask = col_id == i
        ss = [st_ref[b, p] for p in pairs]
        zss = [_mm(s, arbd_ref[sb, p], pp, pp) + uy_ref[sb, p] for p, s in zip(pairs, ss)]
        yield
        upd = []
        for lo, zs in zip(los, zss):
            kbz = jnp.where(rmask2, kb_ref[sb, :, lo:lo + pw], 0.0)
            rhs = cat([jnp.where(lane_lo, kbz, 0.0), jnp.where(lane_lo, 0.0, kbz)], axis=0)
            vu = cat([vt_ref[sb, lo:lo + hd, :], zs[:, 0:m_rows],
                      vt_ref[sb, lo + hd:lo + pw, :], zs[:, 2 * m_rows:3 * m_rows]], axis=1)
            upd.append(_mm(vu, rhs, pp, pp))
        yield
        for p, lo, s, zs, d in zip(pairs, los, ss, zss, upd):
            st_ref[b, p] = s * pc_ref[sb, pl.ds(r0, 1), lo:lo + pw] + d
            for u in range(2):
                rows = slice((2 * p + u) * hd, (2 * p + u + 1) * hd)
                yt_ref[sb, rows, :] = jnp.where(cmask, zs[:, (2 * u + 1) * m_rows:(2 * u + 2) * m_rows],
                                               yt_ref[sb, rows, :])
        yield

    inv_n = 1.0 / hd

    def epilogue(sb):
        ys = [yt_ref[sb, cs, :].T for cs in strips]
        ycs = [y - head_sum(y) * inv_n for y in ys]
        yield
        rstds = [lax.rsqrt(head_stat(yc * yc, cs) * inv_n + RW_GN_EPS) for cs, yc in zip(strips, ycs)]
        yield
        for cs, yc, rstd in zip(strips, ycs, rstds):
            yn = yc * head_bcast(rstd, cs) * lnw_ref[:, cs] + lnb_ref[:, cs]
            o_ref[:, sb * tbs:(sb + 1) * tbs, cs] = (
                (yn + bonus_ref[sb, :, cs]) * g_ref[sb, :, cs]).reshape(nb, tbs, sw)
        yield

    def run(*gens):
        live = list(gens)
        while live:
            live = [g for g in live if next(g, live) is not live]

    def tail(sb):
        for i in range(cps):
            yield from chunk_step(sb, i)
        yield from epilogue(sb)

    if nsub == 1:
        run(prologue(0))
        run(pair_phase(0))

        def chunk_body(i, carry):
            run(chunk_step(0, i))
            return carry

        lax.fori_loop(0, cps, chunk_body, 0)
        run(epilogue(0))
    else:
        run(prologue(0))
        for sb in range(nsub):
            run(pair_phase(sb), *([prologue(sb + 1)] if sb + 1 < nsub else []), *([tail(sb - 1)] if sb else []))
        run(tail(nsub - 1))

    last = p_ref[:, tb - 1:tb, :]
    carry_ref[...] = last
    shout_ref[...] = last

    @pl.when(j == pl.num_programs(1) - 1)
    def _():
        for p in pairs:
            sp = st_ref[:, p]
            sout_ref[:, 2 * p] = sp[:, :, 0:hd]
            sout_ref[:, 2 * p + 1] = sp[:, :, hd:]


def _rwkv(p, shift_prev, s0, wts, nb, tb, c, pp):
    bsz, seq, _ = p.shape
    m_rows = ROWS
    nsub = nb * tb // m_rows
    assert nsub * m_rows == nb * tb and (nsub == 1 or nb == 1)
    hd = RW_HEAD_DIM
    const = lambda shape: pl.BlockSpec(shape, lambda i, j: (0,) * len(shape))
    kern = functools.partial(_rwkv_kernel, nb=nb, tb=tb, c=c, pp=pp)
    return pl.pallas_call(
        kern,
        grid=(bsz // nb, seq // tb),
        in_specs=[
            pl.BlockSpec((nb, tb, GROUP_PAD), lambda i, j: (i, j, 0)),
            pl.BlockSpec((nb, 1, GROUP_PAD), lambda i, j: (i, 0, 0)),
            pl.BlockSpec((nb, RW_HEADS, hd, hd), lambda i, j: (i, 0, 0, 0)),
            const((1, GROUP_PAD)),
            const((LANES, 2 * RW_WIDTH)),
            const((1, RW_WIDTH)),
            const((1, RW_WIDTH)),
            const((2 * LANES, RW_WIDTH)),
            const((1, RW_WIDTH)),
            const((1, RW_WIDTH)),
            const((1, RW_WIDTH)),
            const((1, RW_WIDTH)),
            const((1, RW_WIDTH)),
            const((RW_WIDTH, LANES)),
            const((LANES, RW_WIDTH)),
            const((STRIP, STRIP)),
        ],
        out_specs=[
            pl.BlockSpec((nb, tb, RW_WIDTH), lambda i, j: (i, j, 0)),
            pl.BlockSpec((nb, 1, GROUP_PAD), lambda i, j: (i, 0, 0)),
            pl.BlockSpec((nb, RW_HEADS, hd, hd), lambda i, j: (i, 0, 0, 0)),
        ],
        out_shape=[
            jax.ShapeDtypeStruct((bsz, seq, RW_WIDTH), F32),
            jax.ShapeDtypeStruct((bsz, 1, GROUP_PAD), F32),
            jax.ShapeDtypeStruct((bsz, RW_HEADS, hd, hd), F32),
        ],
        scratch_shapes=[
            pltpu.VMEM((nb, 1, GROUP_PAD), F32),
            pltpu.VMEM((nb, RW_HEADS // 2, hd, 2 * hd), F32),
            pltpu.VMEM((nsub, 2 * m_rows, RW_WIDTH), F32),
            pltpu.VMEM((nsub, 2 * m_rows, RW_WIDTH), F32),
            pltpu.VMEM((nsub, m_rows, RW_WIDTH), F32),
            pltpu.VMEM((nsub, RW_WIDTH, 2 * m_rows), F32),
            pltpu.VMEM((nsub, RW_WIDTH, m_rows), F32),
            pltpu.VMEM((nsub, RW_HEADS // 2, 2 * hd, 4 * m_rows), BF16),
            pltpu.VMEM((nsub, RW_HEADS // 2, hd, 4 * m_rows), F32),
            pltpu.VMEM((nsub, RW_WIDTH, m_rows), F32),
            pltpu.VMEM((nsub, m_rows, RW_WIDTH), F32),
            pltpu.VMEM((nsub, m_rows, RW_WIDTH), F32),
        ],
        compiler_params=pltpu.CompilerParams(
            dimension_semantics=("parallel", "arbitrary"), vmem_limit_bytes=VMEM_LIMIT),
        name="rwkv7",
    )(p, shift_prev, s0, *wts)


def _gla_kernel(p_ref, s0_ref, gw2_ref, gb_ref, nw_ref, o_ref, st_ref,
                acc_ref, qd_ref, v_ref, kt_ref, et_ref, *, nb, tb, c, pp):
    j = pl.program_id(1)
    m_rows = nb * tb
    dk, dv = GLA_DK, GLA_DV

    @pl.when(j == 0)
    def _():
        st_ref[...] = s0_ref[...]

    p = p_ref[...].reshape(m_rows, GROUP_PAD)
    q = p[:, 0:GLA_KEY_WIDTH] * (dk ** -0.5)
    k = p[:, GLA_OFF_K:GLA_OFF_K + GLA_KEY_WIDTH]
    v = p[:, GLA_OFF_V:GLA_OFF_V + GLA_WIDTH]
    xs = p[:, GLA_OFF_GATE:GLA_OFF_GATE + LANES].astype(BF16)
    gout = p[:, GLA_OFF_GOUT:GLA_OFF_GOUT + GLA_WIDTH]
    gk = -_softplus(-(_dot(xs, gw2_ref[...]) + gb_ref[...])) / GLA_GATE_NORM

    same, ri, ci = _chunk_masks(m_rows, c)
    lmask = jnp.where(same & (ci <= ri), 1.0, 0.0).astype(BF16)
    causal = same & (ci <= ri)
    gcum = _mm(lmask, gk, pb=3)
    g3 = gcum.reshape(m_rows // c, c, GLA_KEY_WIDTH)
    gtot = jnp.broadcast_to(g3[:, c - 1:c, :], g3.shape).reshape(m_rows, GLA_KEY_WIDTH)
    qd = q * jnp.exp(gcum)
    kinv = k * jnp.exp(-gcum)
    qd_ref[...] = qd
    v_ref[...] = v
    kt_ref[...] = (k * jnp.exp(gtot - gcum)).T
    et_ref[...] = jnp.exp(gtot).T

    heads = range(GLA_HEADS)
    kq = [slice(h * dk, (h + 1) * dk) for h in heads]
    vq = [slice(h * dv, (h + 1) * dv) for h in heads]
    scores = [jnp.where(causal, _mm(qd[:, ks], kinv[:, ks], pp, pp, NT), 0.0) for ks in kq]
    intra = [_mm(a, v[:, vs], pp, pp) for a, vs in zip(scores, vq)]
    for vs, o in zip(vq, intra):
        acc_ref[:, vs] = o

    sh_c = c.bit_length() - 1
    cpb = tb // c
    span = min(m_rows, LANES)
    row_id = lax.broadcasted_iota(jnp.int32, (span, dv), 0) >> sh_c
    for i in range(m_rows // c):
        b = i // cpb
        r0 = i * c
        t0 = (r0 // span) * span
        rmask = row_id == (r0 - t0) // c
        states = [st_ref[b, h] for h in heads]
        inter = [_mm(qd_ref[r0:r0 + c, ks], s, pp, pp) for ks, s in zip(kq, states)]
        upd = [_mm(kt_ref[ks, t0:t0 + span], jnp.where(rmask, v_ref[t0:t0 + span, vs], 0.0), pp, pp)
               for ks, vs in zip(kq, vq)]
        for h, ks, vs, s, oi, d in zip(heads, kq, vq, states, inter, upd):
            acc_ref[r0:r0 + c, vs] += oi
            st_ref[b, h] = s * et_ref[ks, r0:r0 + 1] + d

    for vs in vq:
        o = acc_ref[:, vs]
        on = o * lax.rsqrt(jnp.mean(o * o, axis=-1, keepdims=True) + HEAD_NORM_EPS) * nw_ref[...]
        gh = gout[:, vs]
        o_ref[:, :, vs] = (on * (gh * _sigmoid(gh))).reshape(nb, tb, dv)


def _gla(p, s0, wts, nb, tb, c, pp):
    bsz, seq, _ = p.shape
    m_rows = nb * tb
    const = lambda shape: pl.BlockSpec(shape, lambda i, j: (0,) * len(shape))
    kern = functools.partial(_gla_kernel, nb=nb, tb=tb, c=c, pp=pp)
    return pl.pallas_call(
        kern,
        grid=(bsz // nb, seq // tb),
        in_specs=[
            pl.BlockSpec((nb, tb, GROUP_PAD), lambda i, j: (i, j, 1)),
            pl.BlockSpec((nb, GLA_HEADS, GLA_DK, GLA_DV), lambda i, j: (i, 0, 0, 0)),
            const((LANES, GLA_KEY_WIDTH)),
            const((1, GLA_KEY_WIDTH)),
            const((1, GLA_DV)),
        ],
        out_specs=[
            pl.BlockSpec((nb, tb, GLA_WIDTH), lambda i, j: (i, j, 0)),
            pl.BlockSpec((nb, GLA_HEADS, GLA_DK, GLA_DV), lambda i, j: (i, 0, 0, 0)),
        ],
        out_shape=[
            jax.ShapeDtypeStruct((bsz, seq, GLA_WIDTH), F32),
            jax.ShapeDtypeStruct((bsz, GLA_HEADS, GLA_DK, GLA_DV), F32),
        ],
        scratch_shapes=[
            pltpu.VMEM((m_rows, GLA_WIDTH), F32),
            pltpu.VMEM((m_rows, GLA_KEY_WIDTH), F32),
            pltpu.VMEM((m_rows, GLA_WIDTH), F32),
            pltpu.VMEM((GLA_KEY_WIDTH, m_rows), F32),
            pltpu.VMEM((GLA_KEY_WIDTH, m_rows), F32),
        ],
        compiler_params=pltpu.CompilerParams(
            dimension_semantics=("parallel", "arbitrary"), vmem_limit_bytes=VMEM_LIMIT),
        name="gla",
    )(p, s0, *wts)


def _post_kernel(x_ref, orw_ref, ogla_ref, wo_ref, g2_ref, wu_ref, wd_ref, gf_ref, o_ref, h_ref):
    jf = pl.program_id(1)

    @pl.when(jf == 0)
    def _():
        x1 = (x_ref[...]
              + _dot(orw_ref[...].astype(BF16), wo_ref[0:RW_WIDTH, :])
              + _dot(ogla_ref[...].astype(BF16), wo_ref[RW_WIDTH:, :]))
        o_ref[...] = x1
        ms = jnp.mean(x1 * x1, axis=-1, keepdims=True)
        h_ref[...] = (x1 * lax.rsqrt(ms + NORM_EPS) * g2_ref[...]).astype(BF16)

    u = jnp.maximum(_dot(h_ref[...], wu_ref[...]), 0.0)
    o_ref[...] += _dot((u * u).astype(BF16), wd_ref[...])

    @pl.when(jf == pl.num_programs(1) - 1)
    def _():
        x2 = o_ref[...]
        ms = jnp.mean(x2 * x2, axis=-1, keepdims=True)
        o_ref[...] = x2 * lax.rsqrt(ms + NORM_EPS) * gf_ref[...]


def _post(x2d, o_rw, o_gla, w_out, g2, w_up, w_down, gf, tm, tf):
    t = x2d.shape[0]
    return pl.pallas_call(
        _post_kernel,
        grid=(t // tm, D_FF // tf),
        in_specs=[
            pl.BlockSpec((tm, D_MODEL), lambda i, j: (i, 0)),
            pl.BlockSpec((tm, RW_WIDTH), lambda i, j: (i, 0)),
            pl.BlockSpec((tm, GLA_WIDTH), lambda i, j: (i, 0)),
            pl.BlockSpec((D_MODEL, D_MODEL), lambda i, j: (0, 0), pipeline_mode=pl.Buffered(1)),
            pl.BlockSpec((1, D_MODEL), lambda i, j: (0, 0)),
            pl.BlockSpec((D_MODEL, tf), lambda i, j: (0, j)),
            pl.BlockSpec((tf, D_MODEL), lambda i, j: (j, 0)),
            pl.BlockSpec((1, D_MODEL), lambda i, j: (0, 0)),
        ],
        out_specs=pl.BlockSpec((tm, D_MODEL), lambda i, j: (i, 0)),
        out_shape=jax.ShapeDtypeStruct((t, D_MODEL), F32),
        scratch_shapes=[pltpu.VMEM((tm, D_MODEL), BF16)],
        compiler_params=pltpu.CompilerParams(
            dimension_semantics=("parallel", "arbitrary"), vmem_limit_bytes=VMEM_LIMIT),
        name="post",
    )(x2d, o_rw, o_gla, w_out, g2, w_up, w_down, gf)


def _pad_cols(w, n):
    return jnp.pad(w, ((0, 0), (0, n - w.shape[1])))


def _prep_weights(w_in, rw_mu, rw_w0, rw_w2, rw_a0, rw_a2, rw_g2, rw_k_k, rw_k_a, rw_r_k, rw_ln_w, rw_ln_b,
                  gla_gw2, gla_gb, gla_norm_w):
    go = RW_PROJ
    w_t = jnp.swapaxes(w_in, 0, 1).astype(BF16)
    pad_rows = lambda w, n: jnp.pad(w, ((0, n - w.shape[0]), (0, 0)))
    w_p = jnp.concatenate([
        pad_rows(w_t[:RW_PROJ], GROUP_PAD),
        pad_rows(w_t[go:go + GLA_OFF_GATE + GLA_GATE_RANK], GLA_OFF_GOUT),
        pad_rows(w_t[go + GLA_OFF_GATE + GLA_GATE_RANK:], GROUP_PAD - GLA_OFF_GOUT),
    ], axis=0)

    row = lambda x: x.reshape(1, -1).astype(F32)
    mu = _pad_cols(row(rw_mu), GROUP_PAD)
    wwa = jnp.zeros((LANES, 2 * RW_WIDTH), F32)
    wwa = wwa.at[0:RW_LORA_W, 0:RW_WIDTH].set(rw_w2)
    wwa = wwa.at[RW_LORA_W:RW_LORA_W + RW_LORA_A, RW_WIDTH:].set(rw_a2)
    g2p = jnp.pad(rw_g2, ((0, 2 * LANES - RW_LORA_G), (0, 0)))
    head_of_col = jnp.arange(RW_WIDTH) // RW_HEAD_DIM
    e1 = (head_of_col[:, None] == jnp.arange(LANES)[None, :]).astype(BF16)
    rw_wts = (mu, wwa.astype(BF16), row(rw_w0), row(rw_a0), g2p.astype(BF16), row(rw_k_k), row(rw_k_a),
              row(rw_r_k), row(rw_ln_w), row(rw_ln_b), e1, e1.T,
              (head_of_col[:STRIP, None] == head_of_col[None, :STRIP]).astype(BF16))
    gw2p = jnp.pad(gla_gw2, ((0, LANES - GLA_GATE_RANK), (0, 0))).astype(BF16)
    gla_wts = (gw2p, row(gla_gb), row(gla_norm_w))
    return w_p, rw_wts, gla_wts


RW_PIECES = 1
GLA_PIECES = 1
RW_STEP_ROWS = 2 * ROWS
STRIP = 256


def _trunk(x, shift, s_rw, s_gla, norm1_g, w_p, rw_wts, gla_wts, w_out, norm2_g, w_up, w_down, norm_f_g,
           nb, tb, c_rw, c_gla, tm):
    bsz, seq, _ = x.shape
    t = bsz * seq
    x2d = x.reshape(t, D_MODEL)
    proj = _inproj(x2d, norm1_g.reshape(1, -1), w_p, tm, INPROJ_TN).reshape(bsz, seq, NP)
    shift_p = _pad_cols(shift, GROUP_PAD).reshape(bsz, 1, GROUP_PAD)
    rw_tb = RW_STEP_ROWS if nb == 1 and seq % RW_STEP_ROWS == 0 else tb
    o_rw, sh_new, s_rw_new = _rwkv(proj, shift_p, s_rw, rw_wts, nb, rw_tb, c_rw, RW_PIECES)
    o_gla, s_gla_new = _gla(proj, s_gla, gla_wts, nb, rw_tb, c_gla, GLA_PIECES)
    y = _post(x2d, o_rw.reshape(t, RW_WIDTH), o_gla.reshape(t, GLA_WIDTH), w_out, norm2_g.reshape(1, -1),
              w_up, w_down, norm_f_g.reshape(1, -1), POST_TM, POST_TF)
    return (y.reshape(bsz, seq, D_MODEL), sh_new[:, 0, :RW_PROJ][None], s_rw_new[None], s_gla_new[None])


def kernel(x_prompt, x_sample, state_rwkv_shift, state_rwkv_wkv, state_gla, norm1_g, w_in, rw_mu, rw_w0,
           rw_w2, rw_a0, rw_a2, rw_g2, rw_k_k, rw_k_a, rw_r_k, rw_ln_w, rw_ln_b, gla_gw2, gla_gb, gla_norm_w,
           w_out, norm2_g, w_up, w_down, norm_f_g):
    w_p, rw_wts, gla_wts = _prep_weights(
        w_in[0], rw_mu[0], rw_w0[0], rw_w2[0], rw_a0[0], rw_a2[0], rw_g2[0], rw_k_k[0], rw_k_a[0],
        rw_r_k[0].reshape(-1), rw_ln_w[0], rw_ln_b[0], gla_gw2[0], gla_gb[0], gla_norm_w[0])
    shared = (norm1_g[0], w_p, rw_wts, gla_wts, w_out[0].astype(BF16), norm2_g[0], w_up[0].astype(BF16),
              w_down[0].astype(BF16), norm_f_g)

    bp, lp, _ = x_prompt.shape
    bs, ls, _ = x_sample.shape
    dt = x_prompt.dtype
    out_p = _trunk(x_prompt, jnp.zeros((bp, RW_PROJ), dt),
                   jnp.zeros((bp, RW_HEADS, RW_HEAD_DIM, RW_HEAD_DIM), dt),
                   jnp.zeros((bp, GLA_HEADS, GLA_DK, GLA_DV), dt), *shared,
                   nb=1, tb=ROWS, c_rw=64, c_gla=GLA_CHUNK, tm=1024)
    out_s = _trunk(x_sample, state_rwkv_shift[0], state_rwkv_wkv[0], state_gla[0], *shared,
                   nb=ROWS // ls, tb=ls, c_rw=ls, c_gla=ls, tm=1024)
    return (out_p[0], out_s[0], out_p[1], out_p[2], out_p[3], out_s[1], out_s[2], out_s[3])
```

```python
import functools

import jax
import jax.numpy as jnp
from jax import lax
from jax.experimental import pallas as pl
from jax.experimental.pallas import tpu as pltpu

F32 = jnp.float32
BF16 = jnp.bfloat16

D_MODEL = 2048
RW_WIDTH = 1024
RW_HEADS = 16
RW_HEAD_DIM = 64
RW_LORA_W = 64
RW_LORA_A = 64
RW_LORA_G = 160
RW_PROJ = 3 * RW_WIDTH + RW_LORA_W + RW_LORA_A + RW_LORA_G
RW_GN_EPS = 64e-5
GLA_WIDTH = 1024
GLA_HEADS = 4
GLA_KEY_WIDTH = 512
GLA_DK = 128
GLA_DV = 256
GLA_GATE_RANK = 16
GLA_GATE_NORM = 16.0
GLA_CHUNK = 64
GLA_PROJ = 2 * GLA_KEY_WIDTH + GLA_WIDTH + GLA_GATE_RANK + GLA_WIDTH
D_FF = 4 * D_MODEL
NORM_EPS = 1e-6
HEAD_NORM_EPS = 1e-5
DECAY_SCALE = 0.6065306597126334

LANES = 128
SUBLANES = 8

GROUP_PAD = 3456
RW_OFF_XWA = 3 * RW_WIDTH
RW_OFF_XG = RW_OFF_XWA + LANES
GLA_OFF_K = GLA_KEY_WIDTH
GLA_OFF_V = 2 * GLA_KEY_WIDTH
GLA_OFF_GATE = GLA_OFF_V + GLA_WIDTH
GLA_OFF_GOUT = GLA_OFF_GATE + LANES
NP = 2 * GROUP_PAD

ROWS = 128
MXU_WIDTH = 256
INPROJ_TN = 3 * MXU_WIDTH
PROJ_DTYPE = BF16
POST_TM = 512
POST_TF = 1024
VMEM_LIMIT = 56 * 1024 * 1024

NN = (((1,), (0,)), ((), ()))
NT = (((1,), (1,)), ((), ()))


def _dot(a, b, dims=NN):
    return lax.dot_general(a, b, dims, preferred_element_type=F32)


def _parts(x, n):
    if x.dtype == BF16:
        return [x]
    out = []
    rem = x
    for i in range(n):
        h = rem.astype(BF16)
        out.append(h)
        if i + 1 < n:
            rem = rem - h.astype(F32)
    return out


def _mm(a, b, pa=1, pb=1, dims=NN):
    pa = 1 if a.dtype == BF16 else pa
    pb = 1 if b.dtype == BF16 else pb
    aa = _parts(a, pa)
    bb = _parts(b, pb)
    n = max(pa, pb)
    if dims == NN and min(pa, pb) == 1 and n * a.shape[1] <= MXU_WIDTH:
        return _dot(jnp.concatenate(aa * (n // pa), axis=1), jnp.concatenate(bb * (n // pb), axis=0))
    acc = None
    for i in range(pa):
        for j in range(pb):
            if i + j < n:
                t = _dot(aa[i], bb[j], dims)
                acc = t if acc is None else acc + t
    return acc


def _softplus(x):
    return jnp.maximum(x, 0.0) + jnp.log(1.0 + jnp.exp(-jnp.abs(x)))


def _sigmoid(x):
    return 0.5 * jnp.tanh(0.5 * x) + 0.5


def _chunk_masks(m, c):
    sh = c.bit_length() - 1
    ri = lax.broadcasted_iota(jnp.int32, (m, m), 0)
    ci = lax.broadcasted_iota(jnp.int32, (m, m), 1)
    same = (ri >> sh) == (ci >> sh)
    return same, ri, ci


def _inproj_kernel(x_ref, g_ref, w_ref, o_ref, h_ref):
    @pl.when(pl.program_id(1) == 0)
    def _():
        x = x_ref[...]
        ms = jnp.mean(x * x, axis=-1, keepdims=True)
        h_ref[...] = (x * lax.rsqrt(ms + NORM_EPS) * g_ref[...]).astype(BF16)

    o_ref[...] = _dot(h_ref[...], w_ref[...], NT).astype(o_ref.dtype)


def _inproj(x2d, g, w_p, tm, tn):
    t = x2d.shape[0]
    return pl.pallas_call(
        _inproj_kernel,
        grid=(t // tm, NP // tn),
        in_specs=[
            pl.BlockSpec((tm, D_MODEL), lambda i, j: (i, 0)),
            pl.BlockSpec((1, D_MODEL), lambda i, j: (0, 0)),
            pl.BlockSpec((tn, D_MODEL), lambda i, j: (j, 0)),
        ],
        out_specs=pl.BlockSpec((tm, tn), lambda i, j: (i, j)),
        out_shape=jax.ShapeDtypeStruct((t, NP), PROJ_DTYPE),
        scratch_shapes=[pltpu.VMEM((tm, D_MODEL), BF16)],
        compiler_params=pltpu.CompilerParams(
            dimension_semantics=("parallel", "arbitrary"), vmem_limit_bytes=VMEM_LIMIT),
        name="inproj",
    )(x2d, g, w_p)


def _rwkv_kernel(p_ref, sh_ref, s0_ref, mu_ref, wwa_ref, w0_ref, a0_ref, g2_ref, kk_ref, ka_ref,
                 rk_ref, lnw_ref, lnb_ref, e1_ref, e2_ref, e12_ref,
                 o_ref, shout_ref, sout_ref,
                 carry_ref, st_ref, bk_ref, kb_ref, pc_ref, art_ref, vt_ref, arbd_ref, uy_ref, yt_ref,
                 g_ref, bonus_ref, *, nb, tb, c, pp):
    j = pl.program_id(1)
    m_rows = ROWS
    nsub = nb * tb // m_rows
    tbs = tb // nsub
    n_lvl = c.bit_length() - 1
    hd = RW_HEAD_DIM

    @pl.when(j == 0)
    def _():
        for p in range(RW_HEADS // 2):
            st_ref[:, p] = jnp.concatenate([s0_ref[:, 2 * p], s0_ref[:, 2 * p + 1]], axis=-1)
        carry_ref[...] = sh_ref[...]

    yt_ref[...] = jnp.zeros_like(yt_ref)

    def shifted(sb, c0, w):
        r0 = sb * m_rows
        pc3 = p_ref[r0:r0 + m_rows, c0:c0 + w].astype(F32).reshape(nb, tbs, w)
        if sb == 0:
            before = carry_ref[:, :, c0:c0 + w]
        else:
            before = p_ref[r0 - 1:r0, c0:c0 + w].astype(F32).reshape(1, 1, w)
        t3 = lax.broadcasted_iota(jnp.int32, pc3.shape, 1)
        prev3 = jnp.where(t3 == 0, before, pltpu.roll(pc3, 1, axis=1))
        return (pc3 + (prev3 - pc3) * mu_ref[:, c0:c0 + w]).reshape(m_rows, w)

    same, ri, ci = _chunk_masks(m_rows, c)
    lmask = jnp.where(same & (ci <= ri), 1.0, 0.0).astype(BF16)
    strict_t = same & (ri < ci)
    incl_t = same & (ri <= ci)

    sw = STRIP
    strips = [slice(c0, c0 + sw) for c0 in range(0, RW_WIDTH, sw)]

    def head_stat(x, cs):
        return _dot(x.astype(BF16), e1_ref[cs, :])

    def head_bcast(s, cs):
        return _mm(s, e2_ref[:, cs], pa=2)

    def head_sum(x):
        return _dot(x.astype(BF16), e12_ref[...])


    def prologue(sb):
        slab = shifted(sb, RW_OFF_XWA, LANES)
        lane = lax.broadcasted_iota(jnp.int32, slab.shape, 1)
        lhs = jnp.where(lane < RW_LORA_W, jnp.tanh(slab), slab).astype(BF16)
        sg = _sigmoid(shifted(sb, RW_OFF_XG, 2 * LANES)).astype(BF16)
        yield
        lws = [_dot(lhs, wwa_ref[:, cs]) for cs in strips]
        las = [_dot(lhs, wwa_ref[:, RW_WIDTH + cs.start:RW_WIDTH + cs.stop]) for cs in strips]
        for cs in strips:
            g_ref[sb, :, cs] = _dot(sg, g2_ref[:, cs])
        yield
        logws = [-DECAY_SCALE * _sigmoid(w0_ref[:, cs] + lw) for cs, lw in zip(strips, lws)]
        gcums = [_mm(lmask, logw, pb=2) for logw in logws]
        yield
        avs = [_sigmoid(a0_ref[:, cs] + la) for cs, la in zip(strips, las)]
        ks = [shifted(sb, RW_WIDTH + cs.start, sw) for cs in strips]
        kkfs = [k * kk_ref[:, cs] for cs, k in zip(strips, ks)]
        rinvs = [lax.rsqrt(jnp.maximum(head_stat(kkf * kkf, cs), 1e-24)) for cs, kkf in zip(strips, kkfs)]
        yield
        k2s = [k * (1.0 + (a - 1.0) * ka_ref[:, cs]) for cs, k, a in zip(strips, ks, avs)]
        rs = [shifted(sb, cs.start, sw) for cs in strips]
        bsums = [head_sum(r * k2 * rk_ref[:, cs]) for cs, r, k2 in zip(strips, rs, k2s)]
        kks = [kkf * head_bcast(rinv, cs) for cs, kkf, rinv in zip(strips, kkfs, rinvs)]
        yield
        for cs, bsum in zip(strips, bsums):
            v = shifted(sb, 2 * RW_WIDTH + cs.start, sw)
            bonus_ref[sb, :, cs] = bsum * v
            vt_ref[sb, cs, :] = v.T
        yield
        for cs, logw, gcum, a, kk, k2, r in zip(strips, logws, gcums, avs, kks, k2s, rs):
            beta = kk * a
            g3 = gcum.reshape(cps, c, sw)
            gtot = jnp.broadcast_to(g3[:, c - 1:c, :], g3.shape).reshape(m_rows, sw)
            e_inv = jnp.exp(-gcum)
            e_rev = jnp.exp(gtot - gcum)
            art_ref[sb, cs, 0:m_rows] = (-kk * jnp.exp(gcum - logw)).T
            art_ref[sb, cs, m_rows:2 * m_rows] = (r * jnp.exp(gcum)).T
            bk_ref[sb, 0:m_rows, cs] = beta * e_inv
            bk_ref[sb, m_rows:, cs] = k2 * e_inv
            kb_ref[sb, 0:m_rows, cs] = k2 * e_rev
            kb_ref[sb, m_rows:, cs] = beta * e_rev
            pc_ref[sb, :, cs] = jnp.exp(gtot)
            yield

    pw = 2 * hd
    cat = jnp.concatenate
    top = lax.broadcasted_iota(jnp.int32, (pw, 1), 0) < hd
    left = lax.broadcasted_iota(jnp.int32, (1, 2 * m_rows), 1) < m_rows

    def split_rows(x):
        return cat([jnp.where(top, x, 0.0), jnp.where(top, 0.0, x)], axis=1)

    def block_diag(xc):
        return cat([jnp.where(left, xc, 0.0), jnp.where(left, 0.0, xc)], axis=0)

    def blk(xt, u, rh, ch):
        col = (2 * u + ch) * m_rows
        return xt[rh * m_rows:(rh + 1) * m_rows, col:col + m_rows]

    n_pairs = RW_HEADS // 2
    pairs = range(n_pairs)
    los = [p * pw for p in pairs]

    def pair_phase(sb):
        for g0 in range(0, n_pairs, PAIR_LOCKSTEP):
            yield from pair_group(sb, pairs[g0:g0 + PAIR_LOCKSTEP], los[g0:g0 + PAIR_LOCKSTEP])

    def pair_group(sb, pairs, los):
        arts = [art_ref[sb, lo:lo + pw, :] for lo in los]
        xts = [_mm(bk_ref[sb, :, lo:lo + pw], split_rows(art), pp, pp) for lo, art in zip(los, arts)]
        yield
        npcs = [cat([jnp.where(strict_t, blk(xt, 0, 0, 0), 0.0), jnp.where(strict_t, blk(xt, 1, 0, 0), 0.0)], axis=1)
                for xt in xts]
        vas = [_mm(split_rows(vt_ref[sb, lo:lo + pw, :]),
                   cat([cat([jnp.where(strict_t, blk(xt, u, 1, 0), 0.0), jnp.where(incl_t, blk(xt, u, 1, 1), 0.0)],
                            axis=1) for u in range(2)], axis=0), pp, pp)
               for lo, xt in zip(los, xts)]
        yield
        zcs = [cat([cat([art[u * hd:(u + 1) * hd, 0:m_rows], va[u * hd:(u + 1) * hd, 0:m_rows]], axis=0)
                    for u in range(2)], axis=1) for art, va in zip(arts, vas)]
        for lvl in range(n_lvl):
            bds = [block_diag(npc) for npc in npcs]
            if lvl + 1 < n_lvl:
                ts = [_mm(cat([zc, npc], axis=0), bd, pp, pp) for zc, npc, bd in zip(zcs, npcs, bds)]
                zcs = [zc + t[0:2 * hd] for zc, t in zip(zcs, ts)]
                npcs = [t[2 * hd:] for t in ts]
            else:
                zcs = [zc + _mm(zc, bd, pp, pp) for zc, bd in zip(zcs, bds)]
            yield
        rycs = [cat([cat([art[u * hd:(u + 1) * hd, m_rows:], va[u * hd:(u + 1) * hd, m_rows:]], axis=0)
                     for u in range(2)], axis=1)
                + _mm(zc, block_diag(cat([jnp.where(incl_t, blk(xt, 0, 0, 1), 0.0),
                                          jnp.where(incl_t, blk(xt, 1, 0, 1), 0.0)], axis=1)), pp, pp)
                for art, va, zc, xt in zip(arts, vas, zcs, xts)]
        yield
        zero = jnp.zeros((hd, 2 * m_rows), BF16)
        for p, zc, ry in zip(pairs, zcs, rycs):
            arbd_ref[sb, p, 0:hd, 0:2 * m_rows] = cat([zc[0:hd, 0:m_rows], ry[0:hd, 0:m_rows]], axis=1).astype(BF16)
            arbd_ref[sb, p, 0:hd, 2 * m_rows:] = zero
            arbd_ref[sb, p, hd:, 0:2 * m_rows] = zero
            arbd_ref[sb, p, hd:, 2 * m_rows:] = cat([zc[0:hd, m_rows:], ry[0:hd, m_rows:]], axis=1).astype(BF16)
            uy_ref[sb, p] = cat([zc[hd:, 0:m_rows], ry[hd:, 0:m_rows], zc[hd:, m_rows:], ry[hd:, m_rows:]], axis=1)
        yield

    sh_c = c.bit_length() - 1
    cps = m_rows // c
    cpb = tbs // c
    row_id2 = (lax.broadcasted_iota(jnp.int32, (2 * m_rows, pw), 0) & (m_rows - 1)) >> sh_c
    col_id = lax.broadcasted_iota(jnp.int32, (hd, m_rows), 1) >> sh_c
    lane_lo = lax.broadcasted_iota(jnp.int32, (1, pw), 1) < hd

    def chunk_step(sb, i):
        b = i // cpb
        r0 = i * c
        rmask2 = row_id2 == i
        cmask = col_id == i
        ss = [st_ref[b, p] for p in pairs]
        zss = [_mm(s, arbd_ref[sb, p], pp, pp) + uy_ref[sb, p] for p, s in zip(pairs, ss)]
        yield
        upd = []
        for lo, zs in zip(los, zss):
            kbz = jnp.where(rmask2, kb_ref[sb, :, lo:lo + pw], 0.0)
            rhs = cat([jnp.where(lane_lo, kbz, 0.0), jnp.where(lane_lo, 0.0, kbz)], axis=0)
            vu = cat([vt_ref[sb, lo:lo + hd, :], zs[:, 0:m_rows],
                      vt_ref[sb, lo + hd:lo + pw, :], zs[:, 2 * m_rows:3 * m_rows]], axis=1)
            upd.append(_mm(vu, rhs, pp, pp))
        yield
        for p, lo, s, zs, d in zip(pairs, los, ss, zss, upd):
            st_ref[b, p] = s * pc_ref[sb, pl.ds(r0, 1), lo:lo + pw] + d
            for u in range(2):
                rows = slice((2 * p + u) * hd, (2 * p + u + 1) * hd)
                yt_ref[sb, rows, :] = jnp.where(cmask, zs[:, (2 * u + 1) * m_rows:(2 * u + 2) * m_rows],
                                               yt_ref[sb, rows, :])
        yield

    inv_n = 1.0 / hd

    def epilogue(sb):
        ys = [yt_ref[sb, cs, :].T for cs in strips]
        ycs = [y - head_sum(y) * inv_n for y in ys]
        yield
        rstds = [lax.rsqrt(head_stat(yc * yc, cs) * inv_n + RW_GN_EPS) for cs, yc in zip(strips, ycs)]
        yield
        for cs, yc, rstd in zip(strips, ycs, rstds):
            yn = yc * head_bcast(rstd, cs) * lnw_ref[:, cs] + lnb_ref[:, cs]
            o_ref[:, sb * tbs:(sb + 1) * tbs, cs] = (
                (yn + bonus_ref[sb, :, cs]) * g_ref[sb, :, cs]).reshape(nb, tbs, sw)
        yield

    def run(*gens):
        live = list(gens)
        while live:
            live = [g for g in live if next(g, live) is not live]

    def tail(sb):
        for i in range(cps):
            yield from chunk_step(sb, i)
        yield from epilogue(sb)

    if nsub == 1:
        run(prologue(0))
        run(pair_phase(0))

        def chunk_body(i, carry):
            run(chunk_step(0, i))
            return carry

        lax.fori_loop(0, cps, chunk_body, 0)
        run(epilogue(0))
    else:
        run(prologue(0))
        for sb in range(nsub):
            run(pair_phase(sb), *([prologue(sb + 1)] if sb + 1 < nsub else []), *([tail(sb - 1)] if sb else []))
        run(tail(nsub - 1))

    if nb == 1:
        last = p_ref[tb - 1:tb, :].astype(F32).reshape(1, 1, GROUP_PAD)
    else:
        last = p_ref[...].astype(F32).reshape(nb, tb, GROUP_PAD)[:, tb - 1:tb, :]
    carry_ref[...] = last
    shout_ref[...] = last

    @pl.when(j == pl.num_programs(1) - 1)
    def _():
        for p in pairs:
            sp = st_ref[:, p]
            sout_ref[:, 2 * p] = sp[:, :, 0:hd]
            sout_ref[:, 2 * p + 1] = sp[:, :, hd:]


def _rwkv(p, shift_prev, s0, wts, nb, tb, c, pp):
    bsz = s0.shape[0]
    seq = p.shape[0] // bsz
    m_rows = ROWS
    nsub = nb * tb // m_rows
    assert nsub * m_rows == nb * tb and (nsub == 1 or nb == 1)
    hd = RW_HEAD_DIM
    const = lambda shape: pl.BlockSpec(shape, lambda i, j: (0,) * len(shape))
    kern = functools.partial(_rwkv_kernel, nb=nb, tb=tb, c=c, pp=pp)
    return pl.pallas_call(
        kern,
        grid=(bsz // nb, seq // tb),
        in_specs=[
            pl.BlockSpec((nb * tb, GROUP_PAD), lambda i, j: (i * (seq // tb) + j, 0)),
            pl.BlockSpec((nb, 1, GROUP_PAD), lambda i, j: (i, 0, 0)),
            pl.BlockSpec((nb, RW_HEADS, hd, hd), lambda i, j: (i, 0, 0, 0)),
            const((1, GROUP_PAD)),
            const((LANES, 2 * RW_WIDTH)),
            const((1, RW_WIDTH)),
            const((1, RW_WIDTH)),
            const((2 * LANES, RW_WIDTH)),
            const((1, RW_WIDTH)),
            const((1, RW_WIDTH)),
            const((1, RW_WIDTH)),
            const((1, RW_WIDTH)),
            const((1, RW_WIDTH)),
            const((RW_WIDTH, LANES)),
            const((LANES, RW_WIDTH)),
            const((STRIP, STRIP)),
        ],
        out_specs=[
            pl.BlockSpec((nb, tb, RW_WIDTH), lambda i, j: (i, j, 0)),
            pl.BlockSpec((nb, 1, GROUP_PAD), lambda i, j: (i, 0, 0)),
            pl.BlockSpec((nb, RW_HEADS, hd, hd), lambda i, j: (i, 0, 0, 0)),
        ],
        out_shape=[
            jax.ShapeDtypeStruct((bsz, seq, RW_WIDTH), F32),
            jax.ShapeDtypeStruct((bsz, 1, GROUP_PAD), F32),
            jax.ShapeDtypeStruct((bsz, RW_HEADS, hd, hd), F32),
        ],
        scratch_shapes=[
            pltpu.VMEM((nb, 1, GROUP_PAD), F32),
            pltpu.VMEM((nb, RW_HEADS // 2, hd, 2 * hd), F32),
            pltpu.VMEM((nsub, 2 * m_rows, RW_WIDTH), F32),
            pltpu.VMEM((nsub, 2 * m_rows, RW_WIDTH), F32),
            pltpu.VMEM((nsub, m_rows, RW_WIDTH), F32),
            pltpu.VMEM((nsub, RW_WIDTH, 2 * m_rows), F32),
            pltpu.VMEM((nsub, RW_WIDTH, m_rows), F32),
            pltpu.VMEM((nsub, RW_HEADS // 2, 2 * hd, 4 * m_rows), BF16),
            pltpu.VMEM((nsub, RW_HEADS // 2, hd, 4 * m_rows), F32),
            pltpu.VMEM((nsub, RW_WIDTH, m_rows), F32),
            pltpu.VMEM((nsub, m_rows, RW_WIDTH), F32),
            pltpu.VMEM((nsub, m_rows, RW_WIDTH), F32),
        ],
        compiler_params=pltpu.CompilerParams(
            dimension_semantics=("parallel", "arbitrary"), vmem_limit_bytes=VMEM_LIMIT),
        name="rwkv7",
    )(p, shift_prev, s0, *wts)


def _gla_kernel(p_ref, s0_ref, gw2_ref, gb_ref, nw_ref, o_ref, st_ref,
                acc_ref, qd_ref, v_ref, kt_ref, et_ref, *, nb, tb, c, pp):
    j = pl.program_id(1)
    m_rows = nb * tb
    dk, dv = GLA_DK, GLA_DV

    @pl.when(j == 0)
    def _():
        st_ref[...] = s0_ref[...]

    p = p_ref[...].astype(F32)
    q = p[:, 0:GLA_KEY_WIDTH] * (dk ** -0.5)
    k = p[:, GLA_OFF_K:GLA_OFF_K + GLA_KEY_WIDTH]
    v = p[:, GLA_OFF_V:GLA_OFF_V + GLA_WIDTH]
    xs = p[:, GLA_OFF_GATE:GLA_OFF_GATE + LANES].astype(BF16)
    gout = p[:, GLA_OFF_GOUT:GLA_OFF_GOUT + GLA_WIDTH]
    gk = -_softplus(-(_dot(xs, gw2_ref[...]) + gb_ref[...])) / GLA_GATE_NORM

    same, ri, ci = _chunk_masks(m_rows, c)
    lmask = jnp.where(same & (ci <= ri), 1.0, 0.0).astype(BF16)
    causal = same & (ci <= ri)
    gcum = _mm(lmask, gk, pb=3)
    g3 = gcum.reshape(m_rows // c, c, GLA_KEY_WIDTH)
    gtot = jnp.broadcast_to(g3[:, c - 1:c, :], g3.shape).reshape(m_rows, GLA_KEY_WIDTH)
    qd = q * jnp.exp(gcum)
    kinv = k * jnp.exp(-gcum)
    qd_ref[...] = qd
    v_ref[...] = v
    kt_ref[...] = (k * jnp.exp(gtot - gcum)).T
    et_ref[...] = jnp.exp(gtot).T

    heads = range(GLA_HEADS)
    kq = [slice(h * dk, (h + 1) * dk) for h in heads]
    vq = [slice(h * dv, (h + 1) * dv) for h in heads]
    scores = [jnp.where(causal, _mm(qd[:, ks], kinv[:, ks], pp, pp, NT), 0.0) for ks in kq]
    intra = [_mm(a, v[:, vs], pp, pp) for a, vs in zip(scores, vq)]
    for vs, o in zip(vq, intra):
        acc_ref[:, vs] = o

    sh_c = c.bit_length() - 1
    cpb = tb // c
    span = min(m_rows, LANES)
    row_id = lax.broadcasted_iota(jnp.int32, (span, dv), 0) >> sh_c
    for i in range(m_rows // c):
        b = i // cpb
        r0 = i * c
        t0 = (r0 // span) * span
        rmask = row_id == (r0 - t0) // c
        states = [st_ref[b, h] for h in heads]
        inter = [_mm(qd_ref[r0:r0 + c, ks], s, pp, pp) for ks, s in zip(kq, states)]
        upd = [_mm(kt_ref[ks, t0:t0 + span], jnp.where(rmask, v_ref[t0:t0 + span, vs], 0.0), pp, pp)
               for ks, vs in zip(kq, vq)]
        for h, ks, vs, s, oi, d in zip(heads, kq, vq, states, inter, upd):
            acc_ref[r0:r0 + c, vs] += oi
            st_ref[b, h] = s * et_ref[ks, r0:r0 + 1] + d

    for vs in vq:
        o = acc_ref[:, vs]
        on = o * lax.rsqrt(jnp.mean(o * o, axis=-1, keepdims=True) + HEAD_NORM_EPS) * nw_ref[...]
        gh = gout[:, vs]
        o_ref[:, :, vs] = (on * (gh * _sigmoid(gh))).reshape(nb, tb, dv)


def _gla(p, s0, wts, nb, tb, c, pp):
    bsz = s0.shape[0]
    seq = p.shape[0] // bsz
    m_rows = nb * tb
    const = lambda shape: pl.BlockSpec(shape, lambda i, j: (0,) * len(shape))
    kern = functools.partial(_gla_kernel, nb=nb, tb=tb, c=c, pp=pp)
    return pl.pallas_call(
        kern,
        grid=(bsz // nb, seq // tb),
        in_specs=[
            pl.BlockSpec((nb * tb, GROUP_PAD), lambda i, j: (i * (seq // tb) + j, 1)),
            pl.BlockSpec((nb, GLA_HEADS, GLA_DK, GLA_DV), lambda i, j: (i, 0, 0, 0)),
            const((LANES, GLA_KEY_WIDTH)),
            const((1, GLA_KEY_WIDTH)),
            const((1, GLA_DV)),
        ],
        out_specs=[
            pl.BlockSpec((nb, tb, GLA_WIDTH), lambda i, j: (i, j, 0)),
            pl.BlockSpec((nb, GLA_HEADS, GLA_DK, GLA_DV), lambda i, j: (i, 0, 0, 0)),
        ],
        out_shape=[
            jax.ShapeDtypeStruct((bsz, seq, GLA_WIDTH), F32),
            jax.ShapeDtypeStruct((bsz, GLA_HEADS, GLA_DK, GLA_DV), F32),
        ],
        scratch_shapes=[
            pltpu.VMEM((m_rows, GLA_WIDTH), F32),
            pltpu.VMEM((m_rows, GLA_KEY_WIDTH), F32),
            pltpu.VMEM((m_rows, GLA_WIDTH), F32),
            pltpu.VMEM((GLA_KEY_WIDTH, m_rows), F32),
            pltpu.VMEM((GLA_KEY_WIDTH, m_rows), F32),
        ],
        compiler_params=pltpu.CompilerParams(
            dimension_semantics=("parallel", "arbitrary"), vmem_limit_bytes=VMEM_LIMIT),
        name="gla",
    )(p, s0, *wts)


def _post_kernel(x_ref, orw_ref, ogla_ref, wo_ref, g2_ref, wu_ref, wd_ref, gf_ref, o_ref, h_ref):
    jf = pl.program_id(1)

    @pl.when(jf == 0)
    def _():
        x1 = (x_ref[...]
              + _dot(orw_ref[...].astype(BF16), wo_ref[0:RW_WIDTH, :])
              + _dot(ogla_ref[...].astype(BF16), wo_ref[RW_WIDTH:, :]))
        o_ref[...] = x1
        ms = jnp.mean(x1 * x1, axis=-1, keepdims=True)
        h_ref[...] = (x1 * lax.rsqrt(ms + NORM_EPS) * g2_ref[...]).astype(BF16)

    u = jnp.maximum(_dot(h_ref[...], wu_ref[...]), 0.0)
    o_ref[...] += _dot((u * u).astype(BF16), wd_ref[...])

    @pl.when(jf == pl.num_programs(1) - 1)
    def _():
        x2 = o_ref[...]
        ms = jnp.mean(x2 * x2, axis=-1, keepdims=True)
        o_ref[...] = x2 * lax.rsqrt(ms + NORM_EPS) * gf_ref[...]


def _post(x2d, o_rw, o_gla, w_out, g2, w_up, w_down, gf, tm, tf):
    t = x2d.shape[0]
    return pl.pallas_call(
        _post_kernel,
        grid=(t // tm, D_FF // tf),
        in_specs=[
            pl.BlockSpec((tm, D_MODEL), lambda i, j: (i, 0)),
            pl.BlockSpec((tm, RW_WIDTH), lambda i, j: (i, 0)),
            pl.BlockSpec((tm, GLA_WIDTH), lambda i, j: (i, 0)),
            pl.BlockSpec((D_MODEL, D_MODEL), lambda i, j: (0, 0), pipeline_mode=pl.Buffered(1)),
            pl.BlockSpec((1, D_MODEL), lambda i, j: (0, 0)),
            pl.BlockSpec((D_MODEL, tf), lambda i, j: (0, j)),
            pl.BlockSpec((tf, D_MODEL), lambda i, j: (j, 0)),
            pl.BlockSpec((1, D_MODEL), lambda i, j: (0, 0)),
        ],
        out_specs=pl.BlockSpec((tm, D_MODEL), lambda i, j: (i, 0)),
        out_shape=jax.ShapeDtypeStruct((t, D_MODEL), F32),
        scratch_shapes=[pltpu.VMEM((tm, D_MODEL), BF16)],
        compiler_params=pltpu.CompilerParams(
            dimension_semantics=("parallel", "arbitrary"), vmem_limit_bytes=VMEM_LIMIT),
        name="post",
    )(x2d, o_rw, o_gla, w_out, g2, w_up, w_down, gf)


def _pad_cols(w, n):
    return jnp.pad(w, ((0, 0), (0, n - w.shape[1])))


def _prep_weights(w_in, rw_mu, rw_w0, rw_w2, rw_a0, rw_a2, rw_g2, rw_k_k, rw_k_a, rw_r_k, rw_ln_w, rw_ln_b,
                  gla_gw2, gla_gb, gla_norm_w):
    go = RW_PROJ
    w_t = jnp.swapaxes(w_in, 0, 1).astype(BF16)
    pad_rows = lambda w, n: jnp.pad(w, ((0, n - w.shape[0]), (0, 0)))
    w_p = jnp.concatenate([
        pad_rows(w_t[:RW_PROJ], GROUP_PAD),
        pad_rows(w_t[go:go + GLA_OFF_GATE + GLA_GATE_RANK], GLA_OFF_GOUT),
        pad_rows(w_t[go + GLA_OFF_GATE + GLA_GATE_RANK:], GROUP_PAD - GLA_OFF_GOUT),
    ], axis=0)

    row = lambda x: x.reshape(1, -1).astype(F32)
    mu = _pad_cols(row(rw_mu), GROUP_PAD)
    wwa = jnp.zeros((LANES, 2 * RW_WIDTH), F32)
    wwa = wwa.at[0:RW_LORA_W, 0:RW_WIDTH].set(rw_w2)
    wwa = wwa.at[RW_LORA_W:RW_LORA_W + RW_LORA_A, RW_WIDTH:].set(rw_a2)
    g2p = jnp.pad(rw_g2, ((0, 2 * LANES - RW_LORA_G), (0, 0)))
    head_of_col = jnp.arange(RW_WIDTH) // RW_HEAD_DIM
    e1 = (head_of_col[:, None] == jnp.arange(LANES)[None, :]).astype(BF16)
    rw_wts = (mu, wwa.astype(BF16), row(rw_w0), row(rw_a0), g2p.astype(BF16), row(rw_k_k), row(rw_k_a),
              row(rw_r_k), row(rw_ln_w), row(rw_ln_b), e1, e1.T,
              (head_of_col[:STRIP, None] == head_of_col[None, :STRIP]).astype(BF16))
    gw2p = jnp.pad(gla_gw2, ((0, LANES - GLA_GATE_RANK), (0, 0))).astype(BF16)
    gla_wts = (gw2p, row(gla_gb), row(gla_norm_w))
    return w_p, rw_wts, gla_wts


RW_PIECES = 1
GLA_PIECES = 1
PAIR_LOCKSTEP = 8
RW_STEP_ROWS = 2 * ROWS
STRIP = 256


def _trunk(x, shift, s_rw, s_gla, norm1_g, w_p, rw_wts, gla_wts, w_out, norm2_g, w_up, w_down, norm_f_g,
           nb, tb, c_rw, c_gla, tm):
    bsz, seq, _ = x.shape
    t = bsz * seq
    x2d = x.reshape(t, D_MODEL)
    proj = _inproj(x2d, norm1_g.reshape(1, -1), w_p, tm, INPROJ_TN)
    shift_p = _pad_cols(shift, GROUP_PAD).reshape(bsz, 1, GROUP_PAD)
    rw_tb = RW_STEP_ROWS if nb == 1 and seq % RW_STEP_ROWS == 0 else tb
    o_rw, sh_new, s_rw_new = _rwkv(proj, shift_p, s_rw, rw_wts, nb, rw_tb, c_rw, RW_PIECES)
    o_gla, s_gla_new = _gla(proj, s_gla, gla_wts, nb, rw_tb, c_gla, GLA_PIECES)
    y = _post(x2d, o_rw.reshape(t, RW_WIDTH), o_gla.reshape(t, GLA_WIDTH), w_out, norm2_g.reshape(1, -1),
              w_up, w_down, norm_f_g.reshape(1, -1), POST_TM, POST_TF)
    return (y.reshape(bsz, seq, D_MODEL), sh_new[:, 0, :RW_PROJ][None], s_rw_new[None], s_gla_new[None])


def kernel(x_prompt, x_sample, state_rwkv_shift, state_rwkv_wkv, state_gla, norm1_g, w_in, rw_mu, rw_w0,
           rw_w2, rw_a0, rw_a2, rw_g2, rw_k_k, rw_k_a, rw_r_k, rw_ln_w, rw_ln_b, gla_gw2, gla_gb, gla_norm_w,
           w_out, norm2_g, w_up, w_down, norm_f_g):
    w_p, rw_wts, gla_wts = _prep_weights(
        w_in[0], rw_mu[0], rw_w0[0], rw_w2[0], rw_a0[0], rw_a2[0], rw_g2[0], rw_k_k[0], rw_k_a[0],
        rw_r_k[0].reshape(-1), rw_ln_w[0], rw_ln_b[0], gla_gw2[0], gla_gb[0], gla_norm_w[0])
    shared = (norm1_g[0], w_p, rw_wts, gla_wts, w_out[0].astype(BF16), norm2_g[0], w_up[0].astype(BF16),
              w_down[0].astype(BF16), norm_f_g)

    bp, lp, _ = x_prompt.shape
    bs, ls, _ = x_sample.shape
    dt = x_prompt.dtype
    out_p = _trunk(x_prompt, jnp.zeros((bp, RW_PROJ), dt),
                   jnp.zeros((bp, RW_HEADS, RW_HEAD_DIM, RW_HEAD_DIM), dt),
                   jnp.zeros((bp, GLA_HEADS, GLA_DK, GLA_DV), dt), *shared,
                   nb=1, tb=ROWS, c_rw=64, c_gla=GLA_CHUNK, tm=1024)
    out_s = _trunk(x_sample, state_rwkv_shift[0], state_rwkv_wkv[0], state_gla[0], *shared,
                   nb=ROWS // ls, tb=ls, c_rw=ls, c_gla=ls, tm=1024)
    return (out_p[0], out_s[0], out_p[1], out_p[2], out_p[3], out_s[1], out_s[2], out_s[3])
```

```python
import functools

import jax
import jax.numpy as jnp
from jax import lax
from jax.experimental import pallas as pl
from jax.experimental.pallas import tpu as pltpu

F32 = jnp.float32
BF16 = jnp.bfloat16

D_MODEL = 2048
RW_WIDTH = 1024
RW_HEADS = 16
RW_HEAD_DIM = 64
RW_LORA_W = 64
RW_LORA_A = 64
RW_LORA_G = 160
RW_PROJ = 3 * RW_WIDTH + RW_LORA_W + RW_LORA_A + RW_LORA_G
RW_GN_EPS = 64e-5
GLA_WIDTH = 1024
GLA_HEADS = 4
GLA_KEY_WIDTH = 512
GLA_DK = 128
GLA_DV = 256
GLA_GATE_RANK = 16
GLA_GATE_NORM = 16.0
GLA_CHUNK = 64
GLA_PROJ = 2 * GLA_KEY_WIDTH + GLA_WIDTH + GLA_GATE_RANK + GLA_WIDTH
D_FF = 4 * D_MODEL
NORM_EPS = 1e-6
HEAD_NORM_EPS = 1e-5
DECAY_SCALE = 0.6065306597126334

LANES = 128
SUBLANES = 8

GROUP_PAD = 3456
RW_OFF_XWA = 3 * RW_WIDTH
RW_OFF_XG = RW_OFF_XWA + LANES
GLA_OFF_K = GLA_KEY_WIDTH
GLA_OFF_V = 2 * GLA_KEY_WIDTH
GLA_OFF_GATE = GLA_OFF_V + GLA_WIDTH
GLA_OFF_GOUT = GLA_OFF_GATE + LANES
NP = 2 * GROUP_PAD

ROWS = 128
MXU_WIDTH = 256
INPROJ_TN = 9 * MXU_WIDTH
PROJ_DTYPE = BF16
POST_TM = 512
POST_TF = 1024
VMEM_LIMIT = 56 * 1024 * 1024

NN = (((1,), (0,)), ((), ()))
NT = (((1,), (1,)), ((), ()))


def _dot(a, b, dims=NN):
    return lax.dot_general(a, b, dims, preferred_element_type=F32)


def _parts(x, n):
    if x.dtype == BF16:
        return [x]
    out = []
    rem = x
    for i in range(n):
        h = rem.astype(BF16)
        out.append(h)
        if i + 1 < n:
            rem = rem - h.astype(F32)
    return out


def _mm(a, b, pa=1, pb=1, dims=NN):
    pa = 1 if a.dtype == BF16 else pa
    pb = 1 if b.dtype == BF16 else pb
    aa = _parts(a, pa)
    bb = _parts(b, pb)
    n = max(pa, pb)
    if dims == NN and min(pa, pb) == 1 and n * a.shape[1] <= MXU_WIDTH:
        return _dot(jnp.concatenate(aa * (n // pa), axis=1), jnp.concatenate(bb * (n // pb), axis=0))
    acc = None
    for i in range(pa):
        for j in range(pb):
            if i + j < n:
                t = _dot(aa[i], bb[j], dims)
                acc = t if acc is None else acc + t
    return acc


def _softplus(x):
    return jnp.maximum(x, 0.0) + jnp.log(1.0 + jnp.exp(-jnp.abs(x)))


def _sigmoid(x):
    return 0.5 * jnp.tanh(0.5 * x) + 0.5


def _chunk_masks(m, c):
    sh = c.bit_length() - 1
    ri = lax.broadcasted_iota(jnp.int32, (m, m), 0)
    ci = lax.broadcasted_iota(jnp.int32, (m, m), 1)
    same = (ri >> sh) == (ci >> sh)
    return same, ri, ci


def _inproj_kernel(x_ref, g_ref, w_ref, o_ref, h_ref):
    @pl.when(pl.program_id(1) == 0)
    def _():
        x = x_ref[...]
        ms = jnp.mean(x * x, axis=-1, keepdims=True)
        h_ref[...] = (x * lax.rsqrt(ms + NORM_EPS) * g_ref[...]).astype(BF16)

    o_ref[...] = _dot(h_ref[...], w_ref[...], NT).astype(o_ref.dtype)


def _inproj(x2d, g, w_p, tm, tn):
    t = x2d.shape[0]
    return pl.pallas_call(
        _inproj_kernel,
        grid=(t // tm, NP // tn),
        in_specs=[
            pl.BlockSpec((tm, D_MODEL), lambda i, j: (i, 0)),
            pl.BlockSpec((1, D_MODEL), lambda i, j: (0, 0)),
            pl.BlockSpec((tn, D_MODEL), lambda i, j: (j, 0)),
        ],
        out_specs=pl.BlockSpec((tm, tn), lambda i, j: (i, j)),
        out_shape=jax.ShapeDtypeStruct((t, NP), PROJ_DTYPE),
        scratch_shapes=[pltpu.VMEM((tm, D_MODEL), BF16)],
        compiler_params=pltpu.CompilerParams(
            dimension_semantics=("parallel", "arbitrary"), vmem_limit_bytes=VMEM_LIMIT),
        name="inproj",
    )(x2d, g, w_p)


def _rwkv_kernel(p_ref, sh_ref, s0_ref, mu_ref, wwa_ref, w0_ref, a0_ref, g2_ref, kk_ref, ka_ref,
                 rk_ref, lnw_ref, lnb_ref, e1_ref, e2_ref, e12_ref,
                 o_ref, shout_ref, sout_ref,
                 carry_ref, st_ref, bk_ref, kb_ref, pc_ref, art_ref, vt_ref, arbd_ref, uy_ref, yt_ref,
                 g_ref, bonus_ref, *, nb, tb, c, pp):
    j = pl.program_id(1)
    m_rows = ROWS
    nsub = nb * tb // m_rows
    tbs = tb // nsub
    n_lvl = c.bit_length() - 1
    hd = RW_HEAD_DIM

    @pl.when(j == 0)
    def _():
        for p in range(RW_HEADS // 2):
            st_ref[:, p] = jnp.concatenate([s0_ref[:, 2 * p], s0_ref[:, 2 * p + 1]], axis=-1)
        carry_ref[...] = sh_ref[...]

    yt_ref[...] = jnp.zeros_like(yt_ref)

    def shifted(sb, c0, w):
        r0 = sb * m_rows
        pc3 = p_ref[r0:r0 + m_rows, c0:c0 + w].astype(F32).reshape(nb, tbs, w)
        if sb == 0:
            before = carry_ref[:, :, c0:c0 + w]
        else:
            before = p_ref[r0 - 1:r0, c0:c0 + w].astype(F32).reshape(1, 1, w)
        t3 = lax.broadcasted_iota(jnp.int32, pc3.shape, 1)
        prev3 = jnp.where(t3 == 0, before, pltpu.roll(pc3, 1, axis=1))
        return (pc3 + (prev3 - pc3) * mu_ref[:, c0:c0 + w]).reshape(m_rows, w)

    same, ri, ci = _chunk_masks(m_rows, c)
    lmask = jnp.where(same & (ci <= ri), 1.0, 0.0).astype(BF16)
    strict_t = same & (ri < ci)
    incl_t = same & (ri <= ci)

    sw = STRIP
    strips = [slice(c0, c0 + sw) for c0 in range(0, RW_WIDTH, sw)]

    def head_stat(x, cs):
        return _dot(x.astype(BF16), e1_ref[cs, :])

    def head_bcast(s, cs):
        return _mm(s, e2_ref[:, cs], pa=2)

    def head_sum(x):
        return _dot(x.astype(BF16), e12_ref[...])


    def prologue(sb):
        slab = shifted(sb, RW_OFF_XWA, LANES)
        lane = lax.broadcasted_iota(jnp.int32, slab.shape, 1)
        lhs = jnp.where(lane < RW_LORA_W, jnp.tanh(slab), slab).astype(BF16)
        sg = _sigmoid(shifted(sb, RW_OFF_XG, 2 * LANES)).astype(BF16)
        yield
        lws = [_dot(lhs, wwa_ref[:, cs]) for cs in strips]
        las = [_dot(lhs, wwa_ref[:, RW_WIDTH + cs.start:RW_WIDTH + cs.stop]) for cs in strips]
        for cs in strips:
            g_ref[sb, :, cs] = _dot(sg, g2_ref[:, cs])
        yield
        logws = [-DECAY_SCALE * _sigmoid(w0_ref[:, cs] + lw) for cs, lw in zip(strips, lws)]
        gcums = [_mm(lmask, logw, pb=2) for logw in logws]
        yield
        avs = [_sigmoid(a0_ref[:, cs] + la) for cs, la in zip(strips, las)]
        ks = [shifted(sb, RW_WIDTH + cs.start, sw) for cs in strips]
        kkfs = [k * kk_ref[:, cs] for cs, k in zip(strips, ks)]
        rinvs = [lax.rsqrt(jnp.maximum(head_stat(kkf * kkf, cs), 1e-24)) for cs, kkf in zip(strips, kkfs)]
        yield
        k2s = [k * (1.0 + (a - 1.0) * ka_ref[:, cs]) for cs, k, a in zip(strips, ks, avs)]
        rs = [shifted(sb, cs.start, sw) for cs in strips]
        bsums = [head_sum(r * k2 * rk_ref[:, cs]) for cs, r, k2 in zip(strips, rs, k2s)]
        kks = [kkf * head_bcast(rinv, cs) for cs, kkf, rinv in zip(strips, kkfs, rinvs)]
        yield
        for cs, bsum in zip(strips, bsums):
            v = shifted(sb, 2 * RW_WIDTH + cs.start, sw)
            bonus_ref[sb, :, cs] = bsum * v
            vt_ref[sb, cs, :] = v.T
        yield
        for cs, logw, gcum, a, kk, k2, r in zip(strips, logws, gcums, avs, kks, k2s, rs):
            beta = kk * a
            g3 = gcum.reshape(cps, c, sw)
            gtot = jnp.broadcast_to(g3[:, c - 1:c, :], g3.shape).reshape(m_rows, sw)
            e_inv = jnp.exp(-gcum)
            e_rev = jnp.exp(gtot - gcum)
            art_ref[sb, cs, 0:m_rows] = (-kk * jnp.exp(gcum - logw)).T
            art_ref[sb, cs, m_rows:2 * m_rows] = (r * jnp.exp(gcum)).T
            bk_ref[sb, 0:m_rows, cs] = beta * e_inv
            bk_ref[sb, m_rows:, cs] = k2 * e_inv
            kb_ref[sb, 0:m_rows, cs] = k2 * e_rev
            kb_ref[sb, m_rows:, cs] = beta * e_rev
            pc_ref[sb, :, cs] = jnp.exp(gtot)
            yield

    pw = 2 * hd
    cat = jnp.concatenate
    top = lax.broadcasted_iota(jnp.int32, (pw, 1), 0) < hd
    left = lax.broadcasted_iota(jnp.int32, (1, 2 * m_rows), 1) < m_rows

    def split_rows(x):
        return cat([jnp.where(top, x, 0.0), jnp.where(top, 0.0, x)], axis=1)

    def block_diag(xc):
        return cat([jnp.where(left, xc, 0.0), jnp.where(left, 0.0, xc)], axis=0)

    def blk(xt, u, rh, ch):
        col = (2 * u + ch) * m_rows
        return xt[rh * m_rows:(rh + 1) * m_rows, col:col + m_rows]

    n_pairs = RW_HEADS // 2
    pairs = range(n_pairs)
    los = [p * pw for p in pairs]

    def pair_phase(sb):
        for g0 in range(0, n_pairs, PAIR_LOCKSTEP):
            yield from pair_group(sb, pairs[g0:g0 + PAIR_LOCKSTEP], los[g0:g0 + PAIR_LOCKSTEP])

    def pair_group(sb, pairs, los):
        arts = [art_ref[sb, lo:lo + pw, :] for lo in los]
        xts = [_mm(bk_ref[sb, :, lo:lo + pw], split_rows(art), pp, pp) for lo, art in zip(los, arts)]
        yield
        npcs = [cat([jnp.where(strict_t, blk(xt, 0, 0, 0), 0.0), jnp.where(strict_t, blk(xt, 1, 0, 0), 0.0)], axis=1)
                for xt in xts]
        vas = [_mm(split_rows(vt_ref[sb, lo:lo + pw, :]),
                   cat([cat([jnp.where(strict_t, blk(xt, u, 1, 0), 0.0), jnp.where(incl_t, blk(xt, u, 1, 1), 0.0)],
                            axis=1) for u in range(2)], axis=0), pp, pp)
               for lo, xt in zip(los, xts)]
        yield
        zcs = [cat([cat([art[u * hd:(u + 1) * hd, 0:m_rows], va[u * hd:(u + 1) * hd, 0:m_rows]], axis=0)
                    for u in range(2)], axis=1) for art, va in zip(arts, vas)]
        for lvl in range(n_lvl):
            bds = [block_diag(npc) for npc in npcs]
            if lvl + 1 < n_lvl:
                ts = [_mm(cat([zc, npc], axis=0), bd, pp, pp) for zc, npc, bd in zip(zcs, npcs, bds)]
                zcs = [zc + t[0:2 * hd] for zc, t in zip(zcs, ts)]
                npcs = [t[2 * hd:] for t in ts]
            else:
                zcs = [zc + _mm(zc, bd, pp, pp) for zc, bd in zip(zcs, bds)]
            yield
        rycs = [cat([cat([art[u * hd:(u + 1) * hd, m_rows:], va[u * hd:(u + 1) * hd, m_rows:]], axis=0)
                     for u in range(2)], axis=1)
                + _mm(zc, block_diag(cat([jnp.where(incl_t, blk(xt, 0, 0, 1), 0.0),
                                          jnp.where(incl_t, blk(xt, 1, 0, 1), 0.0)], axis=1)), pp, pp)
                for art, va, zc, xt in zip(arts, vas, zcs, xts)]
        yield
        zero = jnp.zeros((hd, 2 * m_rows), BF16)
        for p, zc, ry in zip(pairs, zcs, rycs):
            arbd_ref[sb, p, 0:hd, 0:2 * m_rows] = cat([zc[0:hd, 0:m_rows], ry[0:hd, 0:m_rows]], axis=1).astype(BF16)
            arbd_ref[sb, p, 0:hd, 2 * m_rows:] = zero
            arbd_ref[sb, p, hd:, 0:2 * m_rows] = zero
            arbd_ref[sb, p, hd:, 2 * m_rows:] = cat([zc[0:hd, m_rows:], ry[0:hd, m_rows:]], axis=1).astype(BF16)
            uy_ref[sb, p] = cat([zc[hd:, 0:m_rows], ry[hd:, 0:m_rows], zc[hd:, m_rows:], ry[hd:, m_rows:]], axis=1)
        yield

    sh_c = c.bit_length() - 1
    cps = m_rows // c
    cpb = tbs // c
    row_id2 = (lax.broadcasted_iota(jnp.int32, (2 * m_rows, pw), 0) & (m_rows - 1)) >> sh_c
    col_id = lax.broadcasted_iota(jnp.int32, (hd, m_rows), 1) >> sh_c
    lane_lo = lax.broadcasted_iota(jnp.int32, (1, pw), 1) < hd

    def chunk_step(sb, i):
        b = i // cpb
        r0 = i * c
        rmask2 = row_id2 == i
        cmask = col_id == i
        ss = [st_ref[b, p] for p in pairs]
        zss = [_mm(s, arbd_ref[sb, p], pp, pp) + uy_ref[sb, p] for p, s in zip(pairs, ss)]
        yield
        upd = []
        for lo, zs in zip(los, zss):
            kbz = jnp.where(rmask2, kb_ref[sb, :, lo:lo + pw], 0.0)
            rhs = cat([jnp.where(lane_lo, kbz, 0.0), jnp.where(lane_lo, 0.0, kbz)], axis=0)
            vu = cat([vt_ref[sb, lo:lo + hd, :], zs[:, 0:m_rows],
                      vt_ref[sb, lo + hd:lo + pw, :], zs[:, 2 * m_rows:3 * m_rows]], axis=1)
            upd.append(_mm(vu, rhs, pp, pp))
        yield
        for p, lo, s, zs, d in zip(pairs, los, ss, zss, upd):
            st_ref[b, p] = s * pc_ref[sb, pl.ds(r0, 1), lo:lo + pw] + d
            for u in range(2):
                rows = slice((2 * p + u) * hd, (2 * p + u + 1) * hd)
                yt_ref[sb, rows, :] = jnp.where(cmask, zs[:, (2 * u + 1) * m_rows:(2 * u + 2) * m_rows],
                                               yt_ref[sb, rows, :])
        yield

    inv_n = 1.0 / hd

    def epilogue(sb):
        ys = [yt_ref[sb, cs, :].T for cs in strips]
        ycs = [y - head_sum(y) * inv_n for y in ys]
        yield
        rstds = [lax.rsqrt(head_stat(yc * yc, cs) * inv_n + RW_GN_EPS) for cs, yc in zip(strips, ycs)]
        yield
        for cs, yc, rstd in zip(strips, ycs, rstds):
            yn = yc * head_bcast(rstd, cs) * lnw_ref[:, cs] + lnb_ref[:, cs]
            o_ref[:, sb * tbs:(sb + 1) * tbs, cs] = (
                (yn + bonus_ref[sb, :, cs]) * g_ref[sb, :, cs]).reshape(nb, tbs, sw)
        yield

    def run(*gens):
        live = list(gens)
        while live:
            live = [g for g in live if next(g, live) is not live]

    def tail(sb):
        for i in range(cps):
            yield from chunk_step(sb, i)
        yield from epilogue(sb)

    if nsub == 1:
        run(prologue(0))
        run(pair_phase(0))

        def chunk_body(i, carry):
            run(chunk_step(0, i))
            return carry

        lax.fori_loop(0, cps, chunk_body, 0)
        run(epilogue(0))
    else:
        run(prologue(0))
        for sb in range(nsub):
            run(pair_phase(sb), *([prologue(sb + 1)] if sb + 1 < nsub else []), *([tail(sb - 1)] if sb else []))
        run(tail(nsub - 1))

    if nb == 1:
        last = p_ref[tb - 1:tb, :].astype(F32).reshape(1, 1, GROUP_PAD)
    else:
        last = p_ref[...].astype(F32).reshape(nb, tb, GROUP_PAD)[:, tb - 1:tb, :]
    carry_ref[...] = last
    shout_ref[...] = last

    @pl.when(j == pl.num_programs(1) - 1)
    def _():
        for p in pairs:
            sp = st_ref[:, p]
            sout_ref[:, 2 * p] = sp[:, :, 0:hd]
            sout_ref[:, 2 * p + 1] = sp[:, :, hd:]


def _rwkv(p, shift_prev, s0, wts, nb, tb, c, pp):
    bsz = s0.shape[0]
    seq = p.shape[0] // bsz
    m_rows = ROWS
    nsub = nb * tb // m_rows
    assert nsub * m_rows == nb * tb and (nsub == 1 or nb == 1)
    hd = RW_HEAD_DIM
    const = lambda shape: pl.BlockSpec(shape, lambda i, j: (0,) * len(shape))
    kern = functools.partial(_rwkv_kernel, nb=nb, tb=tb, c=c, pp=pp)
    return pl.pallas_call(
        kern,
        grid=(bsz // nb, seq // tb),
        in_specs=[
            pl.BlockSpec((nb * tb, GROUP_PAD), lambda i, j: (i * (seq // tb) + j, 0)),
            pl.BlockSpec((nb, 1, GROUP_PAD), lambda i, j: (i, 0, 0)),
            pl.BlockSpec((nb, RW_HEADS, hd, hd), lambda i, j: (i, 0, 0, 0)),
            const((1, GROUP_PAD)),
            const((LANES, 2 * RW_WIDTH)),
            const((1, RW_WIDTH)),
            const((1, RW_WIDTH)),
            const((2 * LANES, RW_WIDTH)),
            const((1, RW_WIDTH)),
            const((1, RW_WIDTH)),
            const((1, RW_WIDTH)),
            const((1, RW_WIDTH)),
            const((1, RW_WIDTH)),
            const((RW_WIDTH, LANES)),
            const((LANES, RW_WIDTH)),
            const((STRIP, STRIP)),
        ],
        out_specs=[
            pl.BlockSpec((nb, tb, RW_WIDTH), lambda i, j: (i, j, 0)),
            pl.BlockSpec((nb, 1, GROUP_PAD), lambda i, j: (i, 0, 0)),
            pl.BlockSpec((nb, RW_HEADS, hd, hd), lambda i, j: (i, 0, 0, 0)),
        ],
        out_shape=[
            jax.ShapeDtypeStruct((bsz, seq, RW_WIDTH), F32),
            jax.ShapeDtypeStruct((bsz, 1, GROUP_PAD), F32),
            jax.ShapeDtypeStruct((bsz, RW_HEADS, hd, hd), F32),
        ],
        scratch_shapes=[
            pltpu.VMEM((nb, 1, GROUP_PAD), F32),
            pltpu.VMEM((nb, RW_HEADS // 2, hd, 2 * hd), F32),
            pltpu.VMEM((nsub, 2 * m_rows, RW_WIDTH), F32),
            pltpu.VMEM((nsub, 2 * m_rows, RW_WIDTH), F32),
            pltpu.VMEM((nsub, m_rows, RW_WIDTH), F32),
            pltpu.VMEM((nsub, RW_WIDTH, 2 * m_rows), F32),
            pltpu.VMEM((nsub, RW_WIDTH, m_rows), F32),
            pltpu.VMEM((nsub, RW_HEADS // 2, 2 * hd, 4 * m_rows), BF16),
            pltpu.VMEM((nsub, RW_HEADS // 2, hd, 4 * m_rows), F32),
            pltpu.VMEM((nsub, RW_WIDTH, m_rows), F32),
            pltpu.VMEM((nsub, m_rows, RW_WIDTH), F32),
            pltpu.VMEM((nsub, m_rows, RW_WIDTH), F32),
        ],
        compiler_params=pltpu.CompilerParams(
            dimension_semantics=("parallel", "arbitrary"), vmem_limit_bytes=VMEM_LIMIT),
        name="rwkv7",
    )(p, shift_prev, s0, *wts)


def _gla_kernel(p_ref, s0_ref, gw2_ref, gb_ref, nw_ref, o_ref, st_ref,
                acc_ref, qd_ref, v_ref, kt_ref, et_ref, *, nb, tb, c, pp):
    j = pl.program_id(1)
    m_rows = nb * tb
    dk, dv = GLA_DK, GLA_DV

    @pl.when(j == 0)
    def _():
        st_ref[...] = s0_ref[...]

    p = p_ref[...].astype(F32)
    q = p[:, 0:GLA_KEY_WIDTH] * (dk ** -0.5)
    k = p[:, GLA_OFF_K:GLA_OFF_K + GLA_KEY_WIDTH]
    v = p[:, GLA_OFF_V:GLA_OFF_V + GLA_WIDTH]
    xs = p[:, GLA_OFF_GATE:GLA_OFF_GATE + LANES].astype(BF16)
    gout = p[:, GLA_OFF_GOUT:GLA_OFF_GOUT + GLA_WIDTH]
    gk = -_softplus(-(_dot(xs, gw2_ref[...]) + gb_ref[...])) / GLA_GATE_NORM

    same, ri, ci = _chunk_masks(m_rows, c)
    lmask = jnp.where(same & (ci <= ri), 1.0, 0.0).astype(BF16)
    causal = same & (ci <= ri)
    gcum = _mm(lmask, gk, pb=3)
    g3 = gcum.reshape(m_rows // c, c, GLA_KEY_WIDTH)
    gtot = jnp.broadcast_to(g3[:, c - 1:c, :], g3.shape).reshape(m_rows, GLA_KEY_WIDTH)
    qd = q * jnp.exp(gcum)
    kinv = k * jnp.exp(-gcum)
    qd_ref[...] = qd
    v_ref[...] = v
    kt_ref[...] = (k * jnp.exp(gtot - gcum)).T
    et_ref[...] = jnp.exp(gtot).T

    heads = range(GLA_HEADS)
    kq = [slice(h * dk, (h + 1) * dk) for h in heads]
    vq = [slice(h * dv, (h + 1) * dv) for h in heads]
    scores = [jnp.where(causal, _mm(qd[:, ks], kinv[:, ks], pp, pp, NT), 0.0) for ks in kq]
    intra = [_mm(a, v[:, vs], pp, pp) for a, vs in zip(scores, vq)]
    for vs, o in zip(vq, intra):
        acc_ref[:, vs] = o

    sh_c = c.bit_length() - 1
    cpb = tb // c
    span = min(m_rows, LANES)
    row_id = lax.broadcasted_iota(jnp.int32, (span, dv), 0) >> sh_c
    for i in range(m_rows // c):
        b = i // cpb
        r0 = i * c
        t0 = (r0 // span) * span
        rmask = row_id == (r0 - t0) // c
        states = [st_ref[b, h] for h in heads]
        inter = [_mm(qd_ref[r0:r0 + c, ks], s, pp, pp) for ks, s in zip(kq, states)]
        upd = [_mm(kt_ref[ks, t0:t0 + span], jnp.where(rmask, v_ref[t0:t0 + span, vs], 0.0), pp, pp)
               for ks, vs in zip(kq, vq)]
        for h, ks, vs, s, oi, d in zip(heads, kq, vq, states, inter, upd):
            acc_ref[r0:r0 + c, vs] += oi
            st_ref[b, h] = s * et_ref[ks, r0:r0 + 1] + d

    for vs in vq:
        o = acc_ref[:, vs]
        on = o * lax.rsqrt(jnp.mean(o * o, axis=-1, keepdims=True) + HEAD_NORM_EPS) * nw_ref[...]
        gh = gout[:, vs]
        o_ref[:, :, vs] = (on * (gh * _sigmoid(gh))).reshape(nb, tb, dv)


def _gla(p, s0, wts, nb, tb, c, pp):
    bsz = s0.shape[0]
    seq = p.shape[0] // bsz
    m_rows = nb * tb
    const = lambda shape: pl.BlockSpec(shape, lambda i, j: (0,) * len(shape))
    kern = functools.partial(_gla_kernel, nb=nb, tb=tb, c=c, pp=pp)
    return pl.pallas_call(
        kern,
        grid=(bsz // nb, seq // tb),
        in_specs=[
            pl.BlockSpec((nb * tb, GROUP_PAD), lambda i, j: (i * (seq // tb) + j, 1)),
            pl.BlockSpec((nb, GLA_HEADS, GLA_DK, GLA_DV), lambda i, j: (i, 0, 0, 0)),
            const((LANES, GLA_KEY_WIDTH)),
            const((1, GLA_KEY_WIDTH)),
            const((1, GLA_DV)),
        ],
        out_specs=[
            pl.BlockSpec((nb, tb, GLA_WIDTH), lambda i, j: (i, j, 0)),
            pl.BlockSpec((nb, GLA_HEADS, GLA_DK, GLA_DV), lambda i, j: (i, 0, 0, 0)),
        ],
        out_shape=[
            jax.ShapeDtypeStruct((bsz, seq, GLA_WIDTH), F32),
            jax.ShapeDtypeStruct((bsz, GLA_HEADS, GLA_DK, GLA_DV), F32),
        ],
        scratch_shapes=[
            pltpu.VMEM((m_rows, GLA_WIDTH), F32),
            pltpu.VMEM((m_rows, GLA_KEY_WIDTH), F32),
            pltpu.VMEM((m_rows, GLA_WIDTH), F32),
            pltpu.VMEM((GLA_KEY_WIDTH, m_rows), F32),
            pltpu.VMEM((GLA_KEY_WIDTH, m_rows), F32),
        ],
        compiler_params=pltpu.CompilerParams(
            dimension_semantics=("parallel", "arbitrary"), vmem_limit_bytes=VMEM_LIMIT),
        name="gla",
    )(p, s0, *wts)


def _post_kernel(x_ref, orw_ref, ogla_ref, wo_ref, g2_ref, wu_ref, wd_ref, gf_ref, o_ref, h_ref):
    jf = pl.program_id(1)

    @pl.when(jf == 0)
    def _():
        x1 = (x_ref[...]
              + _dot(orw_ref[...].astype(BF16), wo_ref[0:RW_WIDTH, :])
              + _dot(ogla_ref[...].astype(BF16), wo_ref[RW_WIDTH:, :]))
        o_ref[...] = x1
        ms = jnp.mean(x1 * x1, axis=-1, keepdims=True)
        h_ref[...] = (x1 * lax.rsqrt(ms + NORM_EPS) * g2_ref[...]).astype(BF16)

    u = jnp.maximum(_dot(h_ref[...], wu_ref[...]), 0.0)
    o_ref[...] += _dot((u * u).astype(BF16), wd_ref[...])

    @pl.when(jf == pl.num_programs(1) - 1)
    def _():
        x2 = o_ref[...]
        ms = jnp.mean(x2 * x2, axis=-1, keepdims=True)
        o_ref[...] = x2 * lax.rsqrt(ms + NORM_EPS) * gf_ref[...]


def _post(x2d, o_rw, o_gla, w_out, g2, w_up, w_down, gf, tm, tf):
    t = x2d.shape[0]
    return pl.pallas_call(
        _post_kernel,
        grid=(t // tm, D_FF // tf),
        in_specs=[
            pl.BlockSpec((tm, D_MODEL), lambda i, j: (i, 0)),
            pl.BlockSpec((tm, RW_WIDTH), lambda i, j: (i, 0)),
            pl.BlockSpec((tm, GLA_WIDTH), lambda i, j: (i, 0)),
            pl.BlockSpec((D_MODEL, D_MODEL), lambda i, j: (0, 0), pipeline_mode=pl.Buffered(1)),
            pl.BlockSpec((1, D_MODEL), lambda i, j: (0, 0)),
            pl.BlockSpec((D_MODEL, tf), lambda i, j: (0, j)),
            pl.BlockSpec((tf, D_MODEL), lambda i, j: (j, 0)),
            pl.BlockSpec((1, D_MODEL), lambda i, j: (0, 0)),
        ],
        out_specs=pl.BlockSpec((tm, D_MODEL), lambda i, j: (i, 0)),
        out_shape=jax.ShapeDtypeStruct((t, D_MODEL), F32),
        scratch_shapes=[pltpu.VMEM((tm, D_MODEL), BF16)],
        compiler_params=pltpu.CompilerParams(
            dimension_semantics=("parallel", "arbitrary"), vmem_limit_bytes=VMEM_LIMIT),
        name="post",
    )(x2d, o_rw, o_gla, w_out, g2, w_up, w_down, gf)


def _pad_cols(w, n):
    return jnp.pad(w, ((0, 0), (0, n - w.shape[1])))


def _prep_weights(w_in, rw_mu, rw_w0, rw_w2, rw_a0, rw_a2, rw_g2, rw_k_k, rw_k_a, rw_r_k, rw_ln_w, rw_ln_b,
                  gla_gw2, gla_gb, gla_norm_w):
    go = RW_PROJ
    w_t = jnp.swapaxes(w_in, 0, 1).astype(BF16)
    pad_rows = lambda w, n: jnp.pad(w, ((0, n - w.shape[0]), (0, 0)))
    w_p = jnp.concatenate([
        pad_rows(w_t[:RW_PROJ], GROUP_PAD),
        pad_rows(w_t[go:go + GLA_OFF_GATE + GLA_GATE_RANK], GLA_OFF_GOUT),
        pad_rows(w_t[go + GLA_OFF_GATE + GLA_GATE_RANK:], GROUP_PAD - GLA_OFF_GOUT),
    ], axis=0)

    row = lambda x: x.reshape(1, -1).astype(F32)
    mu = _pad_cols(row(rw_mu), GROUP_PAD)
    wwa = jnp.zeros((LANES, 2 * RW_WIDTH), F32)
    wwa = wwa.at[0:RW_LORA_W, 0:RW_WIDTH].set(rw_w2)
    wwa = wwa.at[RW_LORA_W:RW_LORA_W + RW_LORA_A, RW_WIDTH:].set(rw_a2)
    g2p = jnp.pad(rw_g2, ((0, 2 * LANES - RW_LORA_G), (0, 0)))
    head_of_col = jnp.arange(RW_WIDTH) // RW_HEAD_DIM
    e1 = (head_of_col[:, None] == jnp.arange(LANES)[None, :]).astype(BF16)
    rw_wts = (mu, wwa.astype(BF16), row(rw_w0), row(rw_a0), g2p.astype(BF16), row(rw_k_k), row(rw_k_a),
              row(rw_r_k), row(rw_ln_w), row(rw_ln_b), e1, e1.T,
              (head_of_col[:STRIP, None] == head_of_col[None, :STRIP]).astype(BF16))
    gw2p = jnp.pad(gla_gw2, ((0, LANES - GLA_GATE_RANK), (0, 0))).astype(BF16)
    gla_wts = (gw2p, row(gla_gb), row(gla_norm_w))
    return w_p, rw_wts, gla_wts


RW_PIECES = 1
GLA_PIECES = 1
PAIR_LOCKSTEP = 8
RW_STEP_ROWS = 2 * ROWS
STRIP = 256


def _trunk(x, shift, s_rw, s_gla, norm1_g, w_p, rw_wts, gla_wts, w_out, norm2_g, w_up, w_down, norm_f_g,
           nb, tb, c_rw, c_gla, tm):
    bsz, seq, _ = x.shape
    t = bsz * seq
    x2d = x.reshape(t, D_MODEL)
    proj = _inproj(x2d, norm1_g.reshape(1, -1), w_p, tm, INPROJ_TN)
    shift_p = _pad_cols(shift, GROUP_PAD).reshape(bsz, 1, GROUP_PAD)
    rw_tb = RW_STEP_ROWS if nb == 1 and seq % RW_STEP_ROWS == 0 else tb
    o_rw, sh_new, s_rw_new = _rwkv(proj, shift_p, s_rw, rw_wts, nb, rw_tb, c_rw, RW_PIECES)
    o_gla, s_gla_new = _gla(proj, s_gla, gla_wts, nb, rw_tb, c_gla, GLA_PIECES)
    y = _post(x2d, o_rw.reshape(t, RW_WIDTH), o_gla.reshape(t, GLA_WIDTH), w_out, norm2_g.reshape(1, -1),
              w_up, w_down, norm_f_g.reshape(1, -1), POST_TM, POST_TF)
    return (y.reshape(bsz, seq, D_MODEL), sh_new[:, 0, :RW_PROJ][None], s_rw_new[None], s_gla_new[None])


def kernel(x_prompt, x_sample, state_rwkv_shift, state_rwkv_wkv, state_gla, norm1_g, w_in, rw_mu, rw_w0,
           rw_w2, rw_a0, rw_a2, rw_g2, rw_k_k, rw_k_a, rw_r_k, rw_ln_w, rw_ln_b, gla_gw2, gla_gb, gla_norm_w,
           w_out, norm2_g, w_up, w_down, norm_f_g):
    w_p, rw_wts, gla_wts = _prep_weights(
        w_in[0], rw_mu[0], rw_w0[0], rw_w2[0], rw_a0[0], rw_a2[0], rw_g2[0], rw_k_k[0], rw_k_a[0],
        rw_r_k[0].reshape(-1), rw_ln_w[0], rw_ln_b[0], gla_gw2[0], gla_gb[0], gla_norm_w[0])
    shared = (norm1_g[0], w_p, rw_wts, gla_wts, w_out[0].astype(BF16), norm2_g[0], w_up[0].astype(BF16),
              w_down[0].astype(BF16), norm_f_g)

    bp, lp, _ = x_prompt.shape
    bs, ls, _ = x_sample.shape
    dt = x_prompt.dtype
    out_p = _trunk(x_prompt, jnp.zeros((bp, RW_PROJ), dt),
                   jnp.zeros((bp, RW_HEADS, RW_HEAD_DIM, RW_HEAD_DIM), dt),
                   jnp.zeros((bp, GLA_HEADS, GLA_DK, GLA_DV), dt), *shared,
                   nb=1, tb=ROWS, c_rw=64, c_gla=GLA_CHUNK, tm=1024)
    out_s = _trunk(x_sample, state_rwkv_shift[0], state_rwkv_wkv[0], state_gla[0], *shared,
                   nb=ROWS // ls, tb=ls, c_rw=ls, c_gla=ls, tm=1024)
    return (out_p[0], out_s[0], out_p[1], out_p[2], out_p[3], out_s[1], out_s[2], out_s[3])
```

```python
import functools

import jax
import jax.numpy as jnp
from jax import lax
from jax.experimental import pallas as pl
from jax.experimental.pallas import tpu as pltpu

F32 = jnp.float32
BF16 = jnp.bfloat16

D_MODEL = 2048
RW_WIDTH = 1024
RW_HEADS = 16
RW_HEAD_DIM = 64
RW_LORA_W = 64
RW_LORA_A = 64
RW_LORA_G = 160
RW_PROJ = 3 * RW_WIDTH + RW_LORA_W + RW_LORA_A + RW_LORA_G
RW_GN_EPS = 64e-5
GLA_WIDTH = 1024
GLA_HEADS = 4
GLA_KEY_WIDTH = 512
GLA_DK = 128
GLA_DV = 256
GLA_GATE_RANK = 16
GLA_GATE_NORM = 16.0
GLA_CHUNK = 64
GLA_PROJ = 2 * GLA_KEY_WIDTH + GLA_WIDTH + GLA_GATE_RANK + GLA_WIDTH
D_FF = 4 * D_MODEL
NORM_EPS = 1e-6
HEAD_NORM_EPS = 1e-5
DECAY_SCALE = 0.6065306597126334

LANES = 128
SUBLANES = 8

GROUP_PAD = 3456
RW_OFF_XWA = 3 * RW_WIDTH
RW_OFF_XG = RW_OFF_XWA + LANES
GLA_OFF_K = GLA_KEY_WIDTH
GLA_OFF_V = 2 * GLA_KEY_WIDTH
GLA_OFF_GATE = GLA_OFF_V + GLA_WIDTH
GLA_OFF_GOUT = GLA_OFF_GATE + LANES
NP = 2 * GROUP_PAD

ROWS = 128
MXU_WIDTH = 256
INPROJ_TN = 9 * MXU_WIDTH
PROJ_DTYPE = BF16
POST_TM = 512
POST_TF = 1024
VMEM_LIMIT = 56 * 1024 * 1024

NN = (((1,), (0,)), ((), ()))
NT = (((1,), (1,)), ((), ()))


def _dot(a, b, dims=NN):
    return lax.dot_general(a, b, dims, preferred_element_type=F32)


def _parts(x, n):
    if x.dtype == BF16:
        return [x]
    out = []
    rem = x
    for i in range(n):
        h = rem.astype(BF16)
        out.append(h)
        if i + 1 < n:
            rem = rem - h.astype(F32)
    return out


def _mm(a, b, pa=1, pb=1, dims=NN):
    pa = 1 if a.dtype == BF16 else pa
    pb = 1 if b.dtype == BF16 else pb
    aa = _parts(a, pa)
    bb = _parts(b, pb)
    n = max(pa, pb)
    if dims == NN and min(pa, pb) == 1 and n * a.shape[1] <= MXU_WIDTH:
        return _dot(jnp.concatenate(aa * (n // pa), axis=1), jnp.concatenate(bb * (n // pb), axis=0))
    acc = None
    for i in range(pa):
        for j in range(pb):
            if i + j < n:
                t = _dot(aa[i], bb[j], dims)
                acc = t if acc is None else acc + t
    return acc


def _softplus(x):
    return jnp.maximum(x, 0.0) + jnp.log(1.0 + jnp.exp(-jnp.abs(x)))


def _sigmoid(x):
    return 0.5 * jnp.tanh(0.5 * x) + 0.5


def _chunk_masks(m, c):
    sh = c.bit_length() - 1
    ri = lax.broadcasted_iota(jnp.int32, (m, m), 0)
    ci = lax.broadcasted_iota(jnp.int32, (m, m), 1)
    same = (ri >> sh) == (ci >> sh)
    return same, ri, ci


def _inproj_kernel(x_ref, g_ref, w_ref, o_ref, h_ref):
    @pl.when(pl.program_id(1) == 0)
    def _():
        x = x_ref[...]
        ms = jnp.mean(x * x, axis=-1, keepdims=True)
        h_ref[...] = (x * lax.rsqrt(ms + NORM_EPS) * g_ref[...]).astype(BF16)

    o_ref[...] = _dot(h_ref[...], w_ref[...], NT).astype(o_ref.dtype)


def _inproj(x2d, g, w_p, tm, tn):
    t = x2d.shape[0]
    return pl.pallas_call(
        _inproj_kernel,
        grid=(t // tm, NP // tn),
        in_specs=[
            pl.BlockSpec((tm, D_MODEL), lambda i, j: (i, 0)),
            pl.BlockSpec((1, D_MODEL), lambda i, j: (0, 0)),
            pl.BlockSpec((tn, D_MODEL), lambda i, j: (j, 0)),
        ],
        out_specs=pl.BlockSpec((tm, tn), lambda i, j: (i, j)),
        out_shape=jax.ShapeDtypeStruct((t, NP), PROJ_DTYPE),
        scratch_shapes=[pltpu.VMEM((tm, D_MODEL), BF16)],
        compiler_params=pltpu.CompilerParams(
            dimension_semantics=("parallel", "arbitrary"), vmem_limit_bytes=VMEM_LIMIT),
        name="inproj",
    )(x2d, g, w_p)


def _rwkv_kernel(p_ref, sh_ref, s0_ref, mu_ref, wwa_ref, w0_ref, a0_ref, g2_ref, kk_ref, ka_ref,
                 rk_ref, lnw_ref, lnb_ref, e1_ref, e2_ref, e12_ref,
                 o_ref, shout_ref, sout_ref,
                 carry_ref, st_ref, bk_ref, kb_ref, pc_ref, art_ref, vt_ref, arbd_ref, uy_ref, yt_ref,
                 g_ref, bonus_ref, *, nb, tb, c, pp):
    j = pl.program_id(1)
    m_rows = ROWS
    nsub = nb * tb // m_rows
    tbs = tb // nsub
    n_lvl = c.bit_length() - 1
    hd = RW_HEAD_DIM

    @pl.when(j == 0)
    def _():
        for p in range(RW_HEADS // 2):
            st_ref[:, p] = jnp.concatenate([s0_ref[:, 2 * p], s0_ref[:, 2 * p + 1]], axis=-1)
        carry_ref[...] = sh_ref[...]

    yt_ref[...] = jnp.zeros_like(yt_ref)

    def shifted(sb, c0, w):
        r0 = sb * m_rows
        pc3 = p_ref[r0:r0 + m_rows, c0:c0 + w].astype(F32).reshape(nb, tbs, w)
        if sb == 0:
            before = carry_ref[:, :, c0:c0 + w]
        else:
            before = p_ref[r0 - 1:r0, c0:c0 + w].astype(F32).reshape(1, 1, w)
        t3 = lax.broadcasted_iota(jnp.int32, pc3.shape, 1)
        prev3 = jnp.where(t3 == 0, before, pltpu.roll(pc3, 1, axis=1))
        return (pc3 + (prev3 - pc3) * mu_ref[:, c0:c0 + w]).reshape(m_rows, w)

    same, ri, ci = _chunk_masks(m_rows, c)
    lmask = jnp.where(same & (ci <= ri), 1.0, 0.0).astype(BF16)
    strict_t = same & (ri < ci)
    incl_t = same & (ri <= ci)

    sw = STRIP
    strips = [slice(c0, c0 + sw) for c0 in range(0, RW_WIDTH, sw)]

    def head_stat(x, cs):
        return _dot(x.astype(BF16), e1_ref[cs, :])

    def head_bcast(s, cs):
        return _mm(s, e2_ref[:, cs], pa=2)

    def head_sum(x):
        return _dot(x.astype(BF16), e12_ref[...])


    def prologue(sb):
        slab = shifted(sb, RW_OFF_XWA, LANES)
        lane = lax.broadcasted_iota(jnp.int32, slab.shape, 1)
        lhs = jnp.where(lane < RW_LORA_W, jnp.tanh(slab), slab).astype(BF16)
        sg = _sigmoid(shifted(sb, RW_OFF_XG, 2 * LANES)).astype(BF16)
        yield
        lws = [_dot(lhs, wwa_ref[:, cs]) for cs in strips]
        las = [_dot(lhs, wwa_ref[:, RW_WIDTH + cs.start:RW_WIDTH + cs.stop]) for cs in strips]
        for cs in strips:
            g_ref[sb, :, cs] = _dot(sg, g2_ref[:, cs])
        yield
        logws = [-DECAY_SCALE * _sigmoid(w0_ref[:, cs] + lw) for cs, lw in zip(strips, lws)]
        gcums = [_mm(lmask, logw, pb=2) for logw in logws]
        yield
        avs = [_sigmoid(a0_ref[:, cs] + la) for cs, la in zip(strips, las)]
        ks = [shifted(sb, RW_WIDTH + cs.start, sw) for cs in strips]
        kkfs = [k * kk_ref[:, cs] for cs, k in zip(strips, ks)]
        rinvs = [lax.rsqrt(jnp.maximum(head_stat(kkf * kkf, cs), 1e-24)) for cs, kkf in zip(strips, kkfs)]
        yield
        k2s = [k * (1.0 + (a - 1.0) * ka_ref[:, cs]) for cs, k, a in zip(strips, ks, avs)]
        rs = [shifted(sb, cs.start, sw) for cs in strips]
        bsums = [head_sum(r * k2 * rk_ref[:, cs]) for cs, r, k2 in zip(strips, rs, k2s)]
        kks = [kkf * head_bcast(rinv, cs) for cs, kkf, rinv in zip(strips, kkfs, rinvs)]
        yield
        for cs, bsum in zip(strips, bsums):
            v = shifted(sb, 2 * RW_WIDTH + cs.start, sw)
            bonus_ref[sb, :, cs] = bsum * v
            vt_ref[sb, cs, :] = v.T
        yield
        for cs, logw, gcum, a, kk, k2, r in zip(strips, logws, gcums, avs, kks, k2s, rs):
            beta = kk * a
            g3 = gcum.reshape(cps, c, sw)
            gtot = jnp.broadcast_to(g3[:, c - 1:c, :], g3.shape).reshape(m_rows, sw)
            e_inv = jnp.exp(-gcum)
            e_rev = jnp.exp(gtot - gcum)
            art_ref[sb, cs, 0:m_rows] = (-kk * jnp.exp(gcum - logw)).T
            art_ref[sb, cs, m_rows:2 * m_rows] = (r * jnp.exp(gcum)).T
            bk_ref[sb, 0:m_rows, cs] = beta * e_inv
            bk_ref[sb, m_rows:, cs] = k2 * e_inv
            kb_ref[sb, 0:m_rows, cs] = k2 * e_rev
            kb_ref[sb, m_rows:, cs] = beta * e_rev
            pc_ref[sb, :, cs] = jnp.exp(gtot)
            yield

    pw = 2 * hd
    cat = jnp.concatenate
    top = lax.broadcasted_iota(jnp.int32, (pw, 1), 0) < hd
    left = lax.broadcasted_iota(jnp.int32, (1, 2 * m_rows), 1) < m_rows

    def split_rows(x):
        return cat([jnp.where(top, x, 0.0), jnp.where(top, 0.0, x)], axis=1)

    def block_diag(xc):
        return cat([jnp.where(left, xc, 0.0), jnp.where(left, 0.0, xc)], axis=0)

    def blk(xt, u, rh, ch):
        col = (2 * u + ch) * m_rows
        return xt[rh * m_rows:(rh + 1) * m_rows, col:col + m_rows]

    n_pairs = RW_HEADS // 2
    pairs = range(n_pairs)
    los = [p * pw for p in pairs]

    def pair_phase(sb):
        for g0 in range(0, n_pairs, PAIR_LOCKSTEP):
            yield from pair_group(sb, pairs[g0:g0 + PAIR_LOCKSTEP], los[g0:g0 + PAIR_LOCKSTEP])

    def pair_group(sb, pairs, los):
        arts = [art_ref[sb, lo:lo + pw, :] for lo in los]
        xts = [_mm(bk_ref[sb, :, lo:lo + pw], split_rows(art), pp, pp) for lo, art in zip(los, arts)]
        yield
        npcs = [cat([jnp.where(strict_t, blk(xt, 0, 0, 0), 0.0), jnp.where(strict_t, blk(xt, 1, 0, 0), 0.0)], axis=1)
                for xt in xts]
        vas = [_mm(split_rows(vt_ref[sb, lo:lo + pw, :]),
                   cat([cat([jnp.where(strict_t, blk(xt, u, 1, 0), 0.0), jnp.where(incl_t, blk(xt, u, 1, 1), 0.0)],
                            axis=1) for u in range(2)], axis=0), pp, pp)
               for lo, xt in zip(los, xts)]
        yield
        zcs = [cat([cat([art[u * hd:(u + 1) * hd, 0:m_rows], va[u * hd:(u + 1) * hd, 0:m_rows]], axis=0)
                    for u in range(2)], axis=1) for art, va in zip(arts, vas)]
        for lvl in range(n_lvl):
            bds = [block_diag(npc) for npc in npcs]
            if lvl + 1 < n_lvl:
                ts = [_mm(cat([zc, npc], axis=0), bd, pp, pp) for zc, npc, bd in zip(zcs, npcs, bds)]
                zcs = [zc + t[0:2 * hd] for zc, t in zip(zcs, ts)]
                npcs = [t[2 * hd:] for t in ts]
            else:
                zcs = [zc + _mm(zc, bd, pp, pp) for zc, bd in zip(zcs, bds)]
            yield
        rycs = [cat([cat([art[u * hd:(u + 1) * hd, m_rows:], va[u * hd:(u + 1) * hd, m_rows:]], axis=0)
                     for u in range(2)], axis=1)
                + _mm(zc, block_diag(cat([jnp.where(incl_t, blk(xt, 0, 0, 1), 0.0),
                                          jnp.where(incl_t, blk(xt, 1, 0, 1), 0.0)], axis=1)), pp, pp)
                for art, va, zc, xt in zip(arts, vas, zcs, xts)]
        yield
        zero = jnp.zeros((hd, 2 * m_rows), BF16)
        for p, zc, ry in zip(pairs, zcs, rycs):
            arbd_ref[sb, p, 0:hd, 0:2 * m_rows] = cat([zc[0:hd, 0:m_rows], ry[0:hd, 0:m_rows]], axis=1).astype(BF16)
            arbd_ref[sb, p, 0:hd, 2 * m_rows:] = zero
            arbd_ref[sb, p, hd:, 0:2 * m_rows] = zero
            arbd_ref[sb, p, hd:, 2 * m_rows:] = cat([zc[0:hd, m_rows:], ry[0:hd, m_rows:]], axis=1).astype(BF16)
            uy_ref[sb, p] = cat([zc[hd:, 0:m_rows], ry[hd:, 0:m_rows], zc[hd:, m_rows:], ry[hd:, m_rows:]], axis=1)
        yield

    sh_c = c.bit_length() - 1
    cps = m_rows // c
    cpb = tbs // c
    row_id2 = (lax.broadcasted_iota(jnp.int32, (2 * m_rows, pw), 0) & (m_rows - 1)) >> sh_c
    col_id = lax.broadcasted_iota(jnp.int32, (hd, m_rows), 1) >> sh_c
    lane_lo = lax.broadcasted_iota(jnp.int32, (1, pw), 1) < hd

    def chunk_step(sb, i):
        b = i // cpb
        r0 = i * c
        rmask2 = row_id2 == i
        cmask = col_id == i
        ss = [st_ref[b, p] for p in pairs]
        zss = [_mm(s, arbd_ref[sb, p], pp, pp) + uy_ref[sb, p] for p, s in zip(pairs, ss)]
        yield
        upd = []
        for lo, zs in zip(los, zss):
            kbz = jnp.where(rmask2, kb_ref[sb, :, lo:lo + pw], 0.0)
            rhs = cat([jnp.where(lane_lo, kbz, 0.0), jnp.where(lane_lo, 0.0, kbz)], axis=0)
            vu = cat([vt_ref[sb, lo:lo + hd, :], zs[:, 0:m_rows],
                      vt_ref[sb, lo + hd:lo + pw, :], zs[:, 2 * m_rows:3 * m_rows]], axis=1)
            upd.append(_mm(vu, rhs, pp, pp))
        yield
        for p, lo, s, zs, d in zip(pairs, los, ss, zss, upd):
            st_ref[b, p] = s * pc_ref[sb, pl.ds(r0, 1), lo:lo + pw] + d
            for u in range(2):
                rows = slice((2 * p + u) * hd, (2 * p + u + 1) * hd)
                yt_ref[sb, rows, :] = jnp.where(cmask, zs[:, (2 * u + 1) * m_rows:(2 * u + 2) * m_rows],
                                               yt_ref[sb, rows, :])
        yield

    inv_n = 1.0 / hd

    def epilogue(sb):
        ys = [yt_ref[sb, cs, :].T for cs in strips]
        ycs = [y - head_sum(y) * inv_n for y in ys]
        yield
        rstds = [lax.rsqrt(head_stat(yc * yc, cs) * inv_n + RW_GN_EPS) for cs, yc in zip(strips, ycs)]
        yield
        for cs, yc, rstd in zip(strips, ycs, rstds):
            yn = yc * head_bcast(rstd, cs) * lnw_ref[:, cs] + lnb_ref[:, cs]
            o_ref[sb * m_rows:(sb + 1) * m_rows, cs] = (
                (yn + bonus_ref[sb, :, cs]) * g_ref[sb, :, cs]).astype(o_ref.dtype)
        yield

    def run(*gens):
        live = list(gens)
        while live:
            live = [g for g in live if next(g, live) is not live]

    def tail(sb):
        for i in range(cps):
            yield from chunk_step(sb, i)
        yield from epilogue(sb)

    if nsub == 1:
        run(prologue(0))
        run(pair_phase(0))

        def chunk_body(i, carry):
            run(chunk_step(0, i))
            return carry

        lax.fori_loop(0, cps, chunk_body, 0)
        run(epilogue(0))
    else:
        run(prologue(0))
        for sb in range(nsub):
            run(pair_phase(sb), *([prologue(sb + 1)] if sb + 1 < nsub else []), *([tail(sb - 1)] if sb else []))
        run(tail(nsub - 1))

    if nb == 1:
        last = p_ref[tb - 1:tb, :].astype(F32).reshape(1, 1, GROUP_PAD)
    else:
        last = p_ref[...].astype(F32).reshape(nb, tb, GROUP_PAD)[:, tb - 1:tb, :]
    carry_ref[...] = last
    shout_ref[...] = last

    @pl.when(j == pl.num_programs(1) - 1)
    def _():
        for p in pairs:
            sp = st_ref[:, p]
            sout_ref[:, 2 * p] = sp[:, :, 0:hd]
            sout_ref[:, 2 * p + 1] = sp[:, :, hd:]


def _rwkv(p, shift_prev, s0, wts, nb, tb, c, pp):
    bsz = s0.shape[0]
    seq = p.shape[0] // bsz
    m_rows = ROWS
    nsub = nb * tb // m_rows
    assert nsub * m_rows == nb * tb and (nsub == 1 or nb == 1)
    hd = RW_HEAD_DIM
    const = lambda shape: pl.BlockSpec(shape, lambda i, j: (0,) * len(shape))
    kern = functools.partial(_rwkv_kernel, nb=nb, tb=tb, c=c, pp=pp)
    return pl.pallas_call(
        kern,
        grid=(bsz // nb, seq // tb),
        in_specs=[
            pl.BlockSpec((nb * tb, GROUP_PAD), lambda i, j: (i * (seq // tb) + j, 0)),
            pl.BlockSpec((nb, 1, GROUP_PAD), lambda i, j: (i, 0, 0)),
            pl.BlockSpec((nb, RW_HEADS, hd, hd), lambda i, j: (i, 0, 0, 0)),
            const((1, GROUP_PAD)),
            const((LANES, 2 * RW_WIDTH)),
            const((1, RW_WIDTH)),
            const((1, RW_WIDTH)),
            const((2 * LANES, RW_WIDTH)),
            const((1, RW_WIDTH)),
            const((1, RW_WIDTH)),
            const((1, RW_WIDTH)),
            const((1, RW_WIDTH)),
            const((1, RW_WIDTH)),
            const((RW_WIDTH, LANES)),
            const((LANES, RW_WIDTH)),
            const((STRIP, STRIP)),
        ],
        out_specs=[
            pl.BlockSpec((nb * tb, RW_WIDTH), lambda i, j: (i * (seq // tb) + j, 0)),
            pl.BlockSpec((nb, 1, GROUP_PAD), lambda i, j: (i, 0, 0)),
            pl.BlockSpec((nb, RW_HEADS, hd, hd), lambda i, j: (i, 0, 0, 0)),
        ],
        out_shape=[
            jax.ShapeDtypeStruct((bsz * seq, RW_WIDTH), BF16),
            jax.ShapeDtypeStruct((bsz, 1, GROUP_PAD), F32),
            jax.ShapeDtypeStruct((bsz, RW_HEADS, hd, hd), F32),
        ],
        scratch_shapes=[
            pltpu.VMEM((nb, 1, GROUP_PAD), F32),
            pltpu.VMEM((nb, RW_HEADS // 2, hd, 2 * hd), F32),
            pltpu.VMEM((nsub, 2 * m_rows, RW_WIDTH), F32),
            pltpu.VMEM((nsub, 2 * m_rows, RW_WIDTH), F32),
            pltpu.VMEM((nsub, m_rows, RW_WIDTH), F32),
            pltpu.VMEM((nsub, RW_WIDTH, 2 * m_rows), F32),
            pltpu.VMEM((nsub, RW_WIDTH, m_rows), F32),
            pltpu.VMEM((nsub, RW_HEADS // 2, 2 * hd, 4 * m_rows), BF16),
            pltpu.VMEM((nsub, RW_HEADS // 2, hd, 4 * m_rows), F32),
            pltpu.VMEM((nsub, RW_WIDTH, m_rows), F32),
            pltpu.VMEM((nsub, m_rows, RW_WIDTH), F32),
            pltpu.VMEM((nsub, m_rows, RW_WIDTH), F32),
        ],
        compiler_params=pltpu.CompilerParams(
            dimension_semantics=("parallel", "arbitrary"), vmem_limit_bytes=VMEM_LIMIT),
        name="rwkv7",
    )(p, shift_prev, s0, *wts)


def _gla_kernel(p_ref, s0_ref, gw2_ref, gb_ref, nw_ref, o_ref, st_ref,
                acc_ref, qd_ref, v_ref, kt_ref, et_ref, *, nb, tb, c, pp):
    j = pl.program_id(1)
    m_rows = nb * tb
    dk, dv = GLA_DK, GLA_DV

    @pl.when(j == 0)
    def _():
        st_ref[...] = s0_ref[...]

    p = p_ref[...].astype(F32)
    q = p[:, 0:GLA_KEY_WIDTH] * (dk ** -0.5)
    k = p[:, GLA_OFF_K:GLA_OFF_K + GLA_KEY_WIDTH]
    v = p[:, GLA_OFF_V:GLA_OFF_V + GLA_WIDTH]
    xs = p[:, GLA_OFF_GATE:GLA_OFF_GATE + LANES].astype(BF16)
    gout = p[:, GLA_OFF_GOUT:GLA_OFF_GOUT + GLA_WIDTH]
    gk = -_softplus(-(_dot(xs, gw2_ref[...]) + gb_ref[...])) / GLA_GATE_NORM

    same, ri, ci = _chunk_masks(m_rows, c)
    lmask = jnp.where(same & (ci <= ri), 1.0, 0.0).astype(BF16)
    causal = same & (ci <= ri)
    gcum = _mm(lmask, gk, pb=3)
    g3 = gcum.reshape(m_rows // c, c, GLA_KEY_WIDTH)
    gtot = jnp.broadcast_to(g3[:, c - 1:c, :], g3.shape).reshape(m_rows, GLA_KEY_WIDTH)
    qd = q * jnp.exp(gcum)
    kinv = k * jnp.exp(-gcum)
    qd_ref[...] = qd
    v_ref[...] = v
    kt_ref[...] = (k * jnp.exp(gtot - gcum)).T
    et_ref[...] = jnp.exp(gtot).T

    heads = range(GLA_HEADS)
    kq = [slice(h * dk, (h + 1) * dk) for h in heads]
    vq = [slice(h * dv, (h + 1) * dv) for h in heads]
    scores = [jnp.where(causal, _mm(qd[:, ks], kinv[:, ks], pp, pp, NT), 0.0) for ks in kq]
    intra = [_mm(a, v[:, vs], pp, pp) for a, vs in zip(scores, vq)]
    for vs, o in zip(vq, intra):
        acc_ref[:, vs] = o

    sh_c = c.bit_length() - 1
    cpb = tb // c
    span = min(m_rows, LANES)
    row_id = lax.broadcasted_iota(jnp.int32, (span, dv), 0) >> sh_c
    for i in range(m_rows // c):
        b = i // cpb
        r0 = i * c
        t0 = (r0 // span) * span
        rmask = row_id == (r0 - t0) // c
        states = [st_ref[b, h] for h in heads]
        inter = [_mm(qd_ref[r0:r0 + c, ks], s, pp, pp) for ks, s in zip(kq, states)]
        upd = [_mm(kt_ref[ks, t0:t0 + span], jnp.where(rmask, v_ref[t0:t0 + span, vs], 0.0), pp, pp)
               for ks, vs in zip(kq, vq)]
        for h, ks, vs, s, oi, d in zip(heads, kq, vq, states, inter, upd):
            acc_ref[r0:r0 + c, vs] += oi
            st_ref[b, h] = s * et_ref[ks, r0:r0 + 1] + d

    for vs in vq:
        o = acc_ref[:, vs]
        on = o * lax.rsqrt(jnp.mean(o * o, axis=-1, keepdims=True) + HEAD_NORM_EPS) * nw_ref[...]
        gh = gout[:, vs]
        o_ref[:, vs] = (on * (gh * _sigmoid(gh))).astype(o_ref.dtype)


def _gla(p, s0, wts, nb, tb, c, pp):
    bsz = s0.shape[0]
    seq = p.shape[0] // bsz
    m_rows = nb * tb
    const = lambda shape: pl.BlockSpec(shape, lambda i, j: (0,) * len(shape))
    kern = functools.partial(_gla_kernel, nb=nb, tb=tb, c=c, pp=pp)
    return pl.pallas_call(
        kern,
        grid=(bsz // nb, seq // tb),
        in_specs=[
            pl.BlockSpec((nb * tb, GROUP_PAD), lambda i, j: (i * (seq // tb) + j, 1)),
            pl.BlockSpec((nb, GLA_HEADS, GLA_DK, GLA_DV), lambda i, j: (i, 0, 0, 0)),
            const((LANES, GLA_KEY_WIDTH)),
            const((1, GLA_KEY_WIDTH)),
            const((1, GLA_DV)),
        ],
        out_specs=[
            pl.BlockSpec((nb * tb, GLA_WIDTH), lambda i, j: (i * (seq // tb) + j, 0)),
            pl.BlockSpec((nb, GLA_HEADS, GLA_DK, GLA_DV), lambda i, j: (i, 0, 0, 0)),
        ],
        out_shape=[
            jax.ShapeDtypeStruct((bsz * seq, GLA_WIDTH), BF16),
            jax.ShapeDtypeStruct((bsz, GLA_HEADS, GLA_DK, GLA_DV), F32),
        ],
        scratch_shapes=[
            pltpu.VMEM((m_rows, GLA_WIDTH), F32),
            pltpu.VMEM((m_rows, GLA_KEY_WIDTH), F32),
            pltpu.VMEM((m_rows, GLA_WIDTH), F32),
            pltpu.VMEM((GLA_KEY_WIDTH, m_rows), F32),
            pltpu.VMEM((GLA_KEY_WIDTH, m_rows), F32),
        ],
        compiler_params=pltpu.CompilerParams(
            dimension_semantics=("parallel", "arbitrary"), vmem_limit_bytes=VMEM_LIMIT),
        name="gla",
    )(p, s0, *wts)


def _post_kernel(x_ref, orw_ref, ogla_ref, wo_ref, g2_ref, wu_ref, wd_ref, gf_ref, o_ref, h_ref):
    jf = pl.program_id(1)

    @pl.when(jf == 0)
    def _():
        x1 = (x_ref[...]
              + _dot(orw_ref[...], wo_ref[0:RW_WIDTH, :])
              + _dot(ogla_ref[...], wo_ref[RW_WIDTH:, :]))
        o_ref[...] = x1
        ms = jnp.mean(x1 * x1, axis=-1, keepdims=True)
        h_ref[...] = (x1 * lax.rsqrt(ms + NORM_EPS) * g2_ref[...]).astype(BF16)

    u = jnp.maximum(_dot(h_ref[...], wu_ref[...]), 0.0)
    o_ref[...] += _dot((u * u).astype(BF16), wd_ref[...])

    @pl.when(jf == pl.num_programs(1) - 1)
    def _():
        x2 = o_ref[...]
        ms = jnp.mean(x2 * x2, axis=-1, keepdims=True)
        o_ref[...] = x2 * lax.rsqrt(ms + NORM_EPS) * gf_ref[...]


def _post(x2d, o_rw, o_gla, w_out, g2, w_up, w_down, gf, tm, tf):
    t = x2d.shape[0]
    return pl.pallas_call(
        _post_kernel,
        grid=(t // tm, D_FF // tf),
        in_specs=[
            pl.BlockSpec((tm, D_MODEL), lambda i, j: (i, 0)),
            pl.BlockSpec((tm, RW_WIDTH), lambda i, j: (i, 0)),
            pl.BlockSpec((tm, GLA_WIDTH), lambda i, j: (i, 0)),
            pl.BlockSpec((D_MODEL, D_MODEL), lambda i, j: (0, 0), pipeline_mode=pl.Buffered(1)),
            pl.BlockSpec((1, D_MODEL), lambda i, j: (0, 0)),
            pl.BlockSpec((D_MODEL, tf), lambda i, j: (0, j)),
            pl.BlockSpec((tf, D_MODEL), lambda i, j: (j, 0)),
            pl.BlockSpec((1, D_MODEL), lambda i, j: (0, 0)),
        ],
        out_specs=pl.BlockSpec((tm, D_MODEL), lambda i, j: (i, 0)),
        out_shape=jax.ShapeDtypeStruct((t, D_MODEL), F32),
        scratch_shapes=[pltpu.VMEM((tm, D_MODEL), BF16)],
        compiler_params=pltpu.CompilerParams(
            dimension_semantics=("parallel", "arbitrary"), vmem_limit_bytes=VMEM_LIMIT),
        name="post",
    )(x2d, o_rw, o_gla, w_out, g2, w_up, w_down, gf)


def _pad_cols(w, n):
    return jnp.pad(w, ((0, 0), (0, n - w.shape[1])))


def _prep_weights(w_in, rw_mu, rw_w0, rw_w2, rw_a0, rw_a2, rw_g2, rw_k_k, rw_k_a, rw_r_k, rw_ln_w, rw_ln_b,
                  gla_gw2, gla_gb, gla_norm_w):
    go = RW_PROJ
    w_t = jnp.swapaxes(w_in, 0, 1).astype(BF16)
    pad_rows = lambda w, n: jnp.pad(w, ((0, n - w.shape[0]), (0, 0)))
    w_p = jnp.concatenate([
        pad_rows(w_t[:RW_PROJ], GROUP_PAD),
        pad_rows(w_t[go:go + GLA_OFF_GATE + GLA_GATE_RANK], GLA_OFF_GOUT),
        pad_rows(w_t[go + GLA_OFF_GATE + GLA_GATE_RANK:], GROUP_PAD - GLA_OFF_GOUT),
    ], axis=0)

    row = lambda x: x.reshape(1, -1).astype(F32)
    mu = _pad_cols(row(rw_mu), GROUP_PAD)
    wwa = jnp.zeros((LANES, 2 * RW_WIDTH), F32)
    wwa = wwa.at[0:RW_LORA_W, 0:RW_WIDTH].set(rw_w2)
    wwa = wwa.at[RW_LORA_W:RW_LORA_W + RW_LORA_A, RW_WIDTH:].set(rw_a2)
    g2p = jnp.pad(rw_g2, ((0, 2 * LANES - RW_LORA_G), (0, 0)))
    head_of_col = jnp.arange(RW_WIDTH) // RW_HEAD_DIM
    e1 = (head_of_col[:, None] == jnp.arange(LANES)[None, :]).astype(BF16)
    rw_wts = (mu, wwa.astype(BF16), row(rw_w0), row(rw_a0), g2p.astype(BF16), row(rw_k_k), row(rw_k_a),
              row(rw_r_k), row(rw_ln_w), row(rw_ln_b), e1, e1.T,
              (head_of_col[:STRIP, None] == head_of_col[None, :STRIP]).astype(BF16))
    gw2p = jnp.pad(gla_gw2, ((0, LANES - GLA_GATE_RANK), (0, 0))).astype(BF16)
    gla_wts = (gw2p, row(gla_gb), row(gla_norm_w))
    return w_p, rw_wts, gla_wts


RW_PIECES = 1
GLA_PIECES = 1
PAIR_LOCKSTEP = 8
RW_STEP_ROWS = 2 * ROWS
STRIP = 256


def _trunk(x, shift, s_rw, s_gla, norm1_g, w_p, rw_wts, gla_wts, w_out, norm2_g, w_up, w_down, norm_f_g,
           nb, tb, c_rw, c_gla, tm):
    bsz, seq, _ = x.shape
    t = bsz * seq
    x2d = x.reshape(t, D_MODEL)
    proj = _inproj(x2d, norm1_g.reshape(1, -1), w_p, tm, INPROJ_TN)
    shift_p = _pad_cols(shift, GROUP_PAD).reshape(bsz, 1, GROUP_PAD)
    rw_tb = RW_STEP_ROWS if nb == 1 and seq % RW_STEP_ROWS == 0 else tb
    o_rw, sh_new, s_rw_new = _rwkv(proj, shift_p, s_rw, rw_wts, nb, rw_tb, c_rw, RW_PIECES)
    o_gla, s_gla_new = _gla(proj, s_gla, gla_wts, nb, rw_tb, c_gla, GLA_PIECES)
    y = _post(x2d, o_rw, o_gla, w_out, norm2_g.reshape(1, -1),
              w_up, w_down, norm_f_g.reshape(1, -1), POST_TM, POST_TF)
    return (y.reshape(bsz, seq, D_MODEL), sh_new[:, 0, :RW_PROJ][None], s_rw_new[None], s_gla_new[None])


def kernel(x_prompt, x_sample, state_rwkv_shift, state_rwkv_wkv, state_gla, norm1_g, w_in, rw_mu, rw_w0,
           rw_w2, rw_a0, rw_a2, rw_g2, rw_k_k, rw_k_a, rw_r_k, rw_ln_w, rw_ln_b, gla_gw2, gla_gb, gla_norm_w,
           w_out, norm2_g, w_up, w_down, norm_f_g):
    w_p, rw_wts, gla_wts = _prep_weights(
        w_in[0], rw_mu[0], rw_w0[0], rw_w2[0], rw_a0[0], rw_a2[0], rw_g2[0], rw_k_k[0], rw_k_a[0],
        rw_r_k[0].reshape(-1), rw_ln_w[0], rw_ln_b[0], gla_gw2[0], gla_gb[0], gla_norm_w[0])
    shared = (norm1_g[0], w_p, rw_wts, gla_wts, w_out[0].astype(BF16), norm2_g[0], w_up[0].astype(BF16),
              w_down[0].astype(BF16), norm_f_g)

    bp, lp, _ = x_prompt.shape
    bs, ls, _ = x_sample.shape
    dt = x_prompt.dtype
    out_p = _trunk(x_prompt, jnp.zeros((bp, RW_PROJ), dt),
                   jnp.zeros((bp, RW_HEADS, RW_HEAD_DIM, RW_HEAD_DIM), dt),
                   jnp.zeros((bp, GLA_HEADS, GLA_DK, GLA_DV), dt), *shared,
                   nb=1, tb=ROWS, c_rw=64, c_gla=GLA_CHUNK, tm=1024)
    out_s = _trunk(x_sample, state_rwkv_shift[0], state_rwkv_wkv[0], state_gla[0], *shared,
                   nb=ROWS // ls, tb=ls, c_rw=ls, c_gla=ls, tm=1024)
    return (out_p[0], out_s[0], out_p[1], out_p[2], out_p[3], out_s[1], out_s[2], out_s[3])
```

```python
import functools

import jax
import jax.numpy as jnp
from jax import lax
from jax.experimental import pallas as pl
from jax.experimental.pallas import tpu as pltpu

F32 = jnp.float32
BF16 = jnp.bfloat16

D_MODEL = 2048
RW_WIDTH = 1024
RW_HEADS = 16
RW_HEAD_DIM = 64
RW_LORA_W = 64
RW_LORA_A = 64
RW_LORA_G = 160
RW_PROJ = 3 * RW_WIDTH + RW_LORA_W + RW_LORA_A + RW_LORA_G
RW_GN_EPS = 64e-5
GLA_WIDTH = 1024
GLA_HEADS = 4
GLA_KEY_WIDTH = 512
GLA_DK = 128
GLA_DV = 256
GLA_GATE_RANK = 16
GLA_GATE_NORM = 16.0
GLA_CHUNK = 64
GLA_PROJ = 2 * GLA_KEY_WIDTH + GLA_WIDTH + GLA_GATE_RANK + GLA_WIDTH
D_FF = 4 * D_MODEL
NORM_EPS = 1e-6
HEAD_NORM_EPS = 1e-5
DECAY_SCALE = 0.6065306597126334

LANES = 128
SUBLANES = 8

GROUP_PAD = 3456
RW_OFF_XWA = 3 * RW_WIDTH
RW_OFF_XG = RW_OFF_XWA + LANES
GLA_OFF_K = GLA_KEY_WIDTH
GLA_OFF_V = 2 * GLA_KEY_WIDTH
GLA_OFF_GATE = GLA_OFF_V + GLA_WIDTH
GLA_OFF_GOUT = GLA_OFF_GATE + LANES
NP = 2 * GROUP_PAD

ROWS = 128
MXU_WIDTH = 256
INPROJ_TN = 9 * MXU_WIDTH
PROJ_DTYPE = BF16
POST_TM = 512
POST_TF = 1024
VMEM_LIMIT = 56 * 1024 * 1024

NN = (((1,), (0,)), ((), ()))
NT = (((1,), (1,)), ((), ()))


def _dot(a, b, dims=NN):
    return lax.dot_general(a, b, dims, preferred_element_type=F32)


def _parts(x, n):
    if x.dtype == BF16:
        return [x]
    out = []
    rem = x
    for i in range(n):
        h = rem.astype(BF16)
        out.append(h)
        if i + 1 < n:
            rem = rem - h.astype(F32)
    return out


def _mm(a, b, pa=1, pb=1, dims=NN):
    pa = 1 if a.dtype == BF16 else pa
    pb = 1 if b.dtype == BF16 else pb
    aa = _parts(a, pa)
    bb = _parts(b, pb)
    n = max(pa, pb)
    if dims == NN and min(pa, pb) == 1 and n * a.shape[1] <= MXU_WIDTH:
        return _dot(jnp.concatenate(aa * (n // pa), axis=1), jnp.concatenate(bb * (n // pb), axis=0))
    acc = None
    for i in range(pa):
        for j in range(pb):
            if i + j < n:
                t = _dot(aa[i], bb[j], dims)
                acc = t if acc is None else acc + t
    return acc


def _softplus(x):
    return jnp.maximum(x, 0.0) + jnp.log(1.0 + jnp.exp(-jnp.abs(x)))


def _sigmoid(x):
    return 0.5 * jnp.tanh(0.5 * x) + 0.5


def _chunk_masks(m, c):
    sh = c.bit_length() - 1
    ri = lax.broadcasted_iota(jnp.int32, (m, m), 0)
    ci = lax.broadcasted_iota(jnp.int32, (m, m), 1)
    same = (ri >> sh) == (ci >> sh)
    return same, ri, ci


def _inproj_kernel(x_ref, g_ref, w_ref, o_ref, h_ref):
    @pl.when(pl.program_id(1) == 0)
    def _():
        x = x_ref[...]
        ms = jnp.mean(x * x, axis=-1, keepdims=True)
        h_ref[...] = (x * lax.rsqrt(ms + NORM_EPS) * g_ref[...]).astype(BF16)

    o_ref[...] = _dot(h_ref[...], w_ref[...], NT).astype(o_ref.dtype)


def _inproj(x2d, g, w_p, tm, tn):
    t = x2d.shape[0]
    return pl.pallas_call(
        _inproj_kernel,
        grid=(t // tm, NP // tn),
        in_specs=[
            pl.BlockSpec((tm, D_MODEL), lambda i, j: (i, 0)),
            pl.BlockSpec((1, D_MODEL), lambda i, j: (0, 0)),
            pl.BlockSpec((tn, D_MODEL), lambda i, j: (j, 0)),
        ],
        out_specs=pl.BlockSpec((tm, tn), lambda i, j: (i, j)),
        out_shape=jax.ShapeDtypeStruct((t, NP), PROJ_DTYPE),
        scratch_shapes=[pltpu.VMEM((tm, D_MODEL), BF16)],
        compiler_params=pltpu.CompilerParams(
            dimension_semantics=("parallel", "arbitrary"), vmem_limit_bytes=VMEM_LIMIT),
        name="inproj",
    )(x2d, g, w_p)


def _rwkv_kernel(p_ref, sh_ref, s0_ref, mu_ref, wwa_ref, w0_ref, a0_ref, g2_ref, kk_ref, ka_ref,
                 rk_ref, lnw_ref, lnb_ref, e1_ref, e2_ref, e12_ref,
                 o_ref, shout_ref, sout_ref,
                 carry_ref, st_ref, bk_ref, kb_ref, pc_ref, art_ref, vt_ref, arbd_ref, uy_ref, yt_ref,
                 g_ref, bonus_ref, *, nb, tb, c, pp):
    j = pl.program_id(1)
    m_rows = ROWS
    nsub = nb * tb // m_rows
    tbs = tb // nsub
    n_lvl = c.bit_length() - 1
    hd = RW_HEAD_DIM

    @pl.when(j == 0)
    def _():
        for p in range(RW_HEADS // 2):
            st_ref[:, p] = jnp.concatenate([s0_ref[:, 2 * p], s0_ref[:, 2 * p + 1]], axis=-1)
        carry_ref[...] = sh_ref[...]

    yt_ref[...] = jnp.zeros_like(yt_ref)

    def shifted(sb, c0, w):
        r0 = sb * m_rows
        pc3 = p_ref[r0:r0 + m_rows, c0:c0 + w].astype(F32).reshape(nb, tbs, w)
        if sb == 0:
            before = carry_ref[:, :, c0:c0 + w]
        else:
            before = p_ref[r0 - 1:r0, c0:c0 + w].astype(F32).reshape(1, 1, w)
        t3 = lax.broadcasted_iota(jnp.int32, pc3.shape, 1)
        prev3 = jnp.where(t3 == 0, before, pltpu.roll(pc3, 1, axis=1))
        return (pc3 + (prev3 - pc3) * mu_ref[:, c0:c0 + w]).reshape(m_rows, w)

    same, ri, ci = _chunk_masks(m_rows, c)
    lmask = jnp.where(same & (ci <= ri), 1.0, 0.0).astype(BF16)
    strict_t = same & (ri < ci)
    incl_t = same & (ri <= ci)

    sw = STRIP
    strips = [slice(c0, c0 + sw) for c0 in range(0, RW_WIDTH, sw)]

    def head_stat(x, cs):
        return _dot(x.astype(BF16), e1_ref[cs, :])

    def head_bcast(s, cs):
        return _mm(s, e2_ref[:, cs], pa=2)

    def head_sum(x):
        return _dot(x.astype(BF16), e12_ref[...])


    def prologue(sb):
        slab = shifted(sb, RW_OFF_XWA, LANES)
        lane = lax.broadcasted_iota(jnp.int32, slab.shape, 1)
        lhs = jnp.where(lane < RW_LORA_W, jnp.tanh(slab), slab).astype(BF16)
        sg = _sigmoid(shifted(sb, RW_OFF_XG, 2 * LANES)).astype(BF16)
        yield
        lws = [_dot(lhs, wwa_ref[:, cs]) for cs in strips]
        las = [_dot(lhs, wwa_ref[:, RW_WIDTH + cs.start:RW_WIDTH + cs.stop]) for cs in strips]
        for cs in strips:
            g_ref[sb, :, cs] = _dot(sg, g2_ref[:, cs])
        yield
        logws = [-DECAY_SCALE * _sigmoid(w0_ref[:, cs] + lw) for cs, lw in zip(strips, lws)]
        gcums = [_mm(lmask, logw, pb=2) for logw in logws]
        yield
        avs = [_sigmoid(a0_ref[:, cs] + la) for cs, la in zip(strips, las)]
        ks = [shifted(sb, RW_WIDTH + cs.start, sw) for cs in strips]
        kkfs = [k * kk_ref[:, cs] for cs, k in zip(strips, ks)]
        rinvs = [lax.rsqrt(jnp.maximum(head_stat(kkf * kkf, cs), 1e-24)) for cs, kkf in zip(strips, kkfs)]
        yield
        k2s = [k * (1.0 + (a - 1.0) * ka_ref[:, cs]) for cs, k, a in zip(strips, ks, avs)]
        rs = [shifted(sb, cs.start, sw) for cs in strips]
        bsums = [head_sum(r * k2 * rk_ref[:, cs]) for cs, r, k2 in zip(strips, rs, k2s)]
        kks = [kkf * head_bcast(rinv, cs) for cs, kkf, rinv in zip(strips, kkfs, rinvs)]
        yield
        for cs, bsum in zip(strips, bsums):
            v = shifted(sb, 2 * RW_WIDTH + cs.start, sw)
            bonus_ref[sb, :, cs] = bsum * v
            vt_ref[sb, cs, :] = v.T
        yield
        for cs, logw, gcum, a, kk, k2, r in zip(strips, logws, gcums, avs, kks, k2s, rs):
            beta = kk * a
            g3 = gcum.reshape(cps, c, sw)
            gtot = jnp.broadcast_to(g3[:, c - 1:c, :], g3.shape).reshape(m_rows, sw)
            e_inv = jnp.exp(-gcum)
            e_rev = jnp.exp(gtot - gcum)
            art_ref[sb, cs, 0:m_rows] = (-kk * jnp.exp(gcum - logw)).T
            art_ref[sb, cs, m_rows:2 * m_rows] = (r * jnp.exp(gcum)).T
            bk_ref[sb, 0:m_rows, cs] = beta * e_inv
            bk_ref[sb, m_rows:, cs] = k2 * e_inv
            kb_ref[sb, 0:m_rows, cs] = k2 * e_rev
            kb_ref[sb, m_rows:, cs] = beta * e_rev
            pc_ref[sb, :, cs] = jnp.exp(gtot)
            yield

    pw = 2 * hd
    cat = jnp.concatenate
    top = lax.broadcasted_iota(jnp.int32, (pw, 1), 0) < hd
    left = lax.broadcasted_iota(jnp.int32, (1, 2 * m_rows), 1) < m_rows

    def split_rows(x):
        return cat([jnp.where(top, x, 0.0), jnp.where(top, 0.0, x)], axis=1)

    def block_diag(xc):
        return cat([jnp.where(left, xc, 0.0), jnp.where(left, 0.0, xc)], axis=0)

    def blk(xt, u, rh, ch):
        col = (2 * u + ch) * m_rows
        return xt[rh * m_rows:(rh + 1) * m_rows, col:col + m_rows]

    n_pairs = RW_HEADS // 2
    pairs = range(n_pairs)
    los = [p * pw for p in pairs]

    def pair_phase(sb):
        for g0 in range(0, n_pairs, PAIR_LOCKSTEP):
            yield from pair_group(sb, pairs[g0:g0 + PAIR_LOCKSTEP], los[g0:g0 + PAIR_LOCKSTEP])

    def pair_group(sb, pairs, los):
        arts = [art_ref[sb, lo:lo + pw, :] for lo in los]
        xts = [_mm(bk_ref[sb, :, lo:lo + pw], split_rows(art), pp, pp) for lo, art in zip(los, arts)]
        yield
        npcs = [cat([jnp.where(strict_t, blk(xt, 0, 0, 0), 0.0), jnp.where(strict_t, blk(xt, 1, 0, 0), 0.0)], axis=1)
                for xt in xts]
        vas = [_mm(split_rows(vt_ref[sb, lo:lo + pw, :]),
                   cat([cat([jnp.where(strict_t, blk(xt, u, 1, 0), 0.0), jnp.where(incl_t, blk(xt, u, 1, 1), 0.0)],
                            axis=1) for u in range(2)], axis=0), pp, pp)
               for lo, xt in zip(los, xts)]
        yield
        zcs = [cat([cat([art[u * hd:(u + 1) * hd, 0:m_rows], va[u * hd:(u + 1) * hd, 0:m_rows]], axis=0)
                    for u in range(2)], axis=1) for art, va in zip(arts, vas)]
        for lvl in range(n_lvl):
            bds = [block_diag(npc) for npc in npcs]
            if lvl + 1 < n_lvl:
                ts = [_mm(cat([zc, npc], axis=0), bd, pp, pp) for zc, npc, bd in zip(zcs, npcs, bds)]
                zcs = [zc + t[0:2 * hd] for zc, t in zip(zcs, ts)]
                npcs = [t[2 * hd:] for t in ts]
            else:
                zcs = [zc + _mm(zc, bd, pp, pp) for zc, bd in zip(zcs, bds)]
            yield
        rycs = [cat([cat([art[u * hd:(u + 1) * hd, m_rows:], va[u * hd:(u + 1) * hd, m_rows:]], axis=0)
                     for u in range(2)], axis=1)
                + _mm(zc, block_diag(cat([jnp.where(incl_t, blk(xt, 0, 0, 1), 0.0),
                                          jnp.where(incl_t, blk(xt, 1, 0, 1), 0.0)], axis=1)), pp, pp)
                for art, va, zc, xt in zip(arts, vas, zcs, xts)]
        yield
        zero = jnp.zeros((hd, 2 * m_rows), BF16)
        for p, zc, ry in zip(pairs, zcs, rycs):
            arbd_ref[sb, p, 0:hd, 0:2 * m_rows] = cat([zc[0:hd, 0:m_rows], ry[0:hd, 0:m_rows]], axis=1).astype(BF16)
            arbd_ref[sb, p, 0:hd, 2 * m_rows:] = zero
            arbd_ref[sb, p, hd:, 0:2 * m_rows] = zero
            arbd_ref[sb, p, hd:, 2 * m_rows:] = cat([zc[0:hd, m_rows:], ry[0:hd, m_rows:]], axis=1).astype(BF16)
            uy_ref[sb, p] = cat([zc[hd:, 0:m_rows], ry[hd:, 0:m_rows], zc[hd:, m_rows:], ry[hd:, m_rows:]], axis=1)
        yield

    sh_c = c.bit_length() - 1
    cps = m_rows // c
    cpb = tbs // c
    row_id2 = (lax.broadcasted_iota(jnp.int32, (2 * m_rows, pw), 0) & (m_rows - 1)) >> sh_c
    col_id = lax.broadcasted_iota(jnp.int32, (hd, m_rows), 1) >> sh_c
    lane_lo = lax.broadcasted_iota(jnp.int32, (1, pw), 1) < hd

    def chunk_step(sb, i):
        b = i // cpb
        r0 = i * c
        rmask2 = row_id2 == i
        cmask = col_id == i
        ss = [st_ref[b, p] for p in pairs]
        zss = [_mm(s, arbd_ref[sb, p], pp, pp) + uy_ref[sb, p] for p, s in zip(pairs, ss)]
        yield
        upd = []
        for lo, zs in zip(los, zss):
            kbz = jnp.where(rmask2, kb_ref[sb, :, lo:lo + pw], 0.0)
            rhs = cat([jnp.where(lane_lo, kbz, 0.0), jnp.where(lane_lo, 0.0, kbz)], axis=0)
            vu = cat([vt_ref[sb, lo:lo + hd, :], zs[:, 0:m_rows],
                      vt_ref[sb, lo + hd:lo + pw, :], zs[:, 2 * m_rows:3 * m_rows]], axis=1)
            upd.append(_mm(vu, rhs, pp, pp))
        yield
        for p, lo, s, zs, d in zip(pairs, los, ss, zss, upd):
            st_ref[b, p] = s * pc_ref[sb, pl.ds(r0, 1), lo:lo + pw] + d
            for u in range(2):
                rows = slice((2 * p + u) * hd, (2 * p + u + 1) * hd)
                yt_ref[sb, rows, :] = jnp.where(cmask, zs[:, (2 * u + 1) * m_rows:(2 * u + 2) * m_rows],
                                               yt_ref[sb, rows, :])
        yield

    inv_n = 1.0 / hd

    def epilogue(sb):
        ys = [yt_ref[sb, cs, :].T for cs in strips]
        ycs = [y - head_sum(y) * inv_n for y in ys]
        yield
        rstds = [lax.rsqrt(head_stat(yc * yc, cs) * inv_n + RW_GN_EPS) for cs, yc in zip(strips, ycs)]
        yield
        for cs, yc, rstd in zip(strips, ycs, rstds):
            yn = yc * head_bcast(rstd, cs) * lnw_ref[:, cs] + lnb_ref[:, cs]
            o_ref[sb * m_rows:(sb + 1) * m_rows, cs] = (
                (yn + bonus_ref[sb, :, cs]) * g_ref[sb, :, cs]).astype(o_ref.dtype)
        yield

    def run(*gens):
        live = list(gens)
        while live:
            live = [g for g in live if next(g, live) is not live]

    def tail(sb):
        for i in range(cps):
            yield from chunk_step(sb, i)
        yield from epilogue(sb)

    if nsub == 1:
        run(prologue(0))
        run(pair_phase(0))

        together = SAMPLE_CHUNKS_TOGETHER if cpb == 1 and cps % SAMPLE_CHUNKS_TOGETHER == 0 else 1

        def chunk_body(i, carry):
            run(*[chunk_step(0, i * together + u) for u in range(together)])
            return carry

        lax.fori_loop(0, cps // together, chunk_body, 0)
        run(epilogue(0))
    else:
        run(prologue(0))
        for sb in range(nsub):
            run(pair_phase(sb), *([prologue(sb + 1)] if sb + 1 < nsub else []), *([tail(sb - 1)] if sb else []))
        run(tail(nsub - 1))

    if nb == 1:
        last = p_ref[tb - 1:tb, :].astype(F32).reshape(1, 1, GROUP_PAD)
    else:
        last = p_ref[...].astype(F32).reshape(nb, tb, GROUP_PAD)[:, tb - 1:tb, :]
    carry_ref[...] = last
    shout_ref[...] = last

    @pl.when(j == pl.num_programs(1) - 1)
    def _():
        for p in pairs:
            sp = st_ref[:, p]
            sout_ref[:, 2 * p] = sp[:, :, 0:hd]
            sout_ref[:, 2 * p + 1] = sp[:, :, hd:]


def _rwkv(p, shift_prev, s0, wts, nb, tb, c, pp):
    bsz = s0.shape[0]
    seq = p.shape[0] // bsz
    m_rows = ROWS
    nsub = nb * tb // m_rows
    assert nsub * m_rows == nb * tb and (nsub == 1 or nb == 1)
    hd = RW_HEAD_DIM
    const = lambda shape: pl.BlockSpec(shape, lambda i, j: (0,) * len(shape))
    kern = functools.partial(_rwkv_kernel, nb=nb, tb=tb, c=c, pp=pp)
    return pl.pallas_call(
        kern,
        grid=(bsz // nb, seq // tb),
        in_specs=[
            pl.BlockSpec((nb * tb, GROUP_PAD), lambda i, j: (i * (seq // tb) + j, 0)),
            pl.BlockSpec((nb, 1, GROUP_PAD), lambda i, j: (i, 0, 0)),
            pl.BlockSpec((nb, RW_HEADS, hd, hd), lambda i, j: (i, 0, 0, 0)),
            const((1, GROUP_PAD)),
            const((LANES, 2 * RW_WIDTH)),
            const((1, RW_WIDTH)),
            const((1, RW_WIDTH)),
            const((2 * LANES, RW_WIDTH)),
            const((1, RW_WIDTH)),
            const((1, RW_WIDTH)),
            const((1, RW_WIDTH)),
            const((1, RW_WIDTH)),
            const((1, RW_WIDTH)),
            const((RW_WIDTH, LANES)),
            const((LANES, RW_WIDTH)),
            const((STRIP, STRIP)),
        ],
        out_specs=[
            pl.BlockSpec((nb * tb, RW_WIDTH), lambda i, j: (i * (seq // tb) + j, 0)),
            pl.BlockSpec((nb, 1, GROUP_PAD), lambda i, j: (i, 0, 0)),
            pl.BlockSpec((nb, RW_HEADS, hd, hd), lambda i, j: (i, 0, 0, 0)),
        ],
        out_shape=[
            jax.ShapeDtypeStruct((bsz * seq, RW_WIDTH), BF16),
            jax.ShapeDtypeStruct((bsz, 1, GROUP_PAD), F32),
            jax.ShapeDtypeStruct((bsz, RW_HEADS, hd, hd), F32),
        ],
        scratch_shapes=[
            pltpu.VMEM((nb, 1, GROUP_PAD), F32),
            pltpu.VMEM((nb, RW_HEADS // 2, hd, 2 * hd), F32),
            pltpu.VMEM((nsub, 2 * m_rows, RW_WIDTH), F32),
            pltpu.VMEM((nsub, 2 * m_rows, RW_WIDTH), F32),
            pltpu.VMEM((nsub, m_rows, RW_WIDTH), F32),
            pltpu.VMEM((nsub, RW_WIDTH, 2 * m_rows), F32),
            pltpu.VMEM((nsub, RW_WIDTH, m_rows), F32),
            pltpu.VMEM((nsub, RW_HEADS // 2, 2 * hd, 4 * m_rows), BF16),
            pltpu.VMEM((nsub, RW_HEADS // 2, hd, 4 * m_rows), F32),
            pltpu.VMEM((nsub, RW_WIDTH, m_rows), F32),
            pltpu.VMEM((nsub, m_rows, RW_WIDTH), F32),
            pltpu.VMEM((nsub, m_rows, RW_WIDTH), F32),
        ],
        compiler_params=pltpu.CompilerParams(
            dimension_semantics=("parallel", "arbitrary"), vmem_limit_bytes=VMEM_LIMIT),
        name="rwkv7",
    )(p, shift_prev, s0, *wts)


def _gla_kernel(p_ref, s0_ref, gw2_ref, gb_ref, nw_ref, o_ref, st_ref,
                acc_ref, qd_ref, v_ref, kt_ref, et_ref, *, nb, tb, c, pp):
    j = pl.program_id(1)
    m_rows = nb * tb
    dk, dv = GLA_DK, GLA_DV

    @pl.when(j == 0)
    def _():
        st_ref[...] = s0_ref[...]

    p = p_ref[...].astype(F32)
    q = p[:, 0:GLA_KEY_WIDTH] * (dk ** -0.5)
    k = p[:, GLA_OFF_K:GLA_OFF_K + GLA_KEY_WIDTH]
    v = p[:, GLA_OFF_V:GLA_OFF_V + GLA_WIDTH]
    xs = p[:, GLA_OFF_GATE:GLA_OFF_GATE + LANES].astype(BF16)
    gout = p[:, GLA_OFF_GOUT:GLA_OFF_GOUT + GLA_WIDTH]
    gk = -_softplus(-(_dot(xs, gw2_ref[...]) + gb_ref[...])) / GLA_GATE_NORM

    same, ri, ci = _chunk_masks(m_rows, c)
    lmask = jnp.where(same & (ci <= ri), 1.0, 0.0).astype(BF16)
    causal = same & (ci <= ri)
    gcum = _mm(lmask, gk, pb=3)
    g3 = gcum.reshape(m_rows // c, c, GLA_KEY_WIDTH)
    gtot = jnp.broadcast_to(g3[:, c - 1:c, :], g3.shape).reshape(m_rows, GLA_KEY_WIDTH)
    qd = q * jnp.exp(gcum)
    kinv = k * jnp.exp(-gcum)
    qd_ref[...] = qd
    v_ref[...] = v
    kt_ref[...] = (k * jnp.exp(gtot - gcum)).T
    et_ref[...] = jnp.exp(gtot).T

    heads = range(GLA_HEADS)
    kq = [slice(h * dk, (h + 1) * dk) for h in heads]
    vq = [slice(h * dv, (h + 1) * dv) for h in heads]
    scores = [jnp.where(causal, _mm(qd[:, ks], kinv[:, ks], pp, pp, NT), 0.0) for ks in kq]
    intra = [_mm(a, v[:, vs], pp, pp) for a, vs in zip(scores, vq)]
    for vs, o in zip(vq, intra):
        acc_ref[:, vs] = o

    sh_c = c.bit_length() - 1
    cpb = tb // c
    span = min(m_rows, LANES)
    row_id = lax.broadcasted_iota(jnp.int32, (span, dv), 0) >> sh_c
    for i in range(m_rows // c):
        b = i // cpb
        r0 = i * c
        t0 = (r0 // span) * span
        rmask = row_id == (r0 - t0) // c
        states = [st_ref[b, h] for h in heads]
        inter = [_mm(qd_ref[r0:r0 + c, ks], s, pp, pp) for ks, s in zip(kq, states)]
        upd = [_mm(kt_ref[ks, t0:t0 + span], jnp.where(rmask, v_ref[t0:t0 + span, vs], 0.0), pp, pp)
               for ks, vs in zip(kq, vq)]
        for h, ks, vs, s, oi, d in zip(heads, kq, vq, states, inter, upd):
            acc_ref[r0:r0 + c, vs] += oi
            st_ref[b, h] = s * et_ref[ks, r0:r0 + 1] + d

    for vs in vq:
        o = acc_ref[:, vs]
        on = o * lax.rsqrt(jnp.mean(o * o, axis=-1, keepdims=True) + HEAD_NORM_EPS) * nw_ref[...]
        gh = gout[:, vs]
        o_ref[:, vs] = (on * (gh * _sigmoid(gh))).astype(o_ref.dtype)


def _gla(p, s0, wts, nb, tb, c, pp):
    bsz = s0.shape[0]
    seq = p.shape[0] // bsz
    m_rows = nb * tb
    const = lambda shape: pl.BlockSpec(shape, lambda i, j: (0,) * len(shape))
    kern = functools.partial(_gla_kernel, nb=nb, tb=tb, c=c, pp=pp)
    return pl.pallas_call(
        kern,
        grid=(bsz // nb, seq // tb),
        in_specs=[
            pl.BlockSpec((nb * tb, GROUP_PAD), lambda i, j: (i * (seq // tb) + j, 1)),
            pl.BlockSpec((nb, GLA_HEADS, GLA_DK, GLA_DV), lambda i, j: (i, 0, 0, 0)),
            const((LANES, GLA_KEY_WIDTH)),
            const((1, GLA_KEY_WIDTH)),
            const((1, GLA_DV)),
        ],
        out_specs=[
            pl.BlockSpec((nb * tb, GLA_WIDTH), lambda i, j: (i * (seq // tb) + j, 0)),
            pl.BlockSpec((nb, GLA_HEADS, GLA_DK, GLA_DV), lambda i, j: (i, 0, 0, 0)),
        ],
        out_shape=[
            jax.ShapeDtypeStruct((bsz * seq, GLA_WIDTH), BF16),
            jax.ShapeDtypeStruct((bsz, GLA_HEADS, GLA_DK, GLA_DV), F32),
        ],
        scratch_shapes=[
            pltpu.VMEM((m_rows, GLA_WIDTH), F32),
            pltpu.VMEM((m_rows, GLA_KEY_WIDTH), F32),
            pltpu.VMEM((m_rows, GLA_WIDTH), F32),
            pltpu.VMEM((GLA_KEY_WIDTH, m_rows), F32),
            pltpu.VMEM((GLA_KEY_WIDTH, m_rows), F32),
        ],
        compiler_params=pltpu.CompilerParams(
            dimension_semantics=("parallel", "arbitrary"), vmem_limit_bytes=VMEM_LIMIT),
        name="gla",
    )(p, s0, *wts)


def _post_kernel(x_ref, orw_ref, ogla_ref, wo_ref, g2_ref, wu_ref, wd_ref, gf_ref, o_ref, h_ref):
    jf = pl.program_id(1)

    @pl.when(jf == 0)
    def _():
        x1 = (x_ref[...]
              + _dot(orw_ref[...], wo_ref[0:RW_WIDTH, :])
              + _dot(ogla_ref[...], wo_ref[RW_WIDTH:, :]))
        o_ref[...] = x1
        ms = jnp.mean(x1 * x1, axis=-1, keepdims=True)
        h_ref[...] = (x1 * lax.rsqrt(ms + NORM_EPS) * g2_ref[...]).astype(BF16)

    u = jnp.maximum(_dot(h_ref[...], wu_ref[...]), 0.0)
    o_ref[...] += _dot((u * u).astype(BF16), wd_ref[...])

    @pl.when(jf == pl.num_programs(1) - 1)
    def _():
        x2 = o_ref[...]
        ms = jnp.mean(x2 * x2, axis=-1, keepdims=True)
        o_ref[...] = x2 * lax.rsqrt(ms + NORM_EPS) * gf_ref[...]


def _post(x2d, o_rw, o_gla, w_out, g2, w_up, w_down, gf, tm, tf):
    t = x2d.shape[0]
    return pl.pallas_call(
        _post_kernel,
        grid=(t // tm, D_FF // tf),
        in_specs=[
            pl.BlockSpec((tm, D_MODEL), lambda i, j: (i, 0)),
            pl.BlockSpec((tm, RW_WIDTH), lambda i, j: (i, 0)),
            pl.BlockSpec((tm, GLA_WIDTH), lambda i, j: (i, 0)),
            pl.BlockSpec((D_MODEL, D_MODEL), lambda i, j: (0, 0), pipeline_mode=pl.Buffered(1)),
            pl.BlockSpec((1, D_MODEL), lambda i, j: (0, 0)),
            pl.BlockSpec((D_MODEL, tf), lambda i, j: (0, j)),
            pl.BlockSpec((tf, D_MODEL), lambda i, j: (j, 0)),
            pl.BlockSpec((1, D_MODEL), lambda i, j: (0, 0)),
        ],
        out_specs=pl.BlockSpec((tm, D_MODEL), lambda i, j: (i, 0)),
        out_shape=jax.ShapeDtypeStruct((t, D_MODEL), F32),
        scratch_shapes=[pltpu.VMEM((tm, D_MODEL), BF16)],
        compiler_params=pltpu.CompilerParams(
            dimension_semantics=("parallel", "arbitrary"), vmem_limit_bytes=VMEM_LIMIT),
        name="post",
    )(x2d, o_rw, o_gla, w_out, g2, w_up, w_down, gf)


def _pad_cols(w, n):
    return jnp.pad(w, ((0, 0), (0, n - w.shape[1])))


def _prep_weights(w_in, rw_mu, rw_w0, rw_w2, rw_a0, rw_a2, rw_g2, rw_k_k, rw_k_a, rw_r_k, rw_ln_w, rw_ln_b,
                  gla_gw2, gla_gb, gla_norm_w):
    go = RW_PROJ
    w_t = jnp.swapaxes(w_in, 0, 1).astype(BF16)
    pad_rows = lambda w, n: jnp.pad(w, ((0, n - w.shape[0]), (0, 0)))
    w_p = jnp.concatenate([
        pad_rows(w_t[:RW_PROJ], GROUP_PAD),
        pad_rows(w_t[go:go + GLA_OFF_GATE + GLA_GATE_RANK], GLA_OFF_GOUT),
        pad_rows(w_t[go + GLA_OFF_GATE + GLA_GATE_RANK:], GROUP_PAD - GLA_OFF_GOUT),
    ], axis=0)

    row = lambda x: x.reshape(1, -1).astype(F32)
    mu = _pad_cols(row(rw_mu), GROUP_PAD)
    wwa = jnp.zeros((LANES, 2 * RW_WIDTH), F32)
    wwa = wwa.at[0:RW_LORA_W, 0:RW_WIDTH].set(rw_w2)
    wwa = wwa.at[RW_LORA_W:RW_LORA_W + RW_LORA_A, RW_WIDTH:].set(rw_a2)
    g2p = jnp.pad(rw_g2, ((0, 2 * LANES - RW_LORA_G), (0, 0)))
    head_of_col = jnp.arange(RW_WIDTH) // RW_HEAD_DIM
    e1 = (head_of_col[:, None] == jnp.arange(LANES)[None, :]).astype(BF16)
    rw_wts = (mu, wwa.astype(BF16), row(rw_w0), row(rw_a0), g2p.astype(BF16), row(rw_k_k), row(rw_k_a),
              row(rw_r_k), row(rw_ln_w), row(rw_ln_b), e1, e1.T,
              (head_of_col[:STRIP, None] == head_of_col[None, :STRIP]).astype(BF16))
    gw2p = jnp.pad(gla_gw2, ((0, LANES - GLA_GATE_RANK), (0, 0))).astype(BF16)
    gla_wts = (gw2p, row(gla_gb), row(gla_norm_w))
    return w_p, rw_wts, gla_wts


RW_PIECES = 1
GLA_PIECES = 1
PAIR_LOCKSTEP = 8
SAMPLE_CHUNKS_TOGETHER = 4
RW_STEP_ROWS = 2 * ROWS
STRIP = 256


def _trunk(x, shift, s_rw, s_gla, norm1_g, w_p, rw_wts, gla_wts, w_out, norm2_g, w_up, w_down, norm_f_g,
           nb, tb, c_rw, c_gla, tm):
    bsz, seq, _ = x.shape
    t = bsz * seq
    x2d = x.reshape(t, D_MODEL)
    proj = _inproj(x2d, norm1_g.reshape(1, -1), w_p, tm, INPROJ_TN)
    shift_p = _pad_cols(shift, GROUP_PAD).reshape(bsz, 1, GROUP_PAD)
    rw_tb = RW_STEP_ROWS if nb == 1 and seq % RW_STEP_ROWS == 0 else tb
    o_rw, sh_new, s_rw_new = _rwkv(proj, shift_p, s_rw, rw_wts, nb, rw_tb, c_rw, RW_PIECES)
    o_gla, s_gla_new = _gla(proj, s_gla, gla_wts, nb, rw_tb, c_gla, GLA_PIECES)
    y = _post(x2d, o_rw, o_gla, w_out, norm2_g.reshape(1, -1),
              w_up, w_down, norm_f_g.reshape(1, -1), POST_TM, POST_TF)
    return (y.reshape(bsz, seq, D_MODEL), sh_new[:, 0, :RW_PROJ][None], s_rw_new[None], s_gla_new[None])


def kernel(x_prompt, x_sample, state_rwkv_shift, state_rwkv_wkv, state_gla, norm1_g, w_in, rw_mu, rw_w0,
           rw_w2, rw_a0, rw_a2, rw_g2, rw_k_k, rw_k_a, rw_r_k, rw_ln_w, rw_ln_b, gla_gw2, gla_gb, gla_norm_w,
           w_out, norm2_g, w_up, w_down, norm_f_g):
    w_p, rw_wts, gla_wts = _prep_weights(
        w_in[0], rw_mu[0], rw_w0[0], rw_w2[0], rw_a0[0], rw_a2[0], rw_g2[0], rw_k_k[0], rw_k_a[0],
        rw_r_k[0].reshape(-1), rw_ln_w[0], rw_ln_b[0], gla_gw2[0], gla_gb[0], gla_norm_w[0])
    shared = (norm1_g[0], w_p, rw_wts, gla_wts, w_out[0].astype(BF16), norm2_g[0], w_up[0].astype(BF16),
              w_down[0].astype(BF16), norm_f_g)

    bp, lp, _ = x_prompt.shape
    bs, ls, _ = x_sample.shape
    dt = x_prompt.dtype
    out_p = _trunk(x_prompt, jnp.zeros((bp, RW_PROJ), dt),
                   jnp.zeros((bp, RW_HEADS, RW_HEAD_DIM, RW_HEAD_DIM), dt),
                   jnp.zeros((bp, GLA_HEADS, GLA_DK, GLA_DV), dt), *shared,
                   nb=1, tb=ROWS, c_rw=64, c_gla=GLA_CHUNK, tm=1024)
    out_s = _trunk(x_sample, state_rwkv_shift[0], state_rwkv_wkv[0], state_gla[0], *shared,
                   nb=ROWS // ls, tb=ls, c_rw=ls, c_gla=ls, tm=1024)
    return (out_p[0], out_s[0], out_p[1], out_p[2], out_p[3], out_s[1], out_s[2], out_s[3])
```

```python
import functools

import jax
import jax.numpy as jnp
from jax import lax
from jax.experimental import pallas as pl
from jax.experimental.pallas import tpu as pltpu

F32 = jnp.float32
BF16 = jnp.bfloat16

D_MODEL = 2048
RW_WIDTH = 1024
RW_HEADS = 16
RW_HEAD_DIM = 64
RW_LORA_W = 64
RW_LORA_A = 64
RW_LORA_G = 160
RW_PROJ = 3 * RW_WIDTH + RW_LORA_W + RW_LORA_A + RW_LORA_G
RW_GN_EPS = 64e-5
GLA_WIDTH = 1024
GLA_HEADS = 4
GLA_KEY_WIDTH = 512
GLA_DK = 128
GLA_DV = 256
GLA_GATE_RANK = 16
GLA_GATE_NORM = 16.0
GLA_CHUNK = 64
GLA_PROJ = 2 * GLA_KEY_WIDTH + GLA_WIDTH + GLA_GATE_RANK + GLA_WIDTH
D_FF = 4 * D_MODEL
NORM_EPS = 1e-6
HEAD_NORM_EPS = 1e-5
DECAY_SCALE = 0.6065306597126334

LANES = 128
SUBLANES = 8

GROUP_PAD = 3456
RW_OFF_XWA = 3 * RW_WIDTH
RW_OFF_XG = RW_OFF_XWA + LANES
GLA_OFF_K = GLA_KEY_WIDTH
GLA_OFF_V = 2 * GLA_KEY_WIDTH
GLA_OFF_GATE = GLA_OFF_V + GLA_WIDTH
GLA_OFF_GOUT = GLA_OFF_GATE + LANES
NP = 2 * GROUP_PAD

ROWS = 128
MXU_WIDTH = 256
INPROJ_TN = 9 * MXU_WIDTH
PROJ_DTYPE = BF16
POST_TM = 512
POST_TF = 1024
VMEM_LIMIT = 56 * 1024 * 1024

NN = (((1,), (0,)), ((), ()))
NT = (((1,), (1,)), ((), ()))


def _dot(a, b, dims=NN):
    return lax.dot_general(a, b, dims, preferred_element_type=F32)


def _parts(x, n):
    if x.dtype == BF16:
        return [x]
    out = []
    rem = x
    for i in range(n):
        h = rem.astype(BF16)
        out.append(h)
        if i + 1 < n:
            rem = rem - h.astype(F32)
    return out


def _mm(a, b, pa=1, pb=1, dims=NN):
    pa = 1 if a.dtype == BF16 else pa
    pb = 1 if b.dtype == BF16 else pb
    aa = _parts(a, pa)
    bb = _parts(b, pb)
    n = max(pa, pb)
    if dims == NN and min(pa, pb) == 1 and n * a.shape[1] <= MXU_WIDTH:
        return _dot(jnp.concatenate(aa * (n // pa), axis=1), jnp.concatenate(bb * (n // pb), axis=0))
    acc = None
    for i in range(pa):
        for j in range(pb):
            if i + j < n:
                t = _dot(aa[i], bb[j], dims)
                acc = t if acc is None else acc + t
    return acc


def _softplus(x):
    return jnp.maximum(x, 0.0) + jnp.log(1.0 + jnp.exp(-jnp.abs(x)))


def _sigmoid(x):
    return 0.5 * jnp.tanh(0.5 * x) + 0.5


def _chunk_masks(m, c):
    sh = c.bit_length() - 1
    ri = lax.broadcasted_iota(jnp.int32, (m, m), 0)
    ci = lax.broadcasted_iota(jnp.int32, (m, m), 1)
    same = (ri >> sh) == (ci >> sh)
    return same, ri, ci


def _inproj_kernel(x_ref, g_ref, w_ref, o_ref, h_ref):
    @pl.when(pl.program_id(1) == 0)
    def _():
        x = x_ref[...]
        ms = jnp.mean(x * x, axis=-1, keepdims=True)
        h_ref[...] = (x * lax.rsqrt(ms + NORM_EPS) * g_ref[...]).astype(BF16)

    o_ref[...] = _dot(h_ref[...], w_ref[...], NT).astype(o_ref.dtype)


def _inproj(x2d, g, w_p, tm, tn):
    t = x2d.shape[0]
    return pl.pallas_call(
        _inproj_kernel,
        grid=(t // tm, NP // tn),
        in_specs=[
            pl.BlockSpec((tm, D_MODEL), lambda i, j: (i, 0)),
            pl.BlockSpec((1, D_MODEL), lambda i, j: (0, 0)),
            pl.BlockSpec((tn, D_MODEL), lambda i, j: (j, 0)),
        ],
        out_specs=pl.BlockSpec((tm, tn), lambda i, j: (i, j)),
        out_shape=jax.ShapeDtypeStruct((t, NP), PROJ_DTYPE),
        scratch_shapes=[pltpu.VMEM((tm, D_MODEL), BF16)],
        compiler_params=pltpu.CompilerParams(
            dimension_semantics=("parallel", "arbitrary"), vmem_limit_bytes=VMEM_LIMIT),
        name="inproj",
    )(x2d, g, w_p)


def _rwkv_kernel(p_ref, sh_ref, s0_ref, mu_ref, wwa_ref, w0_ref, a0_ref, g2_ref, kk_ref, ka_ref,
                 rk_ref, lnw_ref, lnb_ref, e1_ref, e2_ref, e12_ref,
                 o_ref, shout_ref, sout_ref,
                 carry_ref, st_ref, bk_ref, kb_ref, pc_ref, art_ref, vt_ref, arbd_ref, uy_ref, yt_ref,
                 g_ref, bonus_ref, *, nb, tb, c, pp):
    j = pl.program_id(1)
    m_rows = ROWS
    nsub = nb * tb // m_rows
    tbs = tb // nsub
    n_lvl = c.bit_length() - 1
    hd = RW_HEAD_DIM

    @pl.when(j == 0)
    def _():
        for p in range(RW_HEADS // 2):
            st_ref[:, p] = jnp.concatenate([s0_ref[:, 2 * p], s0_ref[:, 2 * p + 1]], axis=-1)
        carry_ref[...] = sh_ref[...]

    yt_ref[...] = jnp.zeros_like(yt_ref)

    def shifted(sb, c0, w):
        r0 = sb * m_rows
        pc3 = p_ref[r0:r0 + m_rows, c0:c0 + w].astype(F32).reshape(nb, tbs, w)
        if sb == 0:
            before = carry_ref[:, :, c0:c0 + w]
        else:
            before = p_ref[r0 - 1:r0, c0:c0 + w].astype(F32).reshape(1, 1, w)
        t3 = lax.broadcasted_iota(jnp.int32, pc3.shape, 1)
        prev3 = jnp.where(t3 == 0, before, pltpu.roll(pc3, 1, axis=1))
        return (pc3 + (prev3 - pc3) * mu_ref[:, c0:c0 + w]).reshape(m_rows, w)

    same, ri, ci = _chunk_masks(m_rows, c)
    lmask = jnp.where(same & (ci <= ri), 1.0, 0.0).astype(BF16)
    strict_t = same & (ri < ci)
    incl_t = same & (ri <= ci)

    sw = STRIP
    strips = [slice(c0, c0 + sw) for c0 in range(0, RW_WIDTH, sw)]

    def head_stat(x, cs):
        return _dot(x.astype(BF16), e1_ref[cs, :])

    def head_bcast(s, cs):
        return _mm(s, e2_ref[:, cs], pa=2)

    def head_sum(x):
        return _dot(x.astype(BF16), e12_ref[...])


    def prologue(sb):
        slab = shifted(sb, RW_OFF_XWA, LANES)
        lane = lax.broadcasted_iota(jnp.int32, slab.shape, 1)
        lhs = jnp.where(lane < RW_LORA_W, jnp.tanh(slab), slab).astype(BF16)
        sg = _sigmoid(shifted(sb, RW_OFF_XG, 2 * LANES)).astype(BF16)
        yield
        lws = [_dot(lhs, wwa_ref[:, cs]) for cs in strips]
        las = [_dot(lhs, wwa_ref[:, RW_WIDTH + cs.start:RW_WIDTH + cs.stop]) for cs in strips]
        for cs in strips:
            g_ref[sb, :, cs] = _dot(sg, g2_ref[:, cs])
        yield
        logws = [-DECAY_SCALE * _sigmoid(w0_ref[:, cs] + lw) for cs, lw in zip(strips, lws)]
        gcums = [_mm(lmask, logw, pb=2) for logw in logws]
        yield
        avs = [_sigmoid(a0_ref[:, cs] + la) for cs, la in zip(strips, las)]
        ks = [shifted(sb, RW_WIDTH + cs.start, sw) for cs in strips]
        kkfs = [k * kk_ref[:, cs] for cs, k in zip(strips, ks)]
        rinvs = [lax.rsqrt(jnp.maximum(head_stat(kkf * kkf, cs), 1e-24)) for cs, kkf in zip(strips, kkfs)]
        yield
        k2s = [k * (1.0 + (a - 1.0) * ka_ref[:, cs]) for cs, k, a in zip(strips, ks, avs)]
        rs = [shifted(sb, cs.start, sw) for cs in strips]
        bsums = [head_sum(r * k2 * rk_ref[:, cs]) for cs, r, k2 in zip(strips, rs, k2s)]
        kks = [kkf * head_bcast(rinv, cs) for cs, kkf, rinv in zip(strips, kkfs, rinvs)]
        yield
        for cs, bsum in zip(strips, bsums):
            v = shifted(sb, 2 * RW_WIDTH + cs.start, sw)
            bonus_ref[sb, :, cs] = bsum * v
            vt_ref[sb, cs, :] = v.T
        yield
        for cs, logw, gcum, a, kk, k2, r in zip(strips, logws, gcums, avs, kks, k2s, rs):
            beta = kk * a
            g3 = gcum.reshape(cps, c, sw)
            gtot = jnp.broadcast_to(g3[:, c - 1:c, :], g3.shape).reshape(m_rows, sw)
            e_inv = jnp.exp(-gcum)
            e_rev = jnp.exp(gtot - gcum)
            art_ref[sb, cs, 0:m_rows] = (-kk * jnp.exp(gcum - logw)).T
            art_ref[sb, cs, m_rows:2 * m_rows] = (r * jnp.exp(gcum)).T
            bk_ref[sb, 0:m_rows, cs] = beta * e_inv
            bk_ref[sb, m_rows:, cs] = k2 * e_inv
            kb_ref[sb, 0:m_rows, cs] = k2 * e_rev
            kb_ref[sb, m_rows:, cs] = beta * e_rev
            pc_ref[sb, :, cs] = jnp.exp(gtot)
            yield

    pw = 2 * hd
    cat = jnp.concatenate
    top = lax.broadcasted_iota(jnp.int32, (pw, 1), 0) < hd
    left = lax.broadcasted_iota(jnp.int32, (1, 2 * m_rows), 1) < m_rows

    def split_rows(x):
        return cat([jnp.where(top, x, 0.0), jnp.where(top, 0.0, x)], axis=1)

    def block_diag(xc):
        return cat([jnp.where(left, xc, 0.0), jnp.where(left, 0.0, xc)], axis=0)

    def blk(xt, u, rh, ch):
        col = (2 * u + ch) * m_rows
        return xt[rh * m_rows:(rh + 1) * m_rows, col:col + m_rows]

    n_pairs = RW_HEADS // 2
    pairs = range(n_pairs)
    los = [p * pw for p in pairs]

    def pair_phase(sb):
        for g0 in range(0, n_pairs, PAIR_LOCKSTEP):
            yield from pair_group(sb, pairs[g0:g0 + PAIR_LOCKSTEP], los[g0:g0 + PAIR_LOCKSTEP])

    def pair_group(sb, pairs, los):
        arts = [art_ref[sb, lo:lo + pw, :] for lo in los]
        xts = [_mm(bk_ref[sb, :, lo:lo + pw], split_rows(art), pp, pp) for lo, art in zip(los, arts)]
        yield
        npcs = [cat([jnp.where(strict_t, blk(xt, 0, 0, 0), 0.0), jnp.where(strict_t, blk(xt, 1, 0, 0), 0.0)], axis=1)
                for xt in xts]
        vas = [_mm(split_rows(vt_ref[sb, lo:lo + pw, :]),
                   cat([cat([jnp.where(strict_t, blk(xt, u, 1, 0), 0.0), jnp.where(incl_t, blk(xt, u, 1, 1), 0.0)],
                            axis=1) for u in range(2)], axis=0), pp, pp)
               for lo, xt in zip(los, xts)]
        yield
        zcs = [cat([cat([art[u * hd:(u + 1) * hd, 0:m_rows], va[u * hd:(u + 1) * hd, 0:m_rows]], axis=0)
                    for u in range(2)], axis=1) for art, va in zip(arts, vas)]
        for lvl in range(n_lvl):
            bds = [block_diag(npc) for npc in npcs]
            if lvl + 1 < n_lvl:
                ts = [_mm(cat([zc, npc], axis=0), bd, pp, pp) for zc, npc, bd in zip(zcs, npcs, bds)]
                zcs = [zc + t[0:2 * hd] for zc, t in zip(zcs, ts)]
                npcs = [t[2 * hd:] for t in ts]
            else:
                zcs = [zc + _mm(zc, bd, pp, pp) for zc, bd in zip(zcs, bds)]
            yield
        rycs = [cat([cat([art[u * hd:(u + 1) * hd, m_rows:], va[u * hd:(u + 1) * hd, m_rows:]], axis=0)
                     for u in range(2)], axis=1)
                + _mm(zc, block_diag(cat([jnp.where(incl_t, blk(xt, 0, 0, 1), 0.0),
                                          jnp.where(incl_t, blk(xt, 1, 0, 1), 0.0)], axis=1)), pp, pp)
                for art, va, zc, xt in zip(arts, vas, zcs, xts)]
        yield
        zero = jnp.zeros((hd, 2 * m_rows), BF16)
        for p, zc, ry in zip(pairs, zcs, rycs):
            arbd_ref[sb, p, 0:hd, 0:2 * m_rows] = cat([zc[0:hd, 0:m_rows], ry[0:hd, 0:m_rows]], axis=1).astype(BF16)
            arbd_ref[sb, p, 0:hd, 2 * m_rows:] = zero
            arbd_ref[sb, p, hd:, 0:2 * m_rows] = zero
            arbd_ref[sb, p, hd:, 2 * m_rows:] = cat([zc[0:hd, m_rows:], ry[0:hd, m_rows:]], axis=1).astype(BF16)
            uy_ref[sb, p] = cat([zc[hd:, 0:m_rows], ry[hd:, 0:m_rows], zc[hd:, m_rows:], ry[hd:, m_rows:]], axis=1)
        yield

    sh_c = c.bit_length() - 1
    cps = m_rows // c
    cpb = tbs // c
    row_id2 = (lax.broadcasted_iota(jnp.int32, (2 * m_rows, pw), 0) & (m_rows - 1)) >> sh_c
    col_id = lax.broadcasted_iota(jnp.int32, (hd, m_rows), 1) >> sh_c
    lane_lo = lax.broadcasted_iota(jnp.int32, (1, pw), 1) < hd

    def chunk_step(sb, i):
        b = i // cpb
        r0 = i * c
        rmask2 = row_id2 == i
        cmask = col_id == i
        ss = [st_ref[b, p] for p in pairs]
        zss = [_mm(s, arbd_ref[sb, p], pp, pp) + uy_ref[sb, p] for p, s in zip(pairs, ss)]
        yield
        upd = []
        for lo, zs in zip(los, zss):
            kbz = jnp.where(rmask2, kb_ref[sb, :, lo:lo + pw], 0.0)
            rhs = cat([jnp.where(lane_lo, kbz, 0.0), jnp.where(lane_lo, 0.0, kbz)], axis=0)
            vu = cat([vt_ref[sb, lo:lo + hd, :], zs[:, 0:m_rows],
                      vt_ref[sb, lo + hd:lo + pw, :], zs[:, 2 * m_rows:3 * m_rows]], axis=1)
            upd.append(_mm(vu, rhs, pp, pp))
        yield
        for p, lo, s, zs, d in zip(pairs, los, ss, zss, upd):
            st_ref[b, p] = s * pc_ref[sb, pl.ds(r0, 1), lo:lo + pw] + d
            for u in range(2):
                rows = slice((2 * p + u) * hd, (2 * p + u + 1) * hd)
                yt_ref[sb, rows, :] = jnp.where(cmask, zs[:, (2 * u + 1) * m_rows:(2 * u + 2) * m_rows],
                                               yt_ref[sb, rows, :])
        yield

    inv_n = 1.0 / hd

    def epilogue(sb):
        ys = [yt_ref[sb, cs, :].T for cs in strips]
        ycs = [y - head_sum(y) * inv_n for y in ys]
        yield
        rstds = [lax.rsqrt(head_stat(yc * yc, cs) * inv_n + RW_GN_EPS) for cs, yc in zip(strips, ycs)]
        yield
        for cs, yc, rstd in zip(strips, ycs, rstds):
            yn = yc * head_bcast(rstd, cs) * lnw_ref[:, cs] + lnb_ref[:, cs]
            o_ref[sb * m_rows:(sb + 1) * m_rows, cs] = (
                (yn + bonus_ref[sb, :, cs]) * g_ref[sb, :, cs]).astype(o_ref.dtype)
        yield

    def run(*gens):
        live = list(gens)
        while live:
            live = [g for g in live if next(g, live) is not live]

    def tail(sb):
        for i in range(cps):
            yield from chunk_step(sb, i)
        yield from epilogue(sb)

    if nsub == 1:
        run(prologue(0))
        run(pair_phase(0))

        together = SAMPLE_CHUNKS_TOGETHER if cpb == 1 and cps % SAMPLE_CHUNKS_TOGETHER == 0 else 1

        def chunk_body(i, carry):
            run(*[chunk_step(0, i * together + u) for u in range(together)])
            return carry

        lax.fori_loop(0, cps // together, chunk_body, 0)
        run(epilogue(0))
    else:
        run(prologue(0))
        for sb in range(nsub):
            run(pair_phase(sb), *([prologue(sb + 1)] if sb + 1 < nsub else []), *([tail(sb - 1)] if sb else []))
        run(tail(nsub - 1))

    if nb == 1:
        last = p_ref[tb - 1:tb, :].astype(F32).reshape(1, 1, GROUP_PAD)
    else:
        last = p_ref[...].astype(F32).reshape(nb, tb, GROUP_PAD)[:, tb - 1:tb, :]
    carry_ref[...] = last
    shout_ref[...] = last

    @pl.when(j == pl.num_programs(1) - 1)
    def _():
        for p in pairs:
            sp = st_ref[:, p]
            sout_ref[:, 2 * p] = sp[:, :, 0:hd]
            sout_ref[:, 2 * p + 1] = sp[:, :, hd:]


def _rwkv(p, shift_prev, s0, wts, nb, tb, c, pp):
    bsz = s0.shape[0]
    seq = p.shape[0] // bsz
    m_rows = ROWS
    nsub = nb * tb // m_rows
    assert nsub * m_rows == nb * tb and (nsub == 1 or nb == 1)
    hd = RW_HEAD_DIM
    const = lambda shape: pl.BlockSpec(shape, lambda i, j: (0,) * len(shape))
    kern = functools.partial(_rwkv_kernel, nb=nb, tb=tb, c=c, pp=pp)
    return pl.pallas_call(
        kern,
        grid=(bsz // nb, seq // tb),
        in_specs=[
            pl.BlockSpec((nb * tb, GROUP_PAD), lambda i, j: (i * (seq // tb) + j, 0)),
            pl.BlockSpec((nb, 1, GROUP_PAD), lambda i, j: (i, 0, 0)),
            pl.BlockSpec((nb, RW_HEADS, hd, hd), lambda i, j: (i, 0, 0, 0)),
            const((1, GROUP_PAD)),
            const((LANES, 2 * RW_WIDTH)),
            const((1, RW_WIDTH)),
            const((1, RW_WIDTH)),
            const((2 * LANES, RW_WIDTH)),
            const((1, RW_WIDTH)),
            const((1, RW_WIDTH)),
            const((1, RW_WIDTH)),
            const((1, RW_WIDTH)),
            const((1, RW_WIDTH)),
            const((RW_WIDTH, LANES)),
            const((LANES, RW_WIDTH)),
            const((STRIP, STRIP)),
        ],
        out_specs=[
            pl.BlockSpec((nb * tb, RW_WIDTH), lambda i, j: (i * (seq // tb) + j, 0)),
            pl.BlockSpec((nb, 1, GROUP_PAD), lambda i, j: (i, 0, 0)),
            pl.BlockSpec((nb, RW_HEADS, hd, hd), lambda i, j: (i, 0, 0, 0)),
        ],
        out_shape=[
            jax.ShapeDtypeStruct((bsz * seq, RW_WIDTH), BF16),
            jax.ShapeDtypeStruct((bsz, 1, GROUP_PAD), F32),
            jax.ShapeDtypeStruct((bsz, RW_HEADS, hd, hd), F32),
        ],
        scratch_shapes=[
            pltpu.VMEM((nb, 1, GROUP_PAD), F32),
            pltpu.VMEM((nb, RW_HEADS // 2, hd, 2 * hd), F32),
            pltpu.VMEM((nsub, 2 * m_rows, RW_WIDTH), F32),
            pltpu.VMEM((nsub, 2 * m_rows, RW_WIDTH), F32),
            pltpu.VMEM((nsub, m_rows, RW_WIDTH), F32),
            pltpu.VMEM((nsub, RW_WIDTH, 2 * m_rows), F32),
            pltpu.VMEM((nsub, RW_WIDTH, m_rows), F32),
            pltpu.VMEM((nsub, RW_HEADS // 2, 2 * hd, 4 * m_rows), BF16),
            pltpu.VMEM((nsub, RW_HEADS // 2, hd, 4 * m_rows), F32),
            pltpu.VMEM((nsub, RW_WIDTH, m_rows), F32),
            pltpu.VMEM((nsub, m_rows, RW_WIDTH), F32),
            pltpu.VMEM((nsub, m_rows, RW_WIDTH), F32),
        ],
        compiler_params=pltpu.CompilerParams(
            dimension_semantics=("parallel", "arbitrary"), vmem_limit_bytes=VMEM_LIMIT),
        name="rwkv7",
    )(p, shift_prev, s0, *wts)


def _gla_kernel(p_ref, s0_ref, gw2_ref, gb_ref, nw_ref, o_ref, st_ref,
                acc_ref, qd_ref, v_ref, kt_ref, et_ref, *, nb, tb, c, pp):
    j = pl.program_id(1)
    m_rows = nb * tb
    dk, dv = GLA_DK, GLA_DV

    @pl.when(j == 0)
    def _():
        st_ref[...] = s0_ref[...]

    p = p_ref[...].astype(F32)
    q = p[:, 0:GLA_KEY_WIDTH] * (dk ** -0.5)
    k = p[:, GLA_OFF_K:GLA_OFF_K + GLA_KEY_WIDTH]
    v = p[:, GLA_OFF_V:GLA_OFF_V + GLA_WIDTH]
    xs = p[:, GLA_OFF_GATE:GLA_OFF_GATE + LANES].astype(BF16)
    gout = p[:, GLA_OFF_GOUT:GLA_OFF_GOUT + GLA_WIDTH]
    gk = -_softplus(-(_dot(xs, gw2_ref[...]) + gb_ref[...])) / GLA_GATE_NORM

    same, ri, ci = _chunk_masks(m_rows, c)
    lmask = jnp.where(same & (ci <= ri), 1.0, 0.0).astype(BF16)
    causal = same & (ci <= ri)
    gcum = _mm(lmask, gk, pb=3)
    g3 = gcum.reshape(m_rows // c, c, GLA_KEY_WIDTH)
    gtot = jnp.broadcast_to(g3[:, c - 1:c, :], g3.shape).reshape(m_rows, GLA_KEY_WIDTH)
    qd = q * jnp.exp(gcum)
    kinv = k * jnp.exp(-gcum)
    qd_ref[...] = qd
    v_ref[...] = v
    kt_ref[...] = (k * jnp.exp(gtot - gcum)).T
    et_ref[...] = jnp.exp(gtot).T

    heads = range(GLA_HEADS)
    kq = [slice(h * dk, (h + 1) * dk) for h in heads]
    vq = [slice(h * dv, (h + 1) * dv) for h in heads]
    scores = [jnp.where(causal, _mm(qd[:, ks], kinv[:, ks], pp, pp, NT), 0.0) for ks in kq]
    intra = [_mm(a, v[:, vs], pp, pp) for a, vs in zip(scores, vq)]
    for vs, o in zip(vq, intra):
        acc_ref[:, vs] = o

    sh_c = c.bit_length() - 1
    cpb = tb // c
    span = min(m_rows, LANES)
    row_id = lax.broadcasted_iota(jnp.int32, (span, dv), 0) >> sh_c
    for i in range(m_rows // c):
        b = i // cpb
        r0 = i * c
        t0 = (r0 // span) * span
        rmask = row_id == (r0 - t0) // c
        states = [st_ref[b, h] for h in heads]
        inter = [_mm(qd_ref[r0:r0 + c, ks], s, pp, pp) for ks, s in zip(kq, states)]
        upd = [_mm(kt_ref[ks, t0:t0 + span], jnp.where(rmask, v_ref[t0:t0 + span, vs], 0.0), pp, pp)
               for ks, vs in zip(kq, vq)]
        for h, ks, vs, s, oi, d in zip(heads, kq, vq, states, inter, upd):
            acc_ref[r0:r0 + c, vs] += oi
            st_ref[b, h] = s * et_ref[ks, r0:r0 + 1] + d

    for vs in vq:
        o = acc_ref[:, vs]
        on = o * lax.rsqrt(jnp.mean(o * o, axis=-1, keepdims=True) + HEAD_NORM_EPS) * nw_ref[...]
        gh = gout[:, vs]
        o_ref[:, vs] = (on * (gh * _sigmoid(gh))).astype(o_ref.dtype)


def _gla(p, s0, wts, nb, tb, c, pp):
    bsz = s0.shape[0]
    seq = p.shape[0] // bsz
    m_rows = nb * tb
    const = lambda shape: pl.BlockSpec(shape, lambda i, j: (0,) * len(shape))
    kern = functools.partial(_gla_kernel, nb=nb, tb=tb, c=c, pp=pp)
    return pl.pallas_call(
        kern,
        grid=(bsz // nb, seq // tb),
        in_specs=[
            pl.BlockSpec((nb * tb, GROUP_PAD), lambda i, j: (i * (seq // tb) + j, 1)),
            pl.BlockSpec((nb, GLA_HEADS, GLA_DK, GLA_DV), lambda i, j: (i, 0, 0, 0)),
            const((LANES, GLA_KEY_WIDTH)),
            const((1, GLA_KEY_WIDTH)),
            const((1, GLA_DV)),
        ],
        out_specs=[
            pl.BlockSpec((nb * tb, GLA_WIDTH), lambda i, j: (i * (seq // tb) + j, 0)),
            pl.BlockSpec((nb, GLA_HEADS, GLA_DK, GLA_DV), lambda i, j: (i, 0, 0, 0)),
        ],
        out_shape=[
            jax.ShapeDtypeStruct((bsz * seq, GLA_WIDTH), BF16),
            jax.ShapeDtypeStruct((bsz, GLA_HEADS, GLA_DK, GLA_DV), F32),
        ],
        scratch_shapes=[
            pltpu.VMEM((m_rows, GLA_WIDTH), F32),
            pltpu.VMEM((m_rows, GLA_KEY_WIDTH), F32),
            pltpu.VMEM((m_rows, GLA_WIDTH), F32),
            pltpu.VMEM((GLA_KEY_WIDTH, m_rows), F32),
            pltpu.VMEM((GLA_KEY_WIDTH, m_rows), F32),
        ],
        compiler_params=pltpu.CompilerParams(
            dimension_semantics=("parallel", "arbitrary"), vmem_limit_bytes=VMEM_LIMIT),
        name="gla",
    )(p, s0, *wts)


def _post_kernel(x_ref, orw_ref, ogla_ref, wo_ref, g2_ref, wu_ref, wd_ref, gf_ref, o_ref, h_ref):
    jf = pl.program_id(1)

    @pl.when(jf == 0)
    def _():
        x1 = (x_ref[...]
              + _dot(orw_ref[...], wo_ref[0:RW_WIDTH, :])
              + _dot(ogla_ref[...], wo_ref[RW_WIDTH:, :]))
        o_ref[...] = x1
        ms = jnp.mean(x1 * x1, axis=-1, keepdims=True)
        h_ref[...] = (x1 * lax.rsqrt(ms + NORM_EPS) * g2_ref[...]).astype(BF16)

    u = jnp.maximum(_dot(h_ref[...], wu_ref[...]), 0.0)
    o_ref[...] += _dot((u * u).astype(BF16), wd_ref[...])

    @pl.when(jf == pl.num_programs(1) - 1)
    def _():
        x2 = o_ref[...]
        ms = jnp.mean(x2 * x2, axis=-1, keepdims=True)
        o_ref[...] = x2 * lax.rsqrt(ms + NORM_EPS) * gf_ref[...]


def _post(x2d, o_rw, o_gla, w_out, g2, w_up, w_down, gf, tm, tf):
    t = x2d.shape[0]
    return pl.pallas_call(
        _post_kernel,
        grid=(t // tm, D_FF // tf),
        in_specs=[
            pl.BlockSpec((tm, D_MODEL), lambda i, j: (i, 0)),
            pl.BlockSpec((tm, RW_WIDTH), lambda i, j: (i, 0)),
            pl.BlockSpec((tm, GLA_WIDTH), lambda i, j: (i, 0)),
            pl.BlockSpec((D_MODEL, D_MODEL), lambda i, j: (0, 0), pipeline_mode=pl.Buffered(1)),
            pl.BlockSpec((1, D_MODEL), lambda i, j: (0, 0)),
            pl.BlockSpec((D_MODEL, tf), lambda i, j: (0, j)),
            pl.BlockSpec((tf, D_MODEL), lambda i, j: (j, 0)),
            pl.BlockSpec((1, D_MODEL), lambda i, j: (0, 0)),
        ],
        out_specs=pl.BlockSpec((tm, D_MODEL), lambda i, j: (i, 0)),
        out_shape=jax.ShapeDtypeStruct((t, D_MODEL), F32),
        scratch_shapes=[pltpu.VMEM((tm, D_MODEL), BF16)],
        compiler_params=pltpu.CompilerParams(
            dimension_semantics=("parallel", "arbitrary"), vmem_limit_bytes=VMEM_LIMIT),
        name="post",
    )(x2d, o_rw, o_gla, w_out, g2, w_up, w_down, gf)


def _pad_cols(w, n):
    return jnp.pad(w, ((0, 0), (0, n - w.shape[1])))


def _prep_weights(w_in, rw_mu, rw_w0, rw_w2, rw_a0, rw_a2, rw_g2, rw_k_k, rw_k_a, rw_r_k, rw_ln_w, rw_ln_b,
                  gla_gw2, gla_gb, gla_norm_w):
    go = RW_PROJ
    w_t = jnp.swapaxes(w_in, 0, 1)
    pad_rows = lambda w, n: jnp.pad(w.astype(BF16), ((0, n - w.shape[0]), (0, 0)))
    w_p = jnp.concatenate([
        pad_rows(w_t[:RW_PROJ], GROUP_PAD),
        pad_rows(w_t[go:go + GLA_OFF_GATE + GLA_GATE_RANK], GLA_OFF_GOUT),
        pad_rows(w_t[go + GLA_OFF_GATE + GLA_GATE_RANK:], GROUP_PAD - GLA_OFF_GOUT),
    ], axis=0)

    row = lambda x: x.reshape(1, -1).astype(F32)
    mu = _pad_cols(row(rw_mu), GROUP_PAD)
    wwa = jnp.zeros((LANES, 2 * RW_WIDTH), F32)
    wwa = wwa.at[0:RW_LORA_W, 0:RW_WIDTH].set(rw_w2)
    wwa = wwa.at[RW_LORA_W:RW_LORA_W + RW_LORA_A, RW_WIDTH:].set(rw_a2)
    g2p = jnp.pad(rw_g2, ((0, 2 * LANES - RW_LORA_G), (0, 0)))
    head_of_col = jnp.arange(RW_WIDTH) // RW_HEAD_DIM
    e1 = (head_of_col[:, None] == jnp.arange(LANES)[None, :]).astype(BF16)
    rw_wts = (mu, wwa.astype(BF16), row(rw_w0), row(rw_a0), g2p.astype(BF16), row(rw_k_k), row(rw_k_a),
              row(rw_r_k), row(rw_ln_w), row(rw_ln_b), e1, e1.T,
              (head_of_col[:STRIP, None] == head_of_col[None, :STRIP]).astype(BF16))
    gw2p = jnp.pad(gla_gw2, ((0, LANES - GLA_GATE_RANK), (0, 0))).astype(BF16)
    gla_wts = (gw2p, row(gla_gb), row(gla_norm_w))
    return w_p, rw_wts, gla_wts


RW_PIECES = 1
GLA_PIECES = 1
PAIR_LOCKSTEP = 8
SAMPLE_CHUNKS_TOGETHER = 4
RW_STEP_ROWS = 2 * ROWS
STRIP = 256


def _mix(x, shift, s_rw, s_gla, norm1_g, w_p, rw_wts, gla_wts, nb, tb, c_rw, c_gla, tm, after=None):
    bsz, seq, _ = x.shape
    proj = _inproj(x.reshape(bsz * seq, D_MODEL), norm1_g.reshape(1, -1), w_p, tm, INPROJ_TN)
    if after is not None:
        proj, after = lax.optimization_barrier((proj, after))
    shift_p = _pad_cols(shift, GROUP_PAD).reshape(bsz, 1, GROUP_PAD)
    rw_tb = RW_STEP_ROWS if nb == 1 and seq % RW_STEP_ROWS == 0 else tb
    o_rw, sh_new, s_rw_new = _rwkv(proj, shift_p, s_rw, rw_wts, nb, rw_tb, c_rw, RW_PIECES)
    o_gla, s_gla_new = _gla(proj, s_gla, gla_wts, nb, rw_tb, c_gla, GLA_PIECES)
    return (o_rw, o_gla, sh_new[:, 0, :RW_PROJ][None], s_rw_new[None], s_gla_new[None]), after


def _finish(x, o_rw, o_gla, w_out, norm2_g, w_up, w_down, norm_f_g):
    bsz, seq, _ = x.shape
    y = _post(x.reshape(bsz * seq, D_MODEL), o_rw, o_gla, w_out, norm2_g.reshape(1, -1),
              w_up, w_down, norm_f_g.reshape(1, -1), POST_TM, POST_TF)
    return y.reshape(bsz, seq, D_MODEL)


def kernel(x_prompt, x_sample, state_rwkv_shift, state_rwkv_wkv, state_gla, norm1_g, w_in, rw_mu, rw_w0,
           rw_w2, rw_a0, rw_a2, rw_g2, rw_k_k, rw_k_a, rw_r_k, rw_ln_w, rw_ln_b, gla_gw2, gla_gb, gla_norm_w,
           w_out, norm2_g, w_up, w_down, norm_f_g):
    w_p, rw_wts, gla_wts = _prep_weights(
        w_in[0], rw_mu[0], rw_w0[0], rw_w2[0], rw_a0[0], rw_a2[0], rw_g2[0], rw_k_k[0], rw_k_a[0],
        rw_r_k[0].reshape(-1), rw_ln_w[0], rw_ln_b[0], gla_gw2[0], gla_gb[0], gla_norm_w[0])
    mix_w = (norm1_g[0], w_p, rw_wts, gla_wts)
    post_w = (w_out[0].astype(BF16), norm2_g[0], w_up[0].astype(BF16), w_down[0].astype(BF16), norm_f_g)

    bp, lp, _ = x_prompt.shape
    bs, ls, _ = x_sample.shape
    dt = x_prompt.dtype
    mix_p, _ = _mix(x_prompt, jnp.zeros((bp, RW_PROJ), dt),
                    jnp.zeros((bp, RW_HEADS, RW_HEAD_DIM, RW_HEAD_DIM), dt),
                    jnp.zeros((bp, GLA_HEADS, GLA_DK, GLA_DV), dt), *mix_w,
                    nb=1, tb=ROWS, c_rw=64, c_gla=GLA_CHUNK, tm=1024)
    mix_s, o_gla_p = _mix(x_sample, state_rwkv_shift[0], state_rwkv_wkv[0], state_gla[0], *mix_w,
                          nb=ROWS // ls, tb=ls, c_rw=ls, c_gla=ls, tm=1024, after=mix_p[1])
    o_rw_p, wkv_s = lax.optimization_barrier((mix_p[0], mix_s[3]))
    y_p = _finish(x_prompt, o_rw_p, o_gla_p, *post_w)
    y_s = _finish(x_sample, mix_s[0], mix_s[1], *post_w)
    return (y_p, y_s, mix_p[2], mix_p[3], mix_p[4], mix_s[2], wkv_s, mix_s[4])
```

```python
import functools

import jax
import jax.numpy as jnp
from jax import lax
from jax.experimental import pallas as pl
from jax.experimental.pallas import tpu as pltpu

F32 = jnp.float32
BF16 = jnp.bfloat16

D_MODEL = 2048
RW_WIDTH = 1024
RW_HEADS = 16
RW_HEAD_DIM = 64
RW_LORA_W = 64
RW_LORA_A = 64
RW_LORA_G = 160
RW_PROJ = 3 * RW_WIDTH + RW_LORA_W + RW_LORA_A + RW_LORA_G
RW_GN_EPS = 64e-5
GLA_WIDTH = 1024
GLA_HEADS = 4
GLA_KEY_WIDTH = 512
GLA_DK = 128
GLA_DV = 256
GLA_GATE_RANK = 16
GLA_GATE_NORM = 16.0
GLA_CHUNK = 64
GLA_PROJ = 2 * GLA_KEY_WIDTH + GLA_WIDTH + GLA_GATE_RANK + GLA_WIDTH
D_FF = 4 * D_MODEL
NORM_EPS = 1e-6
HEAD_NORM_EPS = 1e-5
DECAY_SCALE = 0.6065306597126334

LANES = 128
SUBLANES = 8

GROUP_PAD = 3456
RW_OFF_XWA = 3 * RW_WIDTH
RW_OFF_XG = RW_OFF_XWA + LANES
GLA_OFF_K = GLA_KEY_WIDTH
GLA_OFF_V = 2 * GLA_KEY_WIDTH
GLA_OFF_GATE = GLA_OFF_V + GLA_WIDTH
GLA_OFF_GOUT = GLA_OFF_GATE + LANES
NP = 2 * GROUP_PAD

ROWS = 128
MXU_WIDTH = 256
INPROJ_TN = 9 * MXU_WIDTH
NORM_ROW_CHUNKS = 4
PROJ_DTYPE = BF16
POST_TM = 512
POST_TF = 1024
VMEM_LIMIT = 56 * 1024 * 1024

NN = (((1,), (0,)), ((), ()))
NT = (((1,), (1,)), ((), ()))


def _dot(a, b, dims=NN):
    return lax.dot_general(a, b, dims, preferred_element_type=F32)


def _parts(x, n):
    if x.dtype == BF16:
        return [x]
    out = []
    rem = x
    for i in range(n):
        h = rem.astype(BF16)
        out.append(h)
        if i + 1 < n:
            rem = rem - h.astype(F32)
    return out


def _mm(a, b, pa=1, pb=1, dims=NN):
    pa = 1 if a.dtype == BF16 else pa
    pb = 1 if b.dtype == BF16 else pb
    aa = _parts(a, pa)
    bb = _parts(b, pb)
    n = max(pa, pb)
    if dims == NN and min(pa, pb) == 1 and n * a.shape[1] <= MXU_WIDTH:
        return _dot(jnp.concatenate(aa * (n // pa), axis=1), jnp.concatenate(bb * (n // pb), axis=0))
    acc = None
    for i in range(pa):
        for j in range(pb):
            if i + j < n:
                t = _dot(aa[i], bb[j], dims)
                acc = t if acc is None else acc + t
    return acc


def _softplus(x):
    return jnp.maximum(x, 0.0) + jnp.log(1.0 + jnp.exp(-jnp.abs(x)))


def _sigmoid(x):
    return 0.5 * jnp.tanh(0.5 * x) + 0.5


def _chunk_masks(m, c):
    sh = c.bit_length() - 1
    ri = lax.broadcasted_iota(jnp.int32, (m, m), 0)
    ci = lax.broadcasted_iota(jnp.int32, (m, m), 1)
    same = (ri >> sh) == (ci >> sh)
    return same, ri, ci


def _inproj_kernel(x_ref, g_ref, w_ref, o_ref, h_ref):
    first = pl.program_id(1) == 0

    @pl.when(first)
    def _():
        rows = x_ref.shape[0] // NORM_ROW_CHUNKS
        for r0 in range(0, x_ref.shape[0], rows):
            x = x_ref[r0:r0 + rows, :]
            ms = jnp.mean(x * x, axis=-1, keepdims=True)
            h = (x * lax.rsqrt(ms + NORM_EPS) * g_ref[...]).astype(BF16)
            h_ref[r0:r0 + rows, :] = h
            o_ref[r0:r0 + rows, :] = _dot(h, w_ref[...], NT).astype(o_ref.dtype)

    @pl.when(jnp.logical_not(first))
    def _():
        o_ref[...] = _dot(h_ref[...], w_ref[...], NT).astype(o_ref.dtype)


def _inproj(x2d, g, w_p, tm, tn):
    t = x2d.shape[0]
    return pl.pallas_call(
        _inproj_kernel,
        grid=(t // tm, NP // tn),
        in_specs=[
            pl.BlockSpec((tm, D_MODEL), lambda i, j: (i, 0)),
            pl.BlockSpec((1, D_MODEL), lambda i, j: (0, 0)),
            pl.BlockSpec((tn, D_MODEL), lambda i, j: (j, 0)),
        ],
        out_specs=pl.BlockSpec((tm, tn), lambda i, j: (i, j)),
        out_shape=jax.ShapeDtypeStruct((t, NP), PROJ_DTYPE),
        scratch_shapes=[pltpu.VMEM((tm, D_MODEL), BF16)],
        compiler_params=pltpu.CompilerParams(
            dimension_semantics=("parallel", "arbitrary"), vmem_limit_bytes=VMEM_LIMIT),
        name="inproj",
    )(x2d, g, w_p)


def _rwkv_kernel(p_ref, sh_ref, s0_ref, mu_ref, wwa_ref, w0_ref, a0_ref, g2_ref, kk_ref, ka_ref,
                 rk_ref, lnw_ref, lnb_ref, e1_ref, e2_ref, e12_ref,
                 o_ref, shout_ref, sout_ref,
                 carry_ref, st_ref, bk_ref, kb_ref, pc_ref, art_ref, vt_ref, arbd_ref, uy_ref, yt_ref,
                 g_ref, bonus_ref, *, nb, tb, c, pp):
    j = pl.program_id(1)
    m_rows = ROWS
    nsub = nb * tb // m_rows
    tbs = tb // nsub
    n_lvl = c.bit_length() - 1
    hd = RW_HEAD_DIM

    @pl.when(j == 0)
    def _():
        for p in range(RW_HEADS // 2):
            st_ref[:, p] = jnp.concatenate([s0_ref[:, 2 * p], s0_ref[:, 2 * p + 1]], axis=-1)
        carry_ref[...] = sh_ref[...]

    yt_ref[...] = jnp.zeros_like(yt_ref)

    def shifted(sb, c0, w):
        r0 = sb * m_rows
        pc3 = p_ref[r0:r0 + m_rows, c0:c0 + w].astype(F32).reshape(nb, tbs, w)
        if sb == 0:
            before = carry_ref[:, :, c0:c0 + w]
        else:
            before = p_ref[r0 - 1:r0, c0:c0 + w].astype(F32).reshape(1, 1, w)
        t3 = lax.broadcasted_iota(jnp.int32, pc3.shape, 1)
        prev3 = jnp.where(t3 == 0, before, pltpu.roll(pc3, 1, axis=1))
        return (pc3 + (prev3 - pc3) * mu_ref[:, c0:c0 + w]).reshape(m_rows, w)

    same, ri, ci = _chunk_masks(m_rows, c)
    lmask = jnp.where(same & (ci <= ri), 1.0, 0.0).astype(BF16)
    strict_t = same & (ri < ci)
    incl_t = same & (ri <= ci)

    sw = STRIP
    strips = [slice(c0, c0 + sw) for c0 in range(0, RW_WIDTH, sw)]

    def head_stat(x, cs):
        return _dot(x.astype(BF16), e1_ref[cs, :])

    def head_bcast(s, cs):
        return _mm(s, e2_ref[:, cs], pa=2)

    def head_sum(x):
        return _dot(x.astype(BF16), e12_ref[...])


    def prologue(sb):
        slab = shifted(sb, RW_OFF_XWA, LANES)
        lane = lax.broadcasted_iota(jnp.int32, slab.shape, 1)
        lhs = jnp.where(lane < RW_LORA_W, jnp.tanh(slab), slab).astype(BF16)
        sg = _sigmoid(shifted(sb, RW_OFF_XG, 2 * LANES)).astype(BF16)
        yield
        lws = [_dot(lhs, wwa_ref[:, cs]) for cs in strips]
        las = [_dot(lhs, wwa_ref[:, RW_WIDTH + cs.start:RW_WIDTH + cs.stop]) for cs in strips]
        for cs in strips:
            g_ref[sb, :, cs] = _dot(sg, g2_ref[:, cs])
        yield
        logws = [-DECAY_SCALE * _sigmoid(w0_ref[:, cs] + lw) for cs, lw in zip(strips, lws)]
        gcums = [_mm(lmask, logw, pb=2) for logw in logws]
        yield
        avs = [_sigmoid(a0_ref[:, cs] + la) for cs, la in zip(strips, las)]
        ks = [shifted(sb, RW_WIDTH + cs.start, sw) for cs in strips]
        kkfs = [k * kk_ref[:, cs] for cs, k in zip(strips, ks)]
        rinvs = [lax.rsqrt(jnp.maximum(head_stat(kkf * kkf, cs), 1e-24)) for cs, kkf in zip(strips, kkfs)]
        yield
        k2s = [k * (1.0 + (a - 1.0) * ka_ref[:, cs]) for cs, k, a in zip(strips, ks, avs)]
        rs = [shifted(sb, cs.start, sw) for cs in strips]
        bsums = [head_sum(r * k2 * rk_ref[:, cs]) for cs, r, k2 in zip(strips, rs, k2s)]
        kks = [kkf * head_bcast(rinv, cs) for cs, kkf, rinv in zip(strips, kkfs, rinvs)]
        yield
        for cs, bsum in zip(strips, bsums):
            v = shifted(sb, 2 * RW_WIDTH + cs.start, sw)
            bonus_ref[sb, :, cs] = bsum * v
            vt_ref[sb, cs, :] = v.T
        yield
        for cs, logw, gcum, a, kk, k2, r in zip(strips, logws, gcums, avs, kks, k2s, rs):
            beta = kk * a
            g3 = gcum.reshape(cps, c, sw)
            gtot = jnp.broadcast_to(g3[:, c - 1:c, :], g3.shape).reshape(m_rows, sw)
            e_inv = jnp.exp(-gcum)
            e_rev = jnp.exp(gtot - gcum)
            art_ref[sb, cs, 0:m_rows] = (-kk * jnp.exp(gcum - logw)).T
            art_ref[sb, cs, m_rows:2 * m_rows] = (r * jnp.exp(gcum)).T
            bk_ref[sb, 0:m_rows, cs] = beta * e_inv
            bk_ref[sb, m_rows:, cs] = k2 * e_inv
            kb_ref[sb, 0:m_rows, cs] = k2 * e_rev
            kb_ref[sb, m_rows:, cs] = beta * e_rev
            pc_ref[sb, :, cs] = jnp.exp(gtot)
            yield

    pw = 2 * hd
    cat = jnp.concatenate
    top = lax.broadcasted_iota(jnp.int32, (pw, 1), 0) < hd
    left = lax.broadcasted_iota(jnp.int32, (1, 2 * m_rows), 1) < m_rows

    def split_rows(x):
        return cat([jnp.where(top, x, 0.0), jnp.where(top, 0.0, x)], axis=1)

    def block_diag(xc):
        return cat([jnp.where(left, xc, 0.0), jnp.where(left, 0.0, xc)], axis=0)

    def blk(xt, u, rh, ch):
        col = (2 * u + ch) * m_rows
        return xt[rh * m_rows:(rh + 1) * m_rows, col:col + m_rows]

    n_pairs = RW_HEADS // 2
    pairs = range(n_pairs)
    los = [p * pw for p in pairs]

    def pair_phase(sb):
        for g0 in range(0, n_pairs, PAIR_LOCKSTEP):
            yield from pair_group(sb, pairs[g0:g0 + PAIR_LOCKSTEP], los[g0:g0 + PAIR_LOCKSTEP])

    def pair_group(sb, pairs, los):
        arts = [art_ref[sb, lo:lo + pw, :] for lo in los]
        xts = [_mm(bk_ref[sb, :, lo:lo + pw], split_rows(art), pp, pp) for lo, art in zip(los, arts)]
        yield
        npcs = [cat([jnp.where(strict_t, blk(xt, 0, 0, 0), 0.0), jnp.where(strict_t, blk(xt, 1, 0, 0), 0.0)], axis=1)
                for xt in xts]
        vas = [_mm(split_rows(vt_ref[sb, lo:lo + pw, :]),
                   cat([cat([jnp.where(strict_t, blk(xt, u, 1, 0), 0.0), jnp.where(incl_t, blk(xt, u, 1, 1), 0.0)],
                            axis=1) for u in range(2)], axis=0), pp, pp)
               for lo, xt in zip(los, xts)]
        yield
        zcs = [cat([cat([art[u * hd:(u + 1) * hd, 0:m_rows], va[u * hd:(u + 1) * hd, 0:m_rows]], axis=0)
                    for u in range(2)], axis=1) for art, va in zip(arts, vas)]
        for lvl in range(n_lvl):
            bds = [block_diag(npc) for npc in npcs]
            if lvl + 1 < n_lvl:
                ts = [_mm(cat([zc, npc], axis=0), bd, pp, pp) for zc, npc, bd in zip(zcs, npcs, bds)]
                zcs = [zc + t[0:2 * hd] for zc, t in zip(zcs, ts)]
                npcs = [t[2 * hd:] for t in ts]
            else:
                zcs = [zc + _mm(zc, bd, pp, pp) for zc, bd in zip(zcs, bds)]
            yield
        rycs = [cat([cat([art[u * hd:(u + 1) * hd, m_rows:], va[u * hd:(u + 1) * hd, m_rows:]], axis=0)
                     for u in range(2)], axis=1)
                + _mm(zc, block_diag(cat([jnp.where(incl_t, blk(xt, 0, 0, 1), 0.0),
                                          jnp.where(incl_t, blk(xt, 1, 0, 1), 0.0)], axis=1)), pp, pp)
                for art, va, zc, xt in zip(arts, vas, zcs, xts)]
        yield
        zero = jnp.zeros((hd, 2 * m_rows), BF16)
        for p, zc, ry in zip(pairs, zcs, rycs):
            arbd_ref[sb, p, 0:hd, 0:2 * m_rows] = cat([zc[0:hd, 0:m_rows], ry[0:hd, 0:m_rows]], axis=1).astype(BF16)
            arbd_ref[sb, p, 0:hd, 2 * m_rows:] = zero
            arbd_ref[sb, p, hd:, 0:2 * m_rows] = zero
            arbd_ref[sb, p, hd:, 2 * m_rows:] = cat([zc[0:hd, m_rows:], ry[0:hd, m_rows:]], axis=1).astype(BF16)
            uy_ref[sb, p] = cat([zc[hd:, 0:m_rows], ry[hd:, 0:m_rows], zc[hd:, m_rows:], ry[hd:, m_rows:]], axis=1)
        yield

    sh_c = c.bit_length() - 1
    cps = m_rows // c
    cpb = tbs // c
    row_id2 = (lax.broadcasted_iota(jnp.int32, (2 * m_rows, pw), 0) & (m_rows - 1)) >> sh_c
    col_id = lax.broadcasted_iota(jnp.int32, (hd, m_rows), 1) >> sh_c
    lane_lo = lax.broadcasted_iota(jnp.int32, (1, pw), 1) < hd

    def chunk_step(sb, i):
        b = i // cpb
        r0 = i * c
        rmask2 = row_id2 == i
        cmask = col_id == i
        ss = [st_ref[b, p] for p in pairs]
        zss = [_mm(s, arbd_ref[sb, p], pp, pp) + uy_ref[sb, p] for p, s in zip(pairs, ss)]
        yield
        upd = []
        for lo, zs in zip(los, zss):
            kbz = jnp.where(rmask2, kb_ref[sb, :, lo:lo + pw], 0.0)
            rhs = cat([jnp.where(lane_lo, kbz, 0.0), jnp.where(lane_lo, 0.0, kbz)], axis=0)
            vu = cat([vt_ref[sb, lo:lo + hd, :], zs[:, 0:m_rows],
                      vt_ref[sb, lo + hd:lo + pw, :], zs[:, 2 * m_rows:3 * m_rows]], axis=1)
            upd.append(_mm(vu, rhs, pp, pp))
        yield
        for p, lo, s, zs, d in zip(pairs, los, ss, zss, upd):
            st_ref[b, p] = s * pc_ref[sb, pl.ds(r0, 1), lo:lo + pw] + d
            for u in range(2):
                rows = slice((2 * p + u) * hd, (2 * p + u + 1) * hd)
                yt_ref[sb, rows, :] = jnp.where(cmask, zs[:, (2 * u + 1) * m_rows:(2 * u + 2) * m_rows],
                                               yt_ref[sb, rows, :])
        yield

    inv_n = 1.0 / hd

    def epilogue(sb):
        ys = [yt_ref[sb, cs, :].T for cs in strips]
        ycs = [y - head_sum(y) * inv_n for y in ys]
        yield
        rstds = [lax.rsqrt(head_stat(yc * yc, cs) * inv_n + RW_GN_EPS) for cs, yc in zip(strips, ycs)]
        yield
        for cs, yc, rstd in zip(strips, ycs, rstds):
            yn = yc * head_bcast(rstd, cs) * lnw_ref[:, cs] + lnb_ref[:, cs]
            o_ref[sb * m_rows:(sb + 1) * m_rows, cs] = (
                (yn + bonus_ref[sb, :, cs]) * g_ref[sb, :, cs]).astype(o_ref.dtype)
        yield

    def run(*gens):
        live = list(gens)
        while live:
            live = [g for g in live if next(g, live) is not live]

    def tail(sb):
        for i in range(cps):
            yield from chunk_step(sb, i)
        yield from epilogue(sb)

    if nsub == 1:
        run(prologue(0))
        run(pair_phase(0))

        together = SAMPLE_CHUNKS_TOGETHER if cpb == 1 and cps % SAMPLE_CHUNKS_TOGETHER == 0 else 1

        def chunk_body(i, carry):
            run(*[chunk_step(0, i * together + u) for u in range(together)])
            return carry

        lax.fori_loop(0, cps // together, chunk_body, 0)
        run(epilogue(0))
    else:
        run(prologue(0))
        for sb in range(nsub):
            run(pair_phase(sb), *([prologue(sb + 1)] if sb + 1 < nsub else []), *([tail(sb - 1)] if sb else []))
        run(tail(nsub - 1))

    if nb == 1:
        last = p_ref[tb - 1:tb, :].astype(F32).reshape(1, 1, GROUP_PAD)
    else:
        last = p_ref[...].astype(F32).reshape(nb, tb, GROUP_PAD)[:, tb - 1:tb, :]
    carry_ref[...] = last
    shout_ref[...] = last

    @pl.when(j == pl.num_programs(1) - 1)
    def _():
        for p in pairs:
            sp = st_ref[:, p]
            sout_ref[:, 2 * p] = sp[:, :, 0:hd]
            sout_ref[:, 2 * p + 1] = sp[:, :, hd:]


def _rwkv(p, shift_prev, s0, wts, nb, tb, c, pp):
    bsz = s0.shape[0]
    seq = p.shape[0] // bsz
    m_rows = ROWS
    nsub = nb * tb // m_rows
    assert nsub * m_rows == nb * tb and (nsub == 1 or nb == 1)
    hd = RW_HEAD_DIM
    const = lambda shape: pl.BlockSpec(shape, lambda i, j: (0,) * len(shape))
    kern = functools.partial(_rwkv_kernel, nb=nb, tb=tb, c=c, pp=pp)
    return pl.pallas_call(
        kern,
        grid=(bsz // nb, seq // tb),
        in_specs=[
            pl.BlockSpec((nb * tb, GROUP_PAD), lambda i, j: (i * (seq // tb) + j, 0)),
            pl.BlockSpec((nb, 1, GROUP_PAD), lambda i, j: (i, 0, 0)),
            pl.BlockSpec((nb, RW_HEADS, hd, hd), lambda i, j: (i, 0, 0, 0)),
            const((1, GROUP_PAD)),
            const((LANES, 2 * RW_WIDTH)),
            const((1, RW_WIDTH)),
            const((1, RW_WIDTH)),
            const((2 * LANES, RW_WIDTH)),
            const((1, RW_WIDTH)),
            const((1, RW_WIDTH)),
            const((1, RW_WIDTH)),
            const((1, RW_WIDTH)),
            const((1, RW_WIDTH)),
            const((RW_WIDTH, LANES)),
            const((LANES, RW_WIDTH)),
            const((STRIP, STRIP)),
        ],
        out_specs=[
            pl.BlockSpec((nb * tb, RW_WIDTH), lambda i, j: (i * (seq // tb) + j, 0)),
            pl.BlockSpec((nb, 1, GROUP_PAD), lambda i, j: (i, 0, 0)),
            pl.BlockSpec((nb, RW_HEADS, hd, hd), lambda i, j: (i, 0, 0, 0)),
        ],
        out_shape=[
            jax.ShapeDtypeStruct((bsz * seq, RW_WIDTH), BF16),
            jax.ShapeDtypeStruct((bsz, 1, GROUP_PAD), F32),
            jax.ShapeDtypeStruct((bsz, RW_HEADS, hd, hd), F32),
        ],
        scratch_shapes=[
            pltpu.VMEM((nb, 1, GROUP_PAD), F32),
            pltpu.VMEM((nb, RW_HEADS // 2, hd, 2 * hd), F32),
            pltpu.VMEM((nsub, 2 * m_rows, RW_WIDTH), F32),
            pltpu.VMEM((nsub, 2 * m_rows, RW_WIDTH), F32),
            pltpu.VMEM((nsub, m_rows, RW_WIDTH), F32),
            pltpu.VMEM((nsub, RW_WIDTH, 2 * m_rows), F32),
            pltpu.VMEM((nsub, RW_WIDTH, m_rows), F32),
            pltpu.VMEM((nsub, RW_HEADS // 2, 2 * hd, 4 * m_rows), BF16),
            pltpu.VMEM((nsub, RW_HEADS // 2, hd, 4 * m_rows), F32),
            pltpu.VMEM((nsub, RW_WIDTH, m_rows), F32),
            pltpu.VMEM((nsub, m_rows, RW_WIDTH), F32),
            pltpu.VMEM((nsub, m_rows, RW_WIDTH), F32),
        ],
        compiler_params=pltpu.CompilerParams(
            dimension_semantics=("parallel", "arbitrary"), vmem_limit_bytes=VMEM_LIMIT),
        name="rwkv7",
    )(p, shift_prev, s0, *wts)


def _gla_kernel(p_ref, s0_ref, gw2_ref, gb_ref, nw_ref, o_ref, st_ref,
                acc_ref, qd_ref, v_ref, kt_ref, et_ref, *, nb, tb, c, pp):
    j = pl.program_id(1)
    m_rows = nb * tb
    dk, dv = GLA_DK, GLA_DV

    @pl.when(j == 0)
    def _():
        st_ref[...] = s0_ref[...]

    p = p_ref[...].astype(F32)
    q = p[:, 0:GLA_KEY_WIDTH] * (dk ** -0.5)
    k = p[:, GLA_OFF_K:GLA_OFF_K + GLA_KEY_WIDTH]
    v = p[:, GLA_OFF_V:GLA_OFF_V + GLA_WIDTH]
    xs = p[:, GLA_OFF_GATE:GLA_OFF_GATE + LANES].astype(BF16)
    gout = p[:, GLA_OFF_GOUT:GLA_OFF_GOUT + GLA_WIDTH]
    gk = -_softplus(-(_dot(xs, gw2_ref[...]) + gb_ref[...])) / GLA_GATE_NORM

    same, ri, ci = _chunk_masks(m_rows, c)
    lmask = jnp.where(same & (ci <= ri), 1.0, 0.0).astype(BF16)
    causal = same & (ci <= ri)
    gcum = _mm(lmask, gk, pb=3)
    g3 = gcum.reshape(m_rows // c, c, GLA_KEY_WIDTH)
    gtot = jnp.broadcast_to(g3[:, c - 1:c, :], g3.shape).reshape(m_rows, GLA_KEY_WIDTH)
    qd = q * jnp.exp(gcum)
    kinv = k * jnp.exp(-gcum)
    qd_ref[...] = qd
    v_ref[...] = v
    kt_ref[...] = (k * jnp.exp(gtot - gcum)).T
    et_ref[...] = jnp.exp(gtot).T

    heads = range(GLA_HEADS)
    kq = [slice(h * dk, (h + 1) * dk) for h in heads]
    vq = [slice(h * dv, (h + 1) * dv) for h in heads]
    scores = [jnp.where(causal, _mm(qd[:, ks], kinv[:, ks], pp, pp, NT), 0.0) for ks in kq]
    intra = [_mm(a, v[:, vs], pp, pp) for a, vs in zip(scores, vq)]
    for vs, o in zip(vq, intra):
        acc_ref[:, vs] = o

    sh_c = c.bit_length() - 1
    cpb = tb // c
    span = min(m_rows, LANES)
    row_id = lax.broadcasted_iota(jnp.int32, (span, dv), 0) >> sh_c
    for i in range(m_rows // c):
        b = i // cpb
        r0 = i * c
        t0 = (r0 // span) * span
        rmask = row_id == (r0 - t0) // c
        states = [st_ref[b, h] for h in heads]
        inter = [_mm(qd_ref[r0:r0 + c, ks], s, pp, pp) for ks, s in zip(kq, states)]
        upd = [_mm(kt_ref[ks, t0:t0 + span], jnp.where(rmask, v_ref[t0:t0 + span, vs], 0.0), pp, pp)
               for ks, vs in zip(kq, vq)]
        for h, ks, vs, s, oi, d in zip(heads, kq, vq, states, inter, upd):
            acc_ref[r0:r0 + c, vs] += oi
            st_ref[b, h] = s * et_ref[ks, r0:r0 + 1] + d

    for vs in vq:
        o = acc_ref[:, vs]
        on = o * lax.rsqrt(jnp.mean(o * o, axis=-1, keepdims=True) + HEAD_NORM_EPS) * nw_ref[...]
        gh = gout[:, vs]
        o_ref[:, vs] = (on * (gh * _sigmoid(gh))).astype(o_ref.dtype)


def _gla(p, s0, wts, nb, tb, c, pp):
    bsz = s0.shape[0]
    seq = p.shape[0] // bsz
    m_rows = nb * tb
    const = lambda shape: pl.BlockSpec(shape, lambda i, j: (0,) * len(shape))
    kern = functools.partial(_gla_kernel, nb=nb, tb=tb, c=c, pp=pp)
    return pl.pallas_call(
        kern,
        grid=(bsz // nb, seq // tb),
        in_specs=[
            pl.BlockSpec((nb * tb, GROUP_PAD), lambda i, j: (i * (seq // tb) + j, 1)),
            pl.BlockSpec((nb, GLA_HEADS, GLA_DK, GLA_DV), lambda i, j: (i, 0, 0, 0)),
            const((LANES, GLA_KEY_WIDTH)),
            const((1, GLA_KEY_WIDTH)),
            const((1, GLA_DV)),
        ],
        out_specs=[
            pl.BlockSpec((nb * tb, GLA_WIDTH), lambda i, j: (i * (seq // tb) + j, 0)),
            pl.BlockSpec((nb, GLA_HEADS, GLA_DK, GLA_DV), lambda i, j: (i, 0, 0, 0)),
        ],
        out_shape=[
            jax.ShapeDtypeStruct((bsz * seq, GLA_WIDTH), BF16),
            jax.ShapeDtypeStruct((bsz, GLA_HEADS, GLA_DK, GLA_DV), F32),
        ],
        scratch_shapes=[
            pltpu.VMEM((m_rows, GLA_WIDTH), F32),
            pltpu.VMEM((m_rows, GLA_KEY_WIDTH), F32),
            pltpu.VMEM((m_rows, GLA_WIDTH), F32),
            pltpu.VMEM((GLA_KEY_WIDTH, m_rows), F32),
            pltpu.VMEM((GLA_KEY_WIDTH, m_rows), F32),
        ],
        compiler_params=pltpu.CompilerParams(
            dimension_semantics=("parallel", "arbitrary"), vmem_limit_bytes=VMEM_LIMIT),
        name="gla",
    )(p, s0, *wts)


def _post_kernel(x_ref, orw_ref, ogla_ref, wo_ref, g2_ref, wu_ref, wd_ref, gf_ref, o_ref, h_ref):
    jf = pl.program_id(1)

    @pl.when(jf == 0)
    def _():
        x1 = (x_ref[...]
              + _dot(orw_ref[...], wo_ref[0:RW_WIDTH, :])
              + _dot(ogla_ref[...], wo_ref[RW_WIDTH:, :]))
        o_ref[...] = x1
        ms = jnp.mean(x1 * x1, axis=-1, keepdims=True)
        h_ref[...] = (x1 * lax.rsqrt(ms + NORM_EPS) * g2_ref[...]).astype(BF16)

    u = jnp.maximum(_dot(h_ref[...], wu_ref[...]), 0.0)
    o_ref[...] += _dot((u * u).astype(BF16), wd_ref[...])

    @pl.when(jf == pl.num_programs(1) - 1)
    def _():
        x2 = o_ref[...]
        ms = jnp.mean(x2 * x2, axis=-1, keepdims=True)
        o_ref[...] = x2 * lax.rsqrt(ms + NORM_EPS) * gf_ref[...]


def _post(x2d, o_rw, o_gla, w_out, g2, w_up, w_down, gf, tm, tf):
    t = x2d.shape[0]
    return pl.pallas_call(
        _post_kernel,
        grid=(t // tm, D_FF // tf),
        in_specs=[
            pl.BlockSpec((tm, D_MODEL), lambda i, j: (i, 0)),
            pl.BlockSpec((tm, RW_WIDTH), lambda i, j: (i, 0)),
            pl.BlockSpec((tm, GLA_WIDTH), lambda i, j: (i, 0)),
            pl.BlockSpec((D_MODEL, D_MODEL), lambda i, j: (0, 0), pipeline_mode=pl.Buffered(1)),
            pl.BlockSpec((1, D_MODEL), lambda i, j: (0, 0)),
            pl.BlockSpec((D_MODEL, tf), lambda i, j: (0, j)),
            pl.BlockSpec((tf, D_MODEL), lambda i, j: (j, 0)),
            pl.BlockSpec((1, D_MODEL), lambda i, j: (0, 0)),
        ],
        out_specs=pl.BlockSpec((tm, D_MODEL), lambda i, j: (i, 0)),
        out_shape=jax.ShapeDtypeStruct((t, D_MODEL), F32),
        scratch_shapes=[pltpu.VMEM((tm, D_MODEL), BF16)],
        compiler_params=pltpu.CompilerParams(
            dimension_semantics=("parallel", "arbitrary"), vmem_limit_bytes=VMEM_LIMIT),
        name="post",
    )(x2d, o_rw, o_gla, w_out, g2, w_up, w_down, gf)


def _pad_cols(w, n):
    return jnp.pad(w, ((0, 0), (0, n - w.shape[1])))


def _prep_weights(w_in, rw_mu, rw_w0, rw_w2, rw_a0, rw_a2, rw_g2, rw_k_k, rw_k_a, rw_r_k, rw_ln_w, rw_ln_b,
                  gla_gw2, gla_gb, gla_norm_w):
    go = RW_PROJ
    w_t = jnp.swapaxes(w_in, 0, 1).astype(BF16)
    pad_rows = lambda w, n: jnp.pad(w, ((0, n - w.shape[0]), (0, 0)))
    w_p = jnp.concatenate([
        pad_rows(w_t[:RW_PROJ], GROUP_PAD),
        pad_rows(w_t[go:go + GLA_OFF_GATE + GLA_GATE_RANK], GLA_OFF_GOUT),
        pad_rows(w_t[go + GLA_OFF_GATE + GLA_GATE_RANK:], GROUP_PAD - GLA_OFF_GOUT),
    ], axis=0)

    row = lambda x: x.reshape(1, -1).astype(F32)
    mu = _pad_cols(row(rw_mu), GROUP_PAD)
    wwa = jnp.zeros((LANES, 2 * RW_WIDTH), F32)
    wwa = wwa.at[0:RW_LORA_W, 0:RW_WIDTH].set(rw_w2)
    wwa = wwa.at[RW_LORA_W:RW_LORA_W + RW_LORA_A, RW_WIDTH:].set(rw_a2)
    g2p = jnp.pad(rw_g2, ((0, 2 * LANES - RW_LORA_G), (0, 0)))
    head_of_col = jnp.arange(RW_WIDTH) // RW_HEAD_DIM
    e1 = (head_of_col[:, None] == jnp.arange(LANES)[None, :]).astype(BF16)
    rw_wts = (mu, wwa.astype(BF16), row(rw_w0), row(rw_a0), g2p.astype(BF16), row(rw_k_k), row(rw_k_a),
              row(rw_r_k), row(rw_ln_w), row(rw_ln_b), e1, e1.T,
              (head_of_col[:STRIP, None] == head_of_col[None, :STRIP]).astype(BF16))
    gw2p = jnp.pad(gla_gw2, ((0, LANES - GLA_GATE_RANK), (0, 0))).astype(BF16)
    gla_wts = (gw2p, row(gla_gb), row(gla_norm_w))
    return w_p, rw_wts, gla_wts


RW_PIECES = 1
GLA_PIECES = 1
PAIR_LOCKSTEP = 8
SAMPLE_CHUNKS_TOGETHER = 4
RW_STEP_ROWS = 2 * ROWS
GLA_STEP_ROWS = 2 * ROWS
STRIP = 256


def _trunk(x, shift, s_rw, s_gla, norm1_g, w_p, rw_wts, gla_wts, w_out, norm2_g, w_up, w_down, norm_f_g,
           nb, tb, c_rw, c_gla, tm):
    bsz, seq, _ = x.shape
    t = bsz * seq
    x2d = x.reshape(t, D_MODEL)
    proj = _inproj(x2d, norm1_g.reshape(1, -1), w_p, tm, INPROJ_TN)
    shift_p = _pad_cols(shift, GROUP_PAD).reshape(bsz, 1, GROUP_PAD)
    rw_tb = RW_STEP_ROWS if nb == 1 and seq % RW_STEP_ROWS == 0 else tb
    o_rw, sh_new, s_rw_new = _rwkv(proj, shift_p, s_rw, rw_wts, nb, rw_tb, c_rw, RW_PIECES)
    gla_tb = GLA_STEP_ROWS if nb == 1 and seq % GLA_STEP_ROWS == 0 else tb
    o_gla, s_gla_new = _gla(proj, s_gla, gla_wts, nb, gla_tb, c_gla, GLA_PIECES)
    y = _post(x2d, o_rw, o_gla, w_out, norm2_g.reshape(1, -1),
              w_up, w_down, norm_f_g.reshape(1, -1), POST_TM, POST_TF)
    return (y.reshape(bsz, seq, D_MODEL), sh_new[:, 0, :RW_PROJ][None], s_rw_new[None], s_gla_new[None])


def kernel(x_prompt, x_sample, state_rwkv_shift, state_rwkv_wkv, state_gla, norm1_g, w_in, rw_mu, rw_w0,
           rw_w2, rw_a0, rw_a2, rw_g2, rw_k_k, rw_k_a, rw_r_k, rw_ln_w, rw_ln_b, gla_gw2, gla_gb, gla_norm_w,
           w_out, norm2_g, w_up, w_down, norm_f_g):
    w_p, rw_wts, gla_wts = _prep_weights(
        w_in[0], rw_mu[0], rw_w0[0], rw_w2[0], rw_a0[0], rw_a2[0], rw_g2[0], rw_k_k[0], rw_k_a[0],
        rw_r_k[0].reshape(-1), rw_ln_w[0], rw_ln_b[0], gla_gw2[0], gla_gb[0], gla_norm_w[0])
    shared = (norm1_g[0], w_p, rw_wts, gla_wts, w_out[0].astype(BF16), norm2_g[0], w_up[0].astype(BF16),
              w_down[0].astype(BF16), norm_f_g)

    bp, lp, _ = x_prompt.shape
    bs, ls, _ = x_sample.shape
    dt = x_prompt.dtype
    out_p = _trunk(x_prompt, jnp.zeros((bp, RW_PROJ), dt),
                   jnp.zeros((bp, RW_HEADS, RW_HEAD_DIM, RW_HEAD_DIM), dt),
                   jnp.zeros((bp, GLA_HEADS, GLA_DK, GLA_DV), dt), *shared,
                   nb=1, tb=ROWS, c_rw=64, c_gla=GLA_CHUNK, tm=1024)
    out_s = _trunk(x_sample, state_rwkv_shift[0], state_rwkv_wkv[0], state_gla[0], *shared,
                   nb=ROWS // ls, tb=ls, c_rw=ls, c_gla=ls, tm=1024)
    return (out_p[0], out_s[0], out_p[1], out_p[2], out_p[3], out_s[1], out_s[2], out_s[3])
```

```python
import functools

import jax
import jax.numpy as jnp
from jax import lax
from jax.experimental import pallas as pl
from jax.experimental.pallas import tpu as pltpu

F32 = jnp.float32
BF16 = jnp.bfloat16

D_MODEL = 2048
RW_WIDTH = 1024
RW_HEADS = 16
RW_HEAD_DIM = 64
RW_LORA_W = 64
RW_LORA_A = 64
RW_LORA_G = 160
RW_PROJ = 3 * RW_WIDTH + RW_LORA_W + RW_LORA_A + RW_LORA_G
RW_GN_EPS = 64e-5
GLA_WIDTH = 1024
GLA_HEADS = 4
GLA_KEY_WIDTH = 512
GLA_DK = 128
GLA_DV = 256
GLA_GATE_RANK = 16
GLA_GATE_NORM = 16.0
GLA_CHUNK = 64
GLA_PROJ = 2 * GLA_KEY_WIDTH + GLA_WIDTH + GLA_GATE_RANK + GLA_WIDTH
D_FF = 4 * D_MODEL
NORM_EPS = 1e-6
HEAD_NORM_EPS = 1e-5
DECAY_SCALE = 0.6065306597126334

LANES = 128
SUBLANES = 8

GROUP_PAD = 3456
RW_OFF_XWA = 3 * RW_WIDTH
RW_OFF_XG = RW_OFF_XWA + LANES
GLA_OFF_K = GLA_KEY_WIDTH
GLA_OFF_V = 2 * GLA_KEY_WIDTH
GLA_OFF_GATE = GLA_OFF_V + GLA_WIDTH
GLA_OFF_GOUT = GLA_OFF_GATE + LANES
NP = 2 * GROUP_PAD

ROWS = 128
MXU_WIDTH = 256
INPROJ_TN = 9 * MXU_WIDTH
PROJ_DTYPE = BF16
POST_TM = 512
POST_TF = 1024
POST_TM_SINGLE = 1024
POST_TF_SINGLE = 512
VMEM_LIMIT = 56 * 1024 * 1024

NN = (((1,), (0,)), ((), ()))
NT = (((1,), (1,)), ((), ()))


def _dot(a, b, dims=NN):
    return lax.dot_general(a, b, dims, preferred_element_type=F32)


def _parts(x, n):
    if x.dtype == BF16:
        return [x]
    out = []
    rem = x
    for i in range(n):
        h = rem.astype(BF16)
        out.append(h)
        if i + 1 < n:
            rem = rem - h.astype(F32)
    return out


def _mm(a, b, pa=1, pb=1, dims=NN):
    pa = 1 if a.dtype == BF16 else pa
    pb = 1 if b.dtype == BF16 else pb
    aa = _parts(a, pa)
    bb = _parts(b, pb)
    n = max(pa, pb)
    if dims == NN and min(pa, pb) == 1 and n * a.shape[1] <= MXU_WIDTH:
        return _dot(jnp.concatenate(aa * (n // pa), axis=1), jnp.concatenate(bb * (n // pb), axis=0))
    acc = None
    for i in range(pa):
        for j in range(pb):
            if i + j < n:
                t = _dot(aa[i], bb[j], dims)
                acc = t if acc is None else acc + t
    return acc


def _softplus(x):
    return jnp.maximum(x, 0.0) + jnp.log(1.0 + jnp.exp(-jnp.abs(x)))


def _sigmoid(x):
    return 0.5 * jnp.tanh(0.5 * x) + 0.5


def _chunk_masks(m, c):
    sh = c.bit_length() - 1
    ri = lax.broadcasted_iota(jnp.int32, (m, m), 0)
    ci = lax.broadcasted_iota(jnp.int32, (m, m), 1)
    same = (ri >> sh) == (ci >> sh)
    return same, ri, ci


def _inproj_kernel(x_ref, g_ref, w_ref, o_ref, h_ref):
    @pl.when(pl.program_id(1) == 0)
    def _():
        x = x_ref[...]
        ms = jnp.mean(x * x, axis=-1, keepdims=True)
        h_ref[...] = (x * lax.rsqrt(ms + NORM_EPS) * g_ref[...]).astype(BF16)

    o_ref[...] = _dot(h_ref[...], w_ref[...], NT).astype(o_ref.dtype)


def _inproj(x2d, g, w_p, tm, tn):
    t = x2d.shape[0]
    return pl.pallas_call(
        _inproj_kernel,
        grid=(t // tm, NP // tn),
        in_specs=[
            pl.BlockSpec((tm, D_MODEL), lambda i, j: (i, 0)),
            pl.BlockSpec((1, D_MODEL), lambda i, j: (0, 0)),
            pl.BlockSpec((tn, D_MODEL), lambda i, j: (j, 0)),
        ],
        out_specs=pl.BlockSpec((tm, tn), lambda i, j: (i, j)),
        out_shape=jax.ShapeDtypeStruct((t, NP), PROJ_DTYPE),
        scratch_shapes=[pltpu.VMEM((tm, D_MODEL), BF16)],
        compiler_params=pltpu.CompilerParams(
            dimension_semantics=("parallel", "arbitrary"), vmem_limit_bytes=VMEM_LIMIT),
        name="inproj",
    )(x2d, g, w_p)


def _rwkv_kernel(p_ref, sh_ref, s0_ref, mu_ref, wwa_ref, w0_ref, a0_ref, g2_ref, kk_ref, ka_ref,
                 rk_ref, lnw_ref, lnb_ref, e1_ref, e2_ref, e12_ref,
                 o_ref, shout_ref, sout_ref,
                 carry_ref, st_ref, bk_ref, kb_ref, pc_ref, art_ref, vt_ref, arbd_ref, uy_ref, yt_ref,
                 g_ref, bonus_ref, *, nb, tb, c, pp):
    j = pl.program_id(1)
    m_rows = ROWS
    nsub = nb * tb // m_rows
    tbs = tb // nsub
    n_lvl = c.bit_length() - 1
    hd = RW_HEAD_DIM

    @pl.when(j == 0)
    def _():
        for p in range(RW_HEADS // 2):
            st_ref[:, p] = jnp.concatenate([s0_ref[:, 2 * p], s0_ref[:, 2 * p + 1]], axis=-1)
        carry_ref[...] = sh_ref[...]

    yt_ref[...] = jnp.zeros_like(yt_ref)

    def shifted(sb, c0, w):
        r0 = sb * m_rows
        pc3 = p_ref[r0:r0 + m_rows, c0:c0 + w].astype(F32).reshape(nb, tbs, w)
        if sb == 0:
            before = carry_ref[:, :, c0:c0 + w]
        else:
            before = p_ref[r0 - 1:r0, c0:c0 + w].astype(F32).reshape(1, 1, w)
        t3 = lax.broadcasted_iota(jnp.int32, pc3.shape, 1)
        prev3 = jnp.where(t3 == 0, before, pltpu.roll(pc3, 1, axis=1))
        return (pc3 + (prev3 - pc3) * mu_ref[:, c0:c0 + w]).reshape(m_rows, w)

    same, ri, ci = _chunk_masks(m_rows, c)
    lmask = jnp.where(same & (ci <= ri), 1.0, 0.0).astype(BF16)
    strict_t = same & (ri < ci)
    incl_t = same & (ri <= ci)

    sw = STRIP
    strips = [slice(c0, c0 + sw) for c0 in range(0, RW_WIDTH, sw)]

    def head_stat(x, cs):
        return _dot(x.astype(BF16), e1_ref[cs, :])

    def head_bcast(s, cs):
        return _mm(s, e2_ref[:, cs], pa=2)

    def head_sum(x):
        return _dot(x.astype(BF16), e12_ref[...])


    def prologue(sb):
        slab = shifted(sb, RW_OFF_XWA, LANES)
        lane = lax.broadcasted_iota(jnp.int32, slab.shape, 1)
        lhs = jnp.where(lane < RW_LORA_W, jnp.tanh(slab), slab).astype(BF16)
        sg = _sigmoid(shifted(sb, RW_OFF_XG, 2 * LANES)).astype(BF16)
        yield
        lws = [_dot(lhs, wwa_ref[:, cs]) for cs in strips]
        las = [_dot(lhs, wwa_ref[:, RW_WIDTH + cs.start:RW_WIDTH + cs.stop]) for cs in strips]
        for cs in strips:
            g_ref[sb, :, cs] = _dot(sg, g2_ref[:, cs])
        yield
        logws = [-DECAY_SCALE * _sigmoid(w0_ref[:, cs] + lw) for cs, lw in zip(strips, lws)]
        gcums = [_mm(lmask, logw, pb=2) for logw in logws]
        yield
        avs = [_sigmoid(a0_ref[:, cs] + la) for cs, la in zip(strips, las)]
        ks = [shifted(sb, RW_WIDTH + cs.start, sw) for cs in strips]
        kkfs = [k * kk_ref[:, cs] for cs, k in zip(strips, ks)]
        rinvs = [lax.rsqrt(jnp.maximum(head_stat(kkf * kkf, cs), 1e-24)) for cs, kkf in zip(strips, kkfs)]
        yield
        k2s = [k * (1.0 + (a - 1.0) * ka_ref[:, cs]) for cs, k, a in zip(strips, ks, avs)]
        rs = [shifted(sb, cs.start, sw) for cs in strips]
        bsums = [head_sum(r * k2 * rk_ref[:, cs]) for cs, r, k2 in zip(strips, rs, k2s)]
        kks = [kkf * head_bcast(rinv, cs) for cs, kkf, rinv in zip(strips, kkfs, rinvs)]
        yield
        for cs, bsum in zip(strips, bsums):
            v = shifted(sb, 2 * RW_WIDTH + cs.start, sw)
            bonus_ref[sb, :, cs] = bsum * v
            vt_ref[sb, cs, :] = v.T
        yield
        for cs, logw, gcum, a, kk, k2, r in zip(strips, logws, gcums, avs, kks, k2s, rs):
            beta = kk * a
            g3 = gcum.reshape(cps, c, sw)
            gtot = jnp.broadcast_to(g3[:, c - 1:c, :], g3.shape).reshape(m_rows, sw)
            e_inv = jnp.exp(-gcum)
            e_rev = jnp.exp(gtot - gcum)
            art_ref[sb, cs, 0:m_rows] = (-kk * jnp.exp(gcum - logw)).T
            art_ref[sb, cs, m_rows:2 * m_rows] = (r * jnp.exp(gcum)).T
            bk_ref[sb, 0:m_rows, cs] = beta * e_inv
            bk_ref[sb, m_rows:, cs] = k2 * e_inv
            kb_ref[sb, 0:m_rows, cs] = k2 * e_rev
            kb_ref[sb, m_rows:, cs] = beta * e_rev
            pc_ref[sb, :, cs] = jnp.exp(gtot)
            yield

    pw = 2 * hd
    cat = jnp.concatenate
    top = lax.broadcasted_iota(jnp.int32, (pw, 1), 0) < hd
    left = lax.broadcasted_iota(jnp.int32, (1, 2 * m_rows), 1) < m_rows

    def split_rows(x):
        return cat([jnp.where(top, x, 0.0), jnp.where(top, 0.0, x)], axis=1)

    def block_diag(xc):
        return cat([jnp.where(left, xc, 0.0), jnp.where(left, 0.0, xc)], axis=0)

    def blk(xt, u, rh, ch):
        col = (2 * u + ch) * m_rows
        return xt[rh * m_rows:(rh + 1) * m_rows, col:col + m_rows]

    n_pairs = RW_HEADS // 2
    pairs = range(n_pairs)
    los = [p * pw for p in pairs]

    def pair_phase(sb):
        for g0 in range(0, n_pairs, PAIR_LOCKSTEP):
            yield from pair_group(sb, pairs[g0:g0 + PAIR_LOCKSTEP], los[g0:g0 + PAIR_LOCKSTEP])

    def pair_group(sb, pairs, los):
        arts = [art_ref[sb, lo:lo + pw, :] for lo in los]
        xts = [_mm(bk_ref[sb, :, lo:lo + pw], split_rows(art), pp, pp) for lo, art in zip(los, arts)]
        yield
        npcs = [cat([jnp.where(strict_t, blk(xt, 0, 0, 0), 0.0), jnp.where(strict_t, blk(xt, 1, 0, 0), 0.0)], axis=1)
                for xt in xts]
        vas = [_mm(split_rows(vt_ref[sb, lo:lo + pw, :]),
                   cat([cat([jnp.where(strict_t, blk(xt, u, 1, 0), 0.0), jnp.where(incl_t, blk(xt, u, 1, 1), 0.0)],
                            axis=1) for u in range(2)], axis=0), pp, pp)
               for lo, xt in zip(los, xts)]
        yield
        zcs = [cat([cat([art[u * hd:(u + 1) * hd, 0:m_rows], va[u * hd:(u + 1) * hd, 0:m_rows]], axis=0)
                    for u in range(2)], axis=1) for art, va in zip(arts, vas)]
        for lvl in range(n_lvl):
            bds = [block_diag(npc) for npc in npcs]
            if lvl + 1 < n_lvl:
                ts = [_mm(cat([zc, npc], axis=0), bd, pp, pp) for zc, npc, bd in zip(zcs, npcs, bds)]
                zcs = [zc + t[0:2 * hd] for zc, t in zip(zcs, ts)]
                npcs = [t[2 * hd:] for t in ts]
            else:
                zcs = [zc + _mm(zc, bd, pp, pp) for zc, bd in zip(zcs, bds)]
            yield
        rycs = [cat([cat([art[u * hd:(u + 1) * hd, m_rows:], va[u * hd:(u + 1) * hd, m_rows:]], axis=0)
                     for u in range(2)], axis=1)
                + _mm(zc, block_diag(cat([jnp.where(incl_t, blk(xt, 0, 0, 1), 0.0),
                                          jnp.where(incl_t, blk(xt, 1, 0, 1), 0.0)], axis=1)), pp, pp)
                for art, va, zc, xt in zip(arts, vas, zcs, xts)]
        yield
        zero = jnp.zeros((hd, 2 * m_rows), BF16)
        for p, zc, ry in zip(pairs, zcs, rycs):
            arbd_ref[sb, p, 0:hd, 0:2 * m_rows] = cat([zc[0:hd, 0:m_rows], ry[0:hd, 0:m_rows]], axis=1).astype(BF16)
            arbd_ref[sb, p, 0:hd, 2 * m_rows:] = zero
            arbd_ref[sb, p, hd:, 0:2 * m_rows] = zero
            arbd_ref[sb, p, hd:, 2 * m_rows:] = cat([zc[0:hd, m_rows:], ry[0:hd, m_rows:]], axis=1).astype(BF16)
            uy_ref[sb, p] = cat([zc[hd:, 0:m_rows], ry[hd:, 0:m_rows], zc[hd:, m_rows:], ry[hd:, m_rows:]], axis=1)
        yield

    sh_c = c.bit_length() - 1
    cps = m_rows // c
    cpb = tbs // c
    row_id2 = (lax.broadcasted_iota(jnp.int32, (2 * m_rows, pw), 0) & (m_rows - 1)) >> sh_c
    col_id = lax.broadcasted_iota(jnp.int32, (hd, m_rows), 1) >> sh_c
    lane_lo = lax.broadcasted_iota(jnp.int32, (1, pw), 1) < hd

    def chunk_step(sb, i):
        b = i // cpb
        r0 = i * c
        rmask2 = row_id2 == i
        cmask = col_id == i
        ss = [st_ref[b, p] for p in pairs]
        zss = [_mm(s, arbd_ref[sb, p], pp, pp) + uy_ref[sb, p] for p, s in zip(pairs, ss)]
        yield
        upd = []
        for lo, zs in zip(los, zss):
            kbz = jnp.where(rmask2, kb_ref[sb, :, lo:lo + pw], 0.0)
            rhs = cat([jnp.where(lane_lo, kbz, 0.0), jnp.where(lane_lo, 0.0, kbz)], axis=0)
            vu = cat([vt_ref[sb, lo:lo + hd, :], zs[:, 0:m_rows],
                      vt_ref[sb, lo + hd:lo + pw, :], zs[:, 2 * m_rows:3 * m_rows]], axis=1)
            upd.append(_mm(vu, rhs, pp, pp))
        yield
        for p, lo, s, zs, d in zip(pairs, los, ss, zss, upd):
            st_ref[b, p] = s * pc_ref[sb, pl.ds(r0, 1), lo:lo + pw] + d
            for u in range(2):
                rows = slice((2 * p + u) * hd, (2 * p + u + 1) * hd)
                yt_ref[sb, rows, :] = jnp.where(cmask, zs[:, (2 * u + 1) * m_rows:(2 * u + 2) * m_rows],
                                               yt_ref[sb, rows, :])
        yield

    inv_n = 1.0 / hd

    def epilogue(sb):
        ys = [yt_ref[sb, cs, :].T for cs in strips]
        ycs = [y - head_sum(y) * inv_n for y in ys]
        yield
        rstds = [lax.rsqrt(head_stat(yc * yc, cs) * inv_n + RW_GN_EPS) for cs, yc in zip(strips, ycs)]
        yield
        for cs, yc, rstd in zip(strips, ycs, rstds):
            yn = yc * head_bcast(rstd, cs) * lnw_ref[:, cs] + lnb_ref[:, cs]
            o_ref[sb * m_rows:(sb + 1) * m_rows, cs] = (
                (yn + bonus_ref[sb, :, cs]) * g_ref[sb, :, cs]).astype(o_ref.dtype)
        yield

    def run(*gens):
        live = list(gens)
        while live:
            live = [g for g in live if next(g, live) is not live]

    def tail(sb):
        for i in range(cps):
            yield from chunk_step(sb, i)
        yield from epilogue(sb)

    if nsub == 1:
        run(prologue(0))
        run(pair_phase(0))

        together = SAMPLE_CHUNKS_TOGETHER if cpb == 1 and cps % SAMPLE_CHUNKS_TOGETHER == 0 else 1

        def chunk_body(i, carry):
            run(*[chunk_step(0, i * together + u) for u in range(together)])
            return carry

        lax.fori_loop(0, cps // together, chunk_body, 0)
        run(epilogue(0))
    else:
        run(prologue(0))
        for sb in range(nsub):
            run(pair_phase(sb), *([prologue(sb + 1)] if sb + 1 < nsub else []), *([tail(sb - 1)] if sb else []))
        run(tail(nsub - 1))

    if nb == 1:
        last = p_ref[tb - 1:tb, :].astype(F32).reshape(1, 1, GROUP_PAD)
    else:
        last = p_ref[...].astype(F32).reshape(nb, tb, GROUP_PAD)[:, tb - 1:tb, :]
    carry_ref[...] = last
    shout_ref[...] = last

    @pl.when(j == pl.num_programs(1) - 1)
    def _():
        for p in pairs:
            sp = st_ref[:, p]
            sout_ref[:, 2 * p] = sp[:, :, 0:hd]
            sout_ref[:, 2 * p + 1] = sp[:, :, hd:]


def _rwkv(p, shift_prev, s0, wts, nb, tb, c, pp):
    bsz = s0.shape[0]
    seq = p.shape[0] // bsz
    m_rows = ROWS
    nsub = nb * tb // m_rows
    assert nsub * m_rows == nb * tb and (nsub == 1 or nb == 1)
    hd = RW_HEAD_DIM
    const = lambda shape: pl.BlockSpec(shape, lambda i, j: (0,) * len(shape))
    kern = functools.partial(_rwkv_kernel, nb=nb, tb=tb, c=c, pp=pp)
    return pl.pallas_call(
        kern,
        grid=(bsz // nb, seq // tb),
        in_specs=[
            pl.BlockSpec((nb * tb, GROUP_PAD), lambda i, j: (i * (seq // tb) + j, 0)),
            pl.BlockSpec((nb, 1, GROUP_PAD), lambda i, j: (i, 0, 0)),
            pl.BlockSpec((nb, RW_HEADS, hd, hd), lambda i, j: (i, 0, 0, 0)),
            const((1, GROUP_PAD)),
            const((LANES, 2 * RW_WIDTH)),
            const((1, RW_WIDTH)),
            const((1, RW_WIDTH)),
            const((2 * LANES, RW_WIDTH)),
            const((1, RW_WIDTH)),
            const((1, RW_WIDTH)),
            const((1, RW_WIDTH)),
            const((1, RW_WIDTH)),
            const((1, RW_WIDTH)),
            const((RW_WIDTH, LANES)),
            const((LANES, RW_WIDTH)),
            const((STRIP, STRIP)),
        ],
        out_specs=[
            pl.BlockSpec((nb * tb, RW_WIDTH), lambda i, j: (i * (seq // tb) + j, 0)),
            pl.BlockSpec((nb, 1, GROUP_PAD), lambda i, j: (i, 0, 0)),
            pl.BlockSpec((nb, RW_HEADS, hd, hd), lambda i, j: (i, 0, 0, 0)),
        ],
        out_shape=[
            jax.ShapeDtypeStruct((bsz * seq, RW_WIDTH), BF16),
            jax.ShapeDtypeStruct((bsz, 1, GROUP_PAD), F32),
            jax.ShapeDtypeStruct((bsz, RW_HEADS, hd, hd), F32),
        ],
        scratch_shapes=[
            pltpu.VMEM((nb, 1, GROUP_PAD), F32),
            pltpu.VMEM((nb, RW_HEADS // 2, hd, 2 * hd), F32),
            pltpu.VMEM((nsub, 2 * m_rows, RW_WIDTH), F32),
            pltpu.VMEM((nsub, 2 * m_rows, RW_WIDTH), F32),
            pltpu.VMEM((nsub, m_rows, RW_WIDTH), F32),
            pltpu.VMEM((nsub, RW_WIDTH, 2 * m_rows), F32),
            pltpu.VMEM((nsub, RW_WIDTH, m_rows), F32),
            pltpu.VMEM((nsub, RW_HEADS // 2, 2 * hd, 4 * m_rows), BF16),
            pltpu.VMEM((nsub, RW_HEADS // 2, hd, 4 * m_rows), F32),
            pltpu.VMEM((nsub, RW_WIDTH, m_rows), F32),
            pltpu.VMEM((nsub, m_rows, RW_WIDTH), F32),
            pltpu.VMEM((nsub, m_rows, RW_WIDTH), F32),
        ],
        compiler_params=pltpu.CompilerParams(
            dimension_semantics=("parallel", "arbitrary"), vmem_limit_bytes=VMEM_LIMIT),
        name="rwkv7",
    )(p, shift_prev, s0, *wts)


def _gla_kernel(p_ref, s0_ref, gw2_ref, gb_ref, nw_ref, o_ref, st_ref,
                acc_ref, qd_ref, v_ref, kt_ref, et_ref, *, nb, tb, c, pp):
    j = pl.program_id(1)
    m_rows = nb * tb
    dk, dv = GLA_DK, GLA_DV

    @pl.when(j == 0)
    def _():
        st_ref[...] = s0_ref[...]

    p = p_ref[...].astype(F32)
    q = p[:, 0:GLA_KEY_WIDTH] * (dk ** -0.5)
    k = p[:, GLA_OFF_K:GLA_OFF_K + GLA_KEY_WIDTH]
    v = p[:, GLA_OFF_V:GLA_OFF_V + GLA_WIDTH]
    xs = p[:, GLA_OFF_GATE:GLA_OFF_GATE + LANES].astype(BF16)
    gout = p[:, GLA_OFF_GOUT:GLA_OFF_GOUT + GLA_WIDTH]
    gk = -_softplus(-(_dot(xs, gw2_ref[...]) + gb_ref[...])) / GLA_GATE_NORM

    same, ri, ci = _chunk_masks(m_rows, c)
    lmask = jnp.where(same & (ci <= ri), 1.0, 0.0).astype(BF16)
    causal = same & (ci <= ri)
    gcum = _mm(lmask, gk, pb=3)
    g3 = gcum.reshape(m_rows // c, c, GLA_KEY_WIDTH)
    gtot = jnp.broadcast_to(g3[:, c - 1:c, :], g3.shape).reshape(m_rows, GLA_KEY_WIDTH)
    qd = q * jnp.exp(gcum)
    kinv = k * jnp.exp(-gcum)
    qd_ref[...] = qd
    v_ref[...] = v
    kt_ref[...] = (k * jnp.exp(gtot - gcum)).T
    et_ref[...] = jnp.exp(gtot).T

    heads = range(GLA_HEADS)
    kq = [slice(h * dk, (h + 1) * dk) for h in heads]
    vq = [slice(h * dv, (h + 1) * dv) for h in heads]
    scores = [jnp.where(causal, _mm(qd[:, ks], kinv[:, ks], pp, pp, NT), 0.0) for ks in kq]
    intra = [_mm(a, v[:, vs], pp, pp) for a, vs in zip(scores, vq)]
    for vs, o in zip(vq, intra):
        acc_ref[:, vs] = o

    sh_c = c.bit_length() - 1
    cpb = tb // c
    span = min(m_rows, LANES)
    row_id = lax.broadcasted_iota(jnp.int32, (span, dv), 0) >> sh_c
    for i in range(m_rows // c):
        b = i // cpb
        r0 = i * c
        t0 = (r0 // span) * span
        rmask = row_id == (r0 - t0) // c
        states = [st_ref[b, h] for h in heads]
        inter = [_mm(qd_ref[r0:r0 + c, ks], s, pp, pp) for ks, s in zip(kq, states)]
        upd = [_mm(kt_ref[ks, t0:t0 + span], jnp.where(rmask, v_ref[t0:t0 + span, vs], 0.0), pp, pp)
               for ks, vs in zip(kq, vq)]
        for h, ks, vs, s, oi, d in zip(heads, kq, vq, states, inter, upd):
            acc_ref[r0:r0 + c, vs] += oi
            st_ref[b, h] = s * et_ref[ks, r0:r0 + 1] + d

    for vs in vq:
        o = acc_ref[:, vs]
        on = o * lax.rsqrt(jnp.mean(o * o, axis=-1, keepdims=True) + HEAD_NORM_EPS) * nw_ref[...]
        gh = gout[:, vs]
        o_ref[:, vs] = (on * (gh * _sigmoid(gh))).astype(o_ref.dtype)


def _gla(p, s0, wts, nb, tb, c, pp):
    bsz = s0.shape[0]
    seq = p.shape[0] // bsz
    m_rows = nb * tb
    const = lambda shape: pl.BlockSpec(shape, lambda i, j: (0,) * len(shape))
    kern = functools.partial(_gla_kernel, nb=nb, tb=tb, c=c, pp=pp)
    return pl.pallas_call(
        kern,
        grid=(bsz // nb, seq // tb),
        in_specs=[
            pl.BlockSpec((nb * tb, GROUP_PAD), lambda i, j: (i * (seq // tb) + j, 1)),
            pl.BlockSpec((nb, GLA_HEADS, GLA_DK, GLA_DV), lambda i, j: (i, 0, 0, 0)),
            const((LANES, GLA_KEY_WIDTH)),
            const((1, GLA_KEY_WIDTH)),
            const((1, GLA_DV)),
        ],
        out_specs=[
            pl.BlockSpec((nb * tb, GLA_WIDTH), lambda i, j: (i * (seq // tb) + j, 0)),
            pl.BlockSpec((nb, GLA_HEADS, GLA_DK, GLA_DV), lambda i, j: (i, 0, 0, 0)),
        ],
        out_shape=[
            jax.ShapeDtypeStruct((bsz * seq, GLA_WIDTH), BF16),
            jax.ShapeDtypeStruct((bsz, GLA_HEADS, GLA_DK, GLA_DV), F32),
        ],
        scratch_shapes=[
            pltpu.VMEM((m_rows, GLA_WIDTH), F32),
            pltpu.VMEM((m_rows, GLA_KEY_WIDTH), F32),
            pltpu.VMEM((m_rows, GLA_WIDTH), F32),
            pltpu.VMEM((GLA_KEY_WIDTH, m_rows), F32),
            pltpu.VMEM((GLA_KEY_WIDTH, m_rows), F32),
        ],
        compiler_params=pltpu.CompilerParams(
            dimension_semantics=("parallel", "arbitrary"), vmem_limit_bytes=VMEM_LIMIT),
        name="gla",
    )(p, s0, *wts)


def _post_kernel(x_ref, orw_ref, ogla_ref, wo_ref, g2_ref, wu_ref, wd_ref, gf_ref, o_ref, h_ref):
    jf = pl.program_id(1)

    @pl.when(jf == 0)
    def _():
        x1 = (x_ref[...]
              + _dot(orw_ref[...], wo_ref[0:RW_WIDTH, :])
              + _dot(ogla_ref[...], wo_ref[RW_WIDTH:, :]))
        o_ref[...] = x1
        ms = jnp.mean(x1 * x1, axis=-1, keepdims=True)
        h_ref[...] = (x1 * lax.rsqrt(ms + NORM_EPS) * g2_ref[...]).astype(BF16)

    u = jnp.maximum(_dot(h_ref[...], wu_ref[...]), 0.0)
    o_ref[...] += _dot((u * u).astype(BF16), wd_ref[...])

    @pl.when(jf == pl.num_programs(1) - 1)
    def _():
        x2 = o_ref[...]
        ms = jnp.mean(x2 * x2, axis=-1, keepdims=True)
        o_ref[...] = x2 * lax.rsqrt(ms + NORM_EPS) * gf_ref[...]


def _post(x2d, o_rw, o_gla, w_out, g2, w_up, w_down, gf):
    t = x2d.shape[0]
    if t <= POST_TM_SINGLE:
        tm, tf, rows_mode = t, POST_TF_SINGLE, pl.Buffered(1)
    else:
        tm, tf, rows_mode = POST_TM, POST_TF, None
    return pl.pallas_call(
        _post_kernel,
        grid=(t // tm, D_FF // tf),
        in_specs=[
            pl.BlockSpec((tm, D_MODEL), lambda i, j: (i, 0), pipeline_mode=rows_mode),
            pl.BlockSpec((tm, RW_WIDTH), lambda i, j: (i, 0), pipeline_mode=rows_mode),
            pl.BlockSpec((tm, GLA_WIDTH), lambda i, j: (i, 0), pipeline_mode=rows_mode),
            pl.BlockSpec((D_MODEL, D_MODEL), lambda i, j: (0, 0), pipeline_mode=pl.Buffered(1)),
            pl.BlockSpec((1, D_MODEL), lambda i, j: (0, 0)),
            pl.BlockSpec((D_MODEL, tf), lambda i, j: (0, j)),
            pl.BlockSpec((tf, D_MODEL), lambda i, j: (j, 0)),
            pl.BlockSpec((1, D_MODEL), lambda i, j: (0, 0)),
        ],
        out_specs=pl.BlockSpec((tm, D_MODEL), lambda i, j: (i, 0)),
        out_shape=jax.ShapeDtypeStruct((t, D_MODEL), F32),
        scratch_shapes=[pltpu.VMEM((tm, D_MODEL), BF16)],
        compiler_params=pltpu.CompilerParams(
            dimension_semantics=("parallel", "arbitrary"), vmem_limit_bytes=VMEM_LIMIT),
        name="post",
    )(x2d, o_rw, o_gla, w_out, g2, w_up, w_down, gf)


def _pad_cols(w, n):
    return jnp.pad(w, ((0, 0), (0, n - w.shape[1])))


def _prep_weights(w_in, rw_mu, rw_w0, rw_w2, rw_a0, rw_a2, rw_g2, rw_k_k, rw_k_a, rw_r_k, rw_ln_w, rw_ln_b,
                  gla_gw2, gla_gb, gla_norm_w):
    go = RW_PROJ
    w_t = jnp.swapaxes(w_in, 0, 1).astype(BF16)
    pad_rows = lambda w, n: jnp.pad(w, ((0, n - w.shape[0]), (0, 0)))
    w_p = jnp.concatenate([
        pad_rows(w_t[:RW_PROJ], GROUP_PAD),
        pad_rows(w_t[go:go + GLA_OFF_GATE + GLA_GATE_RANK], GLA_OFF_GOUT),
        pad_rows(w_t[go + GLA_OFF_GATE + GLA_GATE_RANK:], GROUP_PAD - GLA_OFF_GOUT),
    ], axis=0)

    row = lambda x: x.reshape(1, -1).astype(F32)
    mu = _pad_cols(row(rw_mu), GROUP_PAD)
    wwa = jnp.zeros((LANES, 2 * RW_WIDTH), F32)
    wwa = wwa.at[0:RW_LORA_W, 0:RW_WIDTH].set(rw_w2)
    wwa = wwa.at[RW_LORA_W:RW_LORA_W + RW_LORA_A, RW_WIDTH:].set(rw_a2)
    g2p = jnp.pad(rw_g2, ((0, 2 * LANES - RW_LORA_G), (0, 0)))
    head_of_col = jnp.arange(RW_WIDTH) // RW_HEAD_DIM
    e1 = (head_of_col[:, None] == jnp.arange(LANES)[None, :]).astype(BF16)
    rw_wts = (mu, wwa.astype(BF16), row(rw_w0), row(rw_a0), g2p.astype(BF16), row(rw_k_k), row(rw_k_a),
              row(rw_r_k), row(rw_ln_w), row(rw_ln_b), e1, e1.T,
              (head_of_col[:STRIP, None] == head_of_col[None, :STRIP]).astype(BF16))
    gw2p = jnp.pad(gla_gw2, ((0, LANES - GLA_GATE_RANK), (0, 0))).astype(BF16)
    gla_wts = (gw2p, row(gla_gb), row(gla_norm_w))
    return w_p, rw_wts, gla_wts


RW_PIECES = 1
GLA_PIECES = 1
PAIR_LOCKSTEP = 8
SAMPLE_CHUNKS_TOGETHER = 4
RW_STEP_ROWS = 2 * ROWS
STRIP = 256


def _trunk(x, shift, s_rw, s_gla, norm1_g, w_p, rw_wts, gla_wts, w_out, norm2_g, w_up, w_down, norm_f_g,
           nb, tb, c_rw, c_gla, tm):
    bsz, seq, _ = x.shape
    t = bsz * seq
    x2d = x.reshape(t, D_MODEL)
    proj = _inproj(x2d, norm1_g.reshape(1, -1), w_p, tm, INPROJ_TN)
    shift_p = _pad_cols(shift, GROUP_PAD).reshape(bsz, 1, GROUP_PAD)
    rw_tb = RW_STEP_ROWS if nb == 1 and seq % RW_STEP_ROWS == 0 else tb
    o_rw, sh_new, s_rw_new = _rwkv(proj, shift_p, s_rw, rw_wts, nb, rw_tb, c_rw, RW_PIECES)
    o_gla, s_gla_new = _gla(proj, s_gla, gla_wts, nb, rw_tb, c_gla, GLA_PIECES)
    y = _post(x2d, o_rw, o_gla, w_out, norm2_g.reshape(1, -1),
              w_up, w_down, norm_f_g.reshape(1, -1))
    return (y.reshape(bsz, seq, D_MODEL), sh_new[:, 0, :RW_PROJ][None], s_rw_new[None], s_gla_new[None])


def kernel(x_prompt, x_sample, state_rwkv_shift, state_rwkv_wkv, state_gla, norm1_g, w_in, rw_mu, rw_w0,
           rw_w2, rw_a0, rw_a2, rw_g2, rw_k_k, rw_k_a, rw_r_k, rw_ln_w, rw_ln_b, gla_gw2, gla_gb, gla_norm_w,
           w_out, norm2_g, w_up, w_down, norm_f_g):
    w_p, rw_wts, gla_wts = _prep_weights(
        w_in[0], rw_mu[0], rw_w0[0], rw_w2[0], rw_a0[0], rw_a2[0], rw_g2[0], rw_k_k[0], rw_k_a[0],
        rw_r_k[0].reshape(-1), rw_ln_w[0], rw_ln_b[0], gla_gw2[0], gla_gb[0], gla_norm_w[0])
    shared = (norm1_g[0], w_p, rw_wts, gla_wts, w_out[0].astype(BF16), norm2_g[0], w_up[0].astype(BF16),
              w_down[0].astype(BF16), norm_f_g)

    bp, lp, _ = x_prompt.shape
    bs, ls, _ = x_sample.shape
    dt = x_prompt.dtype
    out_p = _trunk(x_prompt, jnp.zeros((bp, RW_PROJ), dt),
                   jnp.zeros((bp, RW_HEADS, RW_HEAD_DIM, RW_HEAD_DIM), dt),
                   jnp.zeros((bp, GLA_HEADS, GLA_DK, GLA_DV), dt), *shared,
                   nb=1, tb=ROWS, c_rw=64, c_gla=GLA_CHUNK, tm=1024)
    out_s = _trunk(x_sample, state_rwkv_shift[0], state_rwkv_wkv[0], state_gla[0], *shared,
                   nb=ROWS // ls, tb=ls, c_rw=ls, c_gla=ls, tm=1024)
    return (out_p[0], out_s[0], out_p[1], out_p[2], out_p[3], out_s[1], out_s[2], out_s[3])
```

```python
import functools

import jax
import jax.numpy as jnp
from jax import lax
from jax.experimental import pallas as pl
from jax.experimental.pallas import tpu as pltpu

F32 = jnp.float32
BF16 = jnp.bfloat16

D_MODEL = 2048
RW_WIDTH = 1024
RW_HEADS = 16
RW_HEAD_DIM = 64
RW_LORA_W = 64
RW_LORA_A = 64
RW_LORA_G = 160
RW_PROJ = 3 * RW_WIDTH + RW_LORA_W + RW_LORA_A + RW_LORA_G
RW_GN_EPS = 64e-5
GLA_WIDTH = 1024
GLA_HEADS = 4
GLA_KEY_WIDTH = 512
GLA_DK = 128
GLA_DV = 256
GLA_GATE_RANK = 16
GLA_GATE_NORM = 16.0
GLA_CHUNK = 64
GLA_PROJ = 2 * GLA_KEY_WIDTH + GLA_WIDTH + GLA_GATE_RANK + GLA_WIDTH
D_FF = 4 * D_MODEL
NORM_EPS = 1e-6
HEAD_NORM_EPS = 1e-5
DECAY_SCALE = 0.6065306597126334

LANES = 128
SUBLANES = 8

GROUP_PAD = 3456
RW_OFF_XWA = 3 * RW_WIDTH
RW_OFF_XG = RW_OFF_XWA + LANES
GLA_OFF_K = GLA_KEY_WIDTH
GLA_OFF_V = 2 * GLA_KEY_WIDTH
GLA_OFF_GATE = GLA_OFF_V + GLA_WIDTH
GLA_OFF_GOUT = GLA_OFF_GATE + LANES
NP = 2 * GROUP_PAD

ROWS = 128
MXU_WIDTH = 256
INPROJ_TN = 9 * MXU_WIDTH
PROJ_DTYPE = BF16
POST_TM = 512
POST_TF = 1024
VMEM_LIMIT = 56 * 1024 * 1024

NN = (((1,), (0,)), ((), ()))
NT = (((1,), (1,)), ((), ()))


def _dot(a, b, dims=NN):
    return lax.dot_general(a, b, dims, preferred_element_type=F32)


def _parts(x, n):
    if x.dtype == BF16:
        return [x]
    out = []
    rem = x
    for i in range(n):
        h = rem.astype(BF16)
        out.append(h)
        if i + 1 < n:
            rem = rem - h.astype(F32)
    return out


def _mm(a, b, pa=1, pb=1, dims=NN):
    pa = 1 if a.dtype == BF16 else pa
    pb = 1 if b.dtype == BF16 else pb
    aa = _parts(a, pa)
    bb = _parts(b, pb)
    n = max(pa, pb)
    if dims == NN and min(pa, pb) == 1 and n * a.shape[1] <= MXU_WIDTH:
        return _dot(jnp.concatenate(aa * (n // pa), axis=1), jnp.concatenate(bb * (n // pb), axis=0))
    acc = None
    for i in range(pa):
        for j in range(pb):
            if i + j < n:
                t = _dot(aa[i], bb[j], dims)
                acc = t if acc is None else acc + t
    return acc


def _softplus(x):
    return jnp.maximum(x, 0.0) + jnp.log(1.0 + jnp.exp(-jnp.abs(x)))


def _sigmoid(x):
    return 0.5 * jnp.tanh(0.5 * x) + 0.5


def _chunk_masks(m, c):
    sh = c.bit_length() - 1
    ri = lax.broadcasted_iota(jnp.int32, (m, m), 0)
    ci = lax.broadcasted_iota(jnp.int32, (m, m), 1)
    same = (ri >> sh) == (ci >> sh)
    return same, ri, ci


_W_SEGMENTS = (
    (0, RW_PROJ, 0),
    (RW_PROJ, GLA_OFF_GATE + GLA_GATE_RANK, GROUP_PAD),
    (RW_PROJ + GLA_OFF_GATE + GLA_GATE_RANK, GLA_WIDTH, GROUP_PAD + GLA_OFF_GOUT),
)


def _weight_windows(tn):
    blocks = []
    for lo in range(0, NP, tn):
        wins = []
        for src, ln, dst in _W_SEGMENTS:
            a, b = max(dst, lo), min(dst + ln, lo + tn)
            if a < b:
                wins.append((src + a - dst, b - a, a - lo))
        blocks.append(tuple(wins))
    return tuple(blocks)


def _inproj_kernel(x_ref, g_ref, w_hbm, o_ref, h_ref, wbuf, sem, *, windows):
    i, j = pl.program_id(0), pl.program_id(1)
    nj = len(windows)
    step = i * nj + j
    n_steps = pl.num_programs(0) * nj
    slot = lax.rem(step, 2)

    def copies(jb, sl):
        return [pltpu.make_async_copy(w_hbm.at[pl.ds(src, ln), :], wbuf.at[sl, pl.ds(dst, ln), :], sem.at[sl, k])
                for k, (src, ln, dst) in enumerate(windows[jb])]

    @pl.when(step == 0)
    def _():
        for sl in range(min(2, nj)):
            covered = sorted((dst, dst + ln) for _, ln, dst in windows[sl])
            edges = [0] + [e for span in covered for e in span] + [wbuf.shape[1]]
            for a, b in zip(edges[0::2], edges[1::2]):
                if a < b:
                    wbuf[sl, a:b, :] = jnp.zeros((b - a, wbuf.shape[2]), wbuf.dtype)
        for cp in copies(0, 0):
            cp.start()

    for jb in range(nj):
        @pl.when(j == jb)
        def _(jb=jb):
            @pl.when(step + 1 < n_steps)
            def _():
                for cp in copies((jb + 1) % nj, 1 - slot):
                    cp.start()

            for cp in copies(jb, slot):
                cp.wait()

    @pl.when(j == 0)
    def _():
        x = x_ref[...]
        ms = jnp.mean(x * x, axis=-1, keepdims=True)
        h_ref[...] = (x * lax.rsqrt(ms + NORM_EPS) * g_ref[...]).astype(BF16)

    o_ref[...] = _dot(h_ref[...], wbuf[slot], NT).astype(o_ref.dtype)


def _inproj(x2d, g, w_t, tm, tn):
    t = x2d.shape[0]
    windows = _weight_windows(tn)
    return pl.pallas_call(
        functools.partial(_inproj_kernel, windows=windows),
        grid=(t // tm, NP // tn),
        in_specs=[
            pl.BlockSpec((tm, D_MODEL), lambda i, j: (i, 0)),
            pl.BlockSpec((1, D_MODEL), lambda i, j: (0, 0)),
            pl.BlockSpec(memory_space=pltpu.HBM),
        ],
        out_specs=pl.BlockSpec((tm, tn), lambda i, j: (i, j)),
        out_shape=jax.ShapeDtypeStruct((t, NP), PROJ_DTYPE),
        scratch_shapes=[
            pltpu.VMEM((tm, D_MODEL), BF16),
            pltpu.VMEM((2, tn, D_MODEL), BF16),
            pltpu.SemaphoreType.DMA((2, max(len(w) for w in windows))),
        ],
        compiler_params=pltpu.CompilerParams(
            dimension_semantics=("arbitrary", "arbitrary"), vmem_limit_bytes=VMEM_LIMIT),
        name="inproj",
    )(x2d, g, w_t)


def _rwkv_kernel(p_ref, sh_ref, s0_ref, mu_ref, wwa_ref, w0_ref, a0_ref, g2_ref, kk_ref, ka_ref,
                 rk_ref, lnw_ref, lnb_ref, e1_ref, e2_ref, e12_ref,
                 o_ref, shout_ref, sout_ref,
                 carry_ref, st_ref, bk_ref, kb_ref, pc_ref, art_ref, vt_ref, arbd_ref, uy_ref, yt_ref,
                 g_ref, bonus_ref, *, nb, tb, c, pp):
    j = pl.program_id(1)
    m_rows = ROWS
    nsub = nb * tb // m_rows
    tbs = tb // nsub
    n_lvl = c.bit_length() - 1
    hd = RW_HEAD_DIM

    @pl.when(j == 0)
    def _():
        for p in range(RW_HEADS // 2):
            st_ref[:, p] = jnp.concatenate([s0_ref[:, 2 * p], s0_ref[:, 2 * p + 1]], axis=-1)
        carry_ref[...] = sh_ref[...]

    yt_ref[...] = jnp.zeros_like(yt_ref)

    def shifted(sb, c0, w):
        r0 = sb * m_rows
        pc3 = p_ref[r0:r0 + m_rows, c0:c0 + w].astype(F32).reshape(nb, tbs, w)
        if sb == 0:
            before = carry_ref[:, :, c0:c0 + w]
        else:
            before = p_ref[r0 - 1:r0, c0:c0 + w].astype(F32).reshape(1, 1, w)
        t3 = lax.broadcasted_iota(jnp.int32, pc3.shape, 1)
        prev3 = jnp.where(t3 == 0, before, pltpu.roll(pc3, 1, axis=1))
        return (pc3 + (prev3 - pc3) * mu_ref[:, c0:c0 + w]).reshape(m_rows, w)

    same, ri, ci = _chunk_masks(m_rows, c)
    lmask = jnp.where(same & (ci <= ri), 1.0, 0.0).astype(BF16)
    strict_t = same & (ri < ci)
    incl_t = same & (ri <= ci)

    sw = STRIP
    strips = [slice(c0, c0 + sw) for c0 in range(0, RW_WIDTH, sw)]

    def head_stat(x, cs):
        return _dot(x.astype(BF16), e1_ref[cs, :])

    def head_bcast(s, cs):
        return _mm(s, e2_ref[:, cs], pa=2)

    def head_sum(x):
        return _dot(x.astype(BF16), e12_ref[...])


    def prologue(sb):
        slab = shifted(sb, RW_OFF_XWA, LANES)
        lane = lax.broadcasted_iota(jnp.int32, slab.shape, 1)
        lhs = jnp.where(lane < RW_LORA_W, jnp.tanh(slab), slab).astype(BF16)
        sg = _sigmoid(shifted(sb, RW_OFF_XG, 2 * LANES)).astype(BF16)
        yield
        lws = [_dot(lhs, wwa_ref[:, cs]) for cs in strips]
        las = [_dot(lhs, wwa_ref[:, RW_WIDTH + cs.start:RW_WIDTH + cs.stop]) for cs in strips]
        for cs in strips:
            g_ref[sb, :, cs] = _dot(sg, g2_ref[:, cs])
        yield
        logws = [-DECAY_SCALE * _sigmoid(w0_ref[:, cs] + lw) for cs, lw in zip(strips, lws)]
        gcums = [_mm(lmask, logw, pb=2) for logw in logws]
        yield
        avs = [_sigmoid(a0_ref[:, cs] + la) for cs, la in zip(strips, las)]
        ks = [shifted(sb, RW_WIDTH + cs.start, sw) for cs in strips]
        kkfs = [k * kk_ref[:, cs] for cs, k in zip(strips, ks)]
        rinvs = [lax.rsqrt(jnp.maximum(head_stat(kkf * kkf, cs), 1e-24)) for cs, kkf in zip(strips, kkfs)]
        yield
        k2s = [k * (1.0 + (a - 1.0) * ka_ref[:, cs]) for cs, k, a in zip(strips, ks, avs)]
        rs = [shifted(sb, cs.start, sw) for cs in strips]
        bsums = [head_sum(r * k2 * rk_ref[:, cs]) for cs, r, k2 in zip(strips, rs, k2s)]
        kks = [kkf * head_bcast(rinv, cs) for cs, kkf, rinv in zip(strips, kkfs, rinvs)]
        yield
        for cs, bsum in zip(strips, bsums):
            v = shifted(sb, 2 * RW_WIDTH + cs.start, sw)
            bonus_ref[sb, :, cs] = bsum * v
            vt_ref[sb, cs, :] = v.T
        yield
        for cs, logw, gcum, a, kk, k2, r in zip(strips, logws, gcums, avs, kks, k2s, rs):
            beta = kk * a
            g3 = gcum.reshape(cps, c, sw)
            gtot = jnp.broadcast_to(g3[:, c - 1:c, :], g3.shape).reshape(m_rows, sw)
            e_inv = jnp.exp(-gcum)
            e_rev = jnp.exp(gtot - gcum)
            art_ref[sb, cs, 0:m_rows] = (-kk * jnp.exp(gcum - logw)).T
            art_ref[sb, cs, m_rows:2 * m_rows] = (r * jnp.exp(gcum)).T
            bk_ref[sb, 0:m_rows, cs] = beta * e_inv
            bk_ref[sb, m_rows:, cs] = k2 * e_inv
            kb_ref[sb, 0:m_rows, cs] = k2 * e_rev
            kb_ref[sb, m_rows:, cs] = beta * e_rev
            pc_ref[sb, :, cs] = jnp.exp(gtot)
            yield

    pw = 2 * hd
    cat = jnp.concatenate
    top = lax.broadcasted_iota(jnp.int32, (pw, 1), 0) < hd
    left = lax.broadcasted_iota(jnp.int32, (1, 2 * m_rows), 1) < m_rows

    def split_rows(x):
        return cat([jnp.where(top, x, 0.0), jnp.where(top, 0.0, x)], axis=1)

    def block_diag(xc):
        return cat([jnp.where(left, xc, 0.0), jnp.where(left, 0.0, xc)], axis=0)

    def blk(xt, u, rh, ch):
        col = (2 * u + ch) * m_rows
        return xt[rh * m_rows:(rh + 1) * m_rows, col:col + m_rows]

    n_pairs = RW_HEADS // 2
    pairs = range(n_pairs)
    los = [p * pw for p in pairs]

    def pair_phase(sb):
        for g0 in range(0, n_pairs, PAIR_LOCKSTEP):
            yield from pair_group(sb, pairs[g0:g0 + PAIR_LOCKSTEP], los[g0:g0 + PAIR_LOCKSTEP])

    def pair_group(sb, pairs, los):
        arts = [art_ref[sb, lo:lo + pw, :] for lo in los]
        xts = [_mm(bk_ref[sb, :, lo:lo + pw], split_rows(art), pp, pp) for lo, art in zip(los, arts)]
        yield
        npcs = [cat([jnp.where(strict_t, blk(xt, 0, 0, 0), 0.0), jnp.where(strict_t, blk(xt, 1, 0, 0), 0.0)], axis=1)
                for xt in xts]
        vas = [_mm(split_rows(vt_ref[sb, lo:lo + pw, :]),
                   cat([cat([jnp.where(strict_t, blk(xt, u, 1, 0), 0.0), jnp.where(incl_t, blk(xt, u, 1, 1), 0.0)],
                            axis=1) for u in range(2)], axis=0), pp, pp)
               for lo, xt in zip(los, xts)]
        yield
        zcs = [cat([cat([art[u * hd:(u + 1) * hd, 0:m_rows], va[u * hd:(u + 1) * hd, 0:m_rows]], axis=0)
                    for u in range(2)], axis=1) for art, va in zip(arts, vas)]
        for lvl in range(n_lvl):
            bds = [block_diag(npc) for npc in npcs]
            if lvl + 1 < n_lvl:
                ts = [_mm(cat([zc, npc], axis=0), bd, pp, pp) for zc, npc, bd in zip(zcs, npcs, bds)]
                zcs = [zc + t[0:2 * hd] for zc, t in zip(zcs, ts)]
                npcs = [t[2 * hd:] for t in ts]
            else:
                zcs = [zc + _mm(zc, bd, pp, pp) for zc, bd in zip(zcs, bds)]
            yield
        rycs = [cat([cat([art[u * hd:(u + 1) * hd, m_rows:], va[u * hd:(u + 1) * hd, m_rows:]], axis=0)
                     for u in range(2)], axis=1)
                + _mm(zc, block_diag(cat([jnp.where(incl_t, blk(xt, 0, 0, 1), 0.0),
                                          jnp.where(incl_t, blk(xt, 1, 0, 1), 0.0)], axis=1)), pp, pp)
                for art, va, zc, xt in zip(arts, vas, zcs, xts)]
        yield
        zero = jnp.zeros((hd, 2 * m_rows), BF16)
        for p, zc, ry in zip(pairs, zcs, rycs):
            arbd_ref[sb, p, 0:hd, 0:2 * m_rows] = cat([zc[0:hd, 0:m_rows], ry[0:hd, 0:m_rows]], axis=1).astype(BF16)
            arbd_ref[sb, p, 0:hd, 2 * m_rows:] = zero
            arbd_ref[sb, p, hd:, 0:2 * m_rows] = zero
            arbd_ref[sb, p, hd:, 2 * m_rows:] = cat([zc[0:hd, m_rows:], ry[0:hd, m_rows:]], axis=1).astype(BF16)
            uy_ref[sb, p] = cat([zc[hd:, 0:m_rows], ry[hd:, 0:m_rows], zc[hd:, m_rows:], ry[hd:, m_rows:]], axis=1)
        yield

    sh_c = c.bit_length() - 1
    cps = m_rows // c
    cpb = tbs // c
    row_id2 = (lax.broadcasted_iota(jnp.int32, (2 * m_rows, pw), 0) & (m_rows - 1)) >> sh_c
    col_id = lax.broadcasted_iota(jnp.int32, (hd, m_rows), 1) >> sh_c
    lane_lo = lax.broadcasted_iota(jnp.int32, (1, pw), 1) < hd

    def chunk_step(sb, i):
        b = i // cpb
        r0 = i * c
        rmask2 = row_id2 == i
        cmask = col_id == i
        ss = [st_ref[b, p] for p in pairs]
        zss = [_mm(s, arbd_ref[sb, p], pp, pp) + uy_ref[sb, p] for p, s in zip(pairs, ss)]
        yield
        upd = []
        for lo, zs in zip(los, zss):
            kbz = jnp.where(rmask2, kb_ref[sb, :, lo:lo + pw], 0.0)
            rhs = cat([jnp.where(lane_lo, kbz, 0.0), jnp.where(lane_lo, 0.0, kbz)], axis=0)
            vu = cat([vt_ref[sb, lo:lo + hd, :], zs[:, 0:m_rows],
                      vt_ref[sb, lo + hd:lo + pw, :], zs[:, 2 * m_rows:3 * m_rows]], axis=1)
            upd.append(_mm(vu, rhs, pp, pp))
        yield
        for p, lo, s, zs, d in zip(pairs, los, ss, zss, upd):
            st_ref[b, p] = s * pc_ref[sb, pl.ds(r0, 1), lo:lo + pw] + d
            for u in range(2):
                rows = slice((2 * p + u) * hd, (2 * p + u + 1) * hd)
                yt_ref[sb, rows, :] = jnp.where(cmask, zs[:, (2 * u + 1) * m_rows:(2 * u + 2) * m_rows],
                                               yt_ref[sb, rows, :])
        yield

    inv_n = 1.0 / hd

    def epilogue(sb):
        ys = [yt_ref[sb, cs, :].T for cs in strips]
        ycs = [y - head_sum(y) * inv_n for y in ys]
        yield
        rstds = [lax.rsqrt(head_stat(yc * yc, cs) * inv_n + RW_GN_EPS) for cs, yc in zip(strips, ycs)]
        yield
        for cs, yc, rstd in zip(strips, ycs, rstds):
            yn = yc * head_bcast(rstd, cs) * lnw_ref[:, cs] + lnb_ref[:, cs]
            o_ref[sb * m_rows:(sb + 1) * m_rows, cs] = (
                (yn + bonus_ref[sb, :, cs]) * g_ref[sb, :, cs]).astype(o_ref.dtype)
        yield

    def run(*gens):
        live = list(gens)
        while live:
            live = [g for g in live if next(g, live) is not live]

    def tail(sb):
        for i in range(cps):
            yield from chunk_step(sb, i)
        yield from epilogue(sb)

    if nsub == 1:
        run(prologue(0))
        run(pair_phase(0))

        together = SAMPLE_CHUNKS_TOGETHER if cpb == 1 and cps % SAMPLE_CHUNKS_TOGETHER == 0 else 1

        def chunk_body(i, carry):
            run(*[chunk_step(0, i * together + u) for u in range(together)])
            return carry

        lax.fori_loop(0, cps // together, chunk_body, 0)
        run(epilogue(0))
    else:
        run(prologue(0))
        for sb in range(nsub):
            run(pair_phase(sb), *([prologue(sb + 1)] if sb + 1 < nsub else []), *([tail(sb - 1)] if sb else []))
        run(tail(nsub - 1))

    if nb == 1:
        last = p_ref[tb - 1:tb, :].astype(F32).reshape(1, 1, GROUP_PAD)
    else:
        last = p_ref[...].astype(F32).reshape(nb, tb, GROUP_PAD)[:, tb - 1:tb, :]
    carry_ref[...] = last
    shout_ref[...] = last

    @pl.when(j == pl.num_programs(1) - 1)
    def _():
        for p in pairs:
            sp = st_ref[:, p]
            sout_ref[:, 2 * p] = sp[:, :, 0:hd]
            sout_ref[:, 2 * p + 1] = sp[:, :, hd:]


def _rwkv(p, shift_prev, s0, wts, nb, tb, c, pp):
    bsz = s0.shape[0]
    seq = p.shape[0] // bsz
    m_rows = ROWS
    nsub = nb * tb // m_rows
    assert nsub * m_rows == nb * tb and (nsub == 1 or nb == 1)
    hd = RW_HEAD_DIM
    const = lambda shape: pl.BlockSpec(shape, lambda i, j: (0,) * len(shape))
    kern = functools.partial(_rwkv_kernel, nb=nb, tb=tb, c=c, pp=pp)
    return pl.pallas_call(
        kern,
        grid=(bsz // nb, seq // tb),
        in_specs=[
            pl.BlockSpec((nb * tb, GROUP_PAD), lambda i, j: (i * (seq // tb) + j, 0)),
            pl.BlockSpec((nb, 1, GROUP_PAD), lambda i, j: (i, 0, 0)),
            pl.BlockSpec((nb, RW_HEADS, hd, hd), lambda i, j: (i, 0, 0, 0)),
            const((1, GROUP_PAD)),
            const((LANES, 2 * RW_WIDTH)),
            const((1, RW_WIDTH)),
            const((1, RW_WIDTH)),
            const((2 * LANES, RW_WIDTH)),
            const((1, RW_WIDTH)),
            const((1, RW_WIDTH)),
            const((1, RW_WIDTH)),
            const((1, RW_WIDTH)),
            const((1, RW_WIDTH)),
            const((RW_WIDTH, LANES)),
            const((LANES, RW_WIDTH)),
            const((STRIP, STRIP)),
        ],
        out_specs=[
            pl.BlockSpec((nb * tb, RW_WIDTH), lambda i, j: (i * (seq // tb) + j, 0)),
            pl.BlockSpec((nb, 1, GROUP_PAD), lambda i, j: (i, 0, 0)),
            pl.BlockSpec((nb, RW_HEADS, hd, hd), lambda i, j: (i, 0, 0, 0)),
        ],
        out_shape=[
            jax.ShapeDtypeStruct((bsz * seq, RW_WIDTH), BF16),
            jax.ShapeDtypeStruct((bsz, 1, GROUP_PAD), F32),
            jax.ShapeDtypeStruct((bsz, RW_HEADS, hd, hd), F32),
        ],
        scratch_shapes=[
            pltpu.VMEM((nb, 1, GROUP_PAD), F32),
            pltpu.VMEM((nb, RW_HEADS // 2, hd, 2 * hd), F32),
            pltpu.VMEM((nsub, 2 * m_rows, RW_WIDTH), F32),
            pltpu.VMEM((nsub, 2 * m_rows, RW_WIDTH), F32),
            pltpu.VMEM((nsub, m_rows, RW_WIDTH), F32),
            pltpu.VMEM((nsub, RW_WIDTH, 2 * m_rows), F32),
            pltpu.VMEM((nsub, RW_WIDTH, m_rows), F32),
            pltpu.VMEM((nsub, RW_HEADS // 2, 2 * hd, 4 * m_rows), BF16),
            pltpu.VMEM((nsub, RW_HEADS // 2, hd, 4 * m_rows), F32),
            pltpu.VMEM((nsub, RW_WIDTH, m_rows), F32),
            pltpu.VMEM((nsub, m_rows, RW_WIDTH), F32),
            pltpu.VMEM((nsub, m_rows, RW_WIDTH), F32),
        ],
        compiler_params=pltpu.CompilerParams(
            dimension_semantics=("parallel", "arbitrary"), vmem_limit_bytes=VMEM_LIMIT),
        name="rwkv7",
    )(p, shift_prev, s0, *wts)


def _gla_kernel(p_ref, s0_ref, gw2_ref, gb_ref, nw_ref, o_ref, st_ref,
                acc_ref, qd_ref, v_ref, kt_ref, et_ref, *, nb, tb, c, pp):
    j = pl.program_id(1)
    m_rows = nb * tb
    dk, dv = GLA_DK, GLA_DV

    @pl.when(j == 0)
    def _():
        st_ref[...] = s0_ref[...]

    p = p_ref[...].astype(F32)
    q = p[:, 0:GLA_KEY_WIDTH] * (dk ** -0.5)
    k = p[:, GLA_OFF_K:GLA_OFF_K + GLA_KEY_WIDTH]
    v = p[:, GLA_OFF_V:GLA_OFF_V + GLA_WIDTH]
    xs = p[:, GLA_OFF_GATE:GLA_OFF_GATE + LANES].astype(BF16)
    gout = p[:, GLA_OFF_GOUT:GLA_OFF_GOUT + GLA_WIDTH]
    gk = -_softplus(-(_dot(xs, gw2_ref[...]) + gb_ref[...])) / GLA_GATE_NORM

    same, ri, ci = _chunk_masks(m_rows, c)
    lmask = jnp.where(same & (ci <= ri), 1.0, 0.0).astype(BF16)
    causal = same & (ci <= ri)
    gcum = _mm(lmask, gk, pb=3)
    g3 = gcum.reshape(m_rows // c, c, GLA_KEY_WIDTH)
    gtot = jnp.broadcast_to(g3[:, c - 1:c, :], g3.shape).reshape(m_rows, GLA_KEY_WIDTH)
    qd = q * jnp.exp(gcum)
    kinv = k * jnp.exp(-gcum)
    qd_ref[...] = qd
    v_ref[...] = v
    kt_ref[...] = (k * jnp.exp(gtot - gcum)).T
    et_ref[...] = jnp.exp(gtot).T

    heads = range(GLA_HEADS)
    kq = [slice(h * dk, (h + 1) * dk) for h in heads]
    vq = [slice(h * dv, (h + 1) * dv) for h in heads]
    scores = [jnp.where(causal, _mm(qd[:, ks], kinv[:, ks], pp, pp, NT), 0.0) for ks in kq]
    intra = [_mm(a, v[:, vs], pp, pp) for a, vs in zip(scores, vq)]
    for vs, o in zip(vq, intra):
        acc_ref[:, vs] = o

    sh_c = c.bit_length() - 1
    cpb = tb // c
    span = min(m_rows, LANES)
    row_id = lax.broadcasted_iota(jnp.int32, (span, dv), 0) >> sh_c
    for i in range(m_rows // c):
        b = i // cpb
        r0 = i * c
        t0 = (r0 // span) * span
        rmask = row_id == (r0 - t0) // c
        states = [st_ref[b, h] for h in heads]
        inter = [_mm(qd_ref[r0:r0 + c, ks], s, pp, pp) for ks, s in zip(kq, states)]
        upd = [_mm(kt_ref[ks, t0:t0 + span], jnp.where(rmask, v_ref[t0:t0 + span, vs], 0.0), pp, pp)
               for ks, vs in zip(kq, vq)]
        for h, ks, vs, s, oi, d in zip(heads, kq, vq, states, inter, upd):
            acc_ref[r0:r0 + c, vs] += oi
            st_ref[b, h] = s * et_ref[ks, r0:r0 + 1] + d

    for vs in vq:
        o = acc_ref[:, vs]
        on = o * lax.rsqrt(jnp.mean(o * o, axis=-1, keepdims=True) + HEAD_NORM_EPS) * nw_ref[...]
        gh = gout[:, vs]
        o_ref[:, vs] = (on * (gh * _sigmoid(gh))).astype(o_ref.dtype)


def _gla(p, s0, wts, nb, tb, c, pp):
    bsz = s0.shape[0]
    seq = p.shape[0] // bsz
    m_rows = nb * tb
    const = lambda shape: pl.BlockSpec(shape, lambda i, j: (0,) * len(shape))
    kern = functools.partial(_gla_kernel, nb=nb, tb=tb, c=c, pp=pp)
    return pl.pallas_call(
        kern,
        grid=(bsz // nb, seq // tb),
        in_specs=[
            pl.BlockSpec((nb * tb, GROUP_PAD), lambda i, j: (i * (seq // tb) + j, 1)),
            pl.BlockSpec((nb, GLA_HEADS, GLA_DK, GLA_DV), lambda i, j: (i, 0, 0, 0)),
            const((LANES, GLA_KEY_WIDTH)),
            const((1, GLA_KEY_WIDTH)),
            const((1, GLA_DV)),
        ],
        out_specs=[
            pl.BlockSpec((nb * tb, GLA_WIDTH), lambda i, j: (i * (seq // tb) + j, 0)),
            pl.BlockSpec((nb, GLA_HEADS, GLA_DK, GLA_DV), lambda i, j: (i, 0, 0, 0)),
        ],
        out_shape=[
            jax.ShapeDtypeStruct((bsz * seq, GLA_WIDTH), BF16),
            jax.ShapeDtypeStruct((bsz, GLA_HEADS, GLA_DK, GLA_DV), F32),
        ],
        scratch_shapes=[
            pltpu.VMEM((m_rows, GLA_WIDTH), F32),
            pltpu.VMEM((m_rows, GLA_KEY_WIDTH), F32),
            pltpu.VMEM((m_rows, GLA_WIDTH), F32),
            pltpu.VMEM((GLA_KEY_WIDTH, m_rows), F32),
            pltpu.VMEM((GLA_KEY_WIDTH, m_rows), F32),
        ],
        compiler_params=pltpu.CompilerParams(
            dimension_semantics=("parallel", "arbitrary"), vmem_limit_bytes=VMEM_LIMIT),
        name="gla",
    )(p, s0, *wts)


def _post_kernel(x_ref, orw_ref, ogla_ref, wo_ref, g2_ref, wu_ref, wd_ref, gf_ref, o_ref, h_ref):
    jf = pl.program_id(1)

    @pl.when(jf == 0)
    def _():
        x1 = (x_ref[...]
              + _dot(orw_ref[...], wo_ref[0:RW_WIDTH, :])
              + _dot(ogla_ref[...], wo_ref[RW_WIDTH:, :]))
        o_ref[...] = x1
        ms = jnp.mean(x1 * x1, axis=-1, keepdims=True)
        h_ref[...] = (x1 * lax.rsqrt(ms + NORM_EPS) * g2_ref[...]).astype(BF16)

    u = jnp.maximum(_dot(h_ref[...], wu_ref[...]), 0.0)
    o_ref[...] += _dot((u * u).astype(BF16), wd_ref[...])

    @pl.when(jf == pl.num_programs(1) - 1)
    def _():
        x2 = o_ref[...]
        ms = jnp.mean(x2 * x2, axis=-1, keepdims=True)
        o_ref[...] = x2 * lax.rsqrt(ms + NORM_EPS) * gf_ref[...]


def _post(x2d, o_rw, o_gla, w_out, g2, w_up, w_down, gf, tm, tf):
    t = x2d.shape[0]
    return pl.pallas_call(
        _post_kernel,
        grid=(t // tm, D_FF // tf),
        in_specs=[
            pl.BlockSpec((tm, D_MODEL), lambda i, j: (i, 0)),
            pl.BlockSpec((tm, RW_WIDTH), lambda i, j: (i, 0)),
            pl.BlockSpec((tm, GLA_WIDTH), lambda i, j: (i, 0)),
            pl.BlockSpec((D_MODEL, D_MODEL), lambda i, j: (0, 0), pipeline_mode=pl.Buffered(1)),
            pl.BlockSpec((1, D_MODEL), lambda i, j: (0, 0)),
            pl.BlockSpec((D_MODEL, tf), lambda i, j: (0, j)),
            pl.BlockSpec((tf, D_MODEL), lambda i, j: (j, 0)),
            pl.BlockSpec((1, D_MODEL), lambda i, j: (0, 0)),
        ],
        out_specs=pl.BlockSpec((tm, D_MODEL), lambda i, j: (i, 0)),
        out_shape=jax.ShapeDtypeStruct((t, D_MODEL), F32),
        scratch_shapes=[pltpu.VMEM((tm, D_MODEL), BF16)],
        compiler_params=pltpu.CompilerParams(
            dimension_semantics=("parallel", "arbitrary"), vmem_limit_bytes=VMEM_LIMIT),
        name="post",
    )(x2d, o_rw, o_gla, w_out, g2, w_up, w_down, gf)


def _pad_cols(w, n):
    return jnp.pad(w, ((0, 0), (0, n - w.shape[1])))


def _prep_weights(w_in, rw_mu, rw_w0, rw_w2, rw_a0, rw_a2, rw_g2, rw_k_k, rw_k_a, rw_r_k, rw_ln_w, rw_ln_b,
                  gla_gw2, gla_gb, gla_norm_w):
    w_p = jnp.swapaxes(w_in, 0, 1).astype(BF16)

    row = lambda x: x.reshape(1, -1).astype(F32)
    mu = _pad_cols(row(rw_mu), GROUP_PAD)
    wwa = jnp.zeros((LANES, 2 * RW_WIDTH), F32)
    wwa = wwa.at[0:RW_LORA_W, 0:RW_WIDTH].set(rw_w2)
    wwa = wwa.at[RW_LORA_W:RW_LORA_W + RW_LORA_A, RW_WIDTH:].set(rw_a2)
    g2p = jnp.pad(rw_g2, ((0, 2 * LANES - RW_LORA_G), (0, 0)))
    head_of_col = jnp.arange(RW_WIDTH) // RW_HEAD_DIM
    e1 = (head_of_col[:, None] == jnp.arange(LANES)[None, :]).astype(BF16)
    rw_wts = (mu, wwa.astype(BF16), row(rw_w0), row(rw_a0), g2p.astype(BF16), row(rw_k_k), row(rw_k_a),
              row(rw_r_k), row(rw_ln_w), row(rw_ln_b), e1, e1.T,
              (head_of_col[:STRIP, None] == head_of_col[None, :STRIP]).astype(BF16))
    gw2p = jnp.pad(gla_gw2, ((0, LANES - GLA_GATE_RANK), (0, 0))).astype(BF16)
    gla_wts = (gw2p, row(gla_gb), row(gla_norm_w))
    return w_p, rw_wts, gla_wts


RW_PIECES = 1
GLA_PIECES = 1
PAIR_LOCKSTEP = 8
SAMPLE_CHUNKS_TOGETHER = 4
RW_STEP_ROWS = 2 * ROWS
STRIP = 256


def _trunk(x, shift, s_rw, s_gla, norm1_g, w_p, rw_wts, gla_wts, w_out, norm2_g, w_up, w_down, norm_f_g,
           nb, tb, c_rw, c_gla, tm):
    bsz, seq, _ = x.shape
    t = bsz * seq
    x2d = x.reshape(t, D_MODEL)
    proj = _inproj(x2d, norm1_g.reshape(1, -1), w_p, tm, INPROJ_TN)
    shift_p = _pad_cols(shift, GROUP_PAD).reshape(bsz, 1, GROUP_PAD)
    rw_tb = RW_STEP_ROWS if nb == 1 and seq % RW_STEP_ROWS == 0 else tb
    o_rw, sh_new, s_rw_new = _rwkv(proj, shift_p, s_rw, rw_wts, nb, rw_tb, c_rw, RW_PIECES)
    o_gla, s_gla_new = _gla(proj, s_gla, gla_wts, nb, rw_tb, c_gla, GLA_PIECES)
    y = _post(x2d, o_rw, o_gla, w_out, norm2_g.reshape(1, -1),
              w_up, w_down, norm_f_g.reshape(1, -1), POST_TM, POST_TF)
    return (y.reshape(bsz, seq, D_MODEL), sh_new[:, 0, :RW_PROJ][None], s_rw_new[None], s_gla_new[None])


def kernel(x_prompt, x_sample, state_rwkv_shift, state_rwkv_wkv, state_gla, norm1_g, w_in, rw_mu, rw_w0,
           rw_w2, rw_a0, rw_a2, rw_g2, rw_k_k, rw_k_a, rw_r_k, rw_ln_w, rw_ln_b, gla_gw2, gla_gb, gla_norm_w,
           w_out, norm2_g, w_up, w_down, norm_f_g):
    w_p, rw_wts, gla_wts = _prep_weights(
        w_in[0], rw_mu[0], rw_w0[0], rw_w2[0], rw_a0[0], rw_a2[0], rw_g2[0], rw_k_k[0], rw_k_a[0],
        rw_r_k[0].reshape(-1), rw_ln_w[0], rw_ln_b[0], gla_gw2[0], gla_gb[0], gla_norm_w[0])
    shared = (norm1_g[0], w_p, rw_wts, gla_wts, w_out[0].astype(BF16), norm2_g[0], w_up[0].astype(BF16),
              w_down[0].astype(BF16), norm_f_g)

    bp, lp, _ = x_prompt.shape
    bs, ls, _ = x_sample.shape
    dt = x_prompt.dtype
    out_p = _trunk(x_prompt, jnp.zeros((bp, RW_PROJ), dt),
                   jnp.zeros((bp, RW_HEADS, RW_HEAD_DIM, RW_HEAD_DIM), dt),
                   jnp.zeros((bp, GLA_HEADS, GLA_DK, GLA_DV), dt), *shared,
                   nb=1, tb=ROWS, c_rw=64, c_gla=GLA_CHUNK, tm=1024)
    out_s = _trunk(x_sample, state_rwkv_shift[0], state_rwkv_wkv[0], state_gla[0], *shared,
                   nb=ROWS // ls, tb=ls, c_rw=ls, c_gla=ls, tm=1024)
    return (out_p[0], out_s[0], out_p[1], out_p[2], out_p[3], out_s[1], out_s[2], out_s[3])
```

```python
import functools

import jax
import jax.numpy as jnp
from jax import lax
from jax.experimental import pallas as pl
from jax.experimental.pallas import tpu as pltpu

F32 = jnp.float32
BF16 = jnp.bfloat16

D_MODEL = 2048
RW_WIDTH = 1024
RW_HEADS = 16
RW_HEAD_DIM = 64
RW_LORA_W = 64
RW_LORA_A = 64
RW_LORA_G = 160
RW_PROJ = 3 * RW_WIDTH + RW_LORA_W + RW_LORA_A + RW_LORA_G
RW_GN_EPS = 64e-5
GLA_WIDTH = 1024
GLA_HEADS = 4
GLA_KEY_WIDTH = 512
GLA_DK = 128
GLA_DV = 256
GLA_GATE_RANK = 16
GLA_GATE_NORM = 16.0
GLA_CHUNK = 64
GLA_PROJ = 2 * GLA_KEY_WIDTH + GLA_WIDTH + GLA_GATE_RANK + GLA_WIDTH
D_FF = 4 * D_MODEL
NORM_EPS = 1e-6
HEAD_NORM_EPS = 1e-5
DECAY_SCALE = 0.6065306597126334

LANES = 128

GROUP_PAD = 3456
RW_OFF_XWA = 3 * RW_WIDTH
RW_OFF_XG = RW_OFF_XWA + LANES
GLA_OFF_K = GLA_KEY_WIDTH
GLA_OFF_V = 2 * GLA_KEY_WIDTH
GLA_OFF_GATE = GLA_OFF_V + GLA_WIDTH
GLA_OFF_GOUT = GLA_OFF_GATE + LANES
NP = 2 * GROUP_PAD

ROWS = 128
MXU_WIDTH = 256
INPROJ_TM = 1024
INPROJ_TN = 9 * MXU_WIDTH
RW_CHUNK = 64
PROJ_DTYPE = BF16
POST_TM = 512
POST_TF = 1024
VMEM_LIMIT = 56 * 1024 * 1024

NN = (((1,), (0,)), ((), ()))
NT = (((1,), (1,)), ((), ()))


def _dot(a, b, dims=NN):
    return lax.dot_general(a, b, dims, preferred_element_type=F32)


def _parts(x, n):
    if x.dtype == BF16:
        return [x]
    out = []
    rem = x
    for i in range(n):
        h = rem.astype(BF16)
        out.append(h)
        if i + 1 < n:
            rem = rem - h.astype(F32)
    return out


def _mm(a, b, pa=1, pb=1, dims=NN):
    pa = 1 if a.dtype == BF16 else pa
    pb = 1 if b.dtype == BF16 else pb
    aa = _parts(a, pa)
    bb = _parts(b, pb)
    n = max(pa, pb)
    if dims == NN and min(pa, pb) == 1 and n * a.shape[1] <= MXU_WIDTH:
        return _dot(jnp.concatenate(aa * (n // pa), axis=1), jnp.concatenate(bb * (n // pb), axis=0))
    acc = None
    for i in range(pa):
        for j in range(pb):
            if i + j < n:
                t = _dot(aa[i], bb[j], dims)
                acc = t if acc is None else acc + t
    return acc


def _softplus(x):
    return jnp.maximum(x, 0.0) + jnp.log(1.0 + jnp.exp(-jnp.abs(x)))


def _sigmoid(x):
    return 0.5 * jnp.tanh(0.5 * x) + 0.5


def _chunk_masks(m, c):
    sh = c.bit_length() - 1
    ri = lax.broadcasted_iota(jnp.int32, (m, m), 0)
    ci = lax.broadcasted_iota(jnp.int32, (m, m), 1)
    same = (ri >> sh) == (ci >> sh)
    return same, ri, ci


_W_SEGMENTS = (
    (0, RW_PROJ, 0),
    (RW_PROJ, GLA_OFF_GATE + GLA_GATE_RANK, GROUP_PAD),
    (RW_PROJ + GLA_OFF_GATE + GLA_GATE_RANK, GLA_WIDTH, GROUP_PAD + GLA_OFF_GOUT),
)


def _weight_windows(tn):
    blocks = []
    for lo in range(0, NP, tn):
        wins = []
        for src, ln, dst in _W_SEGMENTS:
            a, b = max(dst, lo), min(dst + ln, lo + tn)
            if a < b:
                wins.append((src + a - dst, b - a, a - lo))
        blocks.append(tuple(wins))
    return tuple(blocks)


def _inproj_kernel(x_ref, g_ref, w_hbm, o_ref, h_ref, wbuf, sem, *, windows):
    i, j = pl.program_id(0), pl.program_id(1)
    nj = len(windows)
    step = i * nj + j
    n_steps = pl.num_programs(0) * nj
    slot = lax.rem(step, 2)

    def copies(jb, sl):
        return [pltpu.make_async_copy(w_hbm.at[pl.ds(src, ln), :], wbuf.at[sl, pl.ds(dst, ln), :], sem.at[sl, k])
                for k, (src, ln, dst) in enumerate(windows[jb])]

    @pl.when(step == 0)
    def _():
        for sl in range(min(2, nj)):
            covered = sorted((dst, dst + ln) for _, ln, dst in windows[sl])
            edges = [0] + [e for span in covered for e in span] + [wbuf.shape[1]]
            for a, b in zip(edges[0::2], edges[1::2]):
                if a < b:
                    wbuf[sl, a:b, :] = jnp.zeros((b - a, wbuf.shape[2]), wbuf.dtype)
        for cp in copies(0, 0):
            cp.start()

    for jb in range(nj):
        @pl.when(j == jb)
        def _(jb=jb):
            @pl.when(step + 1 < n_steps)
            def _():
                for cp in copies((jb + 1) % nj, 1 - slot):
                    cp.start()

            for cp in copies(jb, slot):
                cp.wait()

    @pl.when(j == 0)
    def _():
        x = x_ref[...]
        ms = jnp.mean(x * x, axis=-1, keepdims=True)
        h_ref[...] = (x * lax.rsqrt(ms + NORM_EPS) * g_ref[...]).astype(BF16)

    o_ref[...] = _dot(h_ref[...], wbuf[slot], NT).astype(o_ref.dtype)


def _inproj(x2d, g, w_t, tm, tn):
    t = x2d.shape[0]
    windows = _weight_windows(tn)
    return pl.pallas_call(
        functools.partial(_inproj_kernel, windows=windows),
        grid=(t // tm, NP // tn),
        in_specs=[
            pl.BlockSpec((tm, D_MODEL), lambda i, j: (i, 0)),
            pl.BlockSpec((1, D_MODEL), lambda i, j: (0, 0)),
            pl.BlockSpec(memory_space=pltpu.HBM),
        ],
        out_specs=pl.BlockSpec((tm, tn), lambda i, j: (i, j)),
        out_shape=jax.ShapeDtypeStruct((t, NP), PROJ_DTYPE),
        scratch_shapes=[
            pltpu.VMEM((tm, D_MODEL), BF16),
            pltpu.VMEM((2, tn, D_MODEL), BF16),
            pltpu.SemaphoreType.DMA((2, max(len(w) for w in windows))),
        ],
        compiler_params=pltpu.CompilerParams(
            dimension_semantics=("arbitrary", "arbitrary"), vmem_limit_bytes=VMEM_LIMIT),
        name="inproj",
    )(x2d, g, w_t)


def _rwkv_kernel(p_ref, sh_ref, s0_ref, mu_ref, wwa_ref, w0_ref, a0_ref, g2_ref, kk_ref, ka_ref,
                 rk_ref, lnw_ref, lnb_ref, e1_ref, e2_ref, e12_ref,
                 o_ref, shout_ref, sout_ref,
                 carry_ref, st_ref, bk_ref, kb_ref, pc_ref, art_ref, vt_ref, arbd_ref, uy_ref, yt_ref,
                 g_ref, bonus_ref, *, nb, tb, c, pp):
    j = pl.program_id(1)
    m_rows = ROWS
    nsub = nb * tb // m_rows
    tbs = tb // nsub
    n_lvl = c.bit_length() - 1
    hd = RW_HEAD_DIM

    @pl.when(j == 0)
    def _():
        for p in range(RW_HEADS // 2):
            st_ref[:, p] = jnp.concatenate([s0_ref[:, 2 * p], s0_ref[:, 2 * p + 1]], axis=-1)
        carry_ref[...] = sh_ref[...]

    yt_ref[...] = jnp.zeros_like(yt_ref)

    def shifted(sb, c0, w):
        r0 = sb * m_rows
        pc3 = p_ref[r0:r0 + m_rows, c0:c0 + w].astype(F32).reshape(nb, tbs, w)
        if sb == 0:
            before = carry_ref[:, :, c0:c0 + w]
        else:
            before = p_ref[r0 - 1:r0, c0:c0 + w].astype(F32).reshape(1, 1, w)
        t3 = lax.broadcasted_iota(jnp.int32, pc3.shape, 1)
        prev3 = jnp.where(t3 == 0, before, pltpu.roll(pc3, 1, axis=1))
        return (pc3 + (prev3 - pc3) * mu_ref[:, c0:c0 + w]).reshape(m_rows, w)

    same, ri, ci = _chunk_masks(m_rows, c)
    lmask = jnp.where(same & (ci <= ri), 1.0, 0.0).astype(BF16)
    strict_t = same & (ri < ci)
    incl_t = same & (ri <= ci)

    sw = STRIP
    strips = [slice(c0, c0 + sw) for c0 in range(0, RW_WIDTH, sw)]

    def head_stat(x, cs):
        return _dot(x.astype(BF16), e1_ref[cs, :])

    def head_bcast(s, cs):
        return _mm(s, e2_ref[:, cs], pa=2)

    def head_sum(x):
        return _dot(x.astype(BF16), e12_ref[...])


    def prologue(sb):
        slab = shifted(sb, RW_OFF_XWA, LANES)
        lane = lax.broadcasted_iota(jnp.int32, slab.shape, 1)
        lhs = jnp.where(lane < RW_LORA_W, jnp.tanh(slab), slab).astype(BF16)
        sg = _sigmoid(shifted(sb, RW_OFF_XG, 2 * LANES)).astype(BF16)
        yield
        lws = [_dot(lhs, wwa_ref[:, cs]) for cs in strips]
        las = [_dot(lhs, wwa_ref[:, RW_WIDTH + cs.start:RW_WIDTH + cs.stop]) for cs in strips]
        for cs in strips:
            g_ref[sb, :, cs] = _dot(sg, g2_ref[:, cs])
        yield
        logws = [-DECAY_SCALE * _sigmoid(w0_ref[:, cs] + lw) for cs, lw in zip(strips, lws)]
        gcums = [_mm(lmask, logw, pb=2) for logw in logws]
        yield
        avs = [_sigmoid(a0_ref[:, cs] + la) for cs, la in zip(strips, las)]
        ks = [shifted(sb, RW_WIDTH + cs.start, sw) for cs in strips]
        kkfs = [k * kk_ref[:, cs] for cs, k in zip(strips, ks)]
        rinvs = [lax.rsqrt(jnp.maximum(head_stat(kkf * kkf, cs), 1e-24)) for cs, kkf in zip(strips, kkfs)]
        yield
        k2s = [k * (1.0 + (a - 1.0) * ka_ref[:, cs]) for cs, k, a in zip(strips, ks, avs)]
        rs = [shifted(sb, cs.start, sw) for cs in strips]
        bsums = [head_sum(r * k2 * rk_ref[:, cs]) for cs, r, k2 in zip(strips, rs, k2s)]
        kks = [kkf * head_bcast(rinv, cs) for cs, kkf, rinv in zip(strips, kkfs, rinvs)]
        yield
        for cs, bsum in zip(strips, bsums):
            v = shifted(sb, 2 * RW_WIDTH + cs.start, sw)
            bonus_ref[sb, :, cs] = bsum * v
            vt_ref[sb, cs, :] = v.T
        yield
        for cs, logw, gcum, a, kk, k2, r in zip(strips, logws, gcums, avs, kks, k2s, rs):
            beta = kk * a
            g3 = gcum.reshape(cps, c, sw)
            gtot = jnp.broadcast_to(g3[:, c - 1:c, :], g3.shape).reshape(m_rows, sw)
            e_inv = jnp.exp(-gcum)
            e_rev = jnp.exp(gtot - gcum)
            art_ref[sb, cs, 0:m_rows] = (-kk * jnp.exp(gcum - logw)).T
            art_ref[sb, cs, m_rows:2 * m_rows] = (r * jnp.exp(gcum)).T
            bk_ref[sb, 0:m_rows, cs] = beta * e_inv
            bk_ref[sb, m_rows:, cs] = k2 * e_inv
            kb_ref[sb, 0:m_rows, cs] = k2 * e_rev
            kb_ref[sb, m_rows:, cs] = beta * e_rev
            pc_ref[sb, :, cs] = jnp.exp(gtot)
            yield

    pw = 2 * hd
    cat = jnp.concatenate
    top = lax.broadcasted_iota(jnp.int32, (pw, 1), 0) < hd
    left = lax.broadcasted_iota(jnp.int32, (1, 2 * m_rows), 1) < m_rows

    def split_rows(x):
        return cat([jnp.where(top, x, 0.0), jnp.where(top, 0.0, x)], axis=1)

    def block_diag(xc):
        return cat([jnp.where(left, xc, 0.0), jnp.where(left, 0.0, xc)], axis=0)

    def blk(xt, u, rh, ch):
        col = (2 * u + ch) * m_rows
        return xt[rh * m_rows:(rh + 1) * m_rows, col:col + m_rows]

    n_pairs = RW_HEADS // 2
    pairs = range(n_pairs)
    los = [p * pw for p in pairs]

    def pair_phase(sb):
        for g0 in range(0, n_pairs, PAIR_LOCKSTEP):
            yield from pair_group(sb, pairs[g0:g0 + PAIR_LOCKSTEP], los[g0:g0 + PAIR_LOCKSTEP])

    def pair_group(sb, pairs, los):
        arts = [art_ref[sb, lo:lo + pw, :] for lo in los]
        xts = [_mm(bk_ref[sb, :, lo:lo + pw], split_rows(art), pp, pp) for lo, art in zip(los, arts)]
        yield
        npcs = [cat([jnp.where(strict_t, blk(xt, 0, 0, 0), 0.0), jnp.where(strict_t, blk(xt, 1, 0, 0), 0.0)], axis=1)
                for xt in xts]
        vas = [_mm(split_rows(vt_ref[sb, lo:lo + pw, :]),
                   cat([cat([jnp.where(strict_t, blk(xt, u, 1, 0), 0.0), jnp.where(incl_t, blk(xt, u, 1, 1), 0.0)],
                            axis=1) for u in range(2)], axis=0), pp, pp)
               for lo, xt in zip(los, xts)]
        yield
        zcs = [cat([cat([art[u * hd:(u + 1) * hd, 0:m_rows], va[u * hd:(u + 1) * hd, 0:m_rows]], axis=0)
                    for u in range(2)], axis=1) for art, va in zip(arts, vas)]
        for lvl in range(n_lvl):
            bds = [block_diag(npc) for npc in npcs]
            if lvl + 1 < n_lvl:
                ts = [_mm(cat([zc, npc], axis=0), bd, pp, pp) for zc, npc, bd in zip(zcs, npcs, bds)]
                zcs = [zc + t[0:2 * hd] for zc, t in zip(zcs, ts)]
                npcs = [t[2 * hd:] for t in ts]
            else:
                zcs = [zc + _mm(zc, bd, pp, pp) for zc, bd in zip(zcs, bds)]
            yield
        rycs = [cat([cat([art[u * hd:(u + 1) * hd, m_rows:], va[u * hd:(u + 1) * hd, m_rows:]], axis=0)
                     for u in range(2)], axis=1)
                + _mm(zc, block_diag(cat([jnp.where(incl_t, blk(xt, 0, 0, 1), 0.0),
                                          jnp.where(incl_t, blk(xt, 1, 0, 1), 0.0)], axis=1)), pp, pp)
                for art, va, zc, xt in zip(arts, vas, zcs, xts)]
        yield
        zero = jnp.zeros((hd, 2 * m_rows), BF16)
        for p, zc, ry in zip(pairs, zcs, rycs):
            arbd_ref[sb, p, 0:hd, 0:2 * m_rows] = cat([zc[0:hd, 0:m_rows], ry[0:hd, 0:m_rows]], axis=1).astype(BF16)
            arbd_ref[sb, p, 0:hd, 2 * m_rows:] = zero
            arbd_ref[sb, p, hd:, 0:2 * m_rows] = zero
            arbd_ref[sb, p, hd:, 2 * m_rows:] = cat([zc[0:hd, m_rows:], ry[0:hd, m_rows:]], axis=1).astype(BF16)
            uy_ref[sb, p] = cat([zc[hd:, 0:m_rows], ry[hd:, 0:m_rows], zc[hd:, m_rows:], ry[hd:, m_rows:]], axis=1)
        yield

    sh_c = c.bit_length() - 1
    cps = m_rows // c
    cpb = tbs // c
    row_id2 = (lax.broadcasted_iota(jnp.int32, (2 * m_rows, pw), 0) & (m_rows - 1)) >> sh_c
    col_id = lax.broadcasted_iota(jnp.int32, (hd, m_rows), 1) >> sh_c
    lane_lo = lax.broadcasted_iota(jnp.int32, (1, pw), 1) < hd

    def chunk_step(sb, i):
        b = i // cpb
        r0 = i * c
        rmask2 = row_id2 == i
        cmask = col_id == i
        ss = [st_ref[b, p] for p in pairs]
        zss = [_mm(s, arbd_ref[sb, p], pp, pp) + uy_ref[sb, p] for p, s in zip(pairs, ss)]
        yield
        upd = []
        for lo, zs in zip(los, zss):
            kbz = jnp.where(rmask2, kb_ref[sb, :, lo:lo + pw], 0.0)
            rhs = cat([jnp.where(lane_lo, kbz, 0.0), jnp.where(lane_lo, 0.0, kbz)], axis=0)
            vu = cat([vt_ref[sb, lo:lo + hd, :], zs[:, 0:m_rows],
                      vt_ref[sb, lo + hd:lo + pw, :], zs[:, 2 * m_rows:3 * m_rows]], axis=1)
            upd.append(_mm(vu, rhs, pp, pp))
        yield
        for p, lo, s, zs, d in zip(pairs, los, ss, zss, upd):
            st_ref[b, p] = s * pc_ref[sb, pl.ds(r0, 1), lo:lo + pw] + d
            for u in range(2):
                rows = slice((2 * p + u) * hd, (2 * p + u + 1) * hd)
                yt_ref[sb, rows, :] = jnp.where(cmask, zs[:, (2 * u + 1) * m_rows:(2 * u + 2) * m_rows],
                                               yt_ref[sb, rows, :])
        yield

    inv_n = 1.0 / hd

    def epilogue(sb):
        ys = [yt_ref[sb, cs, :].T for cs in strips]
        ycs = [y - head_sum(y) * inv_n for y in ys]
        yield
        rstds = [lax.rsqrt(head_stat(yc * yc, cs) * inv_n + RW_GN_EPS) for cs, yc in zip(strips, ycs)]
        yield
        for cs, yc, rstd in zip(strips, ycs, rstds):
            yn = yc * head_bcast(rstd, cs) * lnw_ref[:, cs] + lnb_ref[:, cs]
            o_ref[sb * m_rows:(sb + 1) * m_rows, cs] = (
                (yn + bonus_ref[sb, :, cs]) * g_ref[sb, :, cs]).astype(o_ref.dtype)
        yield

    def run(*gens):
        live = list(gens)
        while live:
            live = [g for g in live if next(g, live) is not live]

    def tail(sb):
        for i in range(cps):
            yield from chunk_step(sb, i)
        yield from epilogue(sb)

    if nsub == 1:
        run(prologue(0))
        run(pair_phase(0))

        together = SAMPLE_CHUNKS_TOGETHER if cpb == 1 and cps % SAMPLE_CHUNKS_TOGETHER == 0 else 1

        def chunk_body(i, carry):
            run(*[chunk_step(0, i * together + u) for u in range(together)])
            return carry

        lax.fori_loop(0, cps // together, chunk_body, 0)
        run(epilogue(0))
    else:
        run(prologue(0))
        for sb in range(nsub):
            run(pair_phase(sb), *([prologue(sb + 1)] if sb + 1 < nsub else []), *([tail(sb - 1)] if sb else []))
        run(tail(nsub - 1))

    if nb == 1:
        last = p_ref[tb - 1:tb, :].astype(F32).reshape(1, 1, GROUP_PAD)
    else:
        last = p_ref[...].astype(F32).reshape(nb, tb, GROUP_PAD)[:, tb - 1:tb, :]
    carry_ref[...] = last
    shout_ref[...] = last

    @pl.when(j == pl.num_programs(1) - 1)
    def _():
        for p in pairs:
            sp = st_ref[:, p]
            sout_ref[:, 2 * p] = sp[:, :, 0:hd]
            sout_ref[:, 2 * p + 1] = sp[:, :, hd:]


def _rwkv(p, shift_prev, s0, wts, nb, tb, c, pp):
    bsz = s0.shape[0]
    seq = p.shape[0] // bsz
    m_rows = ROWS
    nsub = nb * tb // m_rows
    assert nsub * m_rows == nb * tb and (nsub == 1 or nb == 1)
    hd = RW_HEAD_DIM
    const = lambda shape: pl.BlockSpec(shape, lambda i, j: (0,) * len(shape))
    kern = functools.partial(_rwkv_kernel, nb=nb, tb=tb, c=c, pp=pp)
    return pl.pallas_call(
        kern,
        grid=(bsz // nb, seq // tb),
        in_specs=[
            pl.BlockSpec((nb * tb, GROUP_PAD), lambda i, j: (i * (seq // tb) + j, 0)),
            pl.BlockSpec((nb, 1, GROUP_PAD), lambda i, j: (i, 0, 0)),
            pl.BlockSpec((nb, RW_HEADS, hd, hd), lambda i, j: (i, 0, 0, 0)),
            const((1, GROUP_PAD)),
            const((LANES, 2 * RW_WIDTH)),
            const((1, RW_WIDTH)),
            const((1, RW_WIDTH)),
            const((2 * LANES, RW_WIDTH)),
            const((1, RW_WIDTH)),
            const((1, RW_WIDTH)),
            const((1, RW_WIDTH)),
            const((1, RW_WIDTH)),
            const((1, RW_WIDTH)),
            const((RW_WIDTH, LANES)),
            const((LANES, RW_WIDTH)),
            const((STRIP, STRIP)),
        ],
        out_specs=[
            pl.BlockSpec((nb * tb, RW_WIDTH), lambda i, j: (i * (seq // tb) + j, 0)),
            pl.BlockSpec((nb, 1, GROUP_PAD), lambda i, j: (i, 0, 0)),
            pl.BlockSpec((nb, RW_HEADS, hd, hd), lambda i, j: (i, 0, 0, 0)),
        ],
        out_shape=[
            jax.ShapeDtypeStruct((bsz * seq, RW_WIDTH), BF16),
            jax.ShapeDtypeStruct((bsz, 1, GROUP_PAD), F32),
            jax.ShapeDtypeStruct((bsz, RW_HEADS, hd, hd), F32),
        ],
        scratch_shapes=[
            pltpu.VMEM((nb, 1, GROUP_PAD), F32),
            pltpu.VMEM((nb, RW_HEADS // 2, hd, 2 * hd), F32),
            pltpu.VMEM((nsub, 2 * m_rows, RW_WIDTH), F32),
            pltpu.VMEM((nsub, 2 * m_rows, RW_WIDTH), F32),
            pltpu.VMEM((nsub, m_rows, RW_WIDTH), F32),
            pltpu.VMEM((nsub, RW_WIDTH, 2 * m_rows), F32),
            pltpu.VMEM((nsub, RW_WIDTH, m_rows), F32),
            pltpu.VMEM((nsub, RW_HEADS // 2, 2 * hd, 4 * m_rows), BF16),
            pltpu.VMEM((nsub, RW_HEADS // 2, hd, 4 * m_rows), F32),
            pltpu.VMEM((nsub, RW_WIDTH, m_rows), F32),
            pltpu.VMEM((nsub, m_rows, RW_WIDTH), F32),
            pltpu.VMEM((nsub, m_rows, RW_WIDTH), F32),
        ],
        compiler_params=pltpu.CompilerParams(
            dimension_semantics=("parallel", "arbitrary"), vmem_limit_bytes=VMEM_LIMIT),
        name="rwkv7",
    )(p, shift_prev, s0, *wts)


def _gla_kernel(p_ref, s0_ref, gw2_ref, gb_ref, nw_ref, o_ref, st_ref,
                acc_ref, qd_ref, v_ref, kt_ref, et_ref, *, nb, tb, c, pp):
    j = pl.program_id(1)
    m_rows = nb * tb
    dk, dv = GLA_DK, GLA_DV

    @pl.when(j == 0)
    def _():
        st_ref[...] = s0_ref[...]

    p = p_ref[...].astype(F32)
    q = p[:, 0:GLA_KEY_WIDTH] * (dk ** -0.5)
    k = p[:, GLA_OFF_K:GLA_OFF_K + GLA_KEY_WIDTH]
    v = p[:, GLA_OFF_V:GLA_OFF_V + GLA_WIDTH]
    xs = p[:, GLA_OFF_GATE:GLA_OFF_GATE + LANES].astype(BF16)
    gout = p[:, GLA_OFF_GOUT:GLA_OFF_GOUT + GLA_WIDTH]
    gk = -_softplus(-(_dot(xs, gw2_ref[...]) + gb_ref[...])) / GLA_GATE_NORM

    same, ri, ci = _chunk_masks(m_rows, c)
    lmask = jnp.where(same & (ci <= ri), 1.0, 0.0).astype(BF16)
    causal = same & (ci <= ri)
    gcum = _mm(lmask, gk, pb=3)
    g3 = gcum.reshape(m_rows // c, c, GLA_KEY_WIDTH)
    gtot = jnp.broadcast_to(g3[:, c - 1:c, :], g3.shape).reshape(m_rows, GLA_KEY_WIDTH)
    qd = q * jnp.exp(gcum)
    kinv = k * jnp.exp(-gcum)
    qd_ref[...] = qd
    v_ref[...] = v
    kt_ref[...] = (k * jnp.exp(gtot - gcum)).T
    et_ref[...] = jnp.exp(gtot).T

    heads = range(GLA_HEADS)
    kq = [slice(h * dk, (h + 1) * dk) for h in heads]
    vq = [slice(h * dv, (h + 1) * dv) for h in heads]
    scores = [jnp.where(causal, _mm(qd[:, ks], kinv[:, ks], pp, pp, NT), 0.0) for ks in kq]
    intra = [_mm(a, v[:, vs], pp, pp) for a, vs in zip(scores, vq)]
    for vs, o in zip(vq, intra):
        acc_ref[:, vs] = o

    sh_c = c.bit_length() - 1
    cpb = tb // c
    span = min(m_rows, LANES)
    row_id = lax.broadcasted_iota(jnp.int32, (span, dv), 0) >> sh_c
    for i in range(m_rows // c):
        b = i // cpb
        r0 = i * c
        t0 = (r0 // span) * span
        rmask = row_id == (r0 - t0) // c
        states = [st_ref[b, h] for h in heads]
        inter = [_mm(qd_ref[r0:r0 + c, ks], s, pp, pp) for ks, s in zip(kq, states)]
        upd = [_mm(kt_ref[ks, t0:t0 + span], jnp.where(rmask, v_ref[t0:t0 + span, vs], 0.0), pp, pp)
               for ks, vs in zip(kq, vq)]
        for h, ks, vs, s, oi, d in zip(heads, kq, vq, states, inter, upd):
            acc_ref[r0:r0 + c, vs] += oi
            st_ref[b, h] = s * et_ref[ks, r0:r0 + 1] + d

    for vs in vq:
        o = acc_ref[:, vs]
        on = o * lax.rsqrt(jnp.mean(o * o, axis=-1, keepdims=True) + HEAD_NORM_EPS) * nw_ref[...]
        gh = gout[:, vs]
        o_ref[:, vs] = (on * (gh * _sigmoid(gh))).astype(o_ref.dtype)


def _gla(p, s0, wts, nb, tb, c, pp):
    bsz = s0.shape[0]
    seq = p.shape[0] // bsz
    m_rows = nb * tb
    const = lambda shape: pl.BlockSpec(shape, lambda i, j: (0,) * len(shape))
    kern = functools.partial(_gla_kernel, nb=nb, tb=tb, c=c, pp=pp)
    return pl.pallas_call(
        kern,
        grid=(bsz // nb, seq // tb),
        in_specs=[
            pl.BlockSpec((nb * tb, GROUP_PAD), lambda i, j: (i * (seq // tb) + j, 1)),
            pl.BlockSpec((nb, GLA_HEADS, GLA_DK, GLA_DV), lambda i, j: (i, 0, 0, 0)),
            const((LANES, GLA_KEY_WIDTH)),
            const((1, GLA_KEY_WIDTH)),
            const((1, GLA_DV)),
        ],
        out_specs=[
            pl.BlockSpec((nb * tb, GLA_WIDTH), lambda i, j: (i * (seq // tb) + j, 0)),
            pl.BlockSpec((nb, GLA_HEADS, GLA_DK, GLA_DV), lambda i, j: (i, 0, 0, 0)),
        ],
        out_shape=[
            jax.ShapeDtypeStruct((bsz * seq, GLA_WIDTH), BF16),
            jax.ShapeDtypeStruct((bsz, GLA_HEADS, GLA_DK, GLA_DV), F32),
        ],
        scratch_shapes=[
            pltpu.VMEM((m_rows, GLA_WIDTH), F32),
            pltpu.VMEM((m_rows, GLA_KEY_WIDTH), F32),
            pltpu.VMEM((m_rows, GLA_WIDTH), F32),
            pltpu.VMEM((GLA_KEY_WIDTH, m_rows), F32),
            pltpu.VMEM((GLA_KEY_WIDTH, m_rows), F32),
        ],
        compiler_params=pltpu.CompilerParams(
            dimension_semantics=("parallel", "arbitrary"), vmem_limit_bytes=VMEM_LIMIT),
        name="gla",
    )(p, s0, *wts)


def _post_kernel(x_ref, orw_ref, ogla_ref, wo_ref, g2_ref, wu_ref, wd_ref, gf_ref, o_ref, h_ref):
    jf = pl.program_id(1)

    @pl.when(jf == 0)
    def _():
        x1 = (x_ref[...]
              + _dot(orw_ref[...], wo_ref[0:RW_WIDTH, :])
              + _dot(ogla_ref[...], wo_ref[RW_WIDTH:, :]))
        o_ref[...] = x1
        ms = jnp.mean(x1 * x1, axis=-1, keepdims=True)
        h_ref[...] = (x1 * lax.rsqrt(ms + NORM_EPS) * g2_ref[...]).astype(BF16)

    u = jnp.maximum(_dot(h_ref[...], wu_ref[...]), 0.0)
    o_ref[...] += _dot((u * u).astype(BF16), wd_ref[...])

    @pl.when(jf == pl.num_programs(1) - 1)
    def _():
        x2 = o_ref[...]
        ms = jnp.mean(x2 * x2, axis=-1, keepdims=True)
        o_ref[...] = x2 * lax.rsqrt(ms + NORM_EPS) * gf_ref[...]


def _post(x2d, o_rw, o_gla, w_out, g2, w_up, w_down, gf, tm, tf):
    t = x2d.shape[0]
    return pl.pallas_call(
        _post_kernel,
        grid=(t // tm, D_FF // tf),
        in_specs=[
            pl.BlockSpec((tm, D_MODEL), lambda i, j: (i, 0)),
            pl.BlockSpec((tm, RW_WIDTH), lambda i, j: (i, 0)),
            pl.BlockSpec((tm, GLA_WIDTH), lambda i, j: (i, 0)),
            pl.BlockSpec((D_MODEL, D_MODEL), lambda i, j: (0, 0), pipeline_mode=pl.Buffered(1)),
            pl.BlockSpec((1, D_MODEL), lambda i, j: (0, 0)),
            pl.BlockSpec((D_MODEL, tf), lambda i, j: (0, j)),
            pl.BlockSpec((tf, D_MODEL), lambda i, j: (j, 0)),
            pl.BlockSpec((1, D_MODEL), lambda i, j: (0, 0)),
        ],
        out_specs=pl.BlockSpec((tm, D_MODEL), lambda i, j: (i, 0)),
        out_shape=jax.ShapeDtypeStruct((t, D_MODEL), F32),
        scratch_shapes=[pltpu.VMEM((tm, D_MODEL), BF16)],
        compiler_params=pltpu.CompilerParams(
            dimension_semantics=("parallel", "arbitrary"), vmem_limit_bytes=VMEM_LIMIT),
        name="post",
    )(x2d, o_rw, o_gla, w_out, g2, w_up, w_down, gf)


def _pad_cols(w, n):
    return jnp.pad(w, ((0, 0), (0, n - w.shape[1])))


def _prep_weights(w_in, rw_mu, rw_w0, rw_w2, rw_a0, rw_a2, rw_g2, rw_k_k, rw_k_a, rw_r_k, rw_ln_w, rw_ln_b,
                  gla_gw2, gla_gb, gla_norm_w):
    w_p = jnp.swapaxes(w_in, 0, 1).astype(BF16)

    row = lambda x: x.reshape(1, -1).astype(F32)
    mu = _pad_cols(row(rw_mu), GROUP_PAD)
    wwa = jnp.zeros((LANES, 2 * RW_WIDTH), F32)
    wwa = wwa.at[0:RW_LORA_W, 0:RW_WIDTH].set(rw_w2)
    wwa = wwa.at[RW_LORA_W:RW_LORA_W + RW_LORA_A, RW_WIDTH:].set(rw_a2)
    g2p = jnp.pad(rw_g2, ((0, 2 * LANES - RW_LORA_G), (0, 0)))
    head_of_col = jnp.arange(RW_WIDTH) // RW_HEAD_DIM
    e1 = (head_of_col[:, None] == jnp.arange(LANES)[None, :]).astype(BF16)
    rw_wts = (mu, wwa.astype(BF16), row(rw_w0), row(rw_a0), g2p.astype(BF16), row(rw_k_k), row(rw_k_a),
              row(rw_r_k), row(rw_ln_w), row(rw_ln_b), e1, e1.T,
              (head_of_col[:STRIP, None] == head_of_col[None, :STRIP]).astype(BF16))
    gw2p = jnp.pad(gla_gw2, ((0, LANES - GLA_GATE_RANK), (0, 0))).astype(BF16)
    gla_wts = (gw2p, row(gla_gb), row(gla_norm_w))
    return w_p, rw_wts, gla_wts


RW_PIECES = 1
GLA_PIECES = 1
PAIR_LOCKSTEP = 8
SAMPLE_CHUNKS_TOGETHER = 8
RW_STEP_ROWS = 2 * ROWS
STRIP = 256


def _trunk(x, shift, s_rw, s_gla, norm1_g, w_p, rw_wts, gla_wts, w_out, norm2_g, w_up, w_down, norm_f_g,
           nb, tb, c_rw, c_gla, tm):
    bsz, seq, _ = x.shape
    t = bsz * seq
    x2d = x.reshape(t, D_MODEL)
    proj = _inproj(x2d, norm1_g.reshape(1, -1), w_p, tm, INPROJ_TN)
    shift_p = _pad_cols(shift, GROUP_PAD).reshape(bsz, 1, GROUP_PAD)
    rw_tb = RW_STEP_ROWS if nb == 1 and seq % RW_STEP_ROWS == 0 else tb
    o_rw, sh_new, s_rw_new = _rwkv(proj, shift_p, s_rw, rw_wts, nb, rw_tb, c_rw, RW_PIECES)
    o_gla, s_gla_new = _gla(proj, s_gla, gla_wts, nb, rw_tb, c_gla, GLA_PIECES)
    y = _post(x2d, o_rw, o_gla, w_out, norm2_g.reshape(1, -1),
              w_up, w_down, norm_f_g.reshape(1, -1), POST_TM, POST_TF)
    return (y.reshape(bsz, seq, D_MODEL), sh_new[:, 0, :RW_PROJ][None], s_rw_new[None], s_gla_new[None])


def kernel(x_prompt, x_sample, state_rwkv_shift, state_rwkv_wkv, state_gla, norm1_g, w_in, rw_mu, rw_w0,
           rw_w2, rw_a0, rw_a2, rw_g2, rw_k_k, rw_k_a, rw_r_k, rw_ln_w, rw_ln_b, gla_gw2, gla_gb, gla_norm_w,
           w_out, norm2_g, w_up, w_down, norm_f_g):
    w_p, rw_wts, gla_wts = _prep_weights(
        w_in[0], rw_mu[0], rw_w0[0], rw_w2[0], rw_a0[0], rw_a2[0], rw_g2[0], rw_k_k[0], rw_k_a[0],
        rw_r_k[0].reshape(-1), rw_ln_w[0], rw_ln_b[0], gla_gw2[0], gla_gb[0], gla_norm_w[0])
    shared = (norm1_g[0], w_p, rw_wts, gla_wts, w_out[0].astype(BF16), norm2_g[0], w_up[0].astype(BF16),
              w_down[0].astype(BF16), norm_f_g)

    bp, lp, _ = x_prompt.shape
    bs, ls, _ = x_sample.shape
    dt = x_prompt.dtype
    out_p = _trunk(x_prompt, jnp.zeros((bp, RW_PROJ), dt),
                   jnp.zeros((bp, RW_HEADS, RW_HEAD_DIM, RW_HEAD_DIM), dt),
                   jnp.zeros((bp, GLA_HEADS, GLA_DK, GLA_DV), dt), *shared,
                   nb=1, tb=ROWS, c_rw=RW_CHUNK, c_gla=GLA_CHUNK, tm=INPROJ_TM)
    out_s = _trunk(x_sample, state_rwkv_shift[0], state_rwkv_wkv[0], state_gla[0], *shared,
                   nb=ROWS // ls, tb=ls, c_rw=ls, c_gla=ls, tm=INPROJ_TM)
    return (out_p[0], out_s[0], out_p[1], out_p[2], out_p[3], out_s[1], out_s[2], out_s[3])
```

```python
import functools

import jax
import jax.numpy as jnp
from jax import lax
from jax.experimental import pallas as pl
from jax.experimental.pallas import tpu as pltpu

F32 = jnp.float32
BF16 = jnp.bfloat16

D_MODEL = 2048
RW_WIDTH = 1024
RW_HEADS = 16
RW_HEAD_DIM = 64
RW_LORA_W = 64
RW_LORA_A = 64
RW_LORA_G = 160
RW_PROJ = 3 * RW_WIDTH + RW_LORA_W + RW_LORA_A + RW_LORA_G
RW_GN_EPS = 64e-5
GLA_WIDTH = 1024
GLA_HEADS = 4
GLA_KEY_WIDTH = 512
GLA_DK = 128
GLA_DV = 256
GLA_GATE_RANK = 16
GLA_GATE_NORM = 16.0
GLA_CHUNK = 64
GLA_PROJ = 2 * GLA_KEY_WIDTH + GLA_WIDTH + GLA_GATE_RANK + GLA_WIDTH
D_FF = 4 * D_MODEL
NORM_EPS = 1e-6
HEAD_NORM_EPS = 1e-5
DECAY_SCALE = 0.6065306597126334

LANES = 128

GROUP_PAD = 3456
RW_OFF_XWA = 3 * RW_WIDTH
RW_OFF_XG = RW_OFF_XWA + LANES
GLA_OFF_K = GLA_KEY_WIDTH
GLA_OFF_V = 2 * GLA_KEY_WIDTH
GLA_OFF_GATE = GLA_OFF_V + GLA_WIDTH
GLA_OFF_GOUT = GLA_OFF_GATE + LANES
NP = 2 * GROUP_PAD

ROWS = 128
MXU_WIDTH = 256
INPROJ_TM = 1024
INPROJ_TN = 9 * MXU_WIDTH
RW_CHUNK = 64
PROJ_DTYPE = BF16
POST_TM = 512
POST_TF = 1024
VMEM_LIMIT = 56 * 1024 * 1024

NN = (((1,), (0,)), ((), ()))
NT = (((1,), (1,)), ((), ()))


def _dot(a, b, dims=NN):
    return lax.dot_general(a, b, dims, preferred_element_type=F32)


def _parts(x, n):
    if x.dtype == BF16:
        return [x]
    out = []
    rem = x
    for i in range(n):
        h = rem.astype(BF16)
        out.append(h)
        if i + 1 < n:
            rem = rem - h.astype(F32)
    return out


def _mm(a, b, pa=1, pb=1, dims=NN):
    pa = 1 if a.dtype == BF16 else pa
    pb = 1 if b.dtype == BF16 else pb
    aa = _parts(a, pa)
    bb = _parts(b, pb)
    n = max(pa, pb)
    if dims == NN and min(pa, pb) == 1 and n * a.shape[1] <= MXU_WIDTH:
        return _dot(jnp.concatenate(aa * (n // pa), axis=1), jnp.concatenate(bb * (n // pb), axis=0))
    acc = None
    for i in range(pa):
        for j in range(pb):
            if i + j < n:
                t = _dot(aa[i], bb[j], dims)
                acc = t if acc is None else acc + t
    return acc


def _softplus(x):
    return jnp.maximum(x, 0.0) + jnp.log(1.0 + jnp.exp(-jnp.abs(x)))


def _sigmoid(x):
    return 0.5 * jnp.tanh(0.5 * x) + 0.5


def _chunk_masks(m, c):
    sh = c.bit_length() - 1
    ri = lax.broadcasted_iota(jnp.int32, (m, m), 0)
    ci = lax.broadcasted_iota(jnp.int32, (m, m), 1)
    same = (ri >> sh) == (ci >> sh)
    return same, ri, ci


_W_SEGMENTS = (
    (0, RW_PROJ, 0),
    (RW_PROJ, GLA_OFF_GATE + GLA_GATE_RANK, GROUP_PAD),
    (RW_PROJ + GLA_OFF_GATE + GLA_GATE_RANK, GLA_WIDTH, GROUP_PAD + GLA_OFF_GOUT),
)


def _weight_windows(tn):
    blocks = []
    for lo in range(0, NP, tn):
        wins = []
        for src, ln, dst in _W_SEGMENTS:
            a, b = max(dst, lo), min(dst + ln, lo + tn)
            if a < b:
                wins.append((src + a - dst, b - a, a - lo))
        blocks.append(tuple(wins))
    return tuple(blocks)


def _inproj_kernel(x_ref, g_ref, w_hbm, o_ref, h_ref, wbuf, sem, *, windows):
    i, j = pl.program_id(0), pl.program_id(1)
    nj = len(windows)
    step = i * nj + j
    n_steps = pl.num_programs(0) * nj
    slot = lax.rem(step, 2)

    def copies(jb, sl):
        return [pltpu.make_async_copy(w_hbm.at[pl.ds(src, ln), :], wbuf.at[sl, pl.ds(dst, ln), :], sem.at[sl, k])
                for k, (src, ln, dst) in enumerate(windows[jb])]

    @pl.when(step == 0)
    def _():
        for sl in range(min(2, nj)):
            covered = sorted((dst, dst + ln) for _, ln, dst in windows[sl])
            edges = [0] + [e for span in covered for e in span] + [wbuf.shape[1]]
            for a, b in zip(edges[0::2], edges[1::2]):
                if a < b:
                    wbuf[sl, a:b, :] = jnp.zeros((b - a, wbuf.shape[2]), wbuf.dtype)
        for cp in copies(0, 0):
            cp.start()

    for jb in range(nj):
        @pl.when(j == jb)
        def _(jb=jb):
            @pl.when(step + 1 < n_steps)
            def _():
                for cp in copies((jb + 1) % nj, 1 - slot):
                    cp.start()

            for cp in copies(jb, slot):
                cp.wait()

    @pl.when(j == 0)
    def _():
        x = x_ref[...]
        ms = jnp.mean(x * x, axis=-1, keepdims=True)
        h_ref[...] = (x * lax.rsqrt(ms + NORM_EPS) * g_ref[...]).astype(BF16)

    o_ref[...] = _dot(h_ref[...], wbuf[slot], NT).astype(o_ref.dtype)


def _inproj(x2d, g, w_t, tm, tn):
    t = x2d.shape[0]
    windows = _weight_windows(tn)
    return pl.pallas_call(
        functools.partial(_inproj_kernel, windows=windows),
        grid=(t // tm, NP // tn),
        in_specs=[
            pl.BlockSpec((tm, D_MODEL), lambda i, j: (i, 0)),
            pl.BlockSpec((1, D_MODEL), lambda i, j: (0, 0)),
            pl.BlockSpec(memory_space=pltpu.HBM),
        ],
        out_specs=pl.BlockSpec((tm, tn), lambda i, j: (i, j)),
        out_shape=jax.ShapeDtypeStruct((t, NP), PROJ_DTYPE),
        scratch_shapes=[
            pltpu.VMEM((tm, D_MODEL), BF16),
            pltpu.VMEM((2, tn, D_MODEL), BF16),
            pltpu.SemaphoreType.DMA((2, max(len(w) for w in windows))),
        ],
        compiler_params=pltpu.CompilerParams(
            dimension_semantics=("arbitrary", "arbitrary"), vmem_limit_bytes=VMEM_LIMIT),
        name="inproj",
    )(x2d, g, w_t)


def _rwkv_kernel(p_ref, sh_ref, s0_ref, mu_ref, wwa_ref, w0_ref, a0_ref, g2_ref, kk_ref, ka_ref,
                 rk_ref, lnw_ref, lnb_ref, e1_ref, e2_ref, e12_ref,
                 o_ref, shout_ref, sout_ref,
                 carry_ref, st_ref, bk_ref, kb_ref, pc_ref, art_ref, vt_ref, arbd_ref, uy_ref, yt_ref,
                 g_ref, bonus_ref, *, nb, tb, c, pp):
    j = pl.program_id(1)
    m_rows = ROWS
    nsub = nb * tb // m_rows
    tbs = tb // nsub
    n_lvl = c.bit_length() - 1
    hd = RW_HEAD_DIM

    @pl.when(j == 0)
    def _():
        for p in range(RW_HEADS // 2):
            st_ref[:, p] = jnp.concatenate([s0_ref[:, 2 * p], s0_ref[:, 2 * p + 1]], axis=-1)
        carry_ref[...] = sh_ref[...]

    yt_ref[...] = jnp.zeros_like(yt_ref)

    def shifted(sb, c0, w):
        r0 = sb * m_rows
        pc3 = p_ref[r0:r0 + m_rows, c0:c0 + w].astype(F32).reshape(nb, tbs, w)
        if sb == 0:
            before = carry_ref[:, :, c0:c0 + w]
        else:
            before = p_ref[r0 - 1:r0, c0:c0 + w].astype(F32).reshape(1, 1, w)
        t3 = lax.broadcasted_iota(jnp.int32, pc3.shape, 1)
        prev3 = jnp.where(t3 == 0, before, pltpu.roll(pc3, 1, axis=1))
        return (pc3 + (prev3 - pc3) * mu_ref[:, c0:c0 + w]).reshape(m_rows, w)

    same, ri, ci = _chunk_masks(m_rows, c)
    lmask = jnp.where(same & (ci <= ri), 1.0, 0.0).astype(BF16)
    strict_t = same & (ri < ci)
    incl_t = same & (ri <= ci)

    sw = STRIP
    strips = [slice(c0, c0 + sw) for c0 in range(0, RW_WIDTH, sw)]

    def head_stat(x, cs):
        return _dot(x.astype(BF16), e1_ref[cs, :])

    def head_bcast(s, cs):
        return _mm(s, e2_ref[:, cs], pa=2)

    def head_sum(x):
        return _dot(x.astype(BF16), e12_ref[...])


    def prologue(sb):
        slab = shifted(sb, RW_OFF_XWA, LANES)
        lane = lax.broadcasted_iota(jnp.int32, slab.shape, 1)
        lhs = jnp.where(lane < RW_LORA_W, jnp.tanh(slab), slab).astype(BF16)
        sg = _sigmoid(shifted(sb, RW_OFF_XG, 2 * LANES)).astype(BF16)
        yield
        lws = [_dot(lhs, wwa_ref[:, cs]) for cs in strips]
        las = [_dot(lhs, wwa_ref[:, RW_WIDTH + cs.start:RW_WIDTH + cs.stop]) for cs in strips]
        for cs in strips:
            g_ref[sb, :, cs] = _dot(sg, g2_ref[:, cs])
        yield
        logws = [-DECAY_SCALE * _sigmoid(w0_ref[:, cs] + lw) for cs, lw in zip(strips, lws)]
        gcums = [_mm(lmask, logw, pb=2) for logw in logws]
        yield
        avs = [_sigmoid(a0_ref[:, cs] + la) for cs, la in zip(strips, las)]
        ks = [shifted(sb, RW_WIDTH + cs.start, sw) for cs in strips]
        kkfs = [k * kk_ref[:, cs] for cs, k in zip(strips, ks)]
        rinvs = [lax.rsqrt(jnp.maximum(head_stat(kkf * kkf, cs), 1e-24)) for cs, kkf in zip(strips, kkfs)]
        yield
        k2s = [k * (1.0 + (a - 1.0) * ka_ref[:, cs]) for cs, k, a in zip(strips, ks, avs)]
        rs = [shifted(sb, cs.start, sw) for cs in strips]
        bsums = [head_sum(r * k2 * rk_ref[:, cs]) for cs, r, k2 in zip(strips, rs, k2s)]
        kks = [kkf * head_bcast(rinv, cs) for cs, kkf, rinv in zip(strips, kkfs, rinvs)]
        yield
        for cs, bsum in zip(strips, bsums):
            v = shifted(sb, 2 * RW_WIDTH + cs.start, sw)
            bonus_ref[sb, :, cs] = bsum * v
            vt_ref[sb, cs, :] = v.T
        yield
        for cs, logw, gcum, a, kk, k2, r in zip(strips, logws, gcums, avs, kks, k2s, rs):
            beta = kk * a
            g3 = gcum.reshape(cps, c, sw)
            gtot = jnp.broadcast_to(g3[:, c - 1:c, :], g3.shape).reshape(m_rows, sw)
            e_inv = jnp.exp(-gcum)
            e_rev = jnp.exp(gtot - gcum)
            art_ref[sb, cs, 0:m_rows] = (-kk * jnp.exp(gcum - logw)).T
            art_ref[sb, cs, m_rows:2 * m_rows] = (r * jnp.exp(gcum)).T
            bk_ref[sb, 0:m_rows, cs] = beta * e_inv
            bk_ref[sb, m_rows:, cs] = k2 * e_inv
            kb_ref[sb, 0:m_rows, cs] = k2 * e_rev
            kb_ref[sb, m_rows:, cs] = beta * e_rev
            pc_ref[sb, :, cs] = jnp.exp(gtot)
            yield

    pw = 2 * hd
    cat = jnp.concatenate
    top = lax.broadcasted_iota(jnp.int32, (pw, 1), 0) < hd
    left = lax.broadcasted_iota(jnp.int32, (1, 2 * m_rows), 1) < m_rows

    def split_rows(x):
        return cat([jnp.where(top, x, 0.0), jnp.where(top, 0.0, x)], axis=1)

    def block_diag(xc):
        return cat([jnp.where(left, xc, 0.0), jnp.where(left, 0.0, xc)], axis=0)

    def blk(xt, u, rh, ch):
        col = (2 * u + ch) * m_rows
        return xt[rh * m_rows:(rh + 1) * m_rows, col:col + m_rows]

    n_pairs = RW_HEADS // 2
    pairs = range(n_pairs)
    los = [p * pw for p in pairs]

    def pair_phase(sb):
        for g0 in range(0, n_pairs, PAIR_LOCKSTEP):
            yield from pair_group(sb, pairs[g0:g0 + PAIR_LOCKSTEP], los[g0:g0 + PAIR_LOCKSTEP])

    def pair_group(sb, pairs, los):
        arts = [art_ref[sb, lo:lo + pw, :] for lo in los]
        xts = [_mm(bk_ref[sb, :, lo:lo + pw], split_rows(art), pp, pp) for lo, art in zip(los, arts)]
        yield
        npcs = [cat([jnp.where(strict_t, blk(xt, 0, 0, 0), 0.0), jnp.where(strict_t, blk(xt, 1, 0, 0), 0.0)], axis=1)
                for xt in xts]
        vas = [_mm(split_rows(vt_ref[sb, lo:lo + pw, :]),
                   cat([cat([jnp.where(strict_t, blk(xt, u, 1, 0), 0.0), jnp.where(incl_t, blk(xt, u, 1, 1), 0.0)],
                            axis=1) for u in range(2)], axis=0), pp, pp)
               for lo, xt in zip(los, xts)]
        yield
        zcs = [cat([cat([art[u * hd:(u + 1) * hd, 0:m_rows], va[u * hd:(u + 1) * hd, 0:m_rows]], axis=0)
                    for u in range(2)], axis=1) for art, va in zip(arts, vas)]
        for lvl in range(n_lvl):
            bds = [block_diag(npc) for npc in npcs]
            if lvl + 1 < n_lvl:
                ts = [_mm(cat([zc, npc], axis=0), bd, pp, pp) for zc, npc, bd in zip(zcs, npcs, bds)]
                zcs = [zc + t[0:2 * hd] for zc, t in zip(zcs, ts)]
                npcs = [t[2 * hd:] for t in ts]
            else:
                zcs = [zc + _mm(zc, bd, pp, pp) for zc, bd in zip(zcs, bds)]
            yield
        rycs = [cat([cat([art[u * hd:(u + 1) * hd, m_rows:], va[u * hd:(u + 1) * hd, m_rows:]], axis=0)
                     for u in range(2)], axis=1)
                + _mm(zc, block_diag(cat([jnp.where(incl_t, blk(xt, 0, 0, 1), 0.0),
                                          jnp.where(incl_t, blk(xt, 1, 0, 1), 0.0)], axis=1)), pp, pp)
                for art, va, zc, xt in zip(arts, vas, zcs, xts)]
        yield
        zero = jnp.zeros((hd, 2 * m_rows), BF16)
        for p, zc, ry in zip(pairs, zcs, rycs):
            arbd_ref[sb, p, 0:hd, 0:2 * m_rows] = cat([zc[0:hd, 0:m_rows], ry[0:hd, 0:m_rows]], axis=1).astype(BF16)
            arbd_ref[sb, p, 0:hd, 2 * m_rows:] = zero
            arbd_ref[sb, p, hd:, 0:2 * m_rows] = zero
            arbd_ref[sb, p, hd:, 2 * m_rows:] = cat([zc[0:hd, m_rows:], ry[0:hd, m_rows:]], axis=1).astype(BF16)
            uy_ref[sb, p] = cat([zc[hd:, 0:m_rows], ry[hd:, 0:m_rows], zc[hd:, m_rows:], ry[hd:, m_rows:]], axis=1)
        yield

    sh_c = c.bit_length() - 1
    cps = m_rows // c
    cpb = tbs // c
    row_id2 = (lax.broadcasted_iota(jnp.int32, (2 * m_rows, pw), 0) & (m_rows - 1)) >> sh_c
    col_id = lax.broadcasted_iota(jnp.int32, (hd, m_rows), 1) >> sh_c
    lane_lo = lax.broadcasted_iota(jnp.int32, (1, pw), 1) < hd

    def chunk_step(sb, i):
        b = i // cpb
        r0 = i * c
        rmask2 = row_id2 == i
        cmask = col_id == i
        ss = [st_ref[b, p] for p in pairs]
        zss = [_mm(s, arbd_ref[sb, p], pp, pp) + uy_ref[sb, p] for p, s in zip(pairs, ss)]
        yield
        upd = []
        for lo, zs in zip(los, zss):
            kbz = jnp.where(rmask2, kb_ref[sb, :, lo:lo + pw], 0.0)
            rhs = cat([jnp.where(lane_lo, kbz, 0.0), jnp.where(lane_lo, 0.0, kbz)], axis=0)
            vu = cat([vt_ref[sb, lo:lo + hd, :], zs[:, 0:m_rows],
                      vt_ref[sb, lo + hd:lo + pw, :], zs[:, 2 * m_rows:3 * m_rows]], axis=1)
            upd.append(_mm(vu, rhs, pp, pp))
        yield
        for p, lo, s, zs, d in zip(pairs, los, ss, zss, upd):
            st_ref[b, p] = s * pc_ref[sb, pl.ds(r0, 1), lo:lo + pw] + d
            for u in range(2):
                rows = slice((2 * p + u) * hd, (2 * p + u + 1) * hd)
                yt_ref[sb, rows, :] = jnp.where(cmask, zs[:, (2 * u + 1) * m_rows:(2 * u + 2) * m_rows],
                                               yt_ref[sb, rows, :])
        yield

    inv_n = 1.0 / hd

    def epilogue(sb):
        ys = [yt_ref[sb, cs, :].T for cs in strips]
        ycs = [y - head_sum(y) * inv_n for y in ys]
        yield
        rstds = [lax.rsqrt(head_stat(yc * yc, cs) * inv_n + RW_GN_EPS) for cs, yc in zip(strips, ycs)]
        yield
        for cs, yc, rstd in zip(strips, ycs, rstds):
            yn = yc * head_bcast(rstd, cs) * lnw_ref[:, cs] + lnb_ref[:, cs]
            o_ref[sb * m_rows:(sb + 1) * m_rows, cs] = (
                (yn + bonus_ref[sb, :, cs]) * g_ref[sb, :, cs]).astype(o_ref.dtype)
        yield

    def run(*gens):
        live = list(gens)
        while live:
            live = [g for g in live if next(g, live) is not live]

    def tail(sb):
        for i in range(cps):
            yield from chunk_step(sb, i)
        yield from epilogue(sb)

    if nsub == 1:
        run(prologue(0))
        run(pair_phase(0))

        together = SAMPLE_CHUNKS_TOGETHER if cpb == 1 and cps % SAMPLE_CHUNKS_TOGETHER == 0 else 1

        def chunk_body(i, carry):
            run(*[chunk_step(0, i * together + u) for u in range(together)])
            return carry

        lax.fori_loop(0, cps // together, chunk_body, 0)
        run(epilogue(0))
    else:
        run(prologue(0))
        for sb in range(nsub):
            run(pair_phase(sb), *([prologue(sb + 1)] if sb + 1 < nsub else []), *([tail(sb - 1)] if sb else []))
        run(tail(nsub - 1))

    if nb == 1:
        last = p_ref[tb - 1:tb, :].astype(F32).reshape(1, 1, GROUP_PAD)
    else:
        last = p_ref[...].astype(F32).reshape(nb, tb, GROUP_PAD)[:, tb - 1:tb, :]
    carry_ref[...] = last
    shout_ref[...] = last

    @pl.when(j == pl.num_programs(1) - 1)
    def _():
        for p in pairs:
            sp = st_ref[:, p]
            sout_ref[:, 2 * p] = sp[:, :, 0:hd]
            sout_ref[:, 2 * p + 1] = sp[:, :, hd:]


def _rwkv(p, shift_prev, s0, wts, nb, tb, c, pp):
    bsz = s0.shape[0]
    seq = p.shape[0] // bsz
    m_rows = ROWS
    nsub = nb * tb // m_rows
    assert nsub * m_rows == nb * tb and (nsub == 1 or nb == 1)
    hd = RW_HEAD_DIM
    const = lambda shape: pl.BlockSpec(shape, lambda i, j: (0,) * len(shape))
    kern = functools.partial(_rwkv_kernel, nb=nb, tb=tb, c=c, pp=pp)
    return pl.pallas_call(
        kern,
        grid=(bsz // nb, seq // tb),
        in_specs=[
            pl.BlockSpec((nb * tb, GROUP_PAD), lambda i, j: (i * (seq // tb) + j, 0)),
            pl.BlockSpec((nb, 1, GROUP_PAD), lambda i, j: (i, 0, 0)),
            pl.BlockSpec((nb, RW_HEADS, hd, hd), lambda i, j: (i, 0, 0, 0)),
            const((1, GROUP_PAD)),
            const((LANES, 2 * RW_WIDTH)),
            const((1, RW_WIDTH)),
            const((1, RW_WIDTH)),
            const((2 * LANES, RW_WIDTH)),
            const((1, RW_WIDTH)),
            const((1, RW_WIDTH)),
            const((1, RW_WIDTH)),
            const((1, RW_WIDTH)),
            const((1, RW_WIDTH)),
            const((RW_WIDTH, LANES)),
            const((LANES, RW_WIDTH)),
            const((STRIP, STRIP)),
        ],
        out_specs=[
            pl.BlockSpec((nb * tb, RW_WIDTH), lambda i, j: (i * (seq // tb) + j, 0)),
            pl.BlockSpec((nb, 1, GROUP_PAD), lambda i, j: (i, 0, 0)),
            pl.BlockSpec((nb, RW_HEADS, hd, hd), lambda i, j: (i, 0, 0, 0)),
        ],
        out_shape=[
            jax.ShapeDtypeStruct((bsz * seq, RW_WIDTH), BF16),
            jax.ShapeDtypeStruct((bsz, 1, GROUP_PAD), F32),
            jax.ShapeDtypeStruct((bsz, RW_HEADS, hd, hd), F32),
        ],
        scratch_shapes=[
            pltpu.VMEM((nb, 1, GROUP_PAD), F32),
            pltpu.VMEM((nb, RW_HEADS // 2, hd, 2 * hd), F32),
            pltpu.VMEM((nsub, 2 * m_rows, RW_WIDTH), F32),
            pltpu.VMEM((nsub, 2 * m_rows, RW_WIDTH), F32),
            pltpu.VMEM((nsub, m_rows, RW_WIDTH), F32),
            pltpu.VMEM((nsub, RW_WIDTH, 2 * m_rows), F32),
            pltpu.VMEM((nsub, RW_WIDTH, m_rows), F32),
            pltpu.VMEM((nsub, RW_HEADS // 2, 2 * hd, 4 * m_rows), BF16),
            pltpu.VMEM((nsub, RW_HEADS // 2, hd, 4 * m_rows), F32),
            pltpu.VMEM((nsub, RW_WIDTH, m_rows), F32),
            pltpu.VMEM((nsub, m_rows, RW_WIDTH), F32),
            pltpu.VMEM((nsub, m_rows, RW_WIDTH), F32),
        ],
        compiler_params=pltpu.CompilerParams(
            dimension_semantics=("parallel", "arbitrary"), vmem_limit_bytes=VMEM_LIMIT),
        name="rwkv7",
    )(p, shift_prev, s0, *wts)


def _gla_kernel(p_ref, s0_ref, gw2_ref, gb_ref, nw_ref, o_ref, st_ref,
                acc_ref, qd_ref, v_ref, kt_ref, et_ref, *, nb, tb, c, pp):
    j = pl.program_id(1)
    m_rows = nb * tb
    dk, dv = GLA_DK, GLA_DV

    @pl.when(j == 0)
    def _():
        st_ref[...] = s0_ref[...]

    p = p_ref[...].astype(F32)
    q = p[:, 0:GLA_KEY_WIDTH] * (dk ** -0.5)
    k = p[:, GLA_OFF_K:GLA_OFF_K + GLA_KEY_WIDTH]
    v = p[:, GLA_OFF_V:GLA_OFF_V + GLA_WIDTH]
    xs = p[:, GLA_OFF_GATE:GLA_OFF_GATE + LANES].astype(BF16)
    gout = p[:, GLA_OFF_GOUT:GLA_OFF_GOUT + GLA_WIDTH]
    gk = -_softplus(-(_dot(xs, gw2_ref[...]) + gb_ref[...])) / GLA_GATE_NORM

    same, ri, ci = _chunk_masks(m_rows, c)
    lmask = jnp.where(same & (ci <= ri), 1.0, 0.0).astype(BF16)
    causal = same & (ci <= ri)
    gcum = _mm(lmask, gk, pb=3)
    g3 = gcum.reshape(m_rows // c, c, GLA_KEY_WIDTH)
    gtot = jnp.broadcast_to(g3[:, c - 1:c, :], g3.shape).reshape(m_rows, GLA_KEY_WIDTH)
    qd = q * jnp.exp(gcum)
    kinv = k * jnp.exp(-gcum)
    qd_ref[...] = qd
    v_ref[...] = v
    kt_ref[...] = (k * jnp.exp(gtot - gcum)).T
    et_ref[...] = jnp.exp(gtot).T

    heads = range(GLA_HEADS)
    kq = [slice(h * dk, (h + 1) * dk) for h in heads]
    vq = [slice(h * dv, (h + 1) * dv) for h in heads]
    scores = [jnp.where(causal, _mm(qd[:, ks], kinv[:, ks], pp, pp, NT), 0.0) for ks in kq]
    intra = [_mm(a, v[:, vs], pp, pp) for a, vs in zip(scores, vq)]
    for vs, o in zip(vq, intra):
        acc_ref[:, vs] = o

    sh_c = c.bit_length() - 1
    cpb = tb // c
    span = min(m_rows, LANES)
    row_id = lax.broadcasted_iota(jnp.int32, (span, dv), 0) >> sh_c
    for i in range(m_rows // c):
        b = i // cpb
        r0 = i * c
        t0 = (r0 // span) * span
        rmask = row_id == (r0 - t0) // c
        states = [st_ref[b, h] for h in heads]
        inter = [_mm(qd_ref[r0:r0 + c, ks], s, pp, pp) for ks, s in zip(kq, states)]
        upd = [_mm(kt_ref[ks, t0:t0 + span], jnp.where(rmask, v_ref[t0:t0 + span, vs], 0.0), pp, pp)
               for ks, vs in zip(kq, vq)]
        for h, ks, vs, s, oi, d in zip(heads, kq, vq, states, inter, upd):
            acc_ref[r0:r0 + c, vs] += oi
            st_ref[b, h] = s * et_ref[ks, r0:r0 + 1] + d

    for vs in vq:
        o = acc_ref[:, vs]
        on = o * lax.rsqrt(jnp.mean(o * o, axis=-1, keepdims=True) + HEAD_NORM_EPS) * nw_ref[...]
        gh = gout[:, vs]
        o_ref[:, vs] = (on * (gh * _sigmoid(gh))).astype(o_ref.dtype)


def _gla(p, s0, wts, nb, tb, c, pp):
    bsz = s0.shape[0]
    seq = p.shape[0] // bsz
    m_rows = nb * tb
    const = lambda shape: pl.BlockSpec(shape, lambda i, j: (0,) * len(shape))
    kern = functools.partial(_gla_kernel, nb=nb, tb=tb, c=c, pp=pp)
    return pl.pallas_call(
        kern,
        grid=(bsz // nb, seq // tb),
        in_specs=[
            pl.BlockSpec((nb * tb, GROUP_PAD), lambda i, j: (i * (seq // tb) + j, 1)),
            pl.BlockSpec((nb, GLA_HEADS, GLA_DK, GLA_DV), lambda i, j: (i, 0, 0, 0)),
            const((LANES, GLA_KEY_WIDTH)),
            const((1, GLA_KEY_WIDTH)),
            const((1, GLA_DV)),
        ],
        out_specs=[
            pl.BlockSpec((nb * tb, GLA_WIDTH), lambda i, j: (i * (seq // tb) + j, 0)),
            pl.BlockSpec((nb, GLA_HEADS, GLA_DK, GLA_DV), lambda i, j: (i, 0, 0, 0)),
        ],
        out_shape=[
            jax.ShapeDtypeStruct((bsz * seq, GLA_WIDTH), BF16),
            jax.ShapeDtypeStruct((bsz, GLA_HEADS, GLA_DK, GLA_DV), F32),
        ],
        scratch_shapes=[
            pltpu.VMEM((m_rows, GLA_WIDTH), F32),
            pltpu.VMEM((m_rows, GLA_KEY_WIDTH), F32),
            pltpu.VMEM((m_rows, GLA_WIDTH), F32),
            pltpu.VMEM((GLA_KEY_WIDTH, m_rows), F32),
            pltpu.VMEM((GLA_KEY_WIDTH, m_rows), F32),
        ],
        compiler_params=pltpu.CompilerParams(
            dimension_semantics=("parallel", "arbitrary"), vmem_limit_bytes=VMEM_LIMIT),
        name="gla",
    )(p, s0, *wts)


def _post_kernel(x_ref, orw_ref, ogla_ref, wo_ref, g2_ref, wu_ref, wd_ref, gf_ref, o_ref, h_ref):
    jf = pl.program_id(1)

    @pl.when(jf == 0)
    def _():
        x1 = (x_ref[...]
              + _dot(orw_ref[...], wo_ref[0:RW_WIDTH, :])
              + _dot(ogla_ref[...], wo_ref[RW_WIDTH:, :]))
        o_ref[...] = x1
        ms = jnp.mean(x1 * x1, axis=-1, keepdims=True)
        h_ref[...] = (x1 * lax.rsqrt(ms + NORM_EPS) * g2_ref[...]).astype(BF16)

    u = jnp.maximum(_dot(h_ref[...], wu_ref[...]), 0.0)
    o_ref[...] += _dot((u * u).astype(BF16), wd_ref[...])

    @pl.when(jf == pl.num_programs(1) - 1)
    def _():
        x2 = o_ref[...]
        ms = jnp.mean(x2 * x2, axis=-1, keepdims=True)
        o_ref[...] = x2 * lax.rsqrt(ms + NORM_EPS) * gf_ref[...]


def _post(x2d, o_rw, o_gla, w_out, g2, w_up, w_down, gf, tm, tf):
    t = x2d.shape[0]
    return pl.pallas_call(
        _post_kernel,
        grid=(t // tm, D_FF // tf),
        in_specs=[
            pl.BlockSpec((tm, D_MODEL), lambda i, j: (i, 0)),
            pl.BlockSpec((tm, RW_WIDTH), lambda i, j: (i, 0)),
            pl.BlockSpec((tm, GLA_WIDTH), lambda i, j: (i, 0)),
            pl.BlockSpec((D_MODEL, D_MODEL), lambda i, j: (0, 0), pipeline_mode=pl.Buffered(1)),
            pl.BlockSpec((1, D_MODEL), lambda i, j: (0, 0)),
            pl.BlockSpec((D_MODEL, tf), lambda i, j: (0, j)),
            pl.BlockSpec((tf, D_MODEL), lambda i, j: (j, 0)),
            pl.BlockSpec((1, D_MODEL), lambda i, j: (0, 0)),
        ],
        out_specs=pl.BlockSpec((tm, D_MODEL), lambda i, j: (i, 0)),
        out_shape=jax.ShapeDtypeStruct((t, D_MODEL), F32),
        scratch_shapes=[pltpu.VMEM((tm, D_MODEL), BF16)],
        compiler_params=pltpu.CompilerParams(
            dimension_semantics=("parallel", "arbitrary"), vmem_limit_bytes=VMEM_LIMIT),
        name="post",
    )(x2d, o_rw, o_gla, w_out, g2, w_up, w_down, gf)


def _pad_cols(w, n):
    return jnp.pad(w, ((0, 0), (0, n - w.shape[1])))


def _prep_weights(w_in, rw_mu, rw_w0, rw_w2, rw_a0, rw_a2, rw_g2, rw_k_k, rw_k_a, rw_r_k, rw_ln_w, rw_ln_b,
                  gla_gw2, gla_gb, gla_norm_w):
    w_p = jnp.swapaxes(w_in, 0, 1).astype(BF16)

    row = lambda x: x.reshape(1, -1).astype(F32)
    mu = _pad_cols(row(rw_mu), GROUP_PAD)
    wwa = jnp.zeros((LANES, 2 * RW_WIDTH), F32)
    wwa = wwa.at[0:RW_LORA_W, 0:RW_WIDTH].set(rw_w2)
    wwa = wwa.at[RW_LORA_W:RW_LORA_W + RW_LORA_A, RW_WIDTH:].set(rw_a2)
    g2p = jnp.pad(rw_g2, ((0, 2 * LANES - RW_LORA_G), (0, 0)))
    head_of_col = jnp.arange(RW_WIDTH) // RW_HEAD_DIM
    e1 = (head_of_col[:, None] == jnp.arange(LANES)[None, :]).astype(BF16)
    rw_wts = (mu, wwa.astype(BF16), row(rw_w0), row(rw_a0), g2p.astype(BF16), row(rw_k_k), row(rw_k_a),
              row(rw_r_k), row(rw_ln_w), row(rw_ln_b), e1, e1.T,
              (head_of_col[:STRIP, None] == head_of_col[None, :STRIP]).astype(BF16))
    gw2p = jnp.pad(gla_gw2, ((0, LANES - GLA_GATE_RANK), (0, 0))).astype(BF16)
    gla_wts = (gw2p, row(gla_gb), row(gla_norm_w))
    return w_p, rw_wts, gla_wts


RW_PIECES = 1
GLA_PIECES = 1
PAIR_LOCKSTEP = 8
SAMPLE_CHUNKS_TOGETHER = 8
RW_STEP_ROWS = 4 * ROWS
GLA_STEP_ROWS = 2 * ROWS
STRIP = 256


def _trunk(x, shift, s_rw, s_gla, norm1_g, w_p, rw_wts, gla_wts, w_out, norm2_g, w_up, w_down, norm_f_g,
           nb, tb, c_rw, c_gla, tm):
    bsz, seq, _ = x.shape
    t = bsz * seq
    x2d = x.reshape(t, D_MODEL)
    proj = _inproj(x2d, norm1_g.reshape(1, -1), w_p, tm, INPROJ_TN)
    shift_p = _pad_cols(shift, GROUP_PAD).reshape(bsz, 1, GROUP_PAD)
    rw_tb = RW_STEP_ROWS if nb == 1 and seq % RW_STEP_ROWS == 0 else tb
    o_rw, sh_new, s_rw_new = _rwkv(proj, shift_p, s_rw, rw_wts, nb, rw_tb, c_rw, RW_PIECES)
    gla_tb = GLA_STEP_ROWS if nb == 1 and seq % GLA_STEP_ROWS == 0 else tb
    o_gla, s_gla_new = _gla(proj, s_gla, gla_wts, nb, gla_tb, c_gla, GLA_PIECES)
    y = _post(x2d, o_rw, o_gla, w_out, norm2_g.reshape(1, -1),
              w_up, w_down, norm_f_g.reshape(1, -1), POST_TM, POST_TF)
    return (y.reshape(bsz, seq, D_MODEL), sh_new[:, 0, :RW_PROJ][None], s_rw_new[None], s_gla_new[None])


def kernel(x_prompt, x_sample, state_rwkv_shift, state_rwkv_wkv, state_gla, norm1_g, w_in, rw_mu, rw_w0,
           rw_w2, rw_a0, rw_a2, rw_g2, rw_k_k, rw_k_a, rw_r_k, rw_ln_w, rw_ln_b, gla_gw2, gla_gb, gla_norm_w,
           w_out, norm2_g, w_up, w_down, norm_f_g):
    w_p, rw_wts, gla_wts = _prep_weights(
        w_in[0], rw_mu[0], rw_w0[0], rw_w2[0], rw_a0[0], rw_a2[0], rw_g2[0], rw_k_k[0], rw_k_a[0],
        rw_r_k[0].reshape(-1), rw_ln_w[0], rw_ln_b[0], gla_gw2[0], gla_gb[0], gla_norm_w[0])
    shared = (norm1_g[0], w_p, rw_wts, gla_wts, w_out[0].astype(BF16), norm2_g[0], w_up[0].astype(BF16),
              w_down[0].astype(BF16), norm_f_g)

    bp, lp, _ = x_prompt.shape
    bs, ls, _ = x_sample.shape
    dt = x_prompt.dtype
    out_p = _trunk(x_prompt, jnp.zeros((bp, RW_PROJ), dt),
                   jnp.zeros((bp, RW_HEADS, RW_HEAD_DIM, RW_HEAD_DIM), dt),
                   jnp.zeros((bp, GLA_HEADS, GLA_DK, GLA_DV), dt), *shared,
                   nb=1, tb=ROWS, c_rw=RW_CHUNK, c_gla=GLA_CHUNK, tm=INPROJ_TM)
    out_s = _trunk(x_sample, state_rwkv_shift[0], state_rwkv_wkv[0], state_gla[0], *shared,
                   nb=ROWS // ls, tb=ls, c_rw=ls, c_gla=ls, tm=INPROJ_TM)
    return (out_p[0], out_s[0], out_p[1], out_p[2], out_p[3], out_s[1], out_s[2], out_s[3])
```

```python
import functools

import jax
import jax.numpy as jnp
from jax import lax
from jax.experimental import pallas as pl
from jax.experimental.pallas import tpu as pltpu

F32 = jnp.float32
BF16 = jnp.bfloat16

D_MODEL = 2048
RW_WIDTH = 1024
RW_HEADS = 16
RW_HEAD_DIM = 64
RW_LORA_W = 64
RW_LORA_A = 64
RW_LORA_G = 160
RW_PROJ = 3 * RW_WIDTH + RW_LORA_W + RW_LORA_A + RW_LORA_G
RW_GN_EPS = 64e-5
GLA_WIDTH = 1024
GLA_HEADS = 4
GLA_KEY_WIDTH = 512
GLA_DK = 128
GLA_DV = 256
GLA_GATE_RANK = 16
GLA_GATE_NORM = 16.0
GLA_CHUNK = 64
GLA_PROJ = 2 * GLA_KEY_WIDTH + GLA_WIDTH + GLA_GATE_RANK + GLA_WIDTH
D_FF = 4 * D_MODEL
NORM_EPS = 1e-6
HEAD_NORM_EPS = 1e-5
DECAY_SCALE = 0.6065306597126334

LANES = 128

GROUP_PAD = 3456
RW_OFF_XWA = 3 * RW_WIDTH
RW_OFF_XG = RW_OFF_XWA + LANES
GLA_OFF_K = GLA_KEY_WIDTH
GLA_OFF_V = 2 * GLA_KEY_WIDTH
GLA_OFF_GATE = GLA_OFF_V + GLA_WIDTH
GLA_OFF_GOUT = GLA_OFF_GATE + LANES
NP = 2 * GROUP_PAD

ROWS = 128
MXU_WIDTH = 256
INPROJ_TM = 1024
INPROJ_TN = 9 * MXU_WIDTH
RW_CHUNK = 64
PROJ_DTYPE = BF16
POST_TM = 512
POST_TF = 1024
VMEM_LIMIT = 56 * 1024 * 1024

NN = (((1,), (0,)), ((), ()))
NT = (((1,), (1,)), ((), ()))


def _dot(a, b, dims=NN):
    return lax.dot_general(a, b, dims, preferred_element_type=F32)


def _parts(x, n):
    if x.dtype == BF16:
        return [x]
    out = []
    rem = x
    for i in range(n):
        h = rem.astype(BF16)
        out.append(h)
        if i + 1 < n:
            rem = rem - h.astype(F32)
    return out


def _mm(a, b, pa=1, pb=1, dims=NN):
    pa = 1 if a.dtype == BF16 else pa
    pb = 1 if b.dtype == BF16 else pb
    aa = _parts(a, pa)
    bb = _parts(b, pb)
    n = max(pa, pb)
    if dims == NN and min(pa, pb) == 1 and n * a.shape[1] <= MXU_WIDTH:
        return _dot(jnp.concatenate(aa * (n // pa), axis=1), jnp.concatenate(bb * (n // pb), axis=0))
    acc = None
    for i in range(pa):
        for j in range(pb):
            if i + j < n:
                t = _dot(aa[i], bb[j], dims)
                acc = t if acc is None else acc + t
    return acc


def _softplus(x):
    return jnp.maximum(x, 0.0) + jnp.log(1.0 + jnp.exp(-jnp.abs(x)))


def _sigmoid(x):
    return 0.5 * jnp.tanh(0.5 * x) + 0.5


def _chunk_masks(m, c):
    sh = c.bit_length() - 1
    ri = lax.broadcasted_iota(jnp.int32, (m, m), 0)
    ci = lax.broadcasted_iota(jnp.int32, (m, m), 1)
    same = (ri >> sh) == (ci >> sh)
    return same, ri, ci


_W_SEGMENTS = (
    (0, RW_PROJ, 0),
    (RW_PROJ, GLA_OFF_GATE + GLA_GATE_RANK, GROUP_PAD),
    (RW_PROJ + GLA_OFF_GATE + GLA_GATE_RANK, GLA_WIDTH, GROUP_PAD + GLA_OFF_GOUT),
)


def _weight_windows(tn):
    blocks = []
    for lo in range(0, NP, tn):
        wins = []
        for src, ln, dst in _W_SEGMENTS:
            a, b = max(dst, lo), min(dst + ln, lo + tn)
            if a < b:
                wins.append((src + a - dst, b - a, a - lo))
        blocks.append(tuple(wins))
    return tuple(blocks)


def _inproj_kernel(x_ref, g_ref, w_hbm, o_ref, h_ref, wbuf, sem, *, windows):
    i, j = pl.program_id(0), pl.program_id(1)
    nj = len(windows)
    step = i * nj + j
    n_steps = pl.num_programs(0) * nj
    slot = lax.rem(step, 2)

    def copies(jb, sl):
        return [pltpu.make_async_copy(w_hbm.at[pl.ds(src, ln), :], wbuf.at[sl, pl.ds(dst, ln), :], sem.at[sl, k])
                for k, (src, ln, dst) in enumerate(windows[jb])]

    @pl.when(step == 0)
    def _():
        for sl in range(min(2, nj)):
            covered = sorted((dst, dst + ln) for _, ln, dst in windows[sl])
            edges = [0] + [e for span in covered for e in span] + [wbuf.shape[1]]
            for a, b in zip(edges[0::2], edges[1::2]):
                if a < b:
                    wbuf[sl, a:b, :] = jnp.zeros((b - a, wbuf.shape[2]), wbuf.dtype)
        for cp in copies(0, 0):
            cp.start()

    for jb in range(nj):
        @pl.when(j == jb)
        def _(jb=jb):
            @pl.when(step + 1 < n_steps)
            def _():
                for cp in copies((jb + 1) % nj, 1 - slot):
                    cp.start()

            for cp in copies(jb, slot):
                cp.wait()

    @pl.when(j == 0)
    def _():
        x = x_ref[...]
        ms = jnp.mean(x * x, axis=-1, keepdims=True)
        h_ref[...] = (x * lax.rsqrt(ms + NORM_EPS) * g_ref[...]).astype(BF16)

    o_ref[...] = _dot(h_ref[...], wbuf[slot], NT).astype(o_ref.dtype)


def _inproj(x2d, g, w_t, tm, tn):
    t = x2d.shape[0]
    windows = _weight_windows(tn)
    return pl.pallas_call(
        functools.partial(_inproj_kernel, windows=windows),
        grid=(t // tm, NP // tn),
        in_specs=[
            pl.BlockSpec((tm, D_MODEL), lambda i, j: (i, 0)),
            pl.BlockSpec((1, D_MODEL), lambda i, j: (0, 0)),
            pl.BlockSpec(memory_space=pltpu.HBM),
        ],
        out_specs=pl.BlockSpec((tm, tn), lambda i, j: (i, j)),
        out_shape=jax.ShapeDtypeStruct((t, NP), PROJ_DTYPE),
        scratch_shapes=[
            pltpu.VMEM((tm, D_MODEL), BF16),
            pltpu.VMEM((2, tn, D_MODEL), BF16),
            pltpu.SemaphoreType.DMA((2, max(len(w) for w in windows))),
        ],
        compiler_params=pltpu.CompilerParams(
            dimension_semantics=("arbitrary", "arbitrary"), vmem_limit_bytes=VMEM_LIMIT),
        name="inproj",
    )(x2d, g, w_t)


def _rwkv_kernel(p_ref, sh_ref, s0_ref, mu_ref, wwa_ref, w0_ref, a0_ref, g2_ref, kk_ref, ka_ref,
                 rk_ref, lnw_ref, lnb_ref, e1_ref, e2_ref, e12_ref,
                 o_ref, shout_ref, sout_ref,
                 carry_ref, st_ref, bk_ref, kb_ref, pc_ref, art_ref, vt_ref, arbd_ref, uy_ref, yt_ref,
                 g_ref, bonus_ref, *, nb, tb, c, pp):
    j = pl.program_id(1)
    m_rows = ROWS
    nsub = nb * tb // m_rows
    tbs = tb // nsub
    n_lvl = c.bit_length() - 1
    hd = RW_HEAD_DIM

    @pl.when(j == 0)
    def _():
        for p in range(RW_HEADS // 2):
            st_ref[:, p] = jnp.concatenate([s0_ref[:, 2 * p], s0_ref[:, 2 * p + 1]], axis=-1)
        carry_ref[...] = sh_ref[...]

    yt_ref[...] = jnp.zeros_like(yt_ref)

    def shifted(sb, c0, w):
        r0 = sb * m_rows
        pc3 = p_ref[r0:r0 + m_rows, c0:c0 + w].astype(F32).reshape(nb, tbs, w)
        if sb == 0:
            before = carry_ref[:, :, c0:c0 + w]
        else:
            before = p_ref[r0 - 1:r0, c0:c0 + w].astype(F32).reshape(1, 1, w)
        t3 = lax.broadcasted_iota(jnp.int32, pc3.shape, 1)
        prev3 = jnp.where(t3 == 0, before, pltpu.roll(pc3, 1, axis=1))
        return (pc3 + (prev3 - pc3) * mu_ref[:, c0:c0 + w]).reshape(m_rows, w)

    same, ri, ci = _chunk_masks(m_rows, c)
    lmask = jnp.where(same & (ci <= ri), 1.0, 0.0).astype(BF16)
    strict_t = same & (ri < ci)
    incl_t = same & (ri <= ci)

    sw = STRIP
    strips = [slice(c0, c0 + sw) for c0 in range(0, RW_WIDTH, sw)]

    def head_stat(x, cs):
        return _dot(x.astype(BF16), e1_ref[cs, :])

    def head_bcast(s, cs):
        return _mm(s, e2_ref[:, cs], pa=2)

    def head_sum(x):
        return _dot(x.astype(BF16), e12_ref[...])


    def prologue(sb):
        slab = shifted(sb, RW_OFF_XWA, LANES)
        lane = lax.broadcasted_iota(jnp.int32, slab.shape, 1)
        lhs = jnp.where(lane < RW_LORA_W, jnp.tanh(slab), slab).astype(BF16)
        sg = _sigmoid(shifted(sb, RW_OFF_XG, 2 * LANES)).astype(BF16)
        yield
        lws = [_dot(lhs, wwa_ref[:, cs]) for cs in strips]
        las = [_dot(lhs, wwa_ref[:, RW_WIDTH + cs.start:RW_WIDTH + cs.stop]) for cs in strips]
        for cs in strips:
            g_ref[sb, :, cs] = _dot(sg, g2_ref[:, cs])
        yield
        logws = [-DECAY_SCALE * _sigmoid(w0_ref[:, cs] + lw) for cs, lw in zip(strips, lws)]
        gcums = [_mm(lmask, logw, pb=2) for logw in logws]
        yield
        avs = [_sigmoid(a0_ref[:, cs] + la) for cs, la in zip(strips, las)]
        ks = [shifted(sb, RW_WIDTH + cs.start, sw) for cs in strips]
        kkfs = [k * kk_ref[:, cs] for cs, k in zip(strips, ks)]
        rinvs = [lax.rsqrt(jnp.maximum(head_stat(kkf * kkf, cs), 1e-24)) for cs, kkf in zip(strips, kkfs)]
        yield
        k2s = [k * (1.0 + (a - 1.0) * ka_ref[:, cs]) for cs, k, a in zip(strips, ks, avs)]
        rs = [shifted(sb, cs.start, sw) for cs in strips]
        bsums = [head_sum(r * k2 * rk_ref[:, cs]) for cs, r, k2 in zip(strips, rs, k2s)]
        kks = [kkf * head_bcast(rinv, cs) for cs, kkf, rinv in zip(strips, kkfs, rinvs)]
        yield
        for cs, bsum in zip(strips, bsums):
            v = shifted(sb, 2 * RW_WIDTH + cs.start, sw)
            bonus_ref[sb, :, cs] = bsum * v
            vt_ref[sb, cs, :] = v.T
        yield
        for cs, logw, gcum, a, kk, k2, r in zip(strips, logws, gcums, avs, kks, k2s, rs):
            beta = kk * a
            g3 = gcum.reshape(cps, c, sw)
            gtot = jnp.broadcast_to(g3[:, c - 1:c, :], g3.shape).reshape(m_rows, sw)
            e_inv = jnp.exp(-gcum)
            e_rev = jnp.exp(gtot - gcum)
            art_ref[sb, cs, 0:m_rows] = (-kk * jnp.exp(gcum - logw)).T
            art_ref[sb, cs, m_rows:2 * m_rows] = (r * jnp.exp(gcum)).T
            bk_ref[sb, 0:m_rows, cs] = beta * e_inv
            bk_ref[sb, m_rows:, cs] = k2 * e_inv
            kb_ref[sb, 0:m_rows, cs] = k2 * e_rev
            kb_ref[sb, m_rows:, cs] = beta * e_rev
            pc_ref[sb, :, cs] = jnp.exp(gtot)
            yield

    pw = 2 * hd
    cat = jnp.concatenate
    top = lax.broadcasted_iota(jnp.int32, (pw, 1), 0) < hd
    left = lax.broadcasted_iota(jnp.int32, (1, 2 * m_rows), 1) < m_rows

    def split_rows(x):
        return cat([jnp.where(top, x, 0.0), jnp.where(top, 0.0, x)], axis=1)

    def block_diag(xc):
        return cat([jnp.where(left, xc, 0.0), jnp.where(left, 0.0, xc)], axis=0)

    def blk(xt, u, rh, ch):
        col = (2 * u + ch) * m_rows
        return xt[rh * m_rows:(rh + 1) * m_rows, col:col + m_rows]

    n_pairs = RW_HEADS // 2
    pairs = range(n_pairs)
    los = [p * pw for p in pairs]

    def pair_phase(sb):
        for g0 in range(0, n_pairs, PAIR_LOCKSTEP):
            yield from pair_group(sb, pairs[g0:g0 + PAIR_LOCKSTEP], los[g0:g0 + PAIR_LOCKSTEP])

    def pair_group(sb, pairs, los):
        arts = [art_ref[sb, lo:lo + pw, :] for lo in los]
        xts = [_mm(bk_ref[sb, :, lo:lo + pw], split_rows(art), pp, pp) for lo, art in zip(los, arts)]
        yield
        npcs = [cat([jnp.where(strict_t, blk(xt, 0, 0, 0), 0.0), jnp.where(strict_t, blk(xt, 1, 0, 0), 0.0)], axis=1)
                for xt in xts]
        vas = [_mm(split_rows(vt_ref[sb, lo:lo + pw, :]),
                   cat([cat([jnp.where(strict_t, blk(xt, u, 1, 0), 0.0), jnp.where(incl_t, blk(xt, u, 1, 1), 0.0)],
                            axis=1) for u in range(2)], axis=0), pp, pp)
               for lo, xt in zip(los, xts)]
        yield
        zcs = [cat([cat([art[u * hd:(u + 1) * hd, 0:m_rows], va[u * hd:(u + 1) * hd, 0:m_rows]], axis=0)
                    for u in range(2)], axis=1) for art, va in zip(arts, vas)]
        for lvl in range(n_lvl):
            bds = [block_diag(npc) for npc in npcs]
            if lvl + 1 < n_lvl:
                ts = [_mm(cat([zc, npc], axis=0), bd, pp, pp) for zc, npc, bd in zip(zcs, npcs, bds)]
                zcs = [zc + t[0:2 * hd] for zc, t in zip(zcs, ts)]
                npcs = [t[2 * hd:] for t in ts]
            else:
                zcs = [zc + _mm(zc, bd, pp, pp) for zc, bd in zip(zcs, bds)]
            yield
        rycs = [cat([cat([art[u * hd:(u + 1) * hd, m_rows:], va[u * hd:(u + 1) * hd, m_rows:]], axis=0)
                     for u in range(2)], axis=1)
                + _mm(zc, block_diag(cat([jnp.where(incl_t, blk(xt, 0, 0, 1), 0.0),
                                          jnp.where(incl_t, blk(xt, 1, 0, 1), 0.0)], axis=1)), pp, pp)
                for art, va, zc, xt in zip(arts, vas, zcs, xts)]
        yield
        zero = jnp.zeros((hd, 2 * m_rows), BF16)
        for p, zc, ry in zip(pairs, zcs, rycs):
            arbd_ref[sb, p, 0:hd, 0:2 * m_rows] = cat([zc[0:hd, 0:m_rows], ry[0:hd, 0:m_rows]], axis=1).astype(BF16)
            arbd_ref[sb, p, 0:hd, 2 * m_rows:] = zero
            arbd_ref[sb, p, hd:, 0:2 * m_rows] = zero
            arbd_ref[sb, p, hd:, 2 * m_rows:] = cat([zc[0:hd, m_rows:], ry[0:hd, m_rows:]], axis=1).astype(BF16)
            uy_ref[sb, p] = cat([zc[hd:, 0:m_rows], ry[hd:, 0:m_rows], zc[hd:, m_rows:], ry[hd:, m_rows:]], axis=1)
        yield

    sh_c = c.bit_length() - 1
    cps = m_rows // c
    cpb = tbs // c
    row_id2 = (lax.broadcasted_iota(jnp.int32, (2 * m_rows, pw), 0) & (m_rows - 1)) >> sh_c
    col_id = lax.broadcasted_iota(jnp.int32, (hd, m_rows), 1) >> sh_c
    lane_lo = lax.broadcasted_iota(jnp.int32, (1, pw), 1) < hd

    def chunk_step(sb, i):
        b = i // cpb
        r0 = i * c
        rmask2 = row_id2 == i
        cmask = col_id == i
        ss = [st_ref[b, p] for p in pairs]
        zss = [_mm(s, arbd_ref[sb, p], pp, pp) + uy_ref[sb, p] for p, s in zip(pairs, ss)]
        yield
        upd = []
        for lo, zs in zip(los, zss):
            kbz = jnp.where(rmask2, kb_ref[sb, :, lo:lo + pw], 0.0)
            rhs = cat([jnp.where(lane_lo, kbz, 0.0), jnp.where(lane_lo, 0.0, kbz)], axis=0)
            vu = cat([vt_ref[sb, lo:lo + hd, :], zs[:, 0:m_rows],
                      vt_ref[sb, lo + hd:lo + pw, :], zs[:, 2 * m_rows:3 * m_rows]], axis=1)
            upd.append(_mm(vu, rhs, pp, pp))
        yield
        for p, lo, s, zs, d in zip(pairs, los, ss, zss, upd):
            st_ref[b, p] = s * pc_ref[sb, pl.ds(r0, 1), lo:lo + pw] + d
            for u in range(2):
                rows = slice((2 * p + u) * hd, (2 * p + u + 1) * hd)
                yt_ref[sb, rows, :] = jnp.where(cmask, zs[:, (2 * u + 1) * m_rows:(2 * u + 2) * m_rows],
                                               yt_ref[sb, rows, :])
        yield

    inv_n = 1.0 / hd

    def epilogue(sb):
        ys = [yt_ref[sb, cs, :].T for cs in strips]
        ycs = [y - head_sum(y) * inv_n for y in ys]
        yield
        rstds = [lax.rsqrt(head_stat(yc * yc, cs) * inv_n + RW_GN_EPS) for cs, yc in zip(strips, ycs)]
        yield
        for cs, yc, rstd in zip(strips, ycs, rstds):
            yn = yc * head_bcast(rstd, cs) * lnw_ref[:, cs] + lnb_ref[:, cs]
            o_ref[sb * m_rows:(sb + 1) * m_rows, cs] = (
                (yn + bonus_ref[sb, :, cs]) * g_ref[sb, :, cs]).astype(o_ref.dtype)
        yield

    def run(*gens):
        live = list(gens)
        while live:
            live = [g for g in live if next(g, live) is not live]

    def tail(sb):
        for i in range(cps):
            yield from chunk_step(sb, i)
        yield from epilogue(sb)

    if nsub == 1:
        run(prologue(0))
        run(pair_phase(0))

        together = SAMPLE_CHUNKS_TOGETHER if cpb == 1 and cps % SAMPLE_CHUNKS_TOGETHER == 0 else 1

        def chunk_body(i, carry):
            run(*[chunk_step(0, i * together + u) for u in range(together)])
            return carry

        lax.fori_loop(0, cps // together, chunk_body, 0)
        run(epilogue(0))
    else:
        run(prologue(0))
        for sb in range(nsub):
            run(pair_phase(sb), *([prologue(sb + 1)] if sb + 1 < nsub else []), *([tail(sb - 1)] if sb else []))
        run(tail(nsub - 1))

    if nb == 1:
        last = p_ref[tb - 1:tb, :].astype(F32).reshape(1, 1, GROUP_PAD)
    else:
        last = p_ref[...].astype(F32).reshape(nb, tb, GROUP_PAD)[:, tb - 1:tb, :]
    carry_ref[...] = last
    shout_ref[...] = last

    @pl.when(j == pl.num_programs(1) - 1)
    def _():
        for p in pairs:
            sp = st_ref[:, p]
            sout_ref[:, 2 * p] = sp[:, :, 0:hd]
            sout_ref[:, 2 * p + 1] = sp[:, :, hd:]


def _rwkv(p, shift_prev, s0, wts, nb, tb, c, pp):
    bsz = s0.shape[0]
    seq = p.shape[0] // bsz
    m_rows = ROWS
    nsub = nb * tb // m_rows
    assert nsub * m_rows == nb * tb and (nsub == 1 or nb == 1)
    hd = RW_HEAD_DIM
    const = lambda shape: pl.BlockSpec(shape, lambda i, j: (0,) * len(shape))
    kern = functools.partial(_rwkv_kernel, nb=nb, tb=tb, c=c, pp=pp)
    return pl.pallas_call(
        kern,
        grid=(bsz // nb, seq // tb),
        in_specs=[
            pl.BlockSpec((nb * tb, GROUP_PAD), lambda i, j: (i * (seq // tb) + j, 0)),
            pl.BlockSpec((nb, 1, GROUP_PAD), lambda i, j: (i, 0, 0)),
            pl.BlockSpec((nb, RW_HEADS, hd, hd), lambda i, j: (i, 0, 0, 0)),
            const((1, GROUP_PAD)),
            const((LANES, 2 * RW_WIDTH)),
            const((1, RW_WIDTH)),
            const((1, RW_WIDTH)),
            const((2 * LANES, RW_WIDTH)),
            const((1, RW_WIDTH)),
            const((1, RW_WIDTH)),
            const((1, RW_WIDTH)),
            const((1, RW_WIDTH)),
            const((1, RW_WIDTH)),
            const((RW_WIDTH, LANES)),
            const((LANES, RW_WIDTH)),
            const((STRIP, STRIP)),
        ],
        out_specs=[
            pl.BlockSpec((nb * tb, RW_WIDTH), lambda i, j: (i * (seq // tb) + j, 0)),
            pl.BlockSpec((nb, 1, GROUP_PAD), lambda i, j: (i, 0, 0)),
            pl.BlockSpec((nb, RW_HEADS, hd, hd), lambda i, j: (i, 0, 0, 0)),
        ],
        out_shape=[
            jax.ShapeDtypeStruct((bsz * seq, RW_WIDTH), BF16),
            jax.ShapeDtypeStruct((bsz, 1, GROUP_PAD), F32),
            jax.ShapeDtypeStruct((bsz, RW_HEADS, hd, hd), F32),
        ],
        scratch_shapes=[
            pltpu.VMEM((nb, 1, GROUP_PAD), F32),
            pltpu.VMEM((nb, RW_HEADS // 2, hd, 2 * hd), F32),
            pltpu.VMEM((nsub, 2 * m_rows, RW_WIDTH), F32),
            pltpu.VMEM((nsub, 2 * m_rows, RW_WIDTH), F32),
            pltpu.VMEM((nsub, m_rows, RW_WIDTH), F32),
            pltpu.VMEM((nsub, RW_WIDTH, 2 * m_rows), F32),
            pltpu.VMEM((nsub, RW_WIDTH, m_rows), F32),
            pltpu.VMEM((nsub, RW_HEADS // 2, 2 * hd, 4 * m_rows), BF16),
            pltpu.VMEM((nsub, RW_HEADS // 2, hd, 4 * m_rows), F32),
            pltpu.VMEM((nsub, RW_WIDTH, m_rows), F32),
            pltpu.VMEM((nsub, m_rows, RW_WIDTH), F32),
            pltpu.VMEM((nsub, m_rows, RW_WIDTH), F32),
        ],
        compiler_params=pltpu.CompilerParams(
            dimension_semantics=("parallel", "arbitrary"), vmem_limit_bytes=VMEM_LIMIT),
        name="rwkv7",
    )(p, shift_prev, s0, *wts)


def _gla_kernel(p_ref, s0_ref, gw2_ref, gb_ref, nw_ref, o_ref, st_ref,
                acc_ref, qd_ref, v_ref, kt_ref, et_ref, *, nb, tb, c, pp):
    j = pl.program_id(1)
    m_rows = nb * tb
    dk, dv = GLA_DK, GLA_DV

    @pl.when(j == 0)
    def _():
        st_ref[...] = s0_ref[...]

    p = p_ref[...].astype(F32)
    q = p[:, 0:GLA_KEY_WIDTH] * (dk ** -0.5)
    k = p[:, GLA_OFF_K:GLA_OFF_K + GLA_KEY_WIDTH]
    v = p[:, GLA_OFF_V:GLA_OFF_V + GLA_WIDTH]
    xs = p[:, GLA_OFF_GATE:GLA_OFF_GATE + LANES].astype(BF16)
    gout = p[:, GLA_OFF_GOUT:GLA_OFF_GOUT + GLA_WIDTH]
    gk = -_softplus(-(_dot(xs, gw2_ref[...]) + gb_ref[...])) / GLA_GATE_NORM

    same, ri, ci = _chunk_masks(m_rows, c)
    lmask = jnp.where(same & (ci <= ri), 1.0, 0.0).astype(BF16)
    causal = same & (ci <= ri)
    gcum = _mm(lmask, gk, pb=3)
    g3 = gcum.reshape(m_rows // c, c, GLA_KEY_WIDTH)
    gtot = jnp.broadcast_to(g3[:, c - 1:c, :], g3.shape).reshape(m_rows, GLA_KEY_WIDTH)
    qd = q * jnp.exp(gcum)
    kinv = k * jnp.exp(-gcum)
    qd_ref[...] = qd
    v_ref[...] = v
    kt_ref[...] = (k * jnp.exp(gtot - gcum)).T
    et_ref[...] = jnp.exp(gtot).T

    heads = range(GLA_HEADS)
    kq = [slice(h * dk, (h + 1) * dk) for h in heads]
    vq = [slice(h * dv, (h + 1) * dv) for h in heads]
    scores = [jnp.where(causal, _mm(qd[:, ks], kinv[:, ks], pp, pp, NT), 0.0) for ks in kq]
    intra = [_mm(a, v[:, vs], pp, pp) for a, vs in zip(scores, vq)]
    for vs, o in zip(vq, intra):
        acc_ref[:, vs] = o

    sh_c = c.bit_length() - 1
    cpb = tb // c
    span = min(m_rows, LANES)
    row_id = lax.broadcasted_iota(jnp.int32, (span, dv), 0) >> sh_c
    for i in range(m_rows // c):
        b = i // cpb
        r0 = i * c
        t0 = (r0 // span) * span
        rmask = row_id == (r0 - t0) // c
        states = [st_ref[b, h] for h in heads]
        inter = [_mm(qd_ref[r0:r0 + c, ks], s, pp, pp) for ks, s in zip(kq, states)]
        upd = [_mm(kt_ref[ks, t0:t0 + span], jnp.where(rmask, v_ref[t0:t0 + span, vs], 0.0), pp, pp)
               for ks, vs in zip(kq, vq)]
        for h, ks, vs, s, oi, d in zip(heads, kq, vq, states, inter, upd):
            acc_ref[r0:r0 + c, vs] += oi
            st_ref[b, h] = s * et_ref[ks, r0:r0 + 1] + d

    for vs in vq:
        o = acc_ref[:, vs]
        on = o * lax.rsqrt(jnp.mean(o * o, axis=-1, keepdims=True) + HEAD_NORM_EPS) * nw_ref[...]
        gh = gout[:, vs]
        o_ref[:, vs] = (on * (gh * _sigmoid(gh))).astype(o_ref.dtype)


def _gla(p, s0, wts, nb, tb, c, pp):
    bsz = s0.shape[0]
    seq = p.shape[0] // bsz
    m_rows = nb * tb
    const = lambda shape: pl.BlockSpec(shape, lambda i, j: (0,) * len(shape))
    kern = functools.partial(_gla_kernel, nb=nb, tb=tb, c=c, pp=pp)
    return pl.pallas_call(
        kern,
        grid=(bsz // nb, seq // tb),
        in_specs=[
            pl.BlockSpec((nb * tb, GROUP_PAD), lambda i, j: (i * (seq // tb) + j, 1)),
            pl.BlockSpec((nb, GLA_HEADS, GLA_DK, GLA_DV), lambda i, j: (i, 0, 0, 0)),
            const((LANES, GLA_KEY_WIDTH)),
            const((1, GLA_KEY_WIDTH)),
            const((1, GLA_DV)),
        ],
        out_specs=[
            pl.BlockSpec((nb * tb, GLA_WIDTH), lambda i, j: (i * (seq // tb) + j, 0)),
            pl.BlockSpec((nb, GLA_HEADS, GLA_DK, GLA_DV), lambda i, j: (i, 0, 0, 0)),
        ],
        out_shape=[
            jax.ShapeDtypeStruct((bsz * seq, GLA_WIDTH), BF16),
            jax.ShapeDtypeStruct((bsz, GLA_HEADS, GLA_DK, GLA_DV), F32),
        ],
        scratch_shapes=[
            pltpu.VMEM((m_rows, GLA_WIDTH), F32),
            pltpu.VMEM((m_rows, GLA_KEY_WIDTH), F32),
            pltpu.VMEM((m_rows, GLA_WIDTH), F32),
            pltpu.VMEM((GLA_KEY_WIDTH, m_rows), F32),
            pltpu.VMEM((GLA_KEY_WIDTH, m_rows), F32),
        ],
        compiler_params=pltpu.CompilerParams(
            dimension_semantics=("parallel", "arbitrary"), vmem_limit_bytes=VMEM_LIMIT),
        name="gla",
    )(p, s0, *wts)


def _post_kernel(x_ref, orw_ref, ogla_ref, wo_ref, g2_ref, wu_ref, wd_ref, gf_ref, o_ref, h_ref):
    jf = pl.program_id(1)

    @pl.when(jf == 0)
    def _():
        x1 = (x_ref[...]
              + _dot(orw_ref[...], wo_ref[0:RW_WIDTH, :])
              + _dot(ogla_ref[...], wo_ref[RW_WIDTH:, :]))
        o_ref[...] = x1
        ms = jnp.mean(x1 * x1, axis=-1, keepdims=True)
        h_ref[...] = (x1 * lax.rsqrt(ms + NORM_EPS) * g2_ref[...]).astype(BF16)

    u = jnp.maximum(_dot(h_ref[...], wu_ref[...]), 0.0)
    o_ref[...] += _dot((u * u).astype(BF16), wd_ref[...])

    @pl.when(jf == pl.num_programs(1) - 1)
    def _():
        x2 = o_ref[...]
        ms = jnp.mean(x2 * x2, axis=-1, keepdims=True)
        o_ref[...] = x2 * lax.rsqrt(ms + NORM_EPS) * gf_ref[...]


def _post(x2d, o_rw, o_gla, w_out, g2, w_up, w_down, gf, tm, tf):
    t = x2d.shape[0]
    return pl.pallas_call(
        _post_kernel,
        grid=(t // tm, D_FF // tf),
        in_specs=[
            pl.BlockSpec((tm, D_MODEL), lambda i, j: (i, 0)),
            pl.BlockSpec((tm, RW_WIDTH), lambda i, j: (i, 0)),
            pl.BlockSpec((tm, GLA_WIDTH), lambda i, j: (i, 0)),
            pl.BlockSpec((D_MODEL, D_MODEL), lambda i, j: (0, 0), pipeline_mode=pl.Buffered(1)),
            pl.BlockSpec((1, D_MODEL), lambda i, j: (0, 0)),
            pl.BlockSpec((D_MODEL, tf), lambda i, j: (0, j)),
            pl.BlockSpec((tf, D_MODEL), lambda i, j: (j, 0)),
            pl.BlockSpec((1, D_MODEL), lambda i, j: (0, 0)),
        ],
        out_specs=pl.BlockSpec((tm, D_MODEL), lambda i, j: (i, 0)),
        out_shape=jax.ShapeDtypeStruct((t, D_MODEL), F32),
        scratch_shapes=[pltpu.VMEM((tm, D_MODEL), BF16)],
        compiler_params=pltpu.CompilerParams(
            dimension_semantics=("parallel", "arbitrary"), vmem_limit_bytes=VMEM_LIMIT),
        name="post",
    )(x2d, o_rw, o_gla, w_out, g2, w_up, w_down, gf)


def _pad_cols(w, n):
    return jnp.pad(w, ((0, 0), (0, n - w.shape[1])))


def _prep_weights(w_in, rw_mu, rw_w0, rw_w2, rw_a0, rw_a2, rw_g2, rw_k_k, rw_k_a, rw_r_k, rw_ln_w, rw_ln_b,
                  gla_gw2, gla_gb, gla_norm_w):
    w_p = jnp.swapaxes(w_in, 0, 1).astype(BF16)

    row = lambda x: x.reshape(1, -1).astype(F32)
    mu = _pad_cols(row(rw_mu), GROUP_PAD)
    wwa = jnp.zeros((LANES, 2 * RW_WIDTH), F32)
    wwa = wwa.at[0:RW_LORA_W, 0:RW_WIDTH].set(rw_w2)
    wwa = wwa.at[RW_LORA_W:RW_LORA_W + RW_LORA_A, RW_WIDTH:].set(rw_a2)
    g2p = jnp.pad(rw_g2, ((0, 2 * LANES - RW_LORA_G), (0, 0)))
    head_of_col = jnp.arange(RW_WIDTH) // RW_HEAD_DIM
    e1 = (head_of_col[:, None] == jnp.arange(LANES)[None, :]).astype(BF16)
    rw_wts = (mu, wwa.astype(BF16), row(rw_w0), row(rw_a0), g2p.astype(BF16), row(rw_k_k), row(rw_k_a),
              row(rw_r_k), row(rw_ln_w), row(rw_ln_b), e1, e1.T,
              (head_of_col[:STRIP, None] == head_of_col[None, :STRIP]).astype(BF16))
    gw2p = jnp.pad(gla_gw2, ((0, LANES - GLA_GATE_RANK), (0, 0))).astype(BF16)
    gla_wts = (gw2p, row(gla_gb), row(gla_norm_w))
    return w_p, rw_wts, gla_wts


RW_PIECES = 1
GLA_PIECES = 1
PAIR_LOCKSTEP = 8
SAMPLE_CHUNKS_TOGETHER = 16
RW_STEP_ROWS = 4 * ROWS
GLA_STEP_ROWS = 2 * ROWS
STRIP = 256


def _trunk(x, shift, s_rw, s_gla, norm1_g, w_p, rw_wts, gla_wts, w_out, norm2_g, w_up, w_down, norm_f_g,
           nb, tb, c_rw, c_gla, tm):
    bsz, seq, _ = x.shape
    t = bsz * seq
    x2d = x.reshape(t, D_MODEL)
    proj = _inproj(x2d, norm1_g.reshape(1, -1), w_p, tm, INPROJ_TN)
    shift_p = _pad_cols(shift, GROUP_PAD).reshape(bsz, 1, GROUP_PAD)
    rw_tb = RW_STEP_ROWS if nb == 1 and seq % RW_STEP_ROWS == 0 else tb
    o_rw, sh_new, s_rw_new = _rwkv(proj, shift_p, s_rw, rw_wts, nb, rw_tb, c_rw, RW_PIECES)
    gla_tb = GLA_STEP_ROWS if nb == 1 and seq % GLA_STEP_ROWS == 0 else tb
    o_gla, s_gla_new = _gla(proj, s_gla, gla_wts, nb, gla_tb, c_gla, GLA_PIECES)
    y = _post(x2d, o_rw, o_gla, w_out, norm2_g.reshape(1, -1),
              w_up, w_down, norm_f_g.reshape(1, -1), POST_TM, POST_TF)
    return (y.reshape(bsz, seq, D_MODEL), sh_new[:, 0, :RW_PROJ][None], s_rw_new[None], s_gla_new[None])


def kernel(x_prompt, x_sample, state_rwkv_shift, state_rwkv_wkv, state_gla, norm1_g, w_in, rw_mu, rw_w0,
           rw_w2, rw_a0, rw_a2, rw_g2, rw_k_k, rw_k_a, rw_r_k, rw_ln_w, rw_ln_b, gla_gw2, gla_gb, gla_norm_w,
           w_out, norm2_g, w_up, w_down, norm_f_g):
    w_p, rw_wts, gla_wts = _prep_weights(
        w_in[0], rw_mu[0], rw_w0[0], rw_w2[0], rw_a0[0], rw_a2[0], rw_g2[0], rw_k_k[0], rw_k_a[0],
        rw_r_k[0].reshape(-1), rw_ln_w[0], rw_ln_b[0], gla_gw2[0], gla_gb[0], gla_norm_w[0])
    shared = (norm1_g[0], w_p, rw_wts, gla_wts, w_out[0].astype(BF16), norm2_g[0], w_up[0].astype(BF16),
              w_down[0].astype(BF16), norm_f_g)

    bp, lp, _ = x_prompt.shape
    bs, ls, _ = x_sample.shape
    dt = x_prompt.dtype
    out_p = _trunk(x_prompt, jnp.zeros((bp, RW_PROJ), dt),
                   jnp.zeros((bp, RW_HEADS, RW_HEAD_DIM, RW_HEAD_DIM), dt),
                   jnp.zeros((bp, GLA_HEADS, GLA_DK, GLA_DV), dt), *shared,
                   nb=1, tb=ROWS, c_rw=RW_CHUNK, c_gla=GLA_CHUNK, tm=INPROJ_TM)
    out_s = _trunk(x_sample, state_rwkv_shift[0], state_rwkv_wkv[0], state_gla[0], *shared,
                   nb=ROWS // ls, tb=ls, c_rw=ls, c_gla=ls, tm=INPROJ_TM)
    return (out_p[0], out_s[0], out_p[1], out_p[2], out_p[3], out_s[1], out_s[2], out_s[3])
```

```python
import functools

import jax
import jax.numpy as jnp
from jax import lax
from jax.experimental import pallas as pl
from jax.experimental.pallas import tpu as pltpu

F32 = jnp.float32
BF16 = jnp.bfloat16

D_MODEL = 2048
RW_WIDTH = 1024
RW_HEADS = 16
RW_HEAD_DIM = 64
RW_LORA_W = 64
RW_LORA_A = 64
RW_LORA_G = 160
RW_PROJ = 3 * RW_WIDTH + RW_LORA_W + RW_LORA_A + RW_LORA_G
RW_GN_EPS = 64e-5
GLA_WIDTH = 1024
GLA_HEADS = 4
GLA_KEY_WIDTH = 512
GLA_DK = 128
GLA_DV = 256
GLA_GATE_RANK = 16
GLA_GATE_NORM = 16.0
GLA_CHUNK = 64
GLA_PROJ = 2 * GLA_KEY_WIDTH + GLA_WIDTH + GLA_GATE_RANK + GLA_WIDTH
D_FF = 4 * D_MODEL
NORM_EPS = 1e-6
HEAD_NORM_EPS = 1e-5
DECAY_SCALE = 0.6065306597126334

LANES = 128
SUBLANES = 8

GROUP_PAD = 3456
RW_OFF_XWA = 3 * RW_WIDTH
RW_OFF_XG = RW_OFF_XWA + LANES
GLA_OFF_K = GLA_KEY_WIDTH
GLA_OFF_V = 2 * GLA_KEY_WIDTH
GLA_OFF_GATE = GLA_OFF_V + GLA_WIDTH
GLA_OFF_GOUT = GLA_OFF_GATE + LANES
NP = 2 * GROUP_PAD

ROWS = 128
MXU_WIDTH = 256
INPROJ_TM = 1024
INPROJ_TN = 9 * MXU_WIDTH
RW_CHUNK = 64
PROJ_DTYPE = BF16
POST_TM = 512
POST_TF = 1024
VMEM_LIMIT = 56 * 1024 * 1024

NN = (((1,), (0,)), ((), ()))
NT = (((1,), (1,)), ((), ()))


def _dot(a, b, dims=NN):
    return lax.dot_general(a, b, dims, preferred_element_type=F32)


def _parts(x, n):
    if x.dtype == BF16:
        return [x]
    out = []
    rem = x
    for i in range(n):
        h = rem.astype(BF16)
        out.append(h)
        if i + 1 < n:
            rem = rem - h.astype(F32)
    return out


def _mm(a, b, pa=1, pb=1, dims=NN):
    pa = 1 if a.dtype == BF16 else pa
    pb = 1 if b.dtype == BF16 else pb
    aa = _parts(a, pa)
    bb = _parts(b, pb)
    n = max(pa, pb)
    if dims == NN and min(pa, pb) == 1 and n * a.shape[1] <= MXU_WIDTH:
        return _dot(jnp.concatenate(aa * (n // pa), axis=1), jnp.concatenate(bb * (n // pb), axis=0))
    acc = None
    for i in range(pa):
        for j in range(pb):
            if i + j < n:
                t = _dot(aa[i], bb[j], dims)
                acc = t if acc is None else acc + t
    return acc


def _softplus(x):
    return jnp.maximum(x, 0.0) + jnp.log(1.0 + jnp.exp(-jnp.abs(x)))


def _sigmoid(x):
    return 0.5 * jnp.tanh(0.5 * x) + 0.5


def _chunk_masks(m, c):
    sh = c.bit_length() - 1
    ri = lax.broadcasted_iota(jnp.int32, (m, m), 0)
    ci = lax.broadcasted_iota(jnp.int32, (m, m), 1)
    same = (ri >> sh) == (ci >> sh)
    return same, ri, ci


_W_SEGMENTS = (
    (0, RW_PROJ, 0),
    (RW_PROJ, GLA_OFF_GATE + GLA_GATE_RANK, GROUP_PAD),
    (RW_PROJ + GLA_OFF_GATE + GLA_GATE_RANK, GLA_WIDTH, GROUP_PAD + GLA_OFF_GOUT),
)


def _weight_windows(tn):
    blocks = []
    for lo in range(0, NP, tn):
        wins = []
        for src, ln, dst in _W_SEGMENTS:
            a, b = max(dst, lo), min(dst + ln, lo + tn)
            if a < b:
                wins.append((src + a - dst, b - a, a - lo))
        blocks.append(tuple(wins))
    return tuple(blocks)


def _inproj_kernel(x_ref, g_ref, w_hbm, o_ref, h_ref, wbuf, sem, *, windows):
    i, j = pl.program_id(0), pl.program_id(1)
    nj = len(windows)
    step = i * nj + j
    n_steps = pl.num_programs(0) * nj
    slot = lax.rem(step, 2)

    def copies(jb, sl):
        return [pltpu.make_async_copy(w_hbm.at[pl.ds(src, ln), :], wbuf.at[sl, pl.ds(dst, ln), :], sem.at[sl, k])
                for k, (src, ln, dst) in enumerate(windows[jb])]

    @pl.when(step == 0)
    def _():
        for sl in range(min(2, nj)):
            covered = sorted((dst, dst + ln) for _, ln, dst in windows[sl])
            edges = [0] + [e for span in covered for e in span] + [wbuf.shape[1]]
            for a, b in zip(edges[0::2], edges[1::2]):
                if a < b:
                    wbuf[sl, a:b, :] = jnp.zeros((b - a, wbuf.shape[2]), wbuf.dtype)
        for cp in copies(0, 0):
            cp.start()

    for jb in range(nj):
        @pl.when(j == jb)
        def _(jb=jb):
            @pl.when(step + 1 < n_steps)
            def _():
                for cp in copies((jb + 1) % nj, 1 - slot):
                    cp.start()

            for cp in copies(jb, slot):
                cp.wait()

    @pl.when(j == 0)
    def _():
        x = x_ref[...]
        ms = jnp.mean(x * x, axis=-1, keepdims=True)
        h_ref[...] = (x * lax.rsqrt(ms + NORM_EPS) * g_ref[...]).astype(BF16)

    o_ref[...] = _dot(h_ref[...], wbuf[slot], NT).astype(o_ref.dtype)


def _inproj(x2d, g, w_t, tm, tn):
    t = x2d.shape[0]
    windows = _weight_windows(tn)
    return pl.pallas_call(
        functools.partial(_inproj_kernel, windows=windows),
        grid=(t // tm, NP // tn),
        in_specs=[
            pl.BlockSpec((tm, D_MODEL), lambda i, j: (i, 0)),
            pl.BlockSpec((1, D_MODEL), lambda i, j: (0, 0)),
            pl.BlockSpec(memory_space=pltpu.HBM),
        ],
        out_specs=pl.BlockSpec((tm, tn), lambda i, j: (i, j)),
        out_shape=jax.ShapeDtypeStruct((t, NP), PROJ_DTYPE),
        scratch_shapes=[
            pltpu.VMEM((tm, D_MODEL), BF16),
            pltpu.VMEM((2, tn, D_MODEL), BF16),
            pltpu.SemaphoreType.DMA((2, max(len(w) for w in windows))),
        ],
        compiler_params=pltpu.CompilerParams(
            dimension_semantics=("arbitrary", "arbitrary"), vmem_limit_bytes=VMEM_LIMIT),
        name="inproj",
    )(x2d, g, w_t)


def _rwkv_kernel(p_ref, sh_ref, s0_ref, mu_ref, wwa_ref, w0_ref, a0_ref, g2_ref, kk_ref, ka_ref,
                 rk_ref, lnw_ref, lnb_ref, e1_ref, e2_ref, e12_ref,
                 o_ref, shout_ref, sout_ref,
                 carry_ref, st_ref, bk_ref, kb_ref, pc_ref, art_ref, vt_ref, arbd_ref, uy_ref, yt_ref,
                 g_ref, bonus_ref, zrt_ref, vrm_ref, yrm_ref, *, nb, tb, c, pp):
    j = pl.program_id(1)
    m_rows = ROWS
    nsub = nb * tb // m_rows
    tbs = tb // nsub
    n_lvl = c.bit_length() - 1
    hd = RW_HEAD_DIM
    row_major = _rwkv_row_major(nb, tb, c)

    @pl.when(j == 0)
    def _():
        for p in range(RW_HEADS // 2):
            st_ref[:, p] = jnp.concatenate([s0_ref[:, 2 * p], s0_ref[:, 2 * p + 1]], axis=-1)
        carry_ref[...] = sh_ref[...]

    if not row_major:
        yt_ref[...] = jnp.zeros_like(yt_ref)

    def shifted(sb, c0, w):
        r0 = sb * m_rows
        pc3 = p_ref[r0:r0 + m_rows, c0:c0 + w].astype(F32).reshape(nb, tbs, w)
        if sb == 0:
            before = carry_ref[:, :, c0:c0 + w]
        else:
            before = p_ref[r0 - 1:r0, c0:c0 + w].astype(F32).reshape(1, 1, w)
        t3 = lax.broadcasted_iota(jnp.int32, pc3.shape, 1)
        prev3 = jnp.where(t3 == 0, before, pltpu.roll(pc3, 1, axis=1))
        return (pc3 + (prev3 - pc3) * mu_ref[:, c0:c0 + w]).reshape(m_rows, w)

    same, ri, ci = _chunk_masks(m_rows, c)
    lmask = jnp.where(same & (ci <= ri), 1.0, 0.0).astype(BF16)
    strict_t = same & (ri < ci)
    incl_t = same & (ri <= ci)

    sw = STRIP
    strips = [slice(c0, c0 + sw) for c0 in range(0, RW_WIDTH, sw)]

    def head_stat(x, cs):
        return _dot(x.astype(BF16), e1_ref[cs, :])

    def head_bcast(s, cs):
        return _mm(s, e2_ref[:, cs], pa=2)

    def head_sum(x):
        return _dot(x.astype(BF16), e12_ref[...])


    def prologue(sb):
        slab = shifted(sb, RW_OFF_XWA, LANES)
        lane = lax.broadcasted_iota(jnp.int32, slab.shape, 1)
        lhs = jnp.where(lane < RW_LORA_W, jnp.tanh(slab), slab).astype(BF16)
        sg = _sigmoid(shifted(sb, RW_OFF_XG, 2 * LANES)).astype(BF16)
        yield
        lws = [_dot(lhs, wwa_ref[:, cs]) for cs in strips]
        las = [_dot(lhs, wwa_ref[:, RW_WIDTH + cs.start:RW_WIDTH + cs.stop]) for cs in strips]
        for cs in strips:
            g_ref[sb, :, cs] = _dot(sg, g2_ref[:, cs])
        yield
        logws = [-DECAY_SCALE * _sigmoid(w0_ref[:, cs] + lw) for cs, lw in zip(strips, lws)]
        gcums = [_mm(lmask, logw, pb=2) for logw in logws]
        yield
        avs = [_sigmoid(a0_ref[:, cs] + la) for cs, la in zip(strips, las)]
        ks = [shifted(sb, RW_WIDTH + cs.start, sw) for cs in strips]
        kkfs = [k * kk_ref[:, cs] for cs, k in zip(strips, ks)]
        rinvs = [lax.rsqrt(jnp.maximum(head_stat(kkf * kkf, cs), 1e-24)) for cs, kkf in zip(strips, kkfs)]
        yield
        k2s = [k * (1.0 + (a - 1.0) * ka_ref[:, cs]) for cs, k, a in zip(strips, ks, avs)]
        rs = [shifted(sb, cs.start, sw) for cs in strips]
        bsums = [head_sum(r * k2 * rk_ref[:, cs]) for cs, r, k2 in zip(strips, rs, k2s)]
        kks = [kkf * head_bcast(rinv, cs) for cs, kkf, rinv in zip(strips, kkfs, rinvs)]
        yield
        for cs, bsum in zip(strips, bsums):
            v = shifted(sb, 2 * RW_WIDTH + cs.start, sw)
            bonus_ref[sb, :, cs] = bsum * v
            vt_ref[sb, cs, :] = v.T
            if row_major:
                vrm_ref[sb, :, cs] = v
        yield
        for cs, logw, gcum, a, kk, k2, r in zip(strips, logws, gcums, avs, kks, k2s, rs):
            beta = kk * a
            g3 = gcum.reshape(cps, c, sw)
            gtot = jnp.broadcast_to(g3[:, c - 1:c, :], g3.shape).reshape(m_rows, sw)
            e_inv = jnp.exp(-gcum)
            e_rev = jnp.exp(gtot - gcum)
            art_ref[sb, cs, 0:m_rows] = (-kk * jnp.exp(gcum - logw)).T
            art_ref[sb, cs, m_rows:2 * m_rows] = (r * jnp.exp(gcum)).T
            bk_ref[sb, 0:m_rows, cs] = beta * e_inv
            bk_ref[sb, m_rows:, cs] = k2 * e_inv
            kb_ref[sb, 0:m_rows, cs] = k2 * e_rev
            kb_ref[sb, m_rows:, cs] = beta * e_rev
            pc_ref[sb, :, cs] = jnp.exp(gtot)
            yield

    pw = 2 * hd
    cat = jnp.concatenate
    top = lax.broadcasted_iota(jnp.int32, (pw, 1), 0) < hd
    left = lax.broadcasted_iota(jnp.int32, (1, 2 * m_rows), 1) < m_rows

    def split_rows(x):
        return cat([jnp.where(top, x, 0.0), jnp.where(top, 0.0, x)], axis=1)

    def block_diag(xc):
        return cat([jnp.where(left, xc, 0.0), jnp.where(left, 0.0, xc)], axis=0)

    def blk(xt, u, rh, ch):
        col = (2 * u + ch) * m_rows
        return xt[rh * m_rows:(rh + 1) * m_rows, col:col + m_rows]

    n_pairs = RW_HEADS // 2
    pairs = range(n_pairs)
    los = [p * pw for p in pairs]

    def pair_phase(sb):
        for g0 in range(0, n_pairs, PAIR_LOCKSTEP):
            yield from pair_group(sb, pairs[g0:g0 + PAIR_LOCKSTEP], los[g0:g0 + PAIR_LOCKSTEP])

    def pair_group(sb, pairs, los):
        arts = [art_ref[sb, lo:lo + pw, :] for lo in los]
        xts = [_mm(bk_ref[sb, :, lo:lo + pw], split_rows(art), pp, pp) for lo, art in zip(los, arts)]
        yield
        npcs = [cat([jnp.where(strict_t, blk(xt, 0, 0, 0), 0.0), jnp.where(strict_t, blk(xt, 1, 0, 0), 0.0)], axis=1)
                for xt in xts]
        vas = [_mm(split_rows(vt_ref[sb, lo:lo + pw, :]),
                   cat([cat([jnp.where(strict_t, blk(xt, u, 1, 0), 0.0), jnp.where(incl_t, blk(xt, u, 1, 1), 0.0)],
                            axis=1) for u in range(2)], axis=0), pp, pp)
               for lo, xt in zip(los, xts)]
        yield
        zcs = [cat([cat([art[u * hd:(u + 1) * hd, 0:m_rows], va[u * hd:(u + 1) * hd, 0:m_rows]], axis=0)
                    for u in range(2)], axis=1) for art, va in zip(arts, vas)]
        for lvl in range(n_lvl):
            bds = [block_diag(npc) for npc in npcs]
            if lvl + 1 < n_lvl:
                ts = [_mm(cat([zc, npc], axis=0), bd, pp, pp) for zc, npc, bd in zip(zcs, npcs, bds)]
                zcs = [zc + t[0:2 * hd] for zc, t in zip(zcs, ts)]
                npcs = [t[2 * hd:] for t in ts]
            else:
                zcs = [zc + _mm(zc, bd, pp, pp) for zc, bd in zip(zcs, bds)]
            yield
        rycs = [cat([cat([art[u * hd:(u + 1) * hd, m_rows:], va[u * hd:(u + 1) * hd, m_rows:]], axis=0)
                     for u in range(2)], axis=1)
                + _mm(zc, block_diag(cat([jnp.where(incl_t, blk(xt, 0, 0, 1), 0.0),
                                          jnp.where(incl_t, blk(xt, 1, 0, 1), 0.0)], axis=1)), pp, pp)
                for art, va, zc, xt in zip(arts, vas, zcs, xts)]
        yield
        if row_major:
            for p, zc, ry in zip(pairs, zcs, rycs):
                zrt_ref[sb, p, 0:2 * m_rows] = zc.T
                zrt_ref[sb, p, 2 * m_rows:] = ry.T
            yield
            return
        zero = jnp.zeros((hd, 2 * m_rows), BF16)
        for p, zc, ry in zip(pairs, zcs, rycs):
            arbd_ref[sb, p, 0:hd, 0:2 * m_rows] = cat([zc[0:hd, 0:m_rows], ry[0:hd, 0:m_rows]], axis=1).astype(BF16)
            arbd_ref[sb, p, 0:hd, 2 * m_rows:] = zero
            arbd_ref[sb, p, hd:, 0:2 * m_rows] = zero
            arbd_ref[sb, p, hd:, 2 * m_rows:] = cat([zc[0:hd, m_rows:], ry[0:hd, m_rows:]], axis=1).astype(BF16)
            uy_ref[sb, p] = cat([zc[hd:, 0:m_rows], ry[hd:, 0:m_rows], zc[hd:, m_rows:], ry[hd:, m_rows:]], axis=1)
        yield

    sh_c = c.bit_length() - 1
    cps = m_rows // c
    cpb = tbs // c
    row_id2 = (lax.broadcasted_iota(jnp.int32, (2 * m_rows, pw), 0) & (m_rows - 1)) >> sh_c
    col_id = lax.broadcasted_iota(jnp.int32, (hd, m_rows), 1) >> sh_c
    lane_lo = lax.broadcasted_iota(jnp.int32, (1, pw), 1) < hd

    def chunk_step(sb, i):
        b = i // cpb
        r0 = i * c
        rmask2 = row_id2 == i
        cmask = col_id == i
        ss = [st_ref[b, p] for p in pairs]
        zss = [_mm(s, arbd_ref[sb, p], pp, pp) + uy_ref[sb, p] for p, s in zip(pairs, ss)]
        yield
        upd = []
        for lo, zs in zip(los, zss):
            kbz = jnp.where(rmask2, kb_ref[sb, :, lo:lo + pw], 0.0)
            rhs = cat([jnp.where(lane_lo, kbz, 0.0), jnp.where(lane_lo, 0.0, kbz)], axis=0)
            vu = cat([vt_ref[sb, lo:lo + hd, :], zs[:, 0:m_rows],
                      vt_ref[sb, lo + hd:lo + pw, :], zs[:, 2 * m_rows:3 * m_rows]], axis=1)
            upd.append(_mm(vu, rhs, pp, pp))
        yield
        for p, lo, s, zs, d in zip(pairs, los, ss, zss, upd):
            st_ref[b, p] = s * pc_ref[sb, pl.ds(r0, 1), lo:lo + pw] + d
            for u in range(2):
                rows = slice((2 * p + u) * hd, (2 * p + u + 1) * hd)
                yt_ref[sb, rows, :] = jnp.where(cmask, zs[:, (2 * u + 1) * m_rows:(2 * u + 2) * m_rows],
                                               yt_ref[sb, rows, :])
        yield

    def swap_halves(x):
        return pltpu.roll(x, hd, axis=1)

    def sequence_steps(sb, p, lo):
        gsz = LANES // 4
        per_grp = gsz // c
        groups = range(m_rows // gsz)
        k_seq = (lax.broadcasted_iota(jnp.int32, (LANES, 1), 0) & (gsz - 1)) >> sh_c
        zt = zrt_ref[sb, p, 0:2 * m_rows]
        rt = zrt_ref[sb, p, 2 * m_rows:]
        a0 = jnp.where(lane_lo, zt[0:m_rows], 0.0)
        a1 = jnp.where(lane_lo, 0.0, swap_halves(zt[m_rows:]))
        r0 = jnp.where(lane_lo, rt[0:m_rows], 0.0)
        r1 = jnp.where(lane_lo, 0.0, swap_halves(rt[m_rows:]))
        res = []
        for b in range(nb):
            rb = slice(b * c, (b + 1) * c)
            rb1 = slice(m_rows + b * c, m_rows + (b + 1) * c)
            s = st_ref[b, p]
            lhs = cat([a0[rb], a1[rb], r0[rb], r1[rb]], axis=0)
            add = cat([zt[rb], zt[rb1], rt[rb], rt[rb1]], axis=0)
            res.append(_mm(lhs, cat([s, s], axis=0), pp, pp, NT) + add)
        yield
        for b, x in enumerate(res):
            yrm_ref[sb, b * c:(b + 1) * c, lo:lo + pw] = jnp.where(
                lane_lo, swap_halves(x[2 * c:3 * c]), x[3 * c:4 * c])
        vp = vrm_ref[sb, :, lo:lo + pw]
        vs = swap_halves(vp)
        vuts = []
        for q in groups:
            rq = slice(q * gsz, (q + 1) * gsz)
            members = range(q * per_grp, (q + 1) * per_grp)
            u0 = cat([res[b][0:c] for b in members], axis=0)
            u1 = cat([res[b][c:2 * c] for b in members], axis=0)
            vuts.append(cat([vs[rq], u0, vp[rq], u1], axis=0).T[hd:, :])
        yield
        ds = []
        for q, vut in zip(groups, vuts):
            kq = kb_ref[sb, q * gsz:(q + 1) * gsz, lo:lo + pw]
            bq = kb_ref[sb, m_rows + q * gsz:m_rows + (q + 1) * gsz, lo:lo + pw]
            rhs = cat([jnp.where(lane_lo, kq, 0.0), jnp.where(lane_lo, bq, 0.0),
                       jnp.where(lane_lo, 0.0, kq), jnp.where(lane_lo, 0.0, bq)], axis=0)
            ds.append(_mm(vut, cat([jnp.where(k_seq == n, rhs, 0.0) for n in range(per_grp)], axis=1), pp, pp))
        yield
        for q, d in zip(groups, ds):
            for n in range(per_grp):
                b = q * per_grp + n
                st_ref[b, p] = st_ref[b, p] * pc_ref[sb, b * c:b * c + 1, lo:lo + pw] + d[:, n * pw:(n + 1) * pw]
        yield

    inv_n = 1.0 / hd

    def epilogue(sb):
        ys = [yrm_ref[sb, :, cs] if row_major else yt_ref[sb, cs, :].T for cs in strips]
        ycs = [y - head_sum(y) * inv_n for y in ys]
        yield
        rstds = [lax.rsqrt(head_stat(yc * yc, cs) * inv_n + RW_GN_EPS) for cs, yc in zip(strips, ycs)]
        yield
        for cs, yc, rstd in zip(strips, ycs, rstds):
            yn = yc * head_bcast(rstd, cs) * lnw_ref[:, cs] + lnb_ref[:, cs]
            o_ref[sb * m_rows:(sb + 1) * m_rows, cs] = (
                (yn + bonus_ref[sb, :, cs]) * g_ref[sb, :, cs]).astype(o_ref.dtype)
        yield

    def run(*gens):
        live = list(gens)
        while live:
            live = [g for g in live if next(g, live) is not live]

    def tail(sb):
        for i in range(cps):
            yield from chunk_step(sb, i)
        yield from epilogue(sb)

    if row_major:
        run(prologue(0))
        run(pair_phase(0))
        run(*[sequence_steps(0, p, lo) for p, lo in zip(pairs, los)])
        run(epilogue(0))
    elif nsub == 1:
        run(prologue(0))
        run(pair_phase(0))

        together = SAMPLE_CHUNKS_TOGETHER if cpb == 1 and cps % SAMPLE_CHUNKS_TOGETHER == 0 else 1

        def chunk_body(i, carry):
            run(*[chunk_step(0, i * together + u) for u in range(together)])
            return carry

        lax.fori_loop(0, cps // together, chunk_body, 0)
        run(epilogue(0))
    else:
        run(prologue(0))
        for sb in range(nsub):
            run(pair_phase(sb), *([prologue(sb + 1)] if sb + 1 < nsub else []), *([tail(sb - 1)] if sb else []))
        run(tail(nsub - 1))

    if nb == 1:
        last = p_ref[tb - 1:tb, :].astype(F32).reshape(1, 1, GROUP_PAD)
    else:
        last = p_ref[...].astype(F32).reshape(nb, tb, GROUP_PAD)[:, tb - 1:tb, :]
    carry_ref[...] = last
    shout_ref[...] = last

    @pl.when(j == pl.num_programs(1) - 1)
    def _():
        for p in pairs:
            sp = st_ref[:, p]
            sout_ref[:, 2 * p] = sp[:, :, 0:hd]
            sout_ref[:, 2 * p + 1] = sp[:, :, hd:]


def _rwkv_row_major(nb, tb, c):
    return nb * tb == ROWS and tb == c == SUBLANES


def _rwkv(p, shift_prev, s0, wts, nb, tb, c, pp):
    bsz = s0.shape[0]
    seq = p.shape[0] // bsz
    m_rows = ROWS
    nsub = nb * tb // m_rows
    assert nsub * m_rows == nb * tb and (nsub == 1 or nb == 1)
    hd = RW_HEAD_DIM
    rm = _rwkv_row_major(nb, tb, c)
    small = (1, 1, SUBLANES, LANES)
    const = lambda shape: pl.BlockSpec(shape, lambda i, j: (0,) * len(shape))
    kern = functools.partial(_rwkv_kernel, nb=nb, tb=tb, c=c, pp=pp)
    return pl.pallas_call(
        kern,
        grid=(bsz // nb, seq // tb),
        in_specs=[
            pl.BlockSpec((nb * tb, GROUP_PAD), lambda i, j: (i * (seq // tb) + j, 0)),
            pl.BlockSpec((nb, 1, GROUP_PAD), lambda i, j: (i, 0, 0)),
            pl.BlockSpec((nb, RW_HEADS, hd, hd), lambda i, j: (i, 0, 0, 0)),
            const((1, GROUP_PAD)),
            const((LANES, 2 * RW_WIDTH)),
            const((1, RW_WIDTH)),
            const((1, RW_WIDTH)),
            const((2 * LANES, RW_WIDTH)),
            const((1, RW_WIDTH)),
            const((1, RW_WIDTH)),
            const((1, RW_WIDTH)),
            const((1, RW_WIDTH)),
            const((1, RW_WIDTH)),
            const((RW_WIDTH, LANES)),
            const((LANES, RW_WIDTH)),
            const((STRIP, STRIP)),
        ],
        out_specs=[
            pl.BlockSpec((nb * tb, RW_WIDTH), lambda i, j: (i * (seq // tb) + j, 0)),
            pl.BlockSpec((nb, 1, GROUP_PAD), lambda i, j: (i, 0, 0)),
            pl.BlockSpec((nb, RW_HEADS, hd, hd), lambda i, j: (i, 0, 0, 0)),
        ],
        out_shape=[
            jax.ShapeDtypeStruct((bsz * seq, RW_WIDTH), BF16),
            jax.ShapeDtypeStruct((bsz, 1, GROUP_PAD), F32),
            jax.ShapeDtypeStruct((bsz, RW_HEADS, hd, hd), F32),
        ],
        scratch_shapes=[
            pltpu.VMEM((nb, 1, GROUP_PAD), F32),
            pltpu.VMEM((nb, RW_HEADS // 2, hd, 2 * hd), F32),
            pltpu.VMEM((nsub, 2 * m_rows, RW_WIDTH), F32),
            pltpu.VMEM((nsub, 2 * m_rows, RW_WIDTH), F32),
            pltpu.VMEM((nsub, m_rows, RW_WIDTH), F32),
            pltpu.VMEM((nsub, RW_WIDTH, 2 * m_rows), F32),
            pltpu.VMEM((nsub, RW_WIDTH, m_rows), F32),
            pltpu.VMEM((nsub, RW_HEADS // 2, 2 * hd, 4 * m_rows), BF16),
            pltpu.VMEM((nsub, RW_HEADS // 2, hd, 4 * m_rows), F32),
            pltpu.VMEM((nsub, RW_WIDTH, m_rows), F32),
            pltpu.VMEM((nsub, m_rows, RW_WIDTH), F32),
            pltpu.VMEM((nsub, m_rows, RW_WIDTH), F32),
            pltpu.VMEM((nsub, RW_HEADS // 2, 4 * m_rows, 2 * hd) if rm else small, F32),
            pltpu.VMEM((nsub, m_rows, RW_WIDTH) if rm else small[1:], F32),
            pltpu.VMEM((nsub, m_rows, RW_WIDTH) if rm else small[1:], F32),
        ],
        compiler_params=pltpu.CompilerParams(
            dimension_semantics=("parallel", "arbitrary"), vmem_limit_bytes=VMEM_LIMIT),
        name="rwkv7",
    )(p, shift_prev, s0, *wts)


def _gla_kernel(p_ref, s0_ref, gw2_ref, gb_ref, nw_ref, o_ref, st_ref,
                acc_ref, qd_ref, v_ref, kt_ref, et_ref, *, nb, tb, c, pp):
    j = pl.program_id(1)
    m_rows = nb * tb
    dk, dv = GLA_DK, GLA_DV

    @pl.when(j == 0)
    def _():
        st_ref[...] = s0_ref[...]

    p = p_ref[...].astype(F32)
    q = p[:, 0:GLA_KEY_WIDTH] * (dk ** -0.5)
    k = p[:, GLA_OFF_K:GLA_OFF_K + GLA_KEY_WIDTH]
    v = p[:, GLA_OFF_V:GLA_OFF_V + GLA_WIDTH]
    xs = p[:, GLA_OFF_GATE:GLA_OFF_GATE + LANES].astype(BF16)
    gout = p[:, GLA_OFF_GOUT:GLA_OFF_GOUT + GLA_WIDTH]
    gk = -_softplus(-(_dot(xs, gw2_ref[...]) + gb_ref[...])) / GLA_GATE_NORM

    same, ri, ci = _chunk_masks(m_rows, c)
    lmask = jnp.where(same & (ci <= ri), 1.0, 0.0).astype(BF16)
    causal = same & (ci <= ri)
    gcum = _mm(lmask, gk, pb=3)
    g3 = gcum.reshape(m_rows // c, c, GLA_KEY_WIDTH)
    gtot = jnp.broadcast_to(g3[:, c - 1:c, :], g3.shape).reshape(m_rows, GLA_KEY_WIDTH)
    qd = q * jnp.exp(gcum)
    kinv = k * jnp.exp(-gcum)
    qd_ref[...] = qd
    v_ref[...] = v
    kt_ref[...] = (k * jnp.exp(gtot - gcum)).T
    et_ref[...] = jnp.exp(gtot).T

    heads = range(GLA_HEADS)
    kq = [slice(h * dk, (h + 1) * dk) for h in heads]
    vq = [slice(h * dv, (h + 1) * dv) for h in heads]
    scores = [jnp.where(causal, _mm(qd[:, ks], kinv[:, ks], pp, pp, NT), 0.0) for ks in kq]
    intra = [_mm(a, v[:, vs], pp, pp) for a, vs in zip(scores, vq)]
    for vs, o in zip(vq, intra):
        acc_ref[:, vs] = o

    sh_c = c.bit_length() - 1
    cpb = tb // c
    span = min(m_rows, LANES)
    row_id = lax.broadcasted_iota(jnp.int32, (span, dv), 0) >> sh_c
    for i in range(m_rows // c):
        b = i // cpb
        r0 = i * c
        t0 = (r0 // span) * span
        rmask = row_id == (r0 - t0) // c
        states = [st_ref[b, h] for h in heads]
        inter = [_mm(qd_ref[r0:r0 + c, ks], s, pp, pp) for ks, s in zip(kq, states)]
        upd = [_mm(kt_ref[ks, t0:t0 + span], jnp.where(rmask, v_ref[t0:t0 + span, vs], 0.0), pp, pp)
               for ks, vs in zip(kq, vq)]
        for h, ks, vs, s, oi, d in zip(heads, kq, vq, states, inter, upd):
            acc_ref[r0:r0 + c, vs] += oi
            st_ref[b, h] = s * et_ref[ks, r0:r0 + 1] + d

    for vs in vq:
        o = acc_ref[:, vs]
        on = o * lax.rsqrt(jnp.mean(o * o, axis=-1, keepdims=True) + HEAD_NORM_EPS) * nw_ref[...]
        gh = gout[:, vs]
        o_ref[:, vs] = (on * (gh * _sigmoid(gh))).astype(o_ref.dtype)


def _gla(p, s0, wts, nb, tb, c, pp):
    bsz = s0.shape[0]
    seq = p.shape[0] // bsz
    m_rows = nb * tb
    const = lambda shape: pl.BlockSpec(shape, lambda i, j: (0,) * len(shape))
    kern = functools.partial(_gla_kernel, nb=nb, tb=tb, c=c, pp=pp)
    return pl.pallas_call(
        kern,
        grid=(bsz // nb, seq // tb),
        in_specs=[
            pl.BlockSpec((nb * tb, GROUP_PAD), lambda i, j: (i * (seq // tb) + j, 1)),
            pl.BlockSpec((nb, GLA_HEADS, GLA_DK, GLA_DV), lambda i, j: (i, 0, 0, 0)),
            const((LANES, GLA_KEY_WIDTH)),
            const((1, GLA_KEY_WIDTH)),
            const((1, GLA_DV)),
        ],
        out_specs=[
            pl.BlockSpec((nb * tb, GLA_WIDTH), lambda i, j: (i * (seq // tb) + j, 0)),
            pl.BlockSpec((nb, GLA_HEADS, GLA_DK, GLA_DV), lambda i, j: (i, 0, 0, 0)),
        ],
        out_shape=[
            jax.ShapeDtypeStruct((bsz * seq, GLA_WIDTH), BF16),
            jax.ShapeDtypeStruct((bsz, GLA_HEADS, GLA_DK, GLA_DV), F32),
        ],
        scratch_shapes=[
            pltpu.VMEM((m_rows, GLA_WIDTH), F32),
            pltpu.VMEM((m_rows, GLA_KEY_WIDTH), F32),
            pltpu.VMEM((m_rows, GLA_WIDTH), F32),
            pltpu.VMEM((GLA_KEY_WIDTH, m_rows), F32),
            pltpu.VMEM((GLA_KEY_WIDTH, m_rows), F32),
        ],
        compiler_params=pltpu.CompilerParams(
            dimension_semantics=("parallel", "arbitrary"), vmem_limit_bytes=VMEM_LIMIT),
        name="gla",
    )(p, s0, *wts)


def _post_kernel(x_ref, orw_ref, ogla_ref, wo_ref, g2_ref, wu_ref, wd_ref, gf_ref, o_ref, h_ref):
    jf = pl.program_id(1)

    @pl.when(jf == 0)
    def _():
        x1 = (x_ref[...]
              + _dot(orw_ref[...], wo_ref[0:RW_WIDTH, :])
              + _dot(ogla_ref[...], wo_ref[RW_WIDTH:, :]))
        o_ref[...] = x1
        ms = jnp.mean(x1 * x1, axis=-1, keepdims=True)
        h_ref[...] = (x1 * lax.rsqrt(ms + NORM_EPS) * g2_ref[...]).astype(BF16)

    u = jnp.maximum(_dot(h_ref[...], wu_ref[...]), 0.0)
    o_ref[...] += _dot((u * u).astype(BF16), wd_ref[...])

    @pl.when(jf == pl.num_programs(1) - 1)
    def _():
        x2 = o_ref[...]
        ms = jnp.mean(x2 * x2, axis=-1, keepdims=True)
        o_ref[...] = x2 * lax.rsqrt(ms + NORM_EPS) * gf_ref[...]


def _post(x2d, o_rw, o_gla, w_out, g2, w_up, w_down, gf, tm, tf):
    t = x2d.shape[0]
    return pl.pallas_call(
        _post_kernel,
        grid=(t // tm, D_FF // tf),
        in_specs=[
            pl.BlockSpec((tm, D_MODEL), lambda i, j: (i, 0)),
            pl.BlockSpec((tm, RW_WIDTH), lambda i, j: (i, 0)),
            pl.BlockSpec((tm, GLA_WIDTH), lambda i, j: (i, 0)),
            pl.BlockSpec((D_MODEL, D_MODEL), lambda i, j: (0, 0), pipeline_mode=pl.Buffered(1)),
            pl.BlockSpec((1, D_MODEL), lambda i, j: (0, 0)),
            pl.BlockSpec((D_MODEL, tf), lambda i, j: (0, j)),
            pl.BlockSpec((tf, D_MODEL), lambda i, j: (j, 0)),
            pl.BlockSpec((1, D_MODEL), lambda i, j: (0, 0)),
        ],
        out_specs=pl.BlockSpec((tm, D_MODEL), lambda i, j: (i, 0)),
        out_shape=jax.ShapeDtypeStruct((t, D_MODEL), F32),
        scratch_shapes=[pltpu.VMEM((tm, D_MODEL), BF16)],
        compiler_params=pltpu.CompilerParams(
            dimension_semantics=("parallel", "arbitrary"), vmem_limit_bytes=VMEM_LIMIT),
        name="post",
    )(x2d, o_rw, o_gla, w_out, g2, w_up, w_down, gf)


def _pad_cols(w, n):
    return jnp.pad(w, ((0, 0), (0, n - w.shape[1])))


def _prep_weights(w_in, rw_mu, rw_w0, rw_w2, rw_a0, rw_a2, rw_g2, rw_k_k, rw_k_a, rw_r_k, rw_ln_w, rw_ln_b,
                  gla_gw2, gla_gb, gla_norm_w):
    w_p = jnp.swapaxes(w_in, 0, 1).astype(BF16)

    row = lambda x: x.reshape(1, -1).astype(F32)
    mu = _pad_cols(row(rw_mu), GROUP_PAD)
    wwa = jnp.zeros((LANES, 2 * RW_WIDTH), F32)
    wwa = wwa.at[0:RW_LORA_W, 0:RW_WIDTH].set(rw_w2)
    wwa = wwa.at[RW_LORA_W:RW_LORA_W + RW_LORA_A, RW_WIDTH:].set(rw_a2)
    g2p = jnp.pad(rw_g2, ((0, 2 * LANES - RW_LORA_G), (0, 0)))
    head_of_col = jnp.arange(RW_WIDTH) // RW_HEAD_DIM
    e1 = (head_of_col[:, None] == jnp.arange(LANES)[None, :]).astype(BF16)
    rw_wts = (mu, wwa.astype(BF16), row(rw_w0), row(rw_a0), g2p.astype(BF16), row(rw_k_k), row(rw_k_a),
              row(rw_r_k), row(rw_ln_w), row(rw_ln_b), e1, e1.T,
              (head_of_col[:STRIP, None] == head_of_col[None, :STRIP]).astype(BF16))
    gw2p = jnp.pad(gla_gw2, ((0, LANES - GLA_GATE_RANK), (0, 0))).astype(BF16)
    gla_wts = (gw2p, row(gla_gb), row(gla_norm_w))
    return w_p, rw_wts, gla_wts


RW_PIECES = 1
GLA_PIECES = 1
PAIR_LOCKSTEP = 8
SAMPLE_CHUNKS_TOGETHER = 16
RW_STEP_ROWS = 4 * ROWS
GLA_STEP_ROWS = 2 * ROWS
STRIP = 256


def _trunk(x, shift, s_rw, s_gla, norm1_g, w_p, rw_wts, gla_wts, w_out, norm2_g, w_up, w_down, norm_f_g,
           nb, tb, c_rw, c_gla, tm):
    bsz, seq, _ = x.shape
    t = bsz * seq
    x2d = x.reshape(t, D_MODEL)
    proj = _inproj(x2d, norm1_g.reshape(1, -1), w_p, tm, INPROJ_TN)
    shift_p = _pad_cols(shift, GROUP_PAD).reshape(bsz, 1, GROUP_PAD)
    rw_tb = RW_STEP_ROWS if nb == 1 and seq % RW_STEP_ROWS == 0 else tb
    o_rw, sh_new, s_rw_new = _rwkv(proj, shift_p, s_rw, rw_wts, nb, rw_tb, c_rw, RW_PIECES)
    gla_tb = GLA_STEP_ROWS if nb == 1 and seq % GLA_STEP_ROWS == 0 else tb
    o_gla, s_gla_new = _gla(proj, s_gla, gla_wts, nb, gla_tb, c_gla, GLA_PIECES)
    y = _post(x2d, o_rw, o_gla, w_out, norm2_g.reshape(1, -1),
              w_up, w_down, norm_f_g.reshape(1, -1), POST_TM, POST_TF)
    return (y.reshape(bsz, seq, D_MODEL), sh_new[:, 0, :RW_PROJ][None], s_rw_new[None], s_gla_new[None])


def kernel(x_prompt, x_sample, state_rwkv_shift, state_rwkv_wkv, state_gla, norm1_g, w_in, rw_mu, rw_w0,
           rw_w2, rw_a0, rw_a2, rw_g2, rw_k_k, rw_k_a, rw_r_k, rw_ln_w, rw_ln_b, gla_gw2, gla_gb, gla_norm_w,
           w_out, norm2_g, w_up, w_down, norm_f_g):
    w_p, rw_wts, gla_wts = _prep_weights(
        w_in[0], rw_mu[0], rw_w0[0], rw_w2[0], rw_a0[0], rw_a2[0], rw_g2[0], rw_k_k[0], rw_k_a[0],
        rw_r_k[0].reshape(-1), rw_ln_w[0], rw_ln_b[0], gla_gw2[0], gla_gb[0], gla_norm_w[0])
    shared = (norm1_g[0], w_p, rw_wts, gla_wts, w_out[0].astype(BF16), norm2_g[0], w_up[0].astype(BF16),
              w_down[0].astype(BF16), norm_f_g)

    bp, lp, _ = x_prompt.shape
    bs, ls, _ = x_sample.shape
    dt = x_prompt.dtype
    out_p = _trunk(x_prompt, jnp.zeros((bp, RW_PROJ), dt),
                   jnp.zeros((bp, RW_HEADS, RW_HEAD_DIM, RW_HEAD_DIM), dt),
                   jnp.zeros((bp, GLA_HEADS, GLA_DK, GLA_DV), dt), *shared,
                   nb=1, tb=ROWS, c_rw=RW_CHUNK, c_gla=GLA_CHUNK, tm=INPROJ_TM)
    out_s = _trunk(x_sample, state_rwkv_shift[0], state_rwkv_wkv[0], state_gla[0], *shared,
                   nb=ROWS // ls, tb=ls, c_rw=ls, c_gla=ls, tm=INPROJ_TM)
    return (out_p[0], out_s[0], out_p[1], out_p[2], out_p[3], out_s[1], out_s[2], out_s[3])
```

```python
import functools

import jax
import jax.numpy as jnp
from jax import lax
from jax.experimental import pallas as pl
from jax.experimental.pallas import tpu as pltpu

F32 = jnp.float32
BF16 = jnp.bfloat16

D_MODEL = 2048
RW_WIDTH = 1024
RW_HEADS = 16
RW_HEAD_DIM = 64
RW_LORA_W = 64
RW_LORA_A = 64
RW_LORA_G = 160
RW_PROJ = 3 * RW_WIDTH + RW_LORA_W + RW_LORA_A + RW_LORA_G
RW_GN_EPS = 64e-5
GLA_WIDTH = 1024
GLA_HEADS = 4
GLA_KEY_WIDTH = 512
GLA_DK = 128
GLA_DV = 256
GLA_GATE_RANK = 16
GLA_GATE_NORM = 16.0
GLA_CHUNK = 64
GLA_PROJ = 2 * GLA_KEY_WIDTH + GLA_WIDTH + GLA_GATE_RANK + GLA_WIDTH
D_FF = 4 * D_MODEL
NORM_EPS = 1e-6
HEAD_NORM_EPS = 1e-5
DECAY_SCALE = 0.6065306597126334

LANES = 128
SUBLANES = 8

GROUP_PAD = 3456
RW_OFF_XWA = 3 * RW_WIDTH
RW_OFF_XG = RW_OFF_XWA + LANES
GLA_OFF_K = GLA_KEY_WIDTH
GLA_OFF_V = 2 * GLA_KEY_WIDTH
GLA_OFF_GATE = GLA_OFF_V + GLA_WIDTH
GLA_OFF_GOUT = GLA_OFF_GATE + LANES
NP = 2 * GROUP_PAD

ROWS = 128
MXU_WIDTH = 256
INPROJ_TM = 1024
INPROJ_TN = 9 * MXU_WIDTH
RW_CHUNK = 64
PROJ_DTYPE = BF16
POST_TM = 512
POST_TF = 1024
VMEM_LIMIT = 56 * 1024 * 1024

NN = (((1,), (0,)), ((), ()))
NT = (((1,), (1,)), ((), ()))


def _dot(a, b, dims=NN):
    return lax.dot_general(a, b, dims, preferred_element_type=F32)


def _parts(x, n):
    if x.dtype == BF16:
        return [x]
    out = []
    rem = x
    for i in range(n):
        h = rem.astype(BF16)
        out.append(h)
        if i + 1 < n:
            rem = rem - h.astype(F32)
    return out


def _mm(a, b, pa=1, pb=1, dims=NN):
    pa = 1 if a.dtype == BF16 else pa
    pb = 1 if b.dtype == BF16 else pb
    aa = _parts(a, pa)
    bb = _parts(b, pb)
    n = max(pa, pb)
    if dims == NN and min(pa, pb) == 1 and n * a.shape[1] <= MXU_WIDTH:
        return _dot(jnp.concatenate(aa * (n // pa), axis=1), jnp.concatenate(bb * (n // pb), axis=0))
    acc = None
    for i in range(pa):
        for j in range(pb):
            if i + j < n:
                t = _dot(aa[i], bb[j], dims)
                acc = t if acc is None else acc + t
    return acc


def _softplus(x):
    return jnp.maximum(x, 0.0) + jnp.log(1.0 + jnp.exp(-jnp.abs(x)))


def _sigmoid(x):
    return 0.5 * jnp.tanh(0.5 * x) + 0.5


def _chunk_masks(m, c):
    sh = c.bit_length() - 1
    ri = lax.broadcasted_iota(jnp.int32, (m, m), 0)
    ci = lax.broadcasted_iota(jnp.int32, (m, m), 1)
    same = (ri >> sh) == (ci >> sh)
    return same, ri, ci


_W_SEGMENTS = (
    (0, RW_PROJ, 0),
    (RW_PROJ, GLA_OFF_GATE + GLA_GATE_RANK, GROUP_PAD),
    (RW_PROJ + GLA_OFF_GATE + GLA_GATE_RANK, GLA_WIDTH, GROUP_PAD + GLA_OFF_GOUT),
)


def _weight_windows(tn):
    blocks = []
    for lo in range(0, NP, tn):
        wins = []
        for src, ln, dst in _W_SEGMENTS:
            a, b = max(dst, lo), min(dst + ln, lo + tn)
            if a < b:
                wins.append((src + a - dst, b - a, a - lo))
        blocks.append(tuple(wins))
    return tuple(blocks)


def _inproj_kernel(x_ref, g_ref, w_hbm, o_ref, h_ref, wbuf, sem, *, windows):
    i, j = pl.program_id(0), pl.program_id(1)
    nj = len(windows)
    step = i * nj + j
    n_steps = pl.num_programs(0) * nj
    slot = lax.rem(step, 2)

    def copies(jb, sl):
        return [pltpu.make_async_copy(w_hbm.at[pl.ds(src, ln), :], wbuf.at[sl, pl.ds(dst, ln), :], sem.at[sl, k])
                for k, (src, ln, dst) in enumerate(windows[jb])]

    @pl.when(step == 0)
    def _():
        for sl in range(min(2, nj)):
            covered = sorted((dst, dst + ln) for _, ln, dst in windows[sl])
            edges = [0] + [e for span in covered for e in span] + [wbuf.shape[1]]
            for a, b in zip(edges[0::2], edges[1::2]):
                if a < b:
                    wbuf[sl, a:b, :] = jnp.zeros((b - a, wbuf.shape[2]), wbuf.dtype)
        for cp in copies(0, 0):
            cp.start()

    for jb in range(nj):
        @pl.when(j == jb)
        def _(jb=jb):
            @pl.when(step + 1 < n_steps)
            def _():
                for cp in copies((jb + 1) % nj, 1 - slot):
                    cp.start()

            for cp in copies(jb, slot):
                cp.wait()

    @pl.when(j == 0)
    def _():
        x = x_ref[...]
        ms = jnp.mean(x * x, axis=-1, keepdims=True)
        h_ref[...] = (x * lax.rsqrt(ms + NORM_EPS) * g_ref[...]).astype(BF16)

    o_ref[...] = _dot(h_ref[...], wbuf[slot], NT).astype(o_ref.dtype)


def _inproj(x2d, g, w_t, tm, tn):
    t = x2d.shape[0]
    windows = _weight_windows(tn)
    return pl.pallas_call(
        functools.partial(_inproj_kernel, windows=windows),
        grid=(t // tm, NP // tn),
        in_specs=[
            pl.BlockSpec((tm, D_MODEL), lambda i, j: (i, 0)),
            pl.BlockSpec((1, D_MODEL), lambda i, j: (0, 0)),
            pl.BlockSpec(memory_space=pltpu.HBM),
        ],
        out_specs=pl.BlockSpec((tm, tn), lambda i, j: (i, j)),
        out_shape=jax.ShapeDtypeStruct((t, NP), PROJ_DTYPE),
        scratch_shapes=[
            pltpu.VMEM((tm, D_MODEL), BF16),
            pltpu.VMEM((2, tn, D_MODEL), BF16),
            pltpu.SemaphoreType.DMA((2, max(len(w) for w in windows))),
        ],
        compiler_params=pltpu.CompilerParams(
            dimension_semantics=("arbitrary", "arbitrary"), vmem_limit_bytes=VMEM_LIMIT),
        name="inproj",
    )(x2d, g, w_t)


def _rwkv_kernel(p_ref, sh_ref, s0_ref, mu_ref, wwa_ref, w0_ref, a0_ref, g2_ref, kk_ref, ka_ref,
                 rk_ref, lnw_ref, lnb_ref, e1_ref, e2_ref, e12_ref,
                 o_ref, shout_ref, sout_ref,
                 carry_ref, st_ref, bk_ref, kb_ref, pc_ref, art_ref, vt_ref, arbd_ref, uy_ref, yt_ref,
                 g_ref, bonus_ref, zrt_ref, vrm_ref, yrm_ref, *, nb, tb, c, pp):
    j = pl.program_id(1)
    m_rows = ROWS
    nsub = nb * tb // m_rows
    tbs = tb // nsub
    n_lvl = c.bit_length() - 1
    hd = RW_HEAD_DIM
    row_major = _rwkv_row_major(nb, tb, c)

    @pl.when(j == 0)
    def _():
        for p in range(RW_HEADS // 2):
            st_ref[:, p] = jnp.concatenate([s0_ref[:, 2 * p], s0_ref[:, 2 * p + 1]], axis=-1)
        carry_ref[...] = sh_ref[...]

    if not row_major:
        yt_ref[...] = jnp.zeros_like(yt_ref)

    def shifted(sb, c0, w):
        r0 = sb * m_rows
        pc3 = p_ref[r0:r0 + m_rows, c0:c0 + w].astype(F32).reshape(nb, tbs, w)
        if sb == 0:
            before = carry_ref[:, :, c0:c0 + w]
        else:
            before = p_ref[r0 - 1:r0, c0:c0 + w].astype(F32).reshape(1, 1, w)
        t3 = lax.broadcasted_iota(jnp.int32, pc3.shape, 1)
        prev3 = jnp.where(t3 == 0, before, pltpu.roll(pc3, 1, axis=1))
        return (pc3 + (prev3 - pc3) * mu_ref[:, c0:c0 + w]).reshape(m_rows, w)

    same, ri, ci = _chunk_masks(m_rows, c)
    lmask = jnp.where(same & (ci <= ri), 1.0, 0.0).astype(BF16)
    strict_t = same & (ri < ci)
    incl_t = same & (ri <= ci)

    sw = STRIP
    strips = [slice(c0, c0 + sw) for c0 in range(0, RW_WIDTH, sw)]

    def head_stat(x, cs):
        return _dot(x.astype(BF16), e1_ref[cs, :])

    def head_bcast(s, cs):
        return _mm(s, e2_ref[:, cs], pa=2)

    def head_sum(x):
        return _dot(x.astype(BF16), e12_ref[...])


    def prologue(sb, strips=strips):
        slab = shifted(sb, RW_OFF_XWA, LANES)
        lane = lax.broadcasted_iota(jnp.int32, slab.shape, 1)
        lhs = jnp.where(lane < RW_LORA_W, jnp.tanh(slab), slab).astype(BF16)
        sg = _sigmoid(shifted(sb, RW_OFF_XG, 2 * LANES)).astype(BF16)
        yield
        lws = [_dot(lhs, wwa_ref[:, cs]) for cs in strips]
        las = [_dot(lhs, wwa_ref[:, RW_WIDTH + cs.start:RW_WIDTH + cs.stop]) for cs in strips]
        for cs in strips:
            g_ref[sb, :, cs] = _dot(sg, g2_ref[:, cs])
        yield
        logws = [-DECAY_SCALE * _sigmoid(w0_ref[:, cs] + lw) for cs, lw in zip(strips, lws)]
        gcums = [_mm(lmask, logw, pb=2) for logw in logws]
        yield
        avs = [_sigmoid(a0_ref[:, cs] + la) for cs, la in zip(strips, las)]
        ks = [shifted(sb, RW_WIDTH + cs.start, sw) for cs in strips]
        kkfs = [k * kk_ref[:, cs] for cs, k in zip(strips, ks)]
        rinvs = [lax.rsqrt(jnp.maximum(head_stat(kkf * kkf, cs), 1e-24)) for cs, kkf in zip(strips, kkfs)]
        yield
        k2s = [k * (1.0 + (a - 1.0) * ka_ref[:, cs]) for cs, k, a in zip(strips, ks, avs)]
        rs = [shifted(sb, cs.start, sw) for cs in strips]
        bsums = [head_sum(r * k2 * rk_ref[:, cs]) for cs, r, k2 in zip(strips, rs, k2s)]
        kks = [kkf * head_bcast(rinv, cs) for cs, kkf, rinv in zip(strips, kkfs, rinvs)]
        yield
        for cs, bsum in zip(strips, bsums):
            v = shifted(sb, 2 * RW_WIDTH + cs.start, sw)
            bonus_ref[sb, :, cs] = bsum * v
            vt_ref[sb, cs, :] = v.T
            if row_major:
                vrm_ref[sb, :, cs] = v
        yield
        for cs, logw, gcum, a, kk, k2, r in zip(strips, logws, gcums, avs, kks, k2s, rs):
            beta = kk * a
            g3 = gcum.reshape(cps, c, sw)
            gtot = jnp.broadcast_to(g3[:, c - 1:c, :], g3.shape).reshape(m_rows, sw)
            e_inv = jnp.exp(-gcum)
            e_rev = jnp.exp(gtot - gcum)
            art_ref[sb, cs, 0:m_rows] = (-kk * jnp.exp(gcum - logw)).T
            art_ref[sb, cs, m_rows:2 * m_rows] = (r * jnp.exp(gcum)).T
            bk_ref[sb, 0:m_rows, cs] = beta * e_inv
            bk_ref[sb, m_rows:, cs] = k2 * e_inv
            kb_ref[sb, 0:m_rows, cs] = k2 * e_rev
            kb_ref[sb, m_rows:, cs] = beta * e_rev
            pc_ref[sb, :, cs] = jnp.exp(gtot)
            yield

    pw = 2 * hd
    cat = jnp.concatenate
    top = lax.broadcasted_iota(jnp.int32, (pw, 1), 0) < hd
    left = lax.broadcasted_iota(jnp.int32, (1, 2 * m_rows), 1) < m_rows

    def split_rows(x):
        return cat([jnp.where(top, x, 0.0), jnp.where(top, 0.0, x)], axis=1)

    def block_diag(xc):
        return cat([jnp.where(left, xc, 0.0), jnp.where(left, 0.0, xc)], axis=0)

    def blk(xt, u, rh, ch):
        col = (2 * u + ch) * m_rows
        return xt[rh * m_rows:(rh + 1) * m_rows, col:col + m_rows]

    n_pairs = RW_HEADS // 2
    pairs = range(n_pairs)
    los = [p * pw for p in pairs]

    def pair_phase(sb):
        for g0 in range(0, n_pairs, PAIR_LOCKSTEP):
            yield from pair_group(sb, pairs[g0:g0 + PAIR_LOCKSTEP], los[g0:g0 + PAIR_LOCKSTEP])

    def pair_group(sb, pairs, los):
        arts = [art_ref[sb, lo:lo + pw, :] for lo in los]
        xts = [_mm(bk_ref[sb, :, lo:lo + pw], split_rows(art), pp, pp) for lo, art in zip(los, arts)]
        yield
        npcs = [cat([jnp.where(strict_t, blk(xt, 0, 0, 0), 0.0), jnp.where(strict_t, blk(xt, 1, 0, 0), 0.0)], axis=1)
                for xt in xts]
        vas = [_mm(split_rows(vt_ref[sb, lo:lo + pw, :]),
                   cat([cat([jnp.where(strict_t, blk(xt, u, 1, 0), 0.0), jnp.where(incl_t, blk(xt, u, 1, 1), 0.0)],
                            axis=1) for u in range(2)], axis=0), pp, pp)
               for lo, xt in zip(los, xts)]
        yield
        zcs = [cat([cat([art[u * hd:(u + 1) * hd, 0:m_rows], va[u * hd:(u + 1) * hd, 0:m_rows]], axis=0)
                    for u in range(2)], axis=1) for art, va in zip(arts, vas)]
        for lvl in range(n_lvl):
            bds = [block_diag(npc) for npc in npcs]
            if lvl + 1 < n_lvl:
                ts = [_mm(cat([zc, npc], axis=0), bd, pp, pp) for zc, npc, bd in zip(zcs, npcs, bds)]
                zcs = [zc + t[0:2 * hd] for zc, t in zip(zcs, ts)]
                npcs = [t[2 * hd:] for t in ts]
            else:
                zcs = [zc + _mm(zc, bd, pp, pp) for zc, bd in zip(zcs, bds)]
            yield
        rycs = [cat([cat([art[u * hd:(u + 1) * hd, m_rows:], va[u * hd:(u + 1) * hd, m_rows:]], axis=0)
                     for u in range(2)], axis=1)
                + _mm(zc, block_diag(cat([jnp.where(incl_t, blk(xt, 0, 0, 1), 0.0),
                                          jnp.where(incl_t, blk(xt, 1, 0, 1), 0.0)], axis=1)), pp, pp)
                for art, va, zc, xt in zip(arts, vas, zcs, xts)]
        yield
        if row_major:
            for p, zc, ry in zip(pairs, zcs, rycs):
                zrt_ref[sb, p, 0:2 * m_rows] = zc.T
                zrt_ref[sb, p, 2 * m_rows:] = ry.T
            yield
            return
        zero = jnp.zeros((hd, 2 * m_rows), BF16)
        for p, zc, ry in zip(pairs, zcs, rycs):
            arbd_ref[sb, p, 0:hd, 0:2 * m_rows] = cat([zc[0:hd, 0:m_rows], ry[0:hd, 0:m_rows]], axis=1).astype(BF16)
            arbd_ref[sb, p, 0:hd, 2 * m_rows:] = zero
            arbd_ref[sb, p, hd:, 0:2 * m_rows] = zero
            arbd_ref[sb, p, hd:, 2 * m_rows:] = cat([zc[0:hd, m_rows:], ry[0:hd, m_rows:]], axis=1).astype(BF16)
            uy_ref[sb, p] = cat([zc[hd:, 0:m_rows], ry[hd:, 0:m_rows], zc[hd:, m_rows:], ry[hd:, m_rows:]], axis=1)
        yield

    sh_c = c.bit_length() - 1
    cps = m_rows // c
    cpb = tbs // c
    row_id2 = (lax.broadcasted_iota(jnp.int32, (2 * m_rows, pw), 0) & (m_rows - 1)) >> sh_c
    col_id = lax.broadcasted_iota(jnp.int32, (hd, m_rows), 1) >> sh_c
    lane_lo = lax.broadcasted_iota(jnp.int32, (1, pw), 1) < hd

    def chunk_step(sb, i):
        b = i // cpb
        r0 = i * c
        rmask2 = row_id2 == i
        cmask = col_id == i
        ss = [st_ref[b, p] for p in pairs]
        zss = [_mm(s, arbd_ref[sb, p], pp, pp) + uy_ref[sb, p] for p, s in zip(pairs, ss)]
        yield
        upd = []
        for lo, zs in zip(los, zss):
            kbz = jnp.where(rmask2, kb_ref[sb, :, lo:lo + pw], 0.0)
            rhs = cat([jnp.where(lane_lo, kbz, 0.0), jnp.where(lane_lo, 0.0, kbz)], axis=0)
            vu = cat([vt_ref[sb, lo:lo + hd, :], zs[:, 0:m_rows],
                      vt_ref[sb, lo + hd:lo + pw, :], zs[:, 2 * m_rows:3 * m_rows]], axis=1)
            upd.append(_mm(vu, rhs, pp, pp))
        yield
        for p, lo, s, zs, d in zip(pairs, los, ss, zss, upd):
            st_ref[b, p] = s * pc_ref[sb, pl.ds(r0, 1), lo:lo + pw] + d
            for u in range(2):
                rows = slice((2 * p + u) * hd, (2 * p + u + 1) * hd)
                yt_ref[sb, rows, :] = jnp.where(cmask, zs[:, (2 * u + 1) * m_rows:(2 * u + 2) * m_rows],
                                               yt_ref[sb, rows, :])
        yield

    def swap_halves(x):
        return pltpu.roll(x, hd, axis=1)

    def sequence_steps(sb, p, lo):
        gsz = LANES // 4
        per_grp = gsz // c
        groups = range(m_rows // gsz)
        k_seq = (lax.broadcasted_iota(jnp.int32, (LANES, 1), 0) & (gsz - 1)) >> sh_c
        zt = zrt_ref[sb, p, 0:2 * m_rows]
        rt = zrt_ref[sb, p, 2 * m_rows:]
        a0 = jnp.where(lane_lo, zt[0:m_rows], 0.0)
        a1 = jnp.where(lane_lo, 0.0, swap_halves(zt[m_rows:]))
        r0 = jnp.where(lane_lo, rt[0:m_rows], 0.0)
        r1 = jnp.where(lane_lo, 0.0, swap_halves(rt[m_rows:]))
        res = []
        for b in range(nb):
            rb = slice(b * c, (b + 1) * c)
            rb1 = slice(m_rows + b * c, m_rows + (b + 1) * c)
            s = st_ref[b, p]
            lhs = cat([a0[rb], a1[rb], r0[rb], r1[rb]], axis=0)
            add = cat([zt[rb], zt[rb1], rt[rb], rt[rb1]], axis=0)
            res.append(_mm(lhs, cat([s, s], axis=0), pp, pp, NT) + add)
        yield
        for b, x in enumerate(res):
            yrm_ref[sb, b * c:(b + 1) * c, lo:lo + pw] = jnp.where(
                lane_lo, swap_halves(x[2 * c:3 * c]), x[3 * c:4 * c])
        vp = vrm_ref[sb, :, lo:lo + pw]
        vs = swap_halves(vp)
        vuts = []
        for q in groups:
            rq = slice(q * gsz, (q + 1) * gsz)
            members = range(q * per_grp, (q + 1) * per_grp)
            u0 = cat([res[b][0:c] for b in members], axis=0)
            u1 = cat([res[b][c:2 * c] for b in members], axis=0)
            vuts.append(cat([vs[rq], u0, vp[rq], u1], axis=0).T[hd:, :])
        yield
        ds = []
        for q, vut in zip(groups, vuts):
            kq = kb_ref[sb, q * gsz:(q + 1) * gsz, lo:lo + pw]
            bq = kb_ref[sb, m_rows + q * gsz:m_rows + (q + 1) * gsz, lo:lo + pw]
            rhs = cat([jnp.where(lane_lo, kq, 0.0), jnp.where(lane_lo, bq, 0.0),
                       jnp.where(lane_lo, 0.0, kq), jnp.where(lane_lo, 0.0, bq)], axis=0)
            ds.append(_mm(vut, cat([jnp.where(k_seq == n, rhs, 0.0) for n in range(per_grp)], axis=1), pp, pp))
        yield
        for q, d in zip(groups, ds):
            for n in range(per_grp):
                b = q * per_grp + n
                st_ref[b, p] = st_ref[b, p] * pc_ref[sb, b * c:b * c + 1, lo:lo + pw] + d[:, n * pw:(n + 1) * pw]
        yield

    inv_n = 1.0 / hd

    def epilogue(sb, strips=strips):
        ys = [yrm_ref[sb, :, cs] if row_major else yt_ref[sb, cs, :].T for cs in strips]
        ycs = [y - head_sum(y) * inv_n for y in ys]
        yield
        rstds = [lax.rsqrt(head_stat(yc * yc, cs) * inv_n + RW_GN_EPS) for cs, yc in zip(strips, ycs)]
        yield
        for cs, yc, rstd in zip(strips, ycs, rstds):
            yn = yc * head_bcast(rstd, cs) * lnw_ref[:, cs] + lnb_ref[:, cs]
            o_ref[sb * m_rows:(sb + 1) * m_rows, cs] = (
                (yn + bonus_ref[sb, :, cs]) * g_ref[sb, :, cs]).astype(o_ref.dtype)
        yield

    def run(*gens):
        live = list(gens)
        while live:
            live = [g for g in live if next(g, live) is not live]

    def tail(sb):
        for i in range(cps):
            yield from chunk_step(sb, i)
        yield from epilogue(sb)

    if row_major:
        hs, hp = len(strips) // 2, n_pairs // 2
        steps = [sequence_steps(0, p, lo) for p, lo in zip(pairs, los)]
        run(prologue(0, strips[:hs]))
        run(prologue(0, strips[hs:]), pair_group(0, pairs[:hp], los[:hp]))
        run(pair_group(0, pairs[hp:], los[hp:]), *steps[:hp])
        run(*steps[hp:], epilogue(0, strips[:hs]))
        run(epilogue(0, strips[hs:]))
    elif nsub == 1:
        run(prologue(0))
        run(pair_phase(0))

        together = SAMPLE_CHUNKS_TOGETHER if cpb == 1 and cps % SAMPLE_CHUNKS_TOGETHER == 0 else 1

        def chunk_body(i, carry):
            run(*[chunk_step(0, i * together + u) for u in range(together)])
            return carry

        lax.fori_loop(0, cps // together, chunk_body, 0)
        run(epilogue(0))
    else:
        run(prologue(0))
        for sb in range(nsub):
            run(pair_phase(sb), *([prologue(sb + 1)] if sb + 1 < nsub else []), *([tail(sb - 1)] if sb else []))
        run(tail(nsub - 1))

    if nb == 1:
        last = p_ref[tb - 1:tb, :].astype(F32).reshape(1, 1, GROUP_PAD)
    else:
        last = p_ref[...].astype(F32).reshape(nb, tb, GROUP_PAD)[:, tb - 1:tb, :]
    carry_ref[...] = last
    shout_ref[...] = last

    @pl.when(j == pl.num_programs(1) - 1)
    def _():
        for p in pairs:
            sp = st_ref[:, p]
            sout_ref[:, 2 * p] = sp[:, :, 0:hd]
            sout_ref[:, 2 * p + 1] = sp[:, :, hd:]


def _rwkv_row_major(nb, tb, c):
    return nb * tb == ROWS and tb == c == SUBLANES


def _rwkv(p, shift_prev, s0, wts, nb, tb, c, pp):
    bsz = s0.shape[0]
    seq = p.shape[0] // bsz
    m_rows = ROWS
    nsub = nb * tb // m_rows
    assert nsub * m_rows == nb * tb and (nsub == 1 or nb == 1)
    hd = RW_HEAD_DIM
    rm = _rwkv_row_major(nb, tb, c)
    small = (1, 1, SUBLANES, LANES)
    const = lambda shape: pl.BlockSpec(shape, lambda i, j: (0,) * len(shape))
    kern = functools.partial(_rwkv_kernel, nb=nb, tb=tb, c=c, pp=pp)
    return pl.pallas_call(
        kern,
        grid=(bsz // nb, seq // tb),
        in_specs=[
            pl.BlockSpec((nb * tb, GROUP_PAD), lambda i, j: (i * (seq // tb) + j, 0)),
            pl.BlockSpec((nb, 1, GROUP_PAD), lambda i, j: (i, 0, 0)),
            pl.BlockSpec((nb, RW_HEADS, hd, hd), lambda i, j: (i, 0, 0, 0)),
            const((1, GROUP_PAD)),
            const((LANES, 2 * RW_WIDTH)),
            const((1, RW_WIDTH)),
            const((1, RW_WIDTH)),
            const((2 * LANES, RW_WIDTH)),
            const((1, RW_WIDTH)),
            const((1, RW_WIDTH)),
            const((1, RW_WIDTH)),
            const((1, RW_WIDTH)),
            const((1, RW_WIDTH)),
            const((RW_WIDTH, LANES)),
            const((LANES, RW_WIDTH)),
            const((STRIP, STRIP)),
        ],
        out_specs=[
            pl.BlockSpec((nb * tb, RW_WIDTH), lambda i, j: (i * (seq // tb) + j, 0)),
            pl.BlockSpec((nb, 1, GROUP_PAD), lambda i, j: (i, 0, 0)),
            pl.BlockSpec((nb, RW_HEADS, hd, hd), lambda i, j: (i, 0, 0, 0)),
        ],
        out_shape=[
            jax.ShapeDtypeStruct((bsz * seq, RW_WIDTH), BF16),
            jax.ShapeDtypeStruct((bsz, 1, GROUP_PAD), F32),
            jax.ShapeDtypeStruct((bsz, RW_HEADS, hd, hd), F32),
        ],
        scratch_shapes=[
            pltpu.VMEM((nb, 1, GROUP_PAD), F32),
            pltpu.VMEM((nb, RW_HEADS // 2, hd, 2 * hd), F32),
            pltpu.VMEM((nsub, 2 * m_rows, RW_WIDTH), F32),
            pltpu.VMEM((nsub, 2 * m_rows, RW_WIDTH), F32),
            pltpu.VMEM((nsub, m_rows, RW_WIDTH), F32),
            pltpu.VMEM((nsub, RW_WIDTH, 2 * m_rows), F32),
            pltpu.VMEM((nsub, RW_WIDTH, m_rows), F32),
            pltpu.VMEM((nsub, RW_HEADS // 2, 2 * hd, 4 * m_rows), BF16),
            pltpu.VMEM((nsub, RW_HEADS // 2, hd, 4 * m_rows), F32),
            pltpu.VMEM((nsub, RW_WIDTH, m_rows), F32),
            pltpu.VMEM((nsub, m_rows, RW_WIDTH), F32),
            pltpu.VMEM((nsub, m_rows, RW_WIDTH), F32),
            pltpu.VMEM((nsub, RW_HEADS // 2, 4 * m_rows, 2 * hd) if rm else small, F32),
            pltpu.VMEM((nsub, m_rows, RW_WIDTH) if rm else small[1:], F32),
            pltpu.VMEM((nsub, m_rows, RW_WIDTH) if rm else small[1:], F32),
        ],
        compiler_params=pltpu.CompilerParams(
            dimension_semantics=("parallel", "arbitrary"), vmem_limit_bytes=VMEM_LIMIT),
        name="rwkv7",
    )(p, shift_prev, s0, *wts)


def _gla_kernel(p_ref, s0_ref, gw2_ref, gb_ref, nw_ref, o_ref, st_ref,
                acc_ref, qd_ref, v_ref, kt_ref, et_ref, *, nb, tb, c, pp):
    j = pl.program_id(1)
    m_rows = nb * tb
    dk, dv = GLA_DK, GLA_DV

    @pl.when(j == 0)
    def _():
        st_ref[...] = s0_ref[...]

    p = p_ref[...].astype(F32)
    q = p[:, 0:GLA_KEY_WIDTH] * (dk ** -0.5)
    k = p[:, GLA_OFF_K:GLA_OFF_K + GLA_KEY_WIDTH]
    v = p[:, GLA_OFF_V:GLA_OFF_V + GLA_WIDTH]
    xs = p[:, GLA_OFF_GATE:GLA_OFF_GATE + LANES].astype(BF16)
    gout = p[:, GLA_OFF_GOUT:GLA_OFF_GOUT + GLA_WIDTH]
    gk = -_softplus(-(_dot(xs, gw2_ref[...]) + gb_ref[...])) / GLA_GATE_NORM

    same, ri, ci = _chunk_masks(m_rows, c)
    lmask = jnp.where(same & (ci <= ri), 1.0, 0.0).astype(BF16)
    causal = same & (ci <= ri)
    gcum = _mm(lmask, gk, pb=3)
    g3 = gcum.reshape(m_rows // c, c, GLA_KEY_WIDTH)
    gtot = jnp.broadcast_to(g3[:, c - 1:c, :], g3.shape).reshape(m_rows, GLA_KEY_WIDTH)
    qd = q * jnp.exp(gcum)
    kinv = k * jnp.exp(-gcum)
    qd_ref[...] = qd
    v_ref[...] = v
    kt_ref[...] = (k * jnp.exp(gtot - gcum)).T
    et_ref[...] = jnp.exp(gtot).T

    heads = range(GLA_HEADS)
    kq = [slice(h * dk, (h + 1) * dk) for h in heads]
    vq = [slice(h * dv, (h + 1) * dv) for h in heads]
    scores = [jnp.where(causal, _mm(qd[:, ks], kinv[:, ks], pp, pp, NT), 0.0) for ks in kq]
    intra = [_mm(a, v[:, vs], pp, pp) for a, vs in zip(scores, vq)]
    for vs, o in zip(vq, intra):
        acc_ref[:, vs] = o

    sh_c = c.bit_length() - 1
    cpb = tb // c
    span = min(m_rows, LANES)
    row_id = lax.broadcasted_iota(jnp.int32, (span, dv), 0) >> sh_c
    for i in range(m_rows // c):
        b = i // cpb
        r0 = i * c
        t0 = (r0 // span) * span
        rmask = row_id == (r0 - t0) // c
        states = [st_ref[b, h] for h in heads]
        inter = [_mm(qd_ref[r0:r0 + c, ks], s, pp, pp) for ks, s in zip(kq, states)]
        upd = [_mm(kt_ref[ks, t0:t0 + span], jnp.where(rmask, v_ref[t0:t0 + span, vs], 0.0), pp, pp)
               for ks, vs in zip(kq, vq)]
        for h, ks, vs, s, oi, d in zip(heads, kq, vq, states, inter, upd):
            acc_ref[r0:r0 + c, vs] += oi
            st_ref[b, h] = s * et_ref[ks, r0:r0 + 1] + d

    for vs in vq:
        o = acc_ref[:, vs]
        on = o * lax.rsqrt(jnp.mean(o * o, axis=-1, keepdims=True) + HEAD_NORM_EPS) * nw_ref[...]
        gh = gout[:, vs]
        o_ref[:, vs] = (on * (gh * _sigmoid(gh))).astype(o_ref.dtype)


def _gla(p, s0, wts, nb, tb, c, pp):
    bsz = s0.shape[0]
    seq = p.shape[0] // bsz
    m_rows = nb * tb
    const = lambda shape: pl.BlockSpec(shape, lambda i, j: (0,) * len(shape))
    kern = functools.partial(_gla_kernel, nb=nb, tb=tb, c=c, pp=pp)
    return pl.pallas_call(
        kern,
        grid=(bsz // nb, seq // tb),
        in_specs=[
            pl.BlockSpec((nb * tb, GROUP_PAD), lambda i, j: (i * (seq // tb) + j, 1)),
            pl.BlockSpec((nb, GLA_HEADS, GLA_DK, GLA_DV), lambda i, j: (i, 0, 0, 0)),
            const((LANES, GLA_KEY_WIDTH)),
            const((1, GLA_KEY_WIDTH)),
            const((1, GLA_DV)),
        ],
        out_specs=[
            pl.BlockSpec((nb * tb, GLA_WIDTH), lambda i, j: (i * (seq // tb) + j, 0)),
            pl.BlockSpec((nb, GLA_HEADS, GLA_DK, GLA_DV), lambda i, j: (i, 0, 0, 0)),
        ],
        out_shape=[
            jax.ShapeDtypeStruct((bsz * seq, GLA_WIDTH), BF16),
            jax.ShapeDtypeStruct((bsz, GLA_HEADS, GLA_DK, GLA_DV), F32),
        ],
        scratch_shapes=[
            pltpu.VMEM((m_rows, GLA_WIDTH), F32),
            pltpu.VMEM((m_rows, GLA_KEY_WIDTH), F32),
            pltpu.VMEM((m_rows, GLA_WIDTH), F32),
            pltpu.VMEM((GLA_KEY_WIDTH, m_rows), F32),
            pltpu.VMEM((GLA_KEY_WIDTH, m_rows), F32),
        ],
        compiler_params=pltpu.CompilerParams(
            dimension_semantics=("parallel", "arbitrary"), vmem_limit_bytes=VMEM_LIMIT),
        name="gla",
    )(p, s0, *wts)


def _post_kernel(x_ref, orw_ref, ogla_ref, wo_ref, g2_ref, wu_ref, wd_ref, gf_ref, o_ref, h_ref):
    jf = pl.program_id(1)

    @pl.when(jf == 0)
    def _():
        x1 = (x_ref[...]
              + _dot(orw_ref[...], wo_ref[0:RW_WIDTH, :])
              + _dot(ogla_ref[...], wo_ref[RW_WIDTH:, :]))
        o_ref[...] = x1
        ms = jnp.mean(x1 * x1, axis=-1, keepdims=True)
        h_ref[...] = (x1 * lax.rsqrt(ms + NORM_EPS) * g2_ref[...]).astype(BF16)

    u = jnp.maximum(_dot(h_ref[...], wu_ref[...]), 0.0)
    o_ref[...] += _dot((u * u).astype(BF16), wd_ref[...])

    @pl.when(jf == pl.num_programs(1) - 1)
    def _():
        x2 = o_ref[...]
        ms = jnp.mean(x2 * x2, axis=-1, keepdims=True)
        o_ref[...] = x2 * lax.rsqrt(ms + NORM_EPS) * gf_ref[...]


def _post(x2d, o_rw, o_gla, w_out, g2, w_up, w_down, gf, tm, tf):
    t = x2d.shape[0]
    return pl.pallas_call(
        _post_kernel,
        grid=(t // tm, D_FF // tf),
        in_specs=[
            pl.BlockSpec((tm, D_MODEL), lambda i, j: (i, 0)),
            pl.BlockSpec((tm, RW_WIDTH), lambda i, j: (i, 0)),
            pl.BlockSpec((tm, GLA_WIDTH), lambda i, j: (i, 0)),
            pl.BlockSpec((D_MODEL, D_MODEL), lambda i, j: (0, 0), pipeline_mode=pl.Buffered(1)),
            pl.BlockSpec((1, D_MODEL), lambda i, j: (0, 0)),
            pl.BlockSpec((D_MODEL, tf), lambda i, j: (0, j)),
            pl.BlockSpec((tf, D_MODEL), lambda i, j: (j, 0)),
            pl.BlockSpec((1, D_MODEL), lambda i, j: (0, 0)),
        ],
        out_specs=pl.BlockSpec((tm, D_MODEL), lambda i, j: (i, 0)),
        out_shape=jax.ShapeDtypeStruct((t, D_MODEL), F32),
        scratch_shapes=[pltpu.VMEM((tm, D_MODEL), BF16)],
        compiler_params=pltpu.CompilerParams(
            dimension_semantics=("parallel", "arbitrary"), vmem_limit_bytes=VMEM_LIMIT),
        name="post",
    )(x2d, o_rw, o_gla, w_out, g2, w_up, w_down, gf)


def _pad_cols(w, n):
    return jnp.pad(w, ((0, 0), (0, n - w.shape[1])))


def _prep_weights(w_in, rw_mu, rw_w0, rw_w2, rw_a0, rw_a2, rw_g2, rw_k_k, rw_k_a, rw_r_k, rw_ln_w, rw_ln_b,
                  gla_gw2, gla_gb, gla_norm_w):
    w_p = jnp.swapaxes(w_in, 0, 1).astype(BF16)

    row = lambda x: x.reshape(1, -1).astype(F32)
    mu = _pad_cols(row(rw_mu), GROUP_PAD)
    wwa = jnp.zeros((LANES, 2 * RW_WIDTH), F32)
    wwa = wwa.at[0:RW_LORA_W, 0:RW_WIDTH].set(rw_w2)
    wwa = wwa.at[RW_LORA_W:RW_LORA_W + RW_LORA_A, RW_WIDTH:].set(rw_a2)
    g2p = jnp.pad(rw_g2, ((0, 2 * LANES - RW_LORA_G), (0, 0)))
    head_of_col = jnp.arange(RW_WIDTH) // RW_HEAD_DIM
    e1 = (head_of_col[:, None] == jnp.arange(LANES)[None, :]).astype(BF16)
    rw_wts = (mu, wwa.astype(BF16), row(rw_w0), row(rw_a0), g2p.astype(BF16), row(rw_k_k), row(rw_k_a),
              row(rw_r_k), row(rw_ln_w), row(rw_ln_b), e1, e1.T,
              (head_of_col[:STRIP, None] == head_of_col[None, :STRIP]).astype(BF16))
    gw2p = jnp.pad(gla_gw2, ((0, LANES - GLA_GATE_RANK), (0, 0))).astype(BF16)
    gla_wts = (gw2p, row(gla_gb), row(gla_norm_w))
    return w_p, rw_wts, gla_wts


RW_PIECES = 1
GLA_PIECES = 1
PAIR_LOCKSTEP = 8
SAMPLE_CHUNKS_TOGETHER = 16
RW_STEP_ROWS = 4 * ROWS
GLA_STEP_ROWS = 2 * ROWS
STRIP = 256


def _trunk(x, shift, s_rw, s_gla, norm1_g, w_p, rw_wts, gla_wts, w_out, norm2_g, w_up, w_down, norm_f_g,
           nb, tb, c_rw, c_gla, tm):
    bsz, seq, _ = x.shape
    t = bsz * seq
    x2d = x.reshape(t, D_MODEL)
    proj = _inproj(x2d, norm1_g.reshape(1, -1), w_p, tm, INPROJ_TN)
    shift_p = _pad_cols(shift, GROUP_PAD).reshape(bsz, 1, GROUP_PAD)
    rw_tb = RW_STEP_ROWS if nb == 1 and seq % RW_STEP_ROWS == 0 else tb
    o_rw, sh_new, s_rw_new = _rwkv(proj, shift_p, s_rw, rw_wts, nb, rw_tb, c_rw, RW_PIECES)
    gla_tb = GLA_STEP_ROWS if nb == 1 and seq % GLA_STEP_ROWS == 0 else tb
    o_gla, s_gla_new = _gla(proj, s_gla, gla_wts, nb, gla_tb, c_gla, GLA_PIECES)
    y = _post(x2d, o_rw, o_gla, w_out, norm2_g.reshape(1, -1),
              w_up, w_down, norm_f_g.reshape(1, -1), POST_TM, POST_TF)
    return (y.reshape(bsz, seq, D_MODEL), sh_new[:, 0, :RW_PROJ][None], s_rw_new[None], s_gla_new[None])


def kernel(x_prompt, x_sample, state_rwkv_shift, state_rwkv_wkv, state_gla, norm1_g, w_in, rw_mu, rw_w0,
           rw_w2, rw_a0, rw_a2, rw_g2, rw_k_k, rw_k_a, rw_r_k, rw_ln_w, rw_ln_b, gla_gw2, gla_gb, gla_norm_w,
           w_out, norm2_g, w_up, w_down, norm_f_g):
    w_p, rw_wts, gla_wts = _prep_weights(
        w_in[0], rw_mu[0], rw_w0[0], rw_w2[0], rw_a0[0], rw_a2[0], rw_g2[0], rw_k_k[0], rw_k_a[0],
        rw_r_k[0].reshape(-1), rw_ln_w[0], rw_ln_b[0], gla_gw2[0], gla_gb[0], gla_norm_w[0])
    shared = (norm1_g[0], w_p, rw_wts, gla_wts, w_out[0].astype(BF16), norm2_g[0], w_up[0].astype(BF16),
              w_down[0].astype(BF16), norm_f_g)

    bp, lp, _ = x_prompt.shape
    bs, ls, _ = x_sample.shape
    dt = x_prompt.dtype
    out_p = _trunk(x_prompt, jnp.zeros((bp, RW_PROJ), dt),
                   jnp.zeros((bp, RW_HEADS, RW_HEAD_DIM, RW_HEAD_DIM), dt),
                   jnp.zeros((bp, GLA_HEADS, GLA_DK, GLA_DV), dt), *shared,
                   nb=1, tb=ROWS, c_rw=RW_CHUNK, c_gla=GLA_CHUNK, tm=INPROJ_TM)
    out_s = _trunk(x_sample, state_rwkv_shift[0], state_rwkv_wkv[0], state_gla[0], *shared,
                   nb=ROWS // ls, tb=ls, c_rw=ls, c_gla=ls, tm=INPROJ_TM)
    return (out_p[0], out_s[0], out_p[1], out_p[2], out_p[3], out_s[1], out_s[2], out_s[3])
```

```python
import functools

import jax
import jax.numpy as jnp
from jax import lax
from jax.experimental import pallas as pl
from jax.experimental.pallas import tpu as pltpu

F32 = jnp.float32
BF16 = jnp.bfloat16

D_MODEL = 2048
RW_WIDTH = 1024
RW_HEADS = 16
RW_HEAD_DIM = 64
RW_LORA_W = 64
RW_LORA_A = 64
RW_LORA_G = 160
RW_PROJ = 3 * RW_WIDTH + RW_LORA_W + RW_LORA_A + RW_LORA_G
RW_GN_EPS = 64e-5
GLA_WIDTH = 1024
GLA_HEADS = 4
GLA_KEY_WIDTH = 512
GLA_DK = 128
GLA_DV = 256
GLA_GATE_RANK = 16
GLA_GATE_NORM = 16.0
GLA_CHUNK = 64
GLA_PROJ = 2 * GLA_KEY_WIDTH + GLA_WIDTH + GLA_GATE_RANK + GLA_WIDTH
D_FF = 4 * D_MODEL
NORM_EPS = 1e-6
HEAD_NORM_EPS = 1e-5
DECAY_SCALE = 0.6065306597126334

LANES = 128
SUBLANES = 8

GROUP_PAD = 3456
RW_OFF_XWA = 3 * RW_WIDTH
RW_OFF_XG = RW_OFF_XWA + LANES
GLA_OFF_K = GLA_KEY_WIDTH
GLA_OFF_V = 2 * GLA_KEY_WIDTH
GLA_OFF_GATE = GLA_OFF_V + GLA_WIDTH
GLA_OFF_GOUT = GLA_OFF_GATE + LANES
NP = 2 * GROUP_PAD

ROWS = 128
MXU_WIDTH = 256
INPROJ_TM = 1024
INPROJ_TN = 9 * MXU_WIDTH
RW_CHUNK = 64
PROJ_DTYPE = BF16
POST_TM = 512
POST_TF = 1024
VMEM_LIMIT = 56 * 1024 * 1024

NN = (((1,), (0,)), ((), ()))
NT = (((1,), (1,)), ((), ()))


def _dot(a, b, dims=NN):
    return lax.dot_general(a, b, dims, preferred_element_type=F32)


def _parts(x, n):
    if x.dtype == BF16:
        return [x]
    out = []
    rem = x
    for i in range(n):
        h = rem.astype(BF16)
        out.append(h)
        if i + 1 < n:
            rem = rem - h.astype(F32)
    return out


def _mm(a, b, pa=1, pb=1, dims=NN):
    pa = 1 if a.dtype == BF16 else pa
    pb = 1 if b.dtype == BF16 else pb
    aa = _parts(a, pa)
    bb = _parts(b, pb)
    n = max(pa, pb)
    if dims == NN and min(pa, pb) == 1 and n * a.shape[1] <= MXU_WIDTH:
        return _dot(jnp.concatenate(aa * (n // pa), axis=1), jnp.concatenate(bb * (n // pb), axis=0))
    acc = None
    for i in range(pa):
        for j in range(pb):
            if i + j < n:
                t = _dot(aa[i], bb[j], dims)
                acc = t if acc is None else acc + t
    return acc


def _softplus(x):
    return jnp.maximum(x, 0.0) + jnp.log(1.0 + jnp.exp(-jnp.abs(x)))


def _sigmoid(x):
    return 0.5 * jnp.tanh(0.5 * x) + 0.5


def _chunk_masks(m, c):
    sh = c.bit_length() - 1
    ri = lax.broadcasted_iota(jnp.int32, (m, m), 0)
    ci = lax.broadcasted_iota(jnp.int32, (m, m), 1)
    same = (ri >> sh) == (ci >> sh)
    return same, ri, ci


_W_SEGMENTS = (
    (0, RW_PROJ, 0),
    (RW_PROJ, GLA_OFF_GATE + GLA_GATE_RANK, GROUP_PAD),
    (RW_PROJ + GLA_OFF_GATE + GLA_GATE_RANK, GLA_WIDTH, GROUP_PAD + GLA_OFF_GOUT),
)


def _weight_windows(tn):
    blocks = []
    for lo in range(0, NP, tn):
        wins = []
        for src, ln, dst in _W_SEGMENTS:
            a, b = max(dst, lo), min(dst + ln, lo + tn)
            if a < b:
                wins.append((src + a - dst, b - a, a - lo))
        blocks.append(tuple(wins))
    return tuple(blocks)


def _inproj_kernel(x_ref, g_ref, w_hbm, o_ref, h_ref, wbuf, sem, *, windows):
    i, j = pl.program_id(0), pl.program_id(1)
    nj = len(windows)
    step = i * nj + j
    n_steps = pl.num_programs(0) * nj
    slot = lax.rem(step, 2)

    def copies(jb, sl):
        return [pltpu.make_async_copy(w_hbm.at[pl.ds(src, ln), :], wbuf.at[sl, pl.ds(dst, ln), :], sem.at[sl, k])
                for k, (src, ln, dst) in enumerate(windows[jb])]

    @pl.when(step == 0)
    def _():
        for sl in range(min(2, nj)):
            covered = sorted((dst, dst + ln) for _, ln, dst in windows[sl])
            edges = [0] + [e for span in covered for e in span] + [wbuf.shape[1]]
            for a, b in zip(edges[0::2], edges[1::2]):
                if a < b:
                    wbuf[sl, a:b, :] = jnp.zeros((b - a, wbuf.shape[2]), wbuf.dtype)
        for cp in copies(0, 0):
            cp.start()

    for jb in range(nj):
        @pl.when(j == jb)
        def _(jb=jb):
            @pl.when(step + 1 < n_steps)
            def _():
                for cp in copies((jb + 1) % nj, 1 - slot):
                    cp.start()

            for cp in copies(jb, slot):
                cp.wait()

    @pl.when(j == 0)
    def _():
        x = x_ref[...]
        ms = jnp.mean(x * x, axis=-1, keepdims=True)
        h_ref[...] = (x * lax.rsqrt(ms + NORM_EPS) * g_ref[...]).astype(BF16)

    o_ref[...] = _dot(h_ref[...], wbuf[slot], NT).astype(o_ref.dtype)


def _inproj(x2d, g, w_t, tm, tn):
    t = x2d.shape[0]
    windows = _weight_windows(tn)
    return pl.pallas_call(
        functools.partial(_inproj_kernel, windows=windows),
        grid=(t // tm, NP // tn),
        in_specs=[
            pl.BlockSpec((tm, D_MODEL), lambda i, j: (i, 0)),
            pl.BlockSpec((1, D_MODEL), lambda i, j: (0, 0)),
            pl.BlockSpec(memory_space=pltpu.HBM),
        ],
        out_specs=pl.BlockSpec((tm, tn), lambda i, j: (i, j)),
        out_shape=jax.ShapeDtypeStruct((t, NP), PROJ_DTYPE),
        scratch_shapes=[
            pltpu.VMEM((tm, D_MODEL), BF16),
            pltpu.VMEM((2, tn, D_MODEL), BF16),
            pltpu.SemaphoreType.DMA((2, max(len(w) for w in windows))),
        ],
        compiler_params=pltpu.CompilerParams(
            dimension_semantics=("arbitrary", "arbitrary"), vmem_limit_bytes=VMEM_LIMIT),
        name="inproj",
    )(x2d, g, w_t)


def _rwkv_kernel(p_ref, sh_ref, s0_ref, mu_ref, wwa_ref, w0_ref, a0_ref, g2_ref, kk_ref, ka_ref,
                 rk_ref, lnw_ref, lnb_ref, e1_ref, e2_ref, e12_ref,
                 o_ref, shout_ref, sout_ref,
                 carry_ref, st_ref, bk_ref, kb_ref, pc_ref, art_ref, vt_ref, arbd_ref, uy_ref, yt_ref,
                 g_ref, bonus_ref, zrt_ref, vrm_ref, yrm_ref, *, nb, tb, c, pp):
    j = pl.program_id(1)
    m_rows = ROWS
    nsub = nb * tb // m_rows
    tbs = tb // nsub
    n_lvl = c.bit_length() - 1
    hd = RW_HEAD_DIM
    row_major = _rwkv_row_major(nb, tb, c)

    @pl.when(j == 0)
    def _():
        for p in range(RW_HEADS // 2):
            st_ref[:, p] = jnp.concatenate([s0_ref[:, 2 * p], s0_ref[:, 2 * p + 1]], axis=-1)
        carry_ref[...] = sh_ref[...]

    if not row_major:
        yt_ref[...] = jnp.zeros_like(yt_ref)

    def shifted(sb, c0, w):
        r0 = sb * m_rows
        pc3 = p_ref[r0:r0 + m_rows, c0:c0 + w].astype(F32).reshape(nb, tbs, w)
        if sb == 0:
            before = carry_ref[:, :, c0:c0 + w]
        else:
            before = p_ref[r0 - 1:r0, c0:c0 + w].astype(F32).reshape(1, 1, w)
        t3 = lax.broadcasted_iota(jnp.int32, pc3.shape, 1)
        prev3 = jnp.where(t3 == 0, before, pltpu.roll(pc3, 1, axis=1))
        return (pc3 + (prev3 - pc3) * mu_ref[:, c0:c0 + w]).reshape(m_rows, w)

    same, ri, ci = _chunk_masks(m_rows, c)
    lmask = jnp.where(same & (ci <= ri), 1.0, 0.0).astype(BF16)
    strict_t = same & (ri < ci)
    incl_t = same & (ri <= ci)

    sw = STRIP
    strips = [slice(c0, c0 + sw) for c0 in range(0, RW_WIDTH, sw)]

    def head_stat(x, cs):
        return _dot(x.astype(BF16), e1_ref[cs, :])

    def head_bcast(s, cs):
        return _mm(s, e2_ref[:, cs], pa=2)

    def head_sum(x):
        return _dot(x.astype(BF16), e12_ref[...])


    def prologue(sb):
        slab = shifted(sb, RW_OFF_XWA, LANES)
        lane = lax.broadcasted_iota(jnp.int32, slab.shape, 1)
        lhs = jnp.where(lane < RW_LORA_W, jnp.tanh(slab), slab).astype(BF16)
        sg = _sigmoid(shifted(sb, RW_OFF_XG, 2 * LANES)).astype(BF16)
        yield
        lws = [_dot(lhs, wwa_ref[:, cs]) for cs in strips]
        las = [_dot(lhs, wwa_ref[:, RW_WIDTH + cs.start:RW_WIDTH + cs.stop]) for cs in strips]
        for cs in strips:
            g_ref[sb, :, cs] = _dot(sg, g2_ref[:, cs])
        yield
        logws = [-DECAY_SCALE * _sigmoid(w0_ref[:, cs] + lw) for cs, lw in zip(strips, lws)]
        gcums = [_mm(lmask, logw, pb=2) for logw in logws]
        yield
        avs = [_sigmoid(a0_ref[:, cs] + la) for cs, la in zip(strips, las)]
        ks = [shifted(sb, RW_WIDTH + cs.start, sw) for cs in strips]
        kkfs = [k * kk_ref[:, cs] for cs, k in zip(strips, ks)]
        rinvs = [lax.rsqrt(jnp.maximum(head_stat(kkf * kkf, cs), 1e-24)) for cs, kkf in zip(strips, kkfs)]
        yield
        k2s = [k * (1.0 + (a - 1.0) * ka_ref[:, cs]) for cs, k, a in zip(strips, ks, avs)]
        rs = [shifted(sb, cs.start, sw) for cs in strips]
        bsums = [head_sum(r * k2 * rk_ref[:, cs]) for cs, r, k2 in zip(strips, rs, k2s)]
        kks = [kkf * head_bcast(rinv, cs) for cs, kkf, rinv in zip(strips, kkfs, rinvs)]
        yield
        for cs, bsum in zip(strips, bsums):
            v = shifted(sb, 2 * RW_WIDTH + cs.start, sw)
            bonus_ref[sb, :, cs] = bsum * v
            vt_ref[sb, cs, :] = v.T
            if row_major:
                vrm_ref[sb, :, cs] = v
        yield
        for cs, logw, gcum, a, kk, k2, r in zip(strips, logws, gcums, avs, kks, k2s, rs):
            beta = kk * a
            g3 = gcum.reshape(cps, c, sw)
            gtot = jnp.broadcast_to(g3[:, c - 1:c, :], g3.shape).reshape(m_rows, sw)
            e_inv = jnp.exp(-gcum)
            e_rev = jnp.exp(gtot - gcum)
            art_ref[sb, cs, 0:m_rows] = (-kk * jnp.exp(gcum - logw)).T
            art_ref[sb, cs, m_rows:2 * m_rows] = (r * jnp.exp(gcum)).T
            bk_ref[sb, 0:m_rows, cs] = beta * e_inv
            bk_ref[sb, m_rows:, cs] = k2 * e_inv
            kb_ref[sb, 0:m_rows, cs] = k2 * e_rev
            kb_ref[sb, m_rows:, cs] = beta * e_rev
            pc_ref[sb, :, cs] = jnp.exp(gtot)
            yield

    pw = 2 * hd
    cat = jnp.concatenate
    top = lax.broadcasted_iota(jnp.int32, (pw, 1), 0) < hd
    left = lax.broadcasted_iota(jnp.int32, (1, 2 * m_rows), 1) < m_rows

    def split_rows(x):
        return cat([jnp.where(top, x, 0.0), jnp.where(top, 0.0, x)], axis=1)

    def block_diag(xc):
        return cat([jnp.where(left, xc, 0.0), jnp.where(left, 0.0, xc)], axis=0)

    def blk(xt, u, rh, ch):
        col = (2 * u + ch) * m_rows
        return xt[rh * m_rows:(rh + 1) * m_rows, col:col + m_rows]

    n_pairs = RW_HEADS // 2
    pairs = range(n_pairs)
    los = [p * pw for p in pairs]

    def pair_phase(sb):
        for g0 in range(0, n_pairs, PAIR_LOCKSTEP):
            yield from pair_group(sb, pairs[g0:g0 + PAIR_LOCKSTEP], los[g0:g0 + PAIR_LOCKSTEP])

    def pair_group(sb, pairs, los):
        arts = [art_ref[sb, lo:lo + pw, :] for lo in los]
        xts = [_mm(bk_ref[sb, :, lo:lo + pw], split_rows(art), pp, pp) for lo, art in zip(los, arts)]
        yield
        npcs = [cat([jnp.where(strict_t, blk(xt, 0, 0, 0), 0.0), jnp.where(strict_t, blk(xt, 1, 0, 0), 0.0)], axis=1)
                for xt in xts]
        vas = [_mm(split_rows(vt_ref[sb, lo:lo + pw, :]),
                   cat([cat([jnp.where(strict_t, blk(xt, u, 1, 0), 0.0), jnp.where(incl_t, blk(xt, u, 1, 1), 0.0)],
                            axis=1) for u in range(2)], axis=0), pp, pp)
               for lo, xt in zip(los, xts)]
        yield
        zcs = [cat([cat([art[u * hd:(u + 1) * hd, 0:m_rows], va[u * hd:(u + 1) * hd, 0:m_rows]], axis=0)
                    for u in range(2)], axis=1) for art, va in zip(arts, vas)]
        for lvl in range(n_lvl):
            bds = [block_diag(npc) for npc in npcs]
            if lvl + 1 < n_lvl:
                ts = [_mm(cat([zc, npc], axis=0), bd, pp, pp) for zc, npc, bd in zip(zcs, npcs, bds)]
                zcs = [zc + t[0:2 * hd] for zc, t in zip(zcs, ts)]
                npcs = [t[2 * hd:] for t in ts]
            else:
                zcs = [zc + _mm(zc, bd, pp, pp) for zc, bd in zip(zcs, bds)]
            yield
        rycs = [cat([cat([art[u * hd:(u + 1) * hd, m_rows:], va[u * hd:(u + 1) * hd, m_rows:]], axis=0)
                     for u in range(2)], axis=1)
                + _mm(zc, block_diag(cat([jnp.where(incl_t, blk(xt, 0, 0, 1), 0.0),
                                          jnp.where(incl_t, blk(xt, 1, 0, 1), 0.0)], axis=1)), pp, pp)
                for art, va, zc, xt in zip(arts, vas, zcs, xts)]
        yield
        if row_major:
            for p, zc, ry in zip(pairs, zcs, rycs):
                zrt_ref[sb, p, 0:2 * m_rows] = zc.T
                zrt_ref[sb, p, 2 * m_rows:] = ry.T
            yield
            return
        zero = jnp.zeros((hd, 2 * m_rows), BF16)
        for p, zc, ry in zip(pairs, zcs, rycs):
            arbd_ref[sb, p, 0:hd, 0:2 * m_rows] = cat([zc[0:hd, 0:m_rows], ry[0:hd, 0:m_rows]], axis=1).astype(BF16)
            arbd_ref[sb, p, 0:hd, 2 * m_rows:] = zero
            arbd_ref[sb, p, hd:, 0:2 * m_rows] = zero
            arbd_ref[sb, p, hd:, 2 * m_rows:] = cat([zc[0:hd, m_rows:], ry[0:hd, m_rows:]], axis=1).astype(BF16)
            uy_ref[sb, p] = cat([zc[hd:, 0:m_rows], ry[hd:, 0:m_rows], zc[hd:, m_rows:], ry[hd:, m_rows:]], axis=1)
        yield

    sh_c = c.bit_length() - 1
    cps = m_rows // c
    cpb = tbs // c
    row_id2 = (lax.broadcasted_iota(jnp.int32, (2 * m_rows, pw), 0) & (m_rows - 1)) >> sh_c
    col_id = lax.broadcasted_iota(jnp.int32, (hd, m_rows), 1) >> sh_c
    lane_lo = lax.broadcasted_iota(jnp.int32, (1, pw), 1) < hd

    def chunk_step(sb, i):
        b = i // cpb
        r0 = i * c
        rmask2 = row_id2 == i
        cmask = col_id == i
        ss = [st_ref[b, p] for p in pairs]
        zss = [_mm(s, arbd_ref[sb, p], pp, pp) + uy_ref[sb, p] for p, s in zip(pairs, ss)]
        yield
        upd = []
        for lo, zs in zip(los, zss):
            kbz = jnp.where(rmask2, kb_ref[sb, :, lo:lo + pw], 0.0)
            rhs = cat([jnp.where(lane_lo, kbz, 0.0), jnp.where(lane_lo, 0.0, kbz)], axis=0)
            vu = cat([vt_ref[sb, lo:lo + hd, :], zs[:, 0:m_rows],
                      vt_ref[sb, lo + hd:lo + pw, :], zs[:, 2 * m_rows:3 * m_rows]], axis=1)
            upd.append(_mm(vu, rhs, pp, pp))
        yield
        for p, lo, s, zs, d in zip(pairs, los, ss, zss, upd):
            st_ref[b, p] = s * pc_ref[sb, pl.ds(r0, 1), lo:lo + pw] + d
            for u in range(2):
                rows = slice((2 * p + u) * hd, (2 * p + u + 1) * hd)
                yt_ref[sb, rows, :] = jnp.where(cmask, zs[:, (2 * u + 1) * m_rows:(2 * u + 2) * m_rows],
                                               yt_ref[sb, rows, :])
        yield

    def swap_halves(x):
        return pltpu.roll(x, hd, axis=1)

    def sequence_steps(sb, p, lo):
        gsz = LANES // 4
        per_grp = gsz // c
        groups = range(m_rows // gsz)
        k_seq = (lax.broadcasted_iota(jnp.int32, (LANES, 1), 0) & (gsz - 1)) >> sh_c
        zt = zrt_ref[sb, p, 0:2 * m_rows]
        rt = zrt_ref[sb, p, 2 * m_rows:]
        a0 = jnp.where(lane_lo, zt[0:m_rows], 0.0)
        a1 = jnp.where(lane_lo, 0.0, swap_halves(zt[m_rows:]))
        r0 = jnp.where(lane_lo, rt[0:m_rows], 0.0)
        r1 = jnp.where(lane_lo, 0.0, swap_halves(rt[m_rows:]))
        res = []
        for b in range(nb):
            rb = slice(b * c, (b + 1) * c)
            rb1 = slice(m_rows + b * c, m_rows + (b + 1) * c)
            s = st_ref[b, p]
            lhs = cat([a0[rb], a1[rb], r0[rb], r1[rb]], axis=0)
            add = cat([zt[rb], zt[rb1], rt[rb], rt[rb1]], axis=0)
            res.append(_mm(lhs, cat([s, s], axis=0), pp, pp, NT) + add)
        yield
        for b, x in enumerate(res):
            yrm_ref[sb, b * c:(b + 1) * c, lo:lo + pw] = jnp.where(
                lane_lo, swap_halves(x[2 * c:3 * c]), x[3 * c:4 * c])
        vp = vrm_ref[sb, :, lo:lo + pw]
        vs = swap_halves(vp)
        vuts = []
        for q in groups:
            rq = slice(q * gsz, (q + 1) * gsz)
            members = range(q * per_grp, (q + 1) * per_grp)
            u0 = cat([res[b][0:c] for b in members], axis=0)
            u1 = cat([res[b][c:2 * c] for b in members], axis=0)
            vuts.append(cat([vs[rq], u0, vp[rq], u1], axis=0).T[hd:, :])
        yield
        ds = []
        for q, vut in zip(groups, vuts):
            kq = kb_ref[sb, q * gsz:(q + 1) * gsz, lo:lo + pw]
            bq = kb_ref[sb, m_rows + q * gsz:m_rows + (q + 1) * gsz, lo:lo + pw]
            rhs = cat([jnp.where(lane_lo, kq, 0.0), jnp.where(lane_lo, bq, 0.0),
                       jnp.where(lane_lo, 0.0, kq), jnp.where(lane_lo, 0.0, bq)], axis=0)
            ds.append(_mm(vut, cat([jnp.where(k_seq == n, rhs, 0.0) for n in range(per_grp)], axis=1), pp, pp))
        yield
        for q, d in zip(groups, ds):
            for n in range(per_grp):
                b = q * per_grp + n
                st_ref[b, p] = st_ref[b, p] * pc_ref[sb, b * c:b * c + 1, lo:lo + pw] + d[:, n * pw:(n + 1) * pw]
        yield

    inv_n = 1.0 / hd

    def epilogue(sb):
        ys = [yrm_ref[sb, :, cs] if row_major else yt_ref[sb, cs, :].T for cs in strips]
        ycs = [y - head_sum(y) * inv_n for y in ys]
        yield
        rstds = [lax.rsqrt(head_stat(yc * yc, cs) * inv_n + RW_GN_EPS) for cs, yc in zip(strips, ycs)]
        yield
        for cs, yc, rstd in zip(strips, ycs, rstds):
            yn = yc * head_bcast(rstd, cs) * lnw_ref[:, cs] + lnb_ref[:, cs]
            o_ref[sb * m_rows:(sb + 1) * m_rows, cs] = (
                (yn + bonus_ref[sb, :, cs]) * g_ref[sb, :, cs]).astype(o_ref.dtype)
        yield

    def run(*gens):
        live = list(gens)
        while live:
            live = [g for g in live if next(g, live) is not live]

    def tail(sb):
        for i in range(cps):
            yield from chunk_step(sb, i)
        yield from epilogue(sb)

    if row_major:
        run(prologue(0))
        run(pair_phase(0))
        run(*[sequence_steps(0, p, lo) for p, lo in zip(pairs, los)])
        run(epilogue(0))
    elif nsub == 1:
        run(prologue(0))
        run(pair_phase(0))

        together = SAMPLE_CHUNKS_TOGETHER if cpb == 1 and cps % SAMPLE_CHUNKS_TOGETHER == 0 else 1

        def chunk_body(i, carry):
            run(*[chunk_step(0, i * together + u) for u in range(together)])
            return carry

        lax.fori_loop(0, cps // together, chunk_body, 0)
        run(epilogue(0))
    else:
        run(prologue(0))
        for sb in range(nsub):
            run(pair_phase(sb), *([prologue(sb + 1)] if sb + 1 < nsub else []), *([tail(sb - 1)] if sb else []))
        run(tail(nsub - 1))

    if nb == 1:
        last = p_ref[tb - 1:tb, :].astype(F32).reshape(1, 1, GROUP_PAD)
    else:
        last = p_ref[...].astype(F32).reshape(nb, tb, GROUP_PAD)[:, tb - 1:tb, :]
    carry_ref[...] = last
    shout_ref[...] = last

    @pl.when(j == pl.num_programs(1) - 1)
    def _():
        for p in pairs:
            sp = st_ref[:, p]
            sout_ref[:, 2 * p] = sp[:, :, 0:hd]
            sout_ref[:, 2 * p + 1] = sp[:, :, hd:]


def _rwkv_row_major(nb, tb, c):
    return nb * tb == ROWS and tb == c == SUBLANES


def _rwkv(p, shift_prev, s0, wts, nb, tb, c, pp):
    bsz = s0.shape[0]
    seq = p.shape[0] // bsz
    m_rows = ROWS
    nsub = nb * tb // m_rows
    assert nsub * m_rows == nb * tb and (nsub == 1 or nb == 1)
    hd = RW_HEAD_DIM
    rm = _rwkv_row_major(nb, tb, c)
    small = (1, 1, SUBLANES, LANES)
    const = lambda shape: pl.BlockSpec(shape, lambda i, j: (0,) * len(shape))
    kern = functools.partial(_rwkv_kernel, nb=nb, tb=tb, c=c, pp=pp)
    return pl.pallas_call(
        kern,
        grid=(bsz // nb, seq // tb),
        in_specs=[
            pl.BlockSpec((nb * tb, GROUP_PAD), lambda i, j: (i * (seq // tb) + j, 0)),
            pl.BlockSpec((nb, 1, GROUP_PAD), lambda i, j: (i, 0, 0)),
            pl.BlockSpec((nb, RW_HEADS, hd, hd), lambda i, j: (i, 0, 0, 0)),
            const((1, GROUP_PAD)),
            const((LANES, 2 * RW_WIDTH)),
            const((1, RW_WIDTH)),
            const((1, RW_WIDTH)),
            const((2 * LANES, RW_WIDTH)),
            const((1, RW_WIDTH)),
            const((1, RW_WIDTH)),
            const((1, RW_WIDTH)),
            const((1, RW_WIDTH)),
            const((1, RW_WIDTH)),
            const((RW_WIDTH, LANES)),
            const((LANES, RW_WIDTH)),
            const((STRIP, STRIP)),
        ],
        out_specs=[
            pl.BlockSpec((nb * tb, RW_WIDTH), lambda i, j: (i * (seq // tb) + j, 0)),
            pl.BlockSpec((nb, 1, GROUP_PAD), lambda i, j: (i, 0, 0)),
            pl.BlockSpec((nb, RW_HEADS, hd, hd), lambda i, j: (i, 0, 0, 0)),
        ],
        out_shape=[
            jax.ShapeDtypeStruct((bsz * seq, RW_WIDTH), BF16),
            jax.ShapeDtypeStruct((bsz, 1, GROUP_PAD), F32),
            jax.ShapeDtypeStruct((bsz, RW_HEADS, hd, hd), F32),
        ],
        scratch_shapes=[
            pltpu.VMEM((nb, 1, GROUP_PAD), F32),
            pltpu.VMEM((nb, RW_HEADS // 2, hd, 2 * hd), F32),
            pltpu.VMEM((nsub, 2 * m_rows, RW_WIDTH), F32),
            pltpu.VMEM((nsub, 2 * m_rows, RW_WIDTH), F32),
            pltpu.VMEM((nsub, m_rows, RW_WIDTH), F32),
            pltpu.VMEM((nsub, RW_WIDTH, 2 * m_rows), F32),
            pltpu.VMEM((nsub, RW_WIDTH, m_rows), F32),
            pltpu.VMEM((nsub, RW_HEADS // 2, 2 * hd, 4 * m_rows), BF16),
            pltpu.VMEM((nsub, RW_HEADS // 2, hd, 4 * m_rows), F32),
            pltpu.VMEM((nsub, RW_WIDTH, m_rows), F32),
            pltpu.VMEM((nsub, m_rows, RW_WIDTH), F32),
            pltpu.VMEM((nsub, m_rows, RW_WIDTH), F32),
            pltpu.VMEM((nsub, RW_HEADS // 2, 4 * m_rows, 2 * hd) if rm else small, F32),
            pltpu.VMEM((nsub, m_rows, RW_WIDTH) if rm else small[1:], F32),
            pltpu.VMEM((nsub, m_rows, RW_WIDTH) if rm else small[1:], F32),
        ],
        compiler_params=pltpu.CompilerParams(
            dimension_semantics=("parallel", "arbitrary"), vmem_limit_bytes=VMEM_LIMIT),
        name="rwkv7",
    )(p, shift_prev, s0, *wts)


def _gla_kernel(p_ref, s0_ref, gw2_ref, gb_ref, nw_ref, o_ref, st_ref,
                acc_ref, qd_ref, v_ref, kt_ref, et_ref, *, nb, tb, c, pp):
    j = pl.program_id(1)
    m_rows = nb * tb
    dk, dv = GLA_DK, GLA_DV

    @pl.when(j == 0)
    def _():
        st_ref[...] = s0_ref[...]

    p = p_ref[...].astype(F32)
    q = p[:, 0:GLA_KEY_WIDTH] * (dk ** -0.5)
    k = p[:, GLA_OFF_K:GLA_OFF_K + GLA_KEY_WIDTH]
    v = p[:, GLA_OFF_V:GLA_OFF_V + GLA_WIDTH]
    xs = p[:, GLA_OFF_GATE:GLA_OFF_GATE + LANES].astype(BF16)
    gout = p[:, GLA_OFF_GOUT:GLA_OFF_GOUT + GLA_WIDTH]
    gk = -_softplus(-(_dot(xs, gw2_ref[...]) + gb_ref[...])) / GLA_GATE_NORM

    same, ri, ci = _chunk_masks(m_rows, c)
    lmask = jnp.where(same & (ci <= ri), 1.0, 0.0).astype(BF16)
    causal = same & (ci <= ri)
    gcum = _mm(lmask, gk, pb=3)
    g3 = gcum.reshape(m_rows // c, c, GLA_KEY_WIDTH)
    gtot = jnp.broadcast_to(g3[:, c - 1:c, :], g3.shape).reshape(m_rows, GLA_KEY_WIDTH)
    qd = q * jnp.exp(gcum)
    kinv = k * jnp.exp(-gcum)
    qd_ref[...] = qd
    v_ref[...] = v
    kt_ref[...] = (k * jnp.exp(gtot - gcum)).T
    et_ref[...] = jnp.exp(gtot).T

    heads = range(GLA_HEADS)
    kq = [slice(h * dk, (h + 1) * dk) for h in heads]
    vq = [slice(h * dv, (h + 1) * dv) for h in heads]
    scores = [jnp.where(causal, _mm(qd[:, ks], kinv[:, ks], pp, pp, NT), 0.0) for ks in kq]
    intra = [_mm(a, v[:, vs], pp, pp) for a, vs in zip(scores, vq)]
    for vs, o in zip(vq, intra):
        acc_ref[:, vs] = o

    sh_c = c.bit_length() - 1
    cpb = tb // c
    span = min(m_rows, LANES)
    row_id = lax.broadcasted_iota(jnp.int32, (span, dv), 0) >> sh_c
    for i in range(m_rows // c):
        b = i // cpb
        r0 = i * c
        t0 = (r0 // span) * span
        rmask = row_id == (r0 - t0) // c
        states = [st_ref[b, h] for h in heads]
        inter = [_mm(qd_ref[r0:r0 + c, ks], s, pp, pp) for ks, s in zip(kq, states)]
        upd = [_mm(kt_ref[ks, t0:t0 + span], jnp.where(rmask, v_ref[t0:t0 + span, vs], 0.0), pp, pp)
               for ks, vs in zip(kq, vq)]
        for h, ks, vs, s, oi, d in zip(heads, kq, vq, states, inter, upd):
            acc_ref[r0:r0 + c, vs] += oi
            st_ref[b, h] = s * et_ref[ks, r0:r0 + 1] + d

    for vs in vq:
        o = acc_ref[:, vs]
        on = o * lax.rsqrt(jnp.mean(o * o, axis=-1, keepdims=True) + HEAD_NORM_EPS) * nw_ref[...]
        gh = gout[:, vs]
        o_ref[:, vs] = (on * (gh * _sigmoid(gh))).astype(o_ref.dtype)


def _gla(p, s0, wts, nb, tb, c, pp):
    bsz = s0.shape[0]
    seq = p.shape[0] // bsz
    m_rows = nb * tb
    const = lambda shape: pl.BlockSpec(shape, lambda i, j: (0,) * len(shape))
    kern = functools.partial(_gla_kernel, nb=nb, tb=tb, c=c, pp=pp)
    return pl.pallas_call(
        kern,
        grid=(bsz // nb, seq // tb),
        in_specs=[
            pl.BlockSpec((nb * tb, GROUP_PAD), lambda i, j: (i * (seq // tb) + j, 1)),
            pl.BlockSpec((nb, GLA_HEADS, GLA_DK, GLA_DV), lambda i, j: (i, 0, 0, 0)),
            const((LANES, GLA_KEY_WIDTH)),
            const((1, GLA_KEY_WIDTH)),
            const((1, GLA_DV)),
        ],
        out_specs=[
            pl.BlockSpec((nb * tb, GLA_WIDTH), lambda i, j: (i * (seq // tb) + j, 0)),
            pl.BlockSpec((nb, GLA_HEADS, GLA_DK, GLA_DV), lambda i, j: (i, 0, 0, 0)),
        ],
        out_shape=[
            jax.ShapeDtypeStruct((bsz * seq, GLA_WIDTH), BF16),
            jax.ShapeDtypeStruct((bsz, GLA_HEADS, GLA_DK, GLA_DV), F32),
        ],
        scratch_shapes=[
            pltpu.VMEM((m_rows, GLA_WIDTH), F32),
            pltpu.VMEM((m_rows, GLA_KEY_WIDTH), F32),
            pltpu.VMEM((m_rows, GLA_WIDTH), F32),
            pltpu.VMEM((GLA_KEY_WIDTH, m_rows), F32),
            pltpu.VMEM((GLA_KEY_WIDTH, m_rows), F32),
        ],
        compiler_params=pltpu.CompilerParams(
            dimension_semantics=("parallel", "arbitrary"), vmem_limit_bytes=VMEM_LIMIT),
        name="gla",
    )(p, s0, *wts)


def _post_kernel(x_ref, orw_ref, ogla_ref, wo_ref, g2_ref, wu_ref, wd_ref, gf_ref, o_ref, h_ref):
    jf = pl.program_id(1)

    @pl.when(jf == 0)
    def _():
        x1 = (x_ref[...]
              + _dot(orw_ref[...], wo_ref[0:RW_WIDTH, :])
              + _dot(ogla_ref[...], wo_ref[RW_WIDTH:, :]))
        o_ref[...] = x1
        ms = jnp.mean(x1 * x1, axis=-1, keepdims=True)
        h_ref[...] = (x1 * lax.rsqrt(ms + NORM_EPS) * g2_ref[...]).astype(BF16)

    u = jnp.maximum(_dot(h_ref[...], wu_ref[...]), 0.0)
    o_ref[...] += _dot((u * u).astype(BF16), wd_ref[...])

    @pl.when(jf == pl.num_programs(1) - 1)
    def _():
        x2 = o_ref[...]
        ms = jnp.mean(x2 * x2, axis=-1, keepdims=True)
        o_ref[...] = x2 * lax.rsqrt(ms + NORM_EPS) * gf_ref[...]


def _post(x2d, o_rw, o_gla, w_out, g2, w_up, w_down, gf, tm, tf):
    t = x2d.shape[0]
    return pl.pallas_call(
        _post_kernel,
        grid=(t // tm, D_FF // tf),
        in_specs=[
            pl.BlockSpec((tm, D_MODEL), lambda i, j: (i, 0)),
            pl.BlockSpec((tm, RW_WIDTH), lambda i, j: (i, 0)),
            pl.BlockSpec((tm, GLA_WIDTH), lambda i, j: (i, 0)),
            pl.BlockSpec((D_MODEL, D_MODEL), lambda i, j: (0, 0), pipeline_mode=pl.Buffered(1)),
            pl.BlockSpec((1, D_MODEL), lambda i, j: (0, 0)),
            pl.BlockSpec((D_MODEL, tf), lambda i, j: (0, j)),
            pl.BlockSpec((tf, D_MODEL), lambda i, j: (j, 0)),
            pl.BlockSpec((1, D_MODEL), lambda i, j: (0, 0)),
        ],
        out_specs=pl.BlockSpec((tm, D_MODEL), lambda i, j: (i, 0)),
        out_shape=jax.ShapeDtypeStruct((t, D_MODEL), F32),
        scratch_shapes=[pltpu.VMEM((tm, D_MODEL), BF16)],
        compiler_params=pltpu.CompilerParams(
            dimension_semantics=("parallel", "arbitrary"), vmem_limit_bytes=VMEM_LIMIT),
        name="post",
    )(x2d, o_rw, o_gla, w_out, g2, w_up, w_down, gf)


def _pad_cols(w, n):
    return jnp.pad(w, ((0, 0), (0, n - w.shape[1])))


def _prep_weights(w_in, rw_mu, rw_w0, rw_w2, rw_a0, rw_a2, rw_g2, rw_k_k, rw_k_a, rw_r_k, rw_ln_w, rw_ln_b,
                  gla_gw2, gla_gb, gla_norm_w):
    w_p = jnp.swapaxes(w_in, 0, 1).astype(BF16)

    row = lambda x: x.reshape(1, -1).astype(F32)
    mu = _pad_cols(row(rw_mu), GROUP_PAD)
    wwa = jnp.zeros((LANES, 2 * RW_WIDTH), F32)
    wwa = wwa.at[0:RW_LORA_W, 0:RW_WIDTH].set(rw_w2)
    wwa = wwa.at[RW_LORA_W:RW_LORA_W + RW_LORA_A, RW_WIDTH:].set(rw_a2)
    g2p = jnp.pad(rw_g2, ((0, 2 * LANES - RW_LORA_G), (0, 0)))
    head_of_col = jnp.arange(RW_WIDTH) // RW_HEAD_DIM
    e1 = (head_of_col[:, None] == jnp.arange(LANES)[None, :]).astype(BF16)
    rw_wts = (mu, wwa.astype(BF16), row(rw_w0), row(rw_a0), g2p.astype(BF16), row(rw_k_k), row(rw_k_a),
              row(rw_r_k), row(rw_ln_w), row(rw_ln_b), e1, e1.T,
              (head_of_col[:STRIP, None] == head_of_col[None, :STRIP]).astype(BF16))
    gw2p = jnp.pad(gla_gw2, ((0, LANES - GLA_GATE_RANK), (0, 0))).astype(BF16)
    gla_wts = (gw2p, row(gla_gb), row(gla_norm_w))
    return w_p, rw_wts, gla_wts


RW_PIECES = 1
GLA_PIECES = 1
PAIR_LOCKSTEP = 8
SAMPLE_CHUNKS_TOGETHER = 16
RW_STEP_ROWS = 4 * ROWS
GLA_STEP_ROWS = 2 * ROWS
STRIP = 256


def _trunk(x, shift, s_rw, s_gla, norm1_g, w_p, rw_wts, gla_wts, w_out, norm2_g, w_up, w_down, norm_f_g,
           nb, tb, c_rw, c_gla, tm):
    bsz, seq, _ = x.shape
    t = bsz * seq
    x2d = x.reshape(t, D_MODEL)
    proj = _inproj(x2d, norm1_g.reshape(1, -1), w_p, tm, INPROJ_TN)
    shift_p = _pad_cols(shift, GROUP_PAD).reshape(bsz, 1, GROUP_PAD)
    rw_tb = RW_STEP_ROWS if nb == 1 and seq % RW_STEP_ROWS == 0 else tb
    o_rw, sh_new, s_rw_new = _rwkv(proj, shift_p, s_rw, rw_wts, nb, rw_tb, c_rw, RW_PIECES)
    gla_tb = GLA_STEP_ROWS if nb == 1 and seq % GLA_STEP_ROWS == 0 else tb
    o_gla, s_gla_new = _gla(proj, s_gla, gla_wts, nb, gla_tb, c_gla, GLA_PIECES)
    y = _post(x2d, o_rw, o_gla, w_out, norm2_g.reshape(1, -1),
              w_up, w_down, norm_f_g.reshape(1, -1), POST_TM, POST_TF)
    return (y.reshape(bsz, seq, D_MODEL), sh_new[:, 0, :RW_PROJ][None], s_rw_new[None], s_gla_new[None])


def _transpose_kernel(x_ref, o_ref):
    o_ref[...] = x_ref[...].T


def _batch_major(s, tk=4096):
    b = s.shape[0]
    n = s.size // b
    x = jnp.moveaxis(s, 0, -1).reshape(n, b)
    out = pl.pallas_call(
        _transpose_kernel,
        grid=(n // tk,),
        in_specs=[pl.BlockSpec((tk, b), lambda i: (i, 0))],
        out_specs=pl.BlockSpec((b, tk), lambda i: (0, i)),
        out_shape=jax.ShapeDtypeStruct((b, n), s.dtype),
        compiler_params=pltpu.CompilerParams(dimension_semantics=("parallel",), vmem_limit_bytes=VMEM_LIMIT),
        name="batch_major",
    )(x)
    return out.reshape(s.shape)


def kernel(x_prompt, x_sample, state_rwkv_shift, state_rwkv_wkv, state_gla, norm1_g, w_in, rw_mu, rw_w0,
           rw_w2, rw_a0, rw_a2, rw_g2, rw_k_k, rw_k_a, rw_r_k, rw_ln_w, rw_ln_b, gla_gw2, gla_gb, gla_norm_w,
           w_out, norm2_g, w_up, w_down, norm_f_g):
    w_p, rw_wts, gla_wts = _prep_weights(
        w_in[0], rw_mu[0], rw_w0[0], rw_w2[0], rw_a0[0], rw_a2[0], rw_g2[0], rw_k_k[0], rw_k_a[0],
        rw_r_k[0].reshape(-1), rw_ln_w[0], rw_ln_b[0], gla_gw2[0], gla_gb[0], gla_norm_w[0])
    shared = (norm1_g[0], w_p, rw_wts, gla_wts, w_out[0].astype(BF16), norm2_g[0], w_up[0].astype(BF16),
              w_down[0].astype(BF16), norm_f_g)

    bp, lp, _ = x_prompt.shape
    bs, ls, _ = x_sample.shape
    dt = x_prompt.dtype
    out_p = _trunk(x_prompt, jnp.zeros((bp, RW_PROJ), dt),
                   jnp.zeros((bp, RW_HEADS, RW_HEAD_DIM, RW_HEAD_DIM), dt),
                   jnp.zeros((bp, GLA_HEADS, GLA_DK, GLA_DV), dt), *shared,
                   nb=1, tb=ROWS, c_rw=RW_CHUNK, c_gla=GLA_CHUNK, tm=INPROJ_TM)
    out_s = _trunk(x_sample, state_rwkv_shift[0], _batch_major(state_rwkv_wkv[0]), state_gla[0], *shared,
                   nb=ROWS // ls, tb=ls, c_rw=ls, c_gla=ls, tm=INPROJ_TM)
    return (out_p[0], out_s[0], out_p[1], out_p[2], out_p[3], out_s[1], out_s[2], out_s[3])
```

```python
import functools

import jax
import jax.numpy as jnp
from jax import lax
from jax.experimental import pallas as pl
from jax.experimental.pallas import tpu as pltpu

F32 = jnp.float32
BF16 = jnp.bfloat16

D_MODEL = 2048
RW_WIDTH = 1024
RW_HEADS = 16
RW_HEAD_DIM = 64
RW_LORA_W = 64
RW_LORA_A = 64
RW_LORA_G = 160
RW_PROJ = 3 * RW_WIDTH + RW_LORA_W + RW_LORA_A + RW_LORA_G
RW_GN_EPS = 64e-5
GLA_WIDTH = 1024
GLA_HEADS = 4
GLA_KEY_WIDTH = 512
GLA_DK = 128
GLA_DV = 256
GLA_GATE_RANK = 16
GLA_GATE_NORM = 16.0
GLA_CHUNK = 64
GLA_PROJ = 2 * GLA_KEY_WIDTH + GLA_WIDTH + GLA_GATE_RANK + GLA_WIDTH
D_FF = 4 * D_MODEL
NORM_EPS = 1e-6
HEAD_NORM_EPS = 1e-5
DECAY_SCALE = 0.6065306597126334

LANES = 128
SUBLANES = 8

GROUP_PAD = 3456
RW_OFF_XWA = 3 * RW_WIDTH
RW_OFF_XG = RW_OFF_XWA + LANES
GLA_OFF_K = GLA_KEY_WIDTH
GLA_OFF_V = 2 * GLA_KEY_WIDTH
GLA_OFF_GATE = GLA_OFF_V + GLA_WIDTH
GLA_OFF_GOUT = GLA_OFF_GATE + LANES
NP = 2 * GROUP_PAD

ROWS = 128
MXU_WIDTH = 256
INPROJ_TM = 1024
INPROJ_TN = 9 * MXU_WIDTH
RW_CHUNK = 64
PROJ_DTYPE = BF16
POST_TM = 512
POST_TF = 1024
VMEM_LIMIT = 56 * 1024 * 1024
INPROJ_VMEM_LIMIT = 61 * 1024 * 1024

NN = (((1,), (0,)), ((), ()))
NT = (((1,), (1,)), ((), ()))


def _dot(a, b, dims=NN):
    return lax.dot_general(a, b, dims, preferred_element_type=F32)


def _parts(x, n):
    if x.dtype == BF16:
        return [x]
    out = []
    rem = x
    for i in range(n):
        h = rem.astype(BF16)
        out.append(h)
        if i + 1 < n:
            rem = rem - h.astype(F32)
    return out


def _mm(a, b, pa=1, pb=1, dims=NN):
    pa = 1 if a.dtype == BF16 else pa
    pb = 1 if b.dtype == BF16 else pb
    aa = _parts(a, pa)
    bb = _parts(b, pb)
    n = max(pa, pb)
    if dims == NN and min(pa, pb) == 1 and n * a.shape[1] <= MXU_WIDTH:
        return _dot(jnp.concatenate(aa * (n // pa), axis=1), jnp.concatenate(bb * (n // pb), axis=0))
    acc = None
    for i in range(pa):
        for j in range(pb):
            if i + j < n:
                t = _dot(aa[i], bb[j], dims)
                acc = t if acc is None else acc + t
    return acc


def _softplus(x):
    return jnp.maximum(x, 0.0) + jnp.log(1.0 + jnp.exp(-jnp.abs(x)))


def _sigmoid(x):
    return 0.5 * jnp.tanh(0.5 * x) + 0.5


def _chunk_masks(m, c):
    sh = c.bit_length() - 1
    ri = lax.broadcasted_iota(jnp.int32, (m, m), 0)
    ci = lax.broadcasted_iota(jnp.int32, (m, m), 1)
    same = (ri >> sh) == (ci >> sh)
    return same, ri, ci


_W_SEGMENTS = (
    (0, RW_PROJ, 0),
    (RW_PROJ, GLA_OFF_GATE + GLA_GATE_RANK, GROUP_PAD),
    (RW_PROJ + GLA_OFF_GATE + GLA_GATE_RANK, GLA_WIDTH, GROUP_PAD + GLA_OFF_GOUT),
)


def _weight_windows(tn):
    blocks = []
    for lo in range(0, NP, tn):
        wins = []
        for src, ln, dst in _W_SEGMENTS:
            a, b = max(dst, lo), min(dst + ln, lo + tn)
            if a < b:
                wins.append((src + a - dst, b - a, a - lo))
        blocks.append(tuple(wins))
    return tuple(blocks)


def _inproj_kernel(*refs, windows, first_blocks):
    n_x = len(first_blocks) - 1
    x_refs, (g_ref, w_hbm, o_ref, h_ref, wbuf, sem) = refs[:n_x], refs[n_x:]
    i, j = pl.program_id(0), pl.program_id(1)
    nj = len(windows)
    step = i * nj + j
    n_steps = pl.num_programs(0) * nj
    slot = lax.rem(step, 2)

    def copies(jb, sl):
        return [pltpu.make_async_copy(w_hbm.at[pl.ds(src, ln), :], wbuf.at[sl, pl.ds(dst, ln), :], sem.at[sl, k])
                for k, (src, ln, dst) in enumerate(windows[jb])]

    @pl.when(step == 0)
    def _():
        for sl in range(min(2, nj)):
            covered = sorted((dst, dst + ln) for _, ln, dst in windows[sl])
            edges = [0] + [e for span in covered for e in span] + [wbuf.shape[1]]
            for a, b in zip(edges[0::2], edges[1::2]):
                if a < b:
                    wbuf[sl, a:b, :] = jnp.zeros((b - a, wbuf.shape[2]), wbuf.dtype)
        for cp in copies(0, 0):
            cp.start()

    for jb in range(nj):
        @pl.when(j == jb)
        def _(jb=jb):
            @pl.when(step + 1 < n_steps)
            def _():
                for cp in copies((jb + 1) % nj, 1 - slot):
                    cp.start()

            for cp in copies(jb, slot):
                cp.wait()

    for x_ref, lo, hi in zip(x_refs, first_blocks[:-1], first_blocks[1:]):
        @pl.when((j == 0) & (i >= lo) & (i < hi))
        def _(x_ref=x_ref):
            x = x_ref[...]
            ms = jnp.mean(x * x, axis=-1, keepdims=True)
            h_ref[...] = (x * lax.rsqrt(ms + NORM_EPS) * g_ref[...]).astype(BF16)

    o_ref[...] = _dot(h_ref[...], wbuf[slot], NT).astype(o_ref.dtype)


def _inproj(xs, g, w_t, tm, tn):
    first_blocks = [0]
    for x in xs:
        first_blocks.append(first_blocks[-1] + x.shape[0] // tm)
    t = first_blocks[-1] * tm
    windows = _weight_windows(tn)

    def x_spec(lo, hi):
        return pl.BlockSpec((tm, D_MODEL), lambda i, j: (jnp.clip(i - lo, 0, hi - lo - 1), 0),
                            pipeline_mode=pl.Buffered(1 if hi - lo == 1 else 2))

    return pl.pallas_call(
        functools.partial(_inproj_kernel, windows=windows, first_blocks=tuple(first_blocks)),
        grid=(t // tm, NP // tn),
        in_specs=[
            *[x_spec(lo, hi) for lo, hi in zip(first_blocks[:-1], first_blocks[1:])],
            pl.BlockSpec((1, D_MODEL), lambda i, j: (0, 0)),
            pl.BlockSpec(memory_space=pltpu.HBM),
        ],
        out_specs=pl.BlockSpec((tm, tn), lambda i, j: (i, j)),
        out_shape=jax.ShapeDtypeStruct((t, NP), PROJ_DTYPE),
        scratch_shapes=[
            pltpu.VMEM((tm, D_MODEL), BF16),
            pltpu.VMEM((2, tn, D_MODEL), BF16),
            pltpu.SemaphoreType.DMA((2, max(len(w) for w in windows))),
        ],
        compiler_params=pltpu.CompilerParams(
            dimension_semantics=("arbitrary", "arbitrary"), vmem_limit_bytes=INPROJ_VMEM_LIMIT),
        name="inproj",
    )(*xs, g, w_t)


def _rwkv_kernel(p_ref, sh_ref, s0_ref, mu_ref, wwa_ref, w0_ref, a0_ref, g2_ref, kk_ref, ka_ref,
                 rk_ref, lnw_ref, lnb_ref, e1_ref, e2_ref, e12_ref,
                 o_ref, shout_ref, sout_ref,
                 carry_ref, st_ref, bk_ref, kb_ref, pc_ref, art_ref, vt_ref, arbd_ref, uy_ref, yt_ref,
                 g_ref, bonus_ref, zrt_ref, vrm_ref, yrm_ref, *, nb, tb, c, pp):
    j = pl.program_id(1)
    m_rows = ROWS
    nsub = nb * tb // m_rows
    tbs = tb // nsub
    n_lvl = c.bit_length() - 1
    hd = RW_HEAD_DIM
    row_major = _rwkv_row_major(nb, tb, c)

    @pl.when(j == 0)
    def _():
        for p in range(RW_HEADS // 2):
            st_ref[:, p] = jnp.concatenate([s0_ref[:, 2 * p], s0_ref[:, 2 * p + 1]], axis=-1)
        carry_ref[...] = sh_ref[...]

    if not row_major:
        yt_ref[...] = jnp.zeros_like(yt_ref)

    def shifted(sb, c0, w):
        r0 = sb * m_rows
        pc3 = p_ref[r0:r0 + m_rows, c0:c0 + w].astype(F32).reshape(nb, tbs, w)
        if sb == 0:
            before = carry_ref[:, :, c0:c0 + w]
        else:
            before = p_ref[r0 - 1:r0, c0:c0 + w].astype(F32).reshape(1, 1, w)
        t3 = lax.broadcasted_iota(jnp.int32, pc3.shape, 1)
        prev3 = jnp.where(t3 == 0, before, pltpu.roll(pc3, 1, axis=1))
        return (pc3 + (prev3 - pc3) * mu_ref[:, c0:c0 + w]).reshape(m_rows, w)

    same, ri, ci = _chunk_masks(m_rows, c)
    lmask = jnp.where(same & (ci <= ri), 1.0, 0.0).astype(BF16)
    strict_t = same & (ri < ci)
    incl_t = same & (ri <= ci)

    sw = STRIP
    strips = [slice(c0, c0 + sw) for c0 in range(0, RW_WIDTH, sw)]

    def head_stat(x, cs):
        return _dot(x.astype(BF16), e1_ref[cs, :])

    def head_bcast(s, cs):
        return _mm(s, e2_ref[:, cs], pa=2)

    def head_sum(x):
        return _dot(x.astype(BF16), e12_ref[...])


    def prologue(sb):
        slab = shifted(sb, RW_OFF_XWA, LANES)
        lane = lax.broadcasted_iota(jnp.int32, slab.shape, 1)
        lhs = jnp.where(lane < RW_LORA_W, jnp.tanh(slab), slab).astype(BF16)
        sg = _sigmoid(shifted(sb, RW_OFF_XG, 2 * LANES)).astype(BF16)
        yield
        lws = [_dot(lhs, wwa_ref[:, cs]) for cs in strips]
        las = [_dot(lhs, wwa_ref[:, RW_WIDTH + cs.start:RW_WIDTH + cs.stop]) for cs in strips]
        for cs in strips:
            g_ref[sb, :, cs] = _dot(sg, g2_ref[:, cs])
        yield
        logws = [-DECAY_SCALE * _sigmoid(w0_ref[:, cs] + lw) for cs, lw in zip(strips, lws)]
        gcums = [_mm(lmask, logw, pb=2) for logw in logws]
        yield
        avs = [_sigmoid(a0_ref[:, cs] + la) for cs, la in zip(strips, las)]
        ks = [shifted(sb, RW_WIDTH + cs.start, sw) for cs in strips]
        kkfs = [k * kk_ref[:, cs] for cs, k in zip(strips, ks)]
        rinvs = [lax.rsqrt(jnp.maximum(head_stat(kkf * kkf, cs), 1e-24)) for cs, kkf in zip(strips, kkfs)]
        yield
        k2s = [k * (1.0 + (a - 1.0) * ka_ref[:, cs]) for cs, k, a in zip(strips, ks, avs)]
        rs = [shifted(sb, cs.start, sw) for cs in strips]
        bsums = [head_sum(r * k2 * rk_ref[:, cs]) for cs, r, k2 in zip(strips, rs, k2s)]
        kks = [kkf * head_bcast(rinv, cs) for cs, kkf, rinv in zip(strips, kkfs, rinvs)]
        yield
        for cs, bsum in zip(strips, bsums):
            v = shifted(sb, 2 * RW_WIDTH + cs.start, sw)
            bonus_ref[sb, :, cs] = bsum * v
            vt_ref[sb, cs, :] = v.T
            if row_major:
                vrm_ref[sb, :, cs] = v
        yield
        for cs, logw, gcum, a, kk, k2, r in zip(strips, logws, gcums, avs, kks, k2s, rs):
            beta = kk * a
            g3 = gcum.reshape(cps, c, sw)
            gtot = jnp.broadcast_to(g3[:, c - 1:c, :], g3.shape).reshape(m_rows, sw)
            e_inv = jnp.exp(-gcum)
            e_rev = jnp.exp(gtot - gcum)
            art_ref[sb, cs, 0:m_rows] = (-kk * jnp.exp(gcum - logw)).T
            art_ref[sb, cs, m_rows:2 * m_rows] = (r * jnp.exp(gcum)).T
            bk_ref[sb, 0:m_rows, cs] = beta * e_inv
            bk_ref[sb, m_rows:, cs] = k2 * e_inv
            kb_ref[sb, 0:m_rows, cs] = k2 * e_rev
            kb_ref[sb, m_rows:, cs] = beta * e_rev
            pc_ref[sb, :, cs] = jnp.exp(gtot)
            yield

    pw = 2 * hd
    cat = jnp.concatenate
    top = lax.broadcasted_iota(jnp.int32, (pw, 1), 0) < hd
    left = lax.broadcasted_iota(jnp.int32, (1, 2 * m_rows), 1) < m_rows

    def split_rows(x):
        return cat([jnp.where(top, x, 0.0), jnp.where(top, 0.0, x)], axis=1)

    def block_diag(xc):
        return cat([jnp.where(left, xc, 0.0), jnp.where(left, 0.0, xc)], axis=0)

    def blk(xt, u, rh, ch):
        col = (2 * u + ch) * m_rows
        return xt[rh * m_rows:(rh + 1) * m_rows, col:col + m_rows]

    n_pairs = RW_HEADS // 2
    pairs = range(n_pairs)
    los = [p * pw for p in pairs]

    def pair_phase(sb):
        for g0 in range(0, n_pairs, PAIR_LOCKSTEP):
            yield from pair_group(sb, pairs[g0:g0 + PAIR_LOCKSTEP], los[g0:g0 + PAIR_LOCKSTEP])

    def pair_group(sb, pairs, los):
        arts = [art_ref[sb, lo:lo + pw, :] for lo in los]
        xts = [_mm(bk_ref[sb, :, lo:lo + pw], split_rows(art), pp, pp) for lo, art in zip(los, arts)]
        yield
        npcs = [cat([jnp.where(strict_t, blk(xt, 0, 0, 0), 0.0), jnp.where(strict_t, blk(xt, 1, 0, 0), 0.0)], axis=1)
                for xt in xts]
        vas = [_mm(split_rows(vt_ref[sb, lo:lo + pw, :]),
                   cat([cat([jnp.where(strict_t, blk(xt, u, 1, 0), 0.0), jnp.where(incl_t, blk(xt, u, 1, 1), 0.0)],
                            axis=1) for u in range(2)], axis=0), pp, pp)
               for lo, xt in zip(los, xts)]
        yield
        zcs = [cat([cat([art[u * hd:(u + 1) * hd, 0:m_rows], va[u * hd:(u + 1) * hd, 0:m_rows]], axis=0)
                    for u in range(2)], axis=1) for art, va in zip(arts, vas)]
        for lvl in range(n_lvl):
            bds = [block_diag(npc) for npc in npcs]
            if lvl + 1 < n_lvl:
                ts = [_mm(cat([zc, npc], axis=0), bd, pp, pp) for zc, npc, bd in zip(zcs, npcs, bds)]
                zcs = [zc + t[0:2 * hd] for zc, t in zip(zcs, ts)]
                npcs = [t[2 * hd:] for t in ts]
            else:
                zcs = [zc + _mm(zc, bd, pp, pp) for zc, bd in zip(zcs, bds)]
            yield
        rycs = [cat([cat([art[u * hd:(u + 1) * hd, m_rows:], va[u * hd:(u + 1) * hd, m_rows:]], axis=0)
                     for u in range(2)], axis=1)
                + _mm(zc, block_diag(cat([jnp.where(incl_t, blk(xt, 0, 0, 1), 0.0),
                                          jnp.where(incl_t, blk(xt, 1, 0, 1), 0.0)], axis=1)), pp, pp)
                for art, va, zc, xt in zip(arts, vas, zcs, xts)]
        yield
        if row_major:
            for p, zc, ry in zip(pairs, zcs, rycs):
                zrt_ref[sb, p, 0:2 * m_rows] = zc.T
                zrt_ref[sb, p, 2 * m_rows:] = ry.T
            yield
            return
        zero = jnp.zeros((hd, 2 * m_rows), BF16)
        for p, zc, ry in zip(pairs, zcs, rycs):
            arbd_ref[sb, p, 0:hd, 0:2 * m_rows] = cat([zc[0:hd, 0:m_rows], ry[0:hd, 0:m_rows]], axis=1).astype(BF16)
            arbd_ref[sb, p, 0:hd, 2 * m_rows:] = zero
            arbd_ref[sb, p, hd:, 0:2 * m_rows] = zero
            arbd_ref[sb, p, hd:, 2 * m_rows:] = cat([zc[0:hd, m_rows:], ry[0:hd, m_rows:]], axis=1).astype(BF16)
            uy_ref[sb, p] = cat([zc[hd:, 0:m_rows], ry[hd:, 0:m_rows], zc[hd:, m_rows:], ry[hd:, m_rows:]], axis=1)
        yield

    sh_c = c.bit_length() - 1
    cps = m_rows // c
    cpb = tbs // c
    row_id2 = (lax.broadcasted_iota(jnp.int32, (2 * m_rows, pw), 0) & (m_rows - 1)) >> sh_c
    col_id = lax.broadcasted_iota(jnp.int32, (hd, m_rows), 1) >> sh_c
    lane_lo = lax.broadcasted_iota(jnp.int32, (1, pw), 1) < hd

    def chunk_step(sb, i):
        b = i // cpb
        r0 = i * c
        rmask2 = row_id2 == i
        cmask = col_id == i
        ss = [st_ref[b, p] for p in pairs]
        zss = [_mm(s, arbd_ref[sb, p], pp, pp) + uy_ref[sb, p] for p, s in zip(pairs, ss)]
        yield
        upd = []
        for lo, zs in zip(los, zss):
            kbz = jnp.where(rmask2, kb_ref[sb, :, lo:lo + pw], 0.0)
            rhs = cat([jnp.where(lane_lo, kbz, 0.0), jnp.where(lane_lo, 0.0, kbz)], axis=0)
            vu = cat([vt_ref[sb, lo:lo + hd, :], zs[:, 0:m_rows],
                      vt_ref[sb, lo + hd:lo + pw, :], zs[:, 2 * m_rows:3 * m_rows]], axis=1)
            upd.append(_mm(vu, rhs, pp, pp))
        yield
        for p, lo, s, zs, d in zip(pairs, los, ss, zss, upd):
            st_ref[b, p] = s * pc_ref[sb, pl.ds(r0, 1), lo:lo + pw] + d
            for u in range(2):
                rows = slice((2 * p + u) * hd, (2 * p + u + 1) * hd)
                yt_ref[sb, rows, :] = jnp.where(cmask, zs[:, (2 * u + 1) * m_rows:(2 * u + 2) * m_rows],
                                               yt_ref[sb, rows, :])
        yield

    def swap_halves(x):
        return pltpu.roll(x, hd, axis=1)

    def sequence_steps(sb, p, lo):
        gsz = LANES // 4
        per_grp = gsz // c
        groups = range(m_rows // gsz)
        k_seq = (lax.broadcasted_iota(jnp.int32, (LANES, 1), 0) & (gsz - 1)) >> sh_c
        zt = zrt_ref[sb, p, 0:2 * m_rows]
        rt = zrt_ref[sb, p, 2 * m_rows:]
        a0 = jnp.where(lane_lo, zt[0:m_rows], 0.0)
        a1 = jnp.where(lane_lo, 0.0, swap_halves(zt[m_rows:]))
        r0 = jnp.where(lane_lo, rt[0:m_rows], 0.0)
        r1 = jnp.where(lane_lo, 0.0, swap_halves(rt[m_rows:]))
        res = []
        for b in range(nb):
            rb = slice(b * c, (b + 1) * c)
            rb1 = slice(m_rows + b * c, m_rows + (b + 1) * c)
            s = st_ref[b, p]
            lhs = cat([a0[rb], a1[rb], r0[rb], r1[rb]], axis=0)
            add = cat([zt[rb], zt[rb1], rt[rb], rt[rb1]], axis=0)
            res.append(_mm(lhs, cat([s, s], axis=0), pp, pp, NT) + add)
        yield
        for b, x in enumerate(res):
            yrm_ref[sb, b * c:(b + 1) * c, lo:lo + pw] = jnp.where(
                lane_lo, swap_halves(x[2 * c:3 * c]), x[3 * c:4 * c])
        vp = vrm_ref[sb, :, lo:lo + pw]
        vs = swap_halves(vp)
        vuts = []
        for q in groups:
            rq = slice(q * gsz, (q + 1) * gsz)
            members = range(q * per_grp, (q + 1) * per_grp)
            u0 = cat([res[b][0:c] for b in members], axis=0)
            u1 = cat([res[b][c:2 * c] for b in members], axis=0)
            vuts.append(cat([vs[rq], u0, vp[rq], u1], axis=0).T[hd:, :])
        yield
        ds = []
        for q, vut in zip(groups, vuts):
            kq = kb_ref[sb, q * gsz:(q + 1) * gsz, lo:lo + pw]
            bq = kb_ref[sb, m_rows + q * gsz:m_rows + (q + 1) * gsz, lo:lo + pw]
            rhs = cat([jnp.where(lane_lo, kq, 0.0), jnp.where(lane_lo, bq, 0.0),
                       jnp.where(lane_lo, 0.0, kq), jnp.where(lane_lo, 0.0, bq)], axis=0)
            ds.append(_mm(vut, cat([jnp.where(k_seq == n, rhs, 0.0) for n in range(per_grp)], axis=1), pp, pp))
        yield
        for q, d in zip(groups, ds):
            for n in range(per_grp):
                b = q * per_grp + n
                st_ref[b, p] = st_ref[b, p] * pc_ref[sb, b * c:b * c + 1, lo:lo + pw] + d[:, n * pw:(n + 1) * pw]
        yield

    inv_n = 1.0 / hd

    def epilogue(sb):
        ys = [yrm_ref[sb, :, cs] if row_major else yt_ref[sb, cs, :].T for cs in strips]
        ycs = [y - head_sum(y) * inv_n for y in ys]
        yield
        rstds = [lax.rsqrt(head_stat(yc * yc, cs) * inv_n + RW_GN_EPS) for cs, yc in zip(strips, ycs)]
        yield
        for cs, yc, rstd in zip(strips, ycs, rstds):
            yn = yc * head_bcast(rstd, cs) * lnw_ref[:, cs] + lnb_ref[:, cs]
            o_ref[sb * m_rows:(sb + 1) * m_rows, cs] = (
                (yn + bonus_ref[sb, :, cs]) * g_ref[sb, :, cs]).astype(o_ref.dtype)
        yield

    def run(*gens):
        live = list(gens)
        while live:
            live = [g for g in live if next(g, live) is not live]

    def tail(sb):
        for i in range(cps):
            yield from chunk_step(sb, i)
        yield from epilogue(sb)

    if row_major:
        run(prologue(0))
        run(pair_phase(0))
        run(*[sequence_steps(0, p, lo) for p, lo in zip(pairs, los)])
        run(epilogue(0))
    elif nsub == 1:
        run(prologue(0))
        run(pair_phase(0))

        together = SAMPLE_CHUNKS_TOGETHER if cpb == 1 and cps % SAMPLE_CHUNKS_TOGETHER == 0 else 1

        def chunk_body(i, carry):
            run(*[chunk_step(0, i * together + u) for u in range(together)])
            return carry

        lax.fori_loop(0, cps // together, chunk_body, 0)
        run(epilogue(0))
    else:
        run(prologue(0))
        for sb in range(nsub):
            run(pair_phase(sb), *([prologue(sb + 1)] if sb + 1 < nsub else []), *([tail(sb - 1)] if sb else []))
        run(tail(nsub - 1))

    if nb == 1:
        last = p_ref[tb - 1:tb, :].astype(F32).reshape(1, 1, GROUP_PAD)
    else:
        last = p_ref[...].astype(F32).reshape(nb, tb, GROUP_PAD)[:, tb - 1:tb, :]
    carry_ref[...] = last
    shout_ref[...] = last

    @pl.when(j == pl.num_programs(1) - 1)
    def _():
        for p in pairs:
            sp = st_ref[:, p]
            sout_ref[:, 2 * p] = sp[:, :, 0:hd]
            sout_ref[:, 2 * p + 1] = sp[:, :, hd:]


def _rwkv_row_major(nb, tb, c):
    return nb * tb == ROWS and tb == c == SUBLANES


def _rwkv(p, shift_prev, s0, wts, nb, tb, c, pp, seq=None, row0=0):
    bsz = s0.shape[0]
    seq = seq or p.shape[0] // bsz
    blk0 = row0 // (nb * tb)
    m_rows = ROWS
    nsub = nb * tb // m_rows
    assert nsub * m_rows == nb * tb and (nsub == 1 or nb == 1)
    hd = RW_HEAD_DIM
    rm = _rwkv_row_major(nb, tb, c)
    small = (1, 1, SUBLANES, LANES)
    const = lambda shape: pl.BlockSpec(shape, lambda i, j: (0,) * len(shape))
    kern = functools.partial(_rwkv_kernel, nb=nb, tb=tb, c=c, pp=pp)
    return pl.pallas_call(
        kern,
        grid=(bsz // nb, seq // tb),
        in_specs=[
            pl.BlockSpec((nb * tb, GROUP_PAD), lambda i, j: (blk0 + i * (seq // tb) + j, 0)),
            pl.BlockSpec((nb, 1, GROUP_PAD), lambda i, j: (i, 0, 0)),
            pl.BlockSpec((nb, RW_HEADS, hd, hd), lambda i, j: (i, 0, 0, 0)),
            const((1, GROUP_PAD)),
            const((LANES, 2 * RW_WIDTH)),
            const((1, RW_WIDTH)),
            const((1, RW_WIDTH)),
            const((2 * LANES, RW_WIDTH)),
            const((1, RW_WIDTH)),
            const((1, RW_WIDTH)),
            const((1, RW_WIDTH)),
            const((1, RW_WIDTH)),
            const((1, RW_WIDTH)),
            const((RW_WIDTH, LANES)),
            const((LANES, RW_WIDTH)),
            const((STRIP, STRIP)),
        ],
        out_specs=[
            pl.BlockSpec((nb * tb, RW_WIDTH), lambda i, j: (i * (seq // tb) + j, 0)),
            pl.BlockSpec((nb, 1, GROUP_PAD), lambda i, j: (i, 0, 0)),
            pl.BlockSpec((nb, RW_HEADS, hd, hd), lambda i, j: (i, 0, 0, 0)),
        ],
        out_shape=[
            jax.ShapeDtypeStruct((bsz * seq, RW_WIDTH), BF16),
            jax.ShapeDtypeStruct((bsz, 1, GROUP_PAD), F32),
            jax.ShapeDtypeStruct((bsz, RW_HEADS, hd, hd), F32),
        ],
        scratch_shapes=[
            pltpu.VMEM((nb, 1, GROUP_PAD), F32),
            pltpu.VMEM((nb, RW_HEADS // 2, hd, 2 * hd), F32),
            pltpu.VMEM((nsub, 2 * m_rows, RW_WIDTH), F32),
            pltpu.VMEM((nsub, 2 * m_rows, RW_WIDTH), F32),
            pltpu.VMEM((nsub, m_rows, RW_WIDTH), F32),
            pltpu.VMEM((nsub, RW_WIDTH, 2 * m_rows), F32),
            pltpu.VMEM((nsub, RW_WIDTH, m_rows), F32),
            pltpu.VMEM((nsub, RW_HEADS // 2, 2 * hd, 4 * m_rows), BF16),
            pltpu.VMEM((nsub, RW_HEADS // 2, hd, 4 * m_rows), F32),
            pltpu.VMEM((nsub, RW_WIDTH, m_rows), F32),
            pltpu.VMEM((nsub, m_rows, RW_WIDTH), F32),
            pltpu.VMEM((nsub, m_rows, RW_WIDTH), F32),
            pltpu.VMEM((nsub, RW_HEADS // 2, 4 * m_rows, 2 * hd) if rm else small, F32),
            pltpu.VMEM((nsub, m_rows, RW_WIDTH) if rm else small[1:], F32),
            pltpu.VMEM((nsub, m_rows, RW_WIDTH) if rm else small[1:], F32),
        ],
        compiler_params=pltpu.CompilerParams(
            dimension_semantics=("parallel", "arbitrary"), vmem_limit_bytes=VMEM_LIMIT),
        name="rwkv7",
    )(p, shift_prev, s0, *wts)


def _gla_kernel(p_ref, s0_ref, gw2_ref, gb_ref, nw_ref, o_ref, st_ref,
                acc_ref, qd_ref, v_ref, kt_ref, et_ref, *, nb, tb, c, pp):
    j = pl.program_id(1)
    m_rows = nb * tb
    dk, dv = GLA_DK, GLA_DV

    @pl.when(j == 0)
    def _():
        st_ref[...] = s0_ref[...]

    p = p_ref[...].astype(F32)
    q = p[:, 0:GLA_KEY_WIDTH] * (dk ** -0.5)
    k = p[:, GLA_OFF_K:GLA_OFF_K + GLA_KEY_WIDTH]
    v = p[:, GLA_OFF_V:GLA_OFF_V + GLA_WIDTH]
    xs = p[:, GLA_OFF_GATE:GLA_OFF_GATE + LANES].astype(BF16)
    gout = p[:, GLA_OFF_GOUT:GLA_OFF_GOUT + GLA_WIDTH]
    gk = -_softplus(-(_dot(xs, gw2_ref[...]) + gb_ref[...])) / GLA_GATE_NORM

    same, ri, ci = _chunk_masks(m_rows, c)
    lmask = jnp.where(same & (ci <= ri), 1.0, 0.0).astype(BF16)
    causal = same & (ci <= ri)
    gcum = _mm(lmask, gk, pb=3)
    g3 = gcum.reshape(m_rows // c, c, GLA_KEY_WIDTH)
    gtot = jnp.broadcast_to(g3[:, c - 1:c, :], g3.shape).reshape(m_rows, GLA_KEY_WIDTH)
    qd = q * jnp.exp(gcum)
    kinv = k * jnp.exp(-gcum)
    qd_ref[...] = qd
    v_ref[...] = v
    kt_ref[...] = (k * jnp.exp(gtot - gcum)).T
    et_ref[...] = jnp.exp(gtot).T

    heads = range(GLA_HEADS)
    kq = [slice(h * dk, (h + 1) * dk) for h in heads]
    vq = [slice(h * dv, (h + 1) * dv) for h in heads]
    scores = [jnp.where(causal, _mm(qd[:, ks], kinv[:, ks], pp, pp, NT), 0.0) for ks in kq]
    intra = [_mm(a, v[:, vs], pp, pp) for a, vs in zip(scores, vq)]
    for vs, o in zip(vq, intra):
        acc_ref[:, vs] = o

    sh_c = c.bit_length() - 1
    cpb = tb // c
    span = min(m_rows, LANES)
    row_id = lax.broadcasted_iota(jnp.int32, (span, dv), 0) >> sh_c
    for i in range(m_rows // c):
        b = i // cpb
        r0 = i * c
        t0 = (r0 // span) * span
        rmask = row_id == (r0 - t0) // c
        states = [st_ref[b, h] for h in heads]
        inter = [_mm(qd_ref[r0:r0 + c, ks], s, pp, pp) for ks, s in zip(kq, states)]
        upd = [_mm(kt_ref[ks, t0:t0 + span], jnp.where(rmask, v_ref[t0:t0 + span, vs], 0.0), pp, pp)
               for ks, vs in zip(kq, vq)]
        for h, ks, vs, s, oi, d in zip(heads, kq, vq, states, inter, upd):
            acc_ref[r0:r0 + c, vs] += oi
            st_ref[b, h] = s * et_ref[ks, r0:r0 + 1] + d

    for vs in vq:
        o = acc_ref[:, vs]
        on = o * lax.rsqrt(jnp.mean(o * o, axis=-1, keepdims=True) + HEAD_NORM_EPS) * nw_ref[...]
        gh = gout[:, vs]
        o_ref[:, vs] = (on * (gh * _sigmoid(gh))).astype(o_ref.dtype)


def _gla(p, s0, wts, nb, tb, c, pp, seq=None, row0=0):
    bsz = s0.shape[0]
    seq = seq or p.shape[0] // bsz
    blk0 = row0 // (nb * tb)
    m_rows = nb * tb
    const = lambda shape: pl.BlockSpec(shape, lambda i, j: (0,) * len(shape))
    kern = functools.partial(_gla_kernel, nb=nb, tb=tb, c=c, pp=pp)
    return pl.pallas_call(
        kern,
        grid=(bsz // nb, seq // tb),
        in_specs=[
            pl.BlockSpec((nb * tb, GROUP_PAD), lambda i, j: (blk0 + i * (seq // tb) + j, 1)),
            pl.BlockSpec((nb, GLA_HEADS, GLA_DK, GLA_DV), lambda i, j: (i, 0, 0, 0)),
            const((LANES, GLA_KEY_WIDTH)),
            const((1, GLA_KEY_WIDTH)),
            const((1, GLA_DV)),
        ],
        out_specs=[
            pl.BlockSpec((nb * tb, GLA_WIDTH), lambda i, j: (i * (seq // tb) + j, 0)),
            pl.BlockSpec((nb, GLA_HEADS, GLA_DK, GLA_DV), lambda i, j: (i, 0, 0, 0)),
        ],
        out_shape=[
            jax.ShapeDtypeStruct((bsz * seq, GLA_WIDTH), BF16),
            jax.ShapeDtypeStruct((bsz, GLA_HEADS, GLA_DK, GLA_DV), F32),
        ],
        scratch_shapes=[
            pltpu.VMEM((m_rows, GLA_WIDTH), F32),
            pltpu.VMEM((m_rows, GLA_KEY_WIDTH), F32),
            pltpu.VMEM((m_rows, GLA_WIDTH), F32),
            pltpu.VMEM((GLA_KEY_WIDTH, m_rows), F32),
            pltpu.VMEM((GLA_KEY_WIDTH, m_rows), F32),
        ],
        compiler_params=pltpu.CompilerParams(
            dimension_semantics=("parallel", "arbitrary"), vmem_limit_bytes=VMEM_LIMIT),
        name="gla",
    )(p, s0, *wts)


def _post_kernel(x_ref, orw_ref, ogla_ref, wo_ref, g2_ref, wu_ref, wd_ref, gf_ref, o_ref, h_ref):
    jf = pl.program_id(1)

    @pl.when(jf == 0)
    def _():
        x1 = (x_ref[...]
              + _dot(orw_ref[...], wo_ref[0:RW_WIDTH, :])
              + _dot(ogla_ref[...], wo_ref[RW_WIDTH:, :]))
        o_ref[...] = x1
        ms = jnp.mean(x1 * x1, axis=-1, keepdims=True)
        h_ref[...] = (x1 * lax.rsqrt(ms + NORM_EPS) * g2_ref[...]).astype(BF16)

    u = jnp.maximum(_dot(h_ref[...], wu_ref[...]), 0.0)
    o_ref[...] += _dot((u * u).astype(BF16), wd_ref[...])

    @pl.when(jf == pl.num_programs(1) - 1)
    def _():
        x2 = o_ref[...]
        ms = jnp.mean(x2 * x2, axis=-1, keepdims=True)
        o_ref[...] = x2 * lax.rsqrt(ms + NORM_EPS) * gf_ref[...]


def _post(x2d, o_rw, o_gla, w_out, g2, w_up, w_down, gf, tm, tf):
    t = x2d.shape[0]
    return pl.pallas_call(
        _post_kernel,
        grid=(t // tm, D_FF // tf),
        in_specs=[
            pl.BlockSpec((tm, D_MODEL), lambda i, j: (i, 0)),
            pl.BlockSpec((tm, RW_WIDTH), lambda i, j: (i, 0)),
            pl.BlockSpec((tm, GLA_WIDTH), lambda i, j: (i, 0)),
            pl.BlockSpec((D_MODEL, D_MODEL), lambda i, j: (0, 0), pipeline_mode=pl.Buffered(1)),
            pl.BlockSpec((1, D_MODEL), lambda i, j: (0, 0)),
            pl.BlockSpec((D_MODEL, tf), lambda i, j: (0, j)),
            pl.BlockSpec((tf, D_MODEL), lambda i, j: (j, 0)),
            pl.BlockSpec((1, D_MODEL), lambda i, j: (0, 0)),
        ],
        out_specs=pl.BlockSpec((tm, D_MODEL), lambda i, j: (i, 0)),
        out_shape=jax.ShapeDtypeStruct((t, D_MODEL), F32),
        scratch_shapes=[pltpu.VMEM((tm, D_MODEL), BF16)],
        compiler_params=pltpu.CompilerParams(
            dimension_semantics=("parallel", "arbitrary"), vmem_limit_bytes=VMEM_LIMIT),
        name="post",
    )(x2d, o_rw, o_gla, w_out, g2, w_up, w_down, gf)


def _pad_cols(w, n):
    return jnp.pad(w, ((0, 0), (0, n - w.shape[1])))


def _prep_weights(w_in, rw_mu, rw_w0, rw_w2, rw_a0, rw_a2, rw_g2, rw_k_k, rw_k_a, rw_r_k, rw_ln_w, rw_ln_b,
                  gla_gw2, gla_gb, gla_norm_w):
    w_p = jnp.swapaxes(w_in, 0, 1).astype(BF16)

    row = lambda x: x.reshape(1, -1).astype(F32)
    mu = _pad_cols(row(rw_mu), GROUP_PAD)
    wwa = jnp.zeros((LANES, 2 * RW_WIDTH), F32)
    wwa = wwa.at[0:RW_LORA_W, 0:RW_WIDTH].set(rw_w2)
    wwa = wwa.at[RW_LORA_W:RW_LORA_W + RW_LORA_A, RW_WIDTH:].set(rw_a2)
    g2p = jnp.pad(rw_g2, ((0, 2 * LANES - RW_LORA_G), (0, 0)))
    head_of_col = jnp.arange(RW_WIDTH) // RW_HEAD_DIM
    e1 = (head_of_col[:, None] == jnp.arange(LANES)[None, :]).astype(BF16)
    rw_wts = (mu, wwa.astype(BF16), row(rw_w0), row(rw_a0), g2p.astype(BF16), row(rw_k_k), row(rw_k_a),
              row(rw_r_k), row(rw_ln_w), row(rw_ln_b), e1, e1.T,
              (head_of_col[:STRIP, None] == head_of_col[None, :STRIP]).astype(BF16))
    gw2p = jnp.pad(gla_gw2, ((0, LANES - GLA_GATE_RANK), (0, 0))).astype(BF16)
    gla_wts = (gw2p, row(gla_gb), row(gla_norm_w))
    return w_p, rw_wts, gla_wts


RW_PIECES = 1
GLA_PIECES = 1
PAIR_LOCKSTEP = 8
SAMPLE_CHUNKS_TOGETHER = 16
RW_STEP_ROWS = 4 * ROWS
GLA_STEP_ROWS = 2 * ROWS
STRIP = 256


def _trunk(x, proj, row0, shift, s_rw, s_gla, rw_wts, gla_wts, w_out, norm2_g, w_up, w_down, norm_f_g,
           nb, tb, c_rw, c_gla):
    bsz, seq, _ = x.shape
    t = bsz * seq
    x2d = x.reshape(t, D_MODEL)
    shift_p = _pad_cols(shift, GROUP_PAD).reshape(bsz, 1, GROUP_PAD)
    rw_tb = RW_STEP_ROWS if nb == 1 and seq % RW_STEP_ROWS == 0 else tb
    o_rw, sh_new, s_rw_new = _rwkv(proj, shift_p, s_rw, rw_wts, nb, rw_tb, c_rw, RW_PIECES, seq, row0)
    gla_tb = GLA_STEP_ROWS if nb == 1 and seq % GLA_STEP_ROWS == 0 else tb
    o_gla, s_gla_new = _gla(proj, s_gla, gla_wts, nb, gla_tb, c_gla, GLA_PIECES, seq, row0)
    y = _post(x2d, o_rw, o_gla, w_out, norm2_g.reshape(1, -1),
              w_up, w_down, norm_f_g.reshape(1, -1), POST_TM, POST_TF)
    return (y.reshape(bsz, seq, D_MODEL), sh_new[:, 0, :RW_PROJ][None], s_rw_new[None], s_gla_new[None])


def kernel(x_prompt, x_sample, state_rwkv_shift, state_rwkv_wkv, state_gla, norm1_g, w_in, rw_mu, rw_w0,
           rw_w2, rw_a0, rw_a2, rw_g2, rw_k_k, rw_k_a, rw_r_k, rw_ln_w, rw_ln_b, gla_gw2, gla_gb, gla_norm_w,
           w_out, norm2_g, w_up, w_down, norm_f_g):
    w_p, rw_wts, gla_wts = _prep_weights(
        w_in[0], rw_mu[0], rw_w0[0], rw_w2[0], rw_a0[0], rw_a2[0], rw_g2[0], rw_k_k[0], rw_k_a[0],
        rw_r_k[0].reshape(-1), rw_ln_w[0], rw_ln_b[0], gla_gw2[0], gla_gb[0], gla_norm_w[0])
    shared = (rw_wts, gla_wts, w_out[0].astype(BF16), norm2_g[0], w_up[0].astype(BF16),
              w_down[0].astype(BF16), norm_f_g)

    bp, lp, _ = x_prompt.shape
    bs, ls, _ = x_sample.shape
    dt = x_prompt.dtype
    proj = _inproj([x_prompt.reshape(bp * lp, D_MODEL), x_sample.reshape(bs * ls, D_MODEL)],
                   norm1_g[0].reshape(1, -1), w_p, INPROJ_TM, INPROJ_TN)
    out_p = _trunk(x_prompt, proj, 0, jnp.zeros((bp, RW_PROJ), dt),
                   jnp.zeros((bp, RW_HEADS, RW_HEAD_DIM, RW_HEAD_DIM), dt),
                   jnp.zeros((bp, GLA_HEADS, GLA_DK, GLA_DV), dt), *shared,
                   nb=1, tb=ROWS, c_rw=RW_CHUNK, c_gla=GLA_CHUNK)
    out_s = _trunk(x_sample, proj, bp * lp, state_rwkv_shift[0], state_rwkv_wkv[0], state_gla[0], *shared,
                   nb=ROWS // ls, tb=ls, c_rw=ls, c_gla=ls)
    return (out_p[0], out_s[0], out_p[1], out_p[2], out_p[3], out_s[1], out_s[2], out_s[3])
```
